```python
import jax, jax.numpy as jnp
from jax import lax
import numpy as np

D_MODEL = 1024
BATCH = 8
SEQ = 2048
DEPTH = 4

HEAD_DIM = 64
N_HEADS = D_MODEL // HEAD_DIM
CONV_HEADS = N_HEADS // 4
SG_HEADS = N_HEADS // 4
SB_HEADS = N_HEADS // 2
CONV_WIDTH = CONV_HEADS * HEAD_DIM
SG_WIDTH = SG_HEADS * HEAD_DIM
SB_WIDTH = SB_HEADS * HEAD_DIM
MIX_WIDTH = CONV_WIDTH + SG_WIDTH + SB_WIDTH
CONV_KERNEL = 31
SG_CHUNK = 128
SB_BLOCK = 128
OFF_CONV = 0
OFF_SG = OFF_CONV + 2 * CONV_WIDTH
OFF_SB = OFF_SG + 2 * SG_WIDTH
IN_WIDTH = OFF_SB + 3 * SB_WIDTH
FFN_HIDDEN = ((8 * D_MODEL + 3 * 256 - 1) // (3 * 256)) * 256
RMS_EPS = 1e-6
LN_EPS = 1e-5

kernel_name = "hybrid_conv_gmlp_stickbreaking_trunk"


def rms_norm(x, g):
    xf = x.astype(jnp.float32)
    y = xf * lax.rsqrt(jnp.mean(xf * xf, axis=-1, keepdims=True) + RMS_EPS)
    return (y * g.astype(jnp.float32)).astype(x.dtype)


def layer_norm(x, g, b):
    xf = x.astype(jnp.float32)
    mu = jnp.mean(xf, axis=-1, keepdims=True)
    xc = xf - mu
    var = jnp.mean(xc * xc, axis=-1, keepdims=True)
    y = xc * lax.rsqrt(var + LN_EPS) * g.astype(jnp.float32) + b.astype(jnp.float32)
    return y.astype(x.dtype)


def conv_module(val, gate, conv_w, conv_b, ln_g, ln_b):
    h = val * jax.nn.sigmoid(gate)
    h = lax.conv_general_dilated(
        h, conv_w[:, None, :].astype(h.dtype), window_strides=(1,),
        padding=[(CONV_KERNEL - 1, 0)],
        dimension_numbers=('NWC', 'WIO', 'NWC'),
        feature_group_count=CONV_WIDTH) + conv_b
    h = layer_norm(h, ln_g, ln_b)
    return jax.nn.silu(h)


def spatial_gating(uv, ln_g, ln_b, sg_w, sg_b):
    uv = jax.nn.gelu(uv, approximate=False)
    u, v = uv[..., :SG_WIDTH], uv[..., SG_WIDTH:]
    v = layer_norm(v, ln_g, ln_b)
    bsz, seq, _ = v.shape
    n_chunks = seq // SG_CHUNK
    v = v.reshape(bsz, n_chunks, SG_CHUNK, SG_HEADS, HEAD_DIM)
    causal = jnp.tril(jnp.ones((SG_CHUNK, SG_CHUNK), dtype=bool))
    w = jnp.where(causal[None], sg_w, 0)
    mixed = jnp.einsum('hts,bnshd->bnthd', w, v) + sg_b.T[None, None, :, :, None]
    return u * mixed.reshape(bsz, seq, SG_WIDTH)


def stick_breaking_attention(q, k, v):
    scale = HEAD_DIM ** -0.5
    seq = q.shape[2]
    outs = []
    for blk in range(seq // SB_BLOCK):
        t0 = blk * SB_BLOCK
        kv_len = t0 + SB_BLOCK
        qb = q[:, :, t0:kv_len].astype(jnp.float32)
        kb = k[:, :, :kv_len].astype(jnp.float32)
        vb = v[:, :, :kv_len].astype(jnp.float32)
        z = jnp.einsum('bhtd,bhsd->bhts', qb, kb) * scale
        t_pos = t0 + jnp.arange(SB_BLOCK)
        s_pos = jnp.arange(kv_len)
        causal = s_pos[None, :] < t_pos[:, None]
        log_not_beta = jnp.where(causal, jax.nn.log_sigmoid(-z), 0.0)
        between = lax.cumsum(log_not_beta, axis=3, reverse=True) - log_not_beta
        att = jnp.where(causal, jnp.exp(jax.nn.log_sigmoid(z) + between), 0.0)
        outs.append(jnp.einsum('bhts,bhsd->bhtd', att, vb))
    return jnp.concatenate(outs, axis=2).astype(v.dtype)


def hybrid_layer(x, mix_norm_g, w_in, conv_w, conv_b, conv_ln_g, conv_ln_b,
                 sg_ln_g, sg_ln_b, sg_w, sg_b, q_norm_g, k_norm_g, out_norm_g,
                 w_out, ffn_norm_g, w_gate_up, w_down):
    bsz, seq, _ = x.shape
    h = rms_norm(x, mix_norm_g)
    proj = jnp.einsum('bsd,de->bse', h, w_in)

    y_conv = conv_module(proj[..., OFF_CONV:OFF_CONV + CONV_WIDTH],
                         proj[..., OFF_CONV + CONV_WIDTH:OFF_SG],
                         conv_w, conv_b, conv_ln_g, conv_ln_b)

    y_sg = spatial_gating(proj[..., OFF_SG:OFF_SB], sg_ln_g, sg_ln_b, sg_w, sg_b)

    qkv = proj[..., OFF_SB:].reshape(bsz, seq, 3, SB_HEADS, HEAD_DIM)
    q = rms_norm(qkv[:, :, 0], q_norm_g).transpose(0, 2, 1, 3)
    k = rms_norm(qkv[:, :, 1], k_norm_g).transpose(0, 2, 1, 3)
    v = qkv[:, :, 2].transpose(0, 2, 1, 3)
    y_sb = stick_breaking_attention(q, k, v).transpose(0, 2, 1, 3).reshape(bsz, seq, SB_WIDTH)

    y = jnp.concatenate([
        rms_norm(y_conv, out_norm_g[:CONV_WIDTH]),
        rms_norm(y_sg, out_norm_g[CONV_WIDTH:CONV_WIDTH + SG_WIDTH]),
        rms_norm(y_sb, out_norm_g[CONV_WIDTH + SG_WIDTH:]),
    ], axis=-1)
    x = x + jnp.einsum('bse,ed->bsd', y, w_out)

    h = rms_norm(x, ffn_norm_g)
    gu = jnp.einsum('bsd,df->bsf', h, w_gate_up)
    act = jax.nn.silu(gu[..., :FFN_HIDDEN]) * gu[..., FFN_HIDDEN:]
    return x + jnp.einsum('bsf,fd->bsd', act, w_down)


def _fwd_setup_inputs(seed: int = 0) -> dict:
    key = jax.random.key(seed)
    ks = jax.random.split(key, 20)
    f32 = jnp.float32
    nrm = lambda k, shape: jax.random.normal(k, shape, dtype=f32)
    L = DEPTH
    return {
        "x": nrm(ks[0], (BATCH, SEQ, D_MODEL)),
        "mix_norm_g": 1.0 + 0.02 * nrm(ks[1], (L, D_MODEL)),
        "w_in": nrm(ks[2], (L, D_MODEL, IN_WIDTH)) * D_MODEL ** -0.5,
        "conv_w": nrm(ks[3], (L, CONV_KERNEL, CONV_WIDTH)) * CONV_KERNEL ** -0.5,
        "conv_b": 0.02 * nrm(ks[4], (L, CONV_WIDTH)),
        "conv_ln_g": 1.0 + 0.02 * nrm(ks[5], (L, CONV_WIDTH)),
        "conv_ln_b": 0.02 * nrm(ks[6], (L, CONV_WIDTH)),
        "sg_ln_g": 1.0 + 0.02 * nrm(ks[7], (L, SG_WIDTH)),
        "sg_ln_b": 0.02 * nrm(ks[8], (L, SG_WIDTH)),
        "sg_w": nrm(ks[9], (L, SG_HEADS, SG_CHUNK, SG_CHUNK)) * SG_CHUNK ** -0.5,
        "sg_b": 1.0 + 0.1 * nrm(ks[10], (L, SG_HEADS, SG_CHUNK)),
        "q_norm_g": 1.0 + 0.02 * nrm(ks[11], (L, HEAD_DIM)),
        "k_norm_g": 1.0 + 0.02 * nrm(ks[12], (L, HEAD_DIM)),
        "out_norm_g": 1.0 + 0.02 * nrm(ks[13], (L, MIX_WIDTH)),
        "w_out": nrm(ks[14], (L, MIX_WIDTH, D_MODEL)) * MIX_WIDTH ** -0.5,
        "ffn_norm_g": 1.0 + 0.02 * nrm(ks[15], (L, D_MODEL)),
        "w_gate_up": nrm(ks[16], (L, D_MODEL, 2 * FFN_HIDDEN)) * D_MODEL ** -0.5,
        "w_down": nrm(ks[17], (L, FFN_HIDDEN, D_MODEL)) * FFN_HIDDEN ** -0.5,
    }


def _fwd_reference(x, mix_norm_g, w_in, conv_w, conv_b, conv_ln_g, conv_ln_b,
              sg_ln_g, sg_ln_b, sg_w, sg_b, q_norm_g, k_norm_g, out_norm_g,
              w_out, ffn_norm_g, w_gate_up, w_down):
    for l in range(DEPTH):
        x = hybrid_layer(x, mix_norm_g[l], w_in[l], conv_w[l], conv_b[l],
                         conv_ln_g[l], conv_ln_b[l], sg_ln_g[l], sg_ln_b[l],
                         sg_w[l], sg_b[l], q_norm_g[l], k_norm_g[l], out_norm_g[l],
                         w_out[l], ffn_norm_g[l], w_gate_up[l], w_down[l])
    return x


import jax as _jax
import jax.numpy as _jnp

TWIN_FORMAT = 'train_step'
FWD_PARAMS = ['x', 'mix_norm_g', 'w_in', 'conv_w', 'conv_b', 'conv_ln_g', 'conv_ln_b', 'sg_ln_g', 'sg_ln_b', 'sg_w', 'sg_b', 'q_norm_g', 'k_norm_g', 'out_norm_g', 'w_out', 'ffn_norm_g', 'w_gate_up', 'w_down']
TWIN_WEIGHTS = ['mix_norm_g', 'w_in', 'conv_w', 'conv_b', 'conv_ln_g', 'conv_ln_b', 'sg_ln_g', 'sg_ln_b', 'sg_w', 'sg_b', 'q_norm_g', 'k_norm_g', 'out_norm_g', 'w_out', 'ffn_norm_g', 'w_gate_up', 'w_down']
TWIN_DIFF_INPUT = 'x'
TWIN_INPUTS = ['x', 'mix_norm_g', 'w_in', 'conv_w', 'conv_b', 'conv_ln_g', 'conv_ln_b', 'sg_ln_g', 'sg_ln_b', 'sg_w', 'sg_b', 'q_norm_g', 'k_norm_g', 'out_norm_g', 'w_out', 'ffn_norm_g', 'w_gate_up', 'w_down', 'loss_target', 'm_mix_norm_g', 'm_w_in', 'm_conv_w', 'm_conv_b', 'm_conv_ln_g', 'm_conv_ln_b', 'm_sg_ln_g', 'm_sg_ln_b', 'm_sg_w', 'm_sg_b', 'm_q_norm_g', 'm_k_norm_g', 'm_out_norm_g', 'm_w_out', 'm_ffn_norm_g', 'm_w_gate_up', 'm_w_down', 'v_mix_norm_g', 'v_w_in', 'v_conv_w', 'v_conv_b', 'v_conv_ln_g', 'v_conv_ln_b', 'v_sg_ln_g', 'v_sg_ln_b', 'v_sg_w', 'v_sg_b', 'v_q_norm_g', 'v_k_norm_g', 'v_out_norm_g', 'v_w_out', 'v_ffn_norm_g', 'v_w_gate_up', 'v_w_down']
TWIN_OUTPUTS = ['loss', 'grad_x', 'grad_mix_norm_g', 'grad_w_in', 'grad_conv_w', 'grad_conv_b', 'grad_conv_ln_g', 'grad_conv_ln_b', 'grad_sg_ln_g', 'grad_sg_ln_b', 'grad_sg_w', 'grad_sg_b', 'grad_q_norm_g', 'grad_k_norm_g', 'grad_out_norm_g', 'grad_w_out', 'grad_ffn_norm_g', 'grad_w_gate_up', 'grad_w_down', 'delta_mix_norm_g', 'delta_w_in', 'delta_conv_w', 'delta_conv_b', 'delta_conv_ln_g', 'delta_conv_ln_b', 'delta_sg_ln_g', 'delta_sg_ln_b', 'delta_sg_w', 'delta_sg_b', 'delta_q_norm_g', 'delta_k_norm_g', 'delta_out_norm_g', 'delta_w_out', 'delta_ffn_norm_g', 'delta_w_gate_up', 'delta_w_down', 'new_m_mix_norm_g', 'new_m_w_in', 'new_m_conv_w', 'new_m_conv_b', 'new_m_conv_ln_g', 'new_m_conv_ln_b', 'new_m_sg_ln_g', 'new_m_sg_ln_b', 'new_m_sg_w', 'new_m_sg_b', 'new_m_q_norm_g', 'new_m_k_norm_g', 'new_m_out_norm_g', 'new_m_w_out', 'new_m_ffn_norm_g', 'new_m_w_gate_up', 'new_m_w_down', 'new_v_mix_norm_g', 'new_v_w_in', 'new_v_conv_w', 'new_v_conv_b', 'new_v_conv_ln_g', 'new_v_conv_ln_b', 'new_v_sg_ln_g', 'new_v_sg_ln_b', 'new_v_sg_w', 'new_v_sg_b', 'new_v_q_norm_g', 'new_v_k_norm_g', 'new_v_out_norm_g', 'new_v_w_out', 'new_v_ffn_norm_g', 'new_v_w_gate_up', 'new_v_w_down']
TWIN_LEAF_KINDS = {'loss': 'loss', 'grad_x': 'grad_x', 'grad_mix_norm_g': 'grad_w', 'grad_w_in': 'grad_w', 'grad_conv_w': 'grad_w', 'grad_conv_b': 'grad_w', 'grad_conv_ln_g': 'grad_w', 'grad_conv_ln_b': 'grad_w', 'grad_sg_ln_g': 'grad_w', 'grad_sg_ln_b': 'grad_w', 'grad_sg_w': 'grad_w', 'grad_sg_b': 'grad_w', 'grad_q_norm_g': 'grad_w', 'grad_k_norm_g': 'grad_w', 'grad_out_norm_g': 'grad_w', 'grad_w_out': 'grad_w', 'grad_ffn_norm_g': 'grad_w', 'grad_w_gate_up': 'grad_w', 'grad_w_down': 'grad_w', 'delta_mix_norm_g': 'delta_w', 'delta_w_in': 'delta_w', 'delta_conv_w': 'delta_w', 'delta_conv_b': 'delta_w', 'delta_conv_ln_g': 'delta_w', 'delta_conv_ln_b': 'delta_w', 'delta_sg_ln_g': 'delta_w', 'delta_sg_ln_b': 'delta_w', 'delta_sg_w': 'delta_w', 'delta_sg_b': 'delta_w', 'delta_q_norm_g': 'delta_w', 'delta_k_norm_g': 'delta_w', 'delta_out_norm_g': 'delta_w', 'delta_w_out': 'delta_w', 'delta_ffn_norm_g': 'delta_w', 'delta_w_gate_up': 'delta_w', 'delta_w_down': 'delta_w', 'new_m_mix_norm_g': 'new_m', 'new_m_w_in': 'new_m', 'new_m_conv_w': 'new_m', 'new_m_conv_b': 'new_m', 'new_m_conv_ln_g': 'new_m', 'new_m_conv_ln_b': 'new_m', 'new_m_sg_ln_g': 'new_m', 'new_m_sg_ln_b': 'new_m', 'new_m_sg_w': 'new_m', 'new_m_sg_b': 'new_m', 'new_m_q_norm_g': 'new_m', 'new_m_k_norm_g': 'new_m', 'new_m_out_norm_g': 'new_m', 'new_m_w_out': 'new_m', 'new_m_ffn_norm_g': 'new_m', 'new_m_w_gate_up': 'new_m', 'new_m_w_down': 'new_m', 'new_v_mix_norm_g': 'new_v', 'new_v_w_in': 'new_v', 'new_v_conv_w': 'new_v', 'new_v_conv_b': 'new_v', 'new_v_conv_ln_g': 'new_v', 'new_v_conv_ln_b': 'new_v', 'new_v_sg_ln_g': 'new_v', 'new_v_sg_ln_b': 'new_v', 'new_v_sg_w': 'new_v', 'new_v_sg_b': 'new_v', 'new_v_q_norm_g': 'new_v', 'new_v_k_norm_g': 'new_v', 'new_v_out_norm_g': 'new_v', 'new_v_w_out': 'new_v', 'new_v_ffn_norm_g': 'new_v', 'new_v_w_gate_up': 'new_v', 'new_v_w_down': 'new_v'}


def _forward(args):
    return _fwd_reference(*[args[k] for k in FWD_PARAMS])


def _output_shape():
    out = _jax.eval_shape(lambda: _forward(_fwd_setup_inputs(0)))
    return out.shape, out.dtype

N_MICROBATCH = 1
ADAM_LR = 0.001
ADAM_B1 = 0.9
ADAM_B2 = 0.999
ADAM_EPS = 1e-08
ADAM_WD = 0.01
ADAM_STEP = 10
PER_EXAMPLE_BATCH_AXIS = {'x': 0, 'loss_target': 0}
SHARED_INPUTS = []
_WEIGHT_DTYPES = {'mix_norm_g': _jnp.float32, 'w_in': _jnp.float32, 'conv_w': _jnp.float32, 'conv_b': _jnp.float32, 'conv_ln_g': _jnp.float32, 'conv_ln_b': _jnp.float32, 'sg_ln_g': _jnp.float32, 'sg_ln_b': _jnp.float32, 'sg_w': _jnp.float32, 'sg_b': _jnp.float32, 'q_norm_g': _jnp.float32, 'k_norm_g': _jnp.float32, 'out_norm_g': _jnp.float32, 'w_out': _jnp.float32, 'ffn_norm_g': _jnp.float32, 'w_gate_up': _jnp.float32, 'w_down': _jnp.float32}
MOMENT_SCALE = {'mix_norm_g': 2.104327e+00, 'w_in': 1.351463e+00, 'conv_w': 1.967070e+00, 'conv_b': 2.127444e+01, 'conv_ln_g': 8.464642e+00, 'conv_ln_b': 1.194781e+01, 'sg_ln_g': 3.997403e-01, 'sg_ln_b': 3.996414e-01, 'sg_w': 2.327279e-01, 'sg_b': 3.699247e-01, 'q_norm_g': 7.424290e-01, 'k_norm_g': 7.409946e-01, 'out_norm_g': 1.724015e+01, 'w_out': 4.014398e+00, 'ffn_norm_g': 1.218398e+01, 'w_gate_up': 4.355842e-01, 'w_down': 7.079007e-01}


def _to_microbatches(a, axis):
    t = _jnp.moveaxis(a, axis, 0)
    t = t.reshape((N_MICROBATCH, t.shape[0] // N_MICROBATCH) + t.shape[1:])
    return _jnp.moveaxis(t, 1, axis + 1)


def setup_inputs(seed: int = 0) -> dict:
    inp = _fwd_setup_inputs(seed)
    key = _jax.random.fold_in(_jax.random.key(seed), 7919)
    shape, _ = _output_shape()
    out = dict(inp)
    out["loss_target"] = _jax.random.normal(_jax.random.fold_in(key, 0), shape, _jnp.float32)
    for i, name in enumerate(TWIN_WEIGHTS):
        w = inp[name].astype(_jnp.float32)
        if MOMENT_SCALE is None:
            s = _jnp.sqrt(_jnp.mean(_jnp.square(w)) + 1e-30)
        else:
            s = MOMENT_SCALE[name]
        km, kv = _jax.random.split(_jax.random.fold_in(key, i + 1))
        out[name] = w
        out["m_" + name] = s * _jax.random.normal(km, w.shape, _jnp.float32)
        out["v_" + name] = (s * s) * _jax.random.uniform(kv, w.shape, _jnp.float32, 0.5, 1.5)
    if N_MICROBATCH > 1:
        for name, axis in PER_EXAMPLE_BATCH_AXIS.items():
            out[name] = _to_microbatches(out[name], axis)
    return {'x': out['x'], 'mix_norm_g': out['mix_norm_g'], 'w_in': out['w_in'], 'conv_w': out['conv_w'], 'conv_b': out['conv_b'], 'conv_ln_g': out['conv_ln_g'], 'conv_ln_b': out['conv_ln_b'], 'sg_ln_g': out['sg_ln_g'], 'sg_ln_b': out['sg_ln_b'], 'sg_w': out['sg_w'], 'sg_b': out['sg_b'], 'q_norm_g': out['q_norm_g'], 'k_norm_g': out['k_norm_g'], 'out_norm_g': out['out_norm_g'], 'w_out': out['w_out'], 'ffn_norm_g': out['ffn_norm_g'], 'w_gate_up': out['w_gate_up'], 'w_down': out['w_down'], 'loss_target': out['loss_target'], 'm_mix_norm_g': out['m_mix_norm_g'], 'm_w_in': out['m_w_in'], 'm_conv_w': out['m_conv_w'], 'm_conv_b': out['m_conv_b'], 'm_conv_ln_g': out['m_conv_ln_g'], 'm_conv_ln_b': out['m_conv_ln_b'], 'm_sg_ln_g': out['m_sg_ln_g'], 'm_sg_ln_b': out['m_sg_ln_b'], 'm_sg_w': out['m_sg_w'], 'm_sg_b': out['m_sg_b'], 'm_q_norm_g': out['m_q_norm_g'], 'm_k_norm_g': out['m_k_norm_g'], 'm_out_norm_g': out['m_out_norm_g'], 'm_w_out': out['m_w_out'], 'm_ffn_norm_g': out['m_ffn_norm_g'], 'm_w_gate_up': out['m_w_gate_up'], 'm_w_down': out['m_w_down'], 'v_mix_norm_g': out['v_mix_norm_g'], 'v_w_in': out['v_w_in'], 'v_conv_w': out['v_conv_w'], 'v_conv_b': out['v_conv_b'], 'v_conv_ln_g': out['v_conv_ln_g'], 'v_conv_ln_b': out['v_conv_ln_b'], 'v_sg_ln_g': out['v_sg_ln_g'], 'v_sg_ln_b': out['v_sg_ln_b'], 'v_sg_w': out['v_sg_w'], 'v_sg_b': out['v_sg_b'], 'v_q_norm_g': out['v_q_norm_g'], 'v_k_norm_g': out['v_k_norm_g'], 'v_out_norm_g': out['v_out_norm_g'], 'v_w_out': out['v_w_out'], 'v_ffn_norm_g': out['v_ffn_norm_g'], 'v_w_gate_up': out['v_w_gate_up'], 'v_w_down': out['v_w_down']}


def _loss(weights, diff, rest, loss_target):
    with _jax.named_scope("forward"):
        args = {**rest, TWIN_DIFF_INPUT: diff, **{k: w.astype(_WEIGHT_DTYPES[k]) for k, w in weights.items()}}
        y = _forward(args)
    with _jax.named_scope("loss_head"):
        err = _jnp.square(y.astype(_jnp.float32) - loss_target)
        return 0.5 * _jnp.sum(_jnp.mean(err, axis=-1)) if err.ndim else 0.5 * err


def _adamw(w, g, m, v):
    m = ADAM_B1 * m + (1.0 - ADAM_B1) * g
    v = ADAM_B2 * v + (1.0 - ADAM_B2) * _jnp.square(g)
    m_hat = m / (1.0 - ADAM_B1 ** ADAM_STEP)
    v_hat = v / (1.0 - ADAM_B2 ** ADAM_STEP)
    delta = -ADAM_LR * (m_hat / (_jnp.sqrt(v_hat) + ADAM_EPS) + ADAM_WD * w)
    return delta, m, v


def reference(x, mix_norm_g, w_in, conv_w, conv_b, conv_ln_g, conv_ln_b, sg_ln_g, sg_ln_b, sg_w, sg_b, q_norm_g, k_norm_g, out_norm_g, w_out, ffn_norm_g, w_gate_up, w_down, loss_target, m_mix_norm_g, m_w_in, m_conv_w, m_conv_b, m_conv_ln_g, m_conv_ln_b, m_sg_ln_g, m_sg_ln_b, m_sg_w, m_sg_b, m_q_norm_g, m_k_norm_g, m_out_norm_g, m_w_out, m_ffn_norm_g, m_w_gate_up, m_w_down, v_mix_norm_g, v_w_in, v_conv_w, v_conv_b, v_conv_ln_g, v_conv_ln_b, v_sg_ln_g, v_sg_ln_b, v_sg_w, v_sg_b, v_q_norm_g, v_k_norm_g, v_out_norm_g, v_w_out, v_ffn_norm_g, v_w_gate_up, v_w_down):
    given = dict(x=x, mix_norm_g=mix_norm_g, w_in=w_in, conv_w=conv_w, conv_b=conv_b, conv_ln_g=conv_ln_g, conv_ln_b=conv_ln_b, sg_ln_g=sg_ln_g, sg_ln_b=sg_ln_b, sg_w=sg_w, sg_b=sg_b, q_norm_g=q_norm_g, k_norm_g=k_norm_g, out_norm_g=out_norm_g, w_out=w_out, ffn_norm_g=ffn_norm_g, w_gate_up=w_gate_up, w_down=w_down, loss_target=loss_target, m_mix_norm_g=m_mix_norm_g, m_w_in=m_w_in, m_conv_w=m_conv_w, m_conv_b=m_conv_b, m_conv_ln_g=m_conv_ln_g, m_conv_ln_b=m_conv_ln_b, m_sg_ln_g=m_sg_ln_g, m_sg_ln_b=m_sg_ln_b, m_sg_w=m_sg_w, m_sg_b=m_sg_b, m_q_norm_g=m_q_norm_g, m_k_norm_g=m_k_norm_g, m_out_norm_g=m_out_norm_g, m_w_out=m_w_out, m_ffn_norm_g=m_ffn_norm_g, m_w_gate_up=m_w_gate_up, m_w_down=m_w_down, v_mix_norm_g=v_mix_norm_g, v_w_in=v_w_in, v_conv_w=v_conv_w, v_conv_b=v_conv_b, v_conv_ln_g=v_conv_ln_g, v_conv_ln_b=v_conv_ln_b, v_sg_ln_g=v_sg_ln_g, v_sg_ln_b=v_sg_ln_b, v_sg_w=v_sg_w, v_sg_b=v_sg_b, v_q_norm_g=v_q_norm_g, v_k_norm_g=v_k_norm_g, v_out_norm_g=v_out_norm_g, v_w_out=v_w_out, v_ffn_norm_g=v_ffn_norm_g, v_w_gate_up=v_w_gate_up, v_w_down=v_w_down)
    weights = {n: given[n] for n in TWIN_WEIGHTS}
    shared = {n: given[n] for n in SHARED_INPUTS}
    per_example = {n: given[n] for n in ['x']}
    grad_fn = _jax.value_and_grad(_loss, argnums=(0, 1))

    def one_microbatch(ex, loss_target):
        ex = dict(ex)
        diff = ex.pop(TWIN_DIFF_INPUT)
        return grad_fn(weights, diff, {**shared, **ex}, loss_target)

    if N_MICROBATCH == 1:
        loss, (grad_w, grad_x) = one_microbatch(per_example, given["loss_target"])
    else:
        def body(carry, xs):
            loss_sum, grad_sum = carry
            l_k, (gw_k, gx_k) = one_microbatch(xs[0], xs[1])
            with _jax.named_scope("update"):
                return (loss_sum + l_k, _jax.tree.map(_jnp.add, grad_sum, gw_k)), gx_k

        init = (_jnp.zeros((), _jnp.float32), _jax.tree.map(_jnp.zeros_like, weights))
        (loss, grad_w), grad_x = _jax.lax.scan(body, init, (per_example, given["loss_target"]))
    with _jax.named_scope("update"):
        delta_w, new_m, new_v = {}, {}, {}
        for n in TWIN_WEIGHTS:
            delta_w[n], new_m[n], new_v[n] = _adamw(weights[n], grad_w[n], given["m_" + n], given["v_" + n])
    return (loss, grad_x, *[grad_w[n] for n in TWIN_WEIGHTS], *[delta_w[n] for n in TWIN_WEIGHTS],
            *[new_m[n] for n in TWIN_WEIGHTS], *[new_v[n] for n in TWIN_WEIGHTS])
```

```python
import functools

import jax
import jax.numpy as jnp
from jax import lax
from jax.experimental import pallas as pl
from jax.experimental.pallas import tpu as pltpu

F32 = jnp.float32
BF16 = jnp.bfloat16
MXU_DT = jnp.bfloat16

D_MODEL = 1024
DEPTH = 4
HEAD_DIM = 64
CONV_W = 256
SG_W = 256
SB_W = 512
CONV_K = 31
CHUNK = 128
OFF_SG = 2 * CONV_W
OFF_SB = OFF_SG + 2 * SG_W
IN_W = OFF_SB + 3 * SB_W
FFN = 2816
N_CHIP = 4
RMS_EPS = 1e-6
LN_EPS = 1e-5
ADAM_LR = 0.001
ADAM_B1 = 0.9
ADAM_B2 = 0.999
ADAM_EPS = 1e-08
ADAM_WD = 0.01
ADAM_STEP = 10
BC1 = 1.0 - ADAM_B1 ** ADAM_STEP
BC2 = 1.0 - ADAM_B2 ** ADAM_STEP
HALO = 32
MESH = pl.DeviceIdType.MESH
ANY = pl.BlockSpec(memory_space=pl.ANY)
VMEM_LIMIT = 56 * 1024 * 1024


def _cp(**kw):
    return pltpu.CompilerParams(vmem_limit_bytes=VMEM_LIMIT, **kw)


def _dot(a, b, dims):
    dn = {"nn": (((1,), (0,)), ((), ())), "nt": (((1,), (1,)), ((), ())), "tn": (((0,), (0,)), ((), ()))}[dims]
    return lax.dot_general(a.astype(MXU_DT), b.astype(MXU_DT), dn, preferred_element_type=F32)


def _cumdot(x, m):
    hi = x.astype(MXU_DT)
    lo = (x - hi.astype(F32)).astype(MXU_DT)
    dn = (((1,), (0,)), ((), ()))
    return (lax.dot_general(hi, m, dn, preferred_element_type=F32)
            + lax.dot_general(lo, m, dn, preferred_element_type=F32))


def _iota(shape, axis):
    return lax.broadcasted_iota(jnp.int32, shape, axis)


def _rms(x, g):
    return x * lax.rsqrt(jnp.mean(x * x, axis=-1, keepdims=True) + RMS_EPS) * g


def _ln(x, g, b):
    mu = jnp.mean(x, axis=-1, keepdims=True)
    xc = x - mu
    var = jnp.mean(xc * xc, axis=-1, keepdims=True)
    return xc * lax.rsqrt(var + LN_EPS) * g + b


def _glu(val, gate):
    return val * jax.nn.sigmoid(gate)


def _ln_silu(c, g, b):
    return jax.nn.silu(_ln(c, g, b))


_ERF_ALPHA = (-2.72614225801306e-10, 2.77068142495902e-08, -2.10102402082508e-06, -5.69250639462346e-05,
              -7.34990630326855e-04, -2.95459980854025e-03, -1.60960333262415e-02)
_ERF_BETA = (-1.45660718464996e-05, -2.13374055278905e-04, -1.68282697438203e-03, -7.37332916720468e-03,
             -1.42647390514189e-02)


def _erf(x):
    x = jnp.clip(x, -4.0, 4.0)
    x2 = x * x
    p = jnp.full_like(x, _ERF_ALPHA[0])
    for a in _ERF_ALPHA[1:]:
        p = p * x2 + a
    q = jnp.full_like(x, _ERF_BETA[0])
    for b in _ERF_BETA[1:]:
        q = q * x2 + b
    return x * p / q


@jax.custom_jvp
def _gelu(x):
    return 0.5 * x * (1.0 + _erf(x * (2.0 ** -0.5)))


@_gelu.defjvp
def _gelu_jvp(primals, tangents):
    (x,), (t,) = primals, tangents
    cdf = 0.5 * (1.0 + _erf(x * (2.0 ** -0.5)))
    pdf = jnp.exp(-0.5 * x * x) * ((2.0 * jnp.pi) ** -0.5)
    return x * cdf, t * (cdf + x * pdf)


def _sg_pre(uvp, g, b):
    uv = _gelu(uvp)
    return uv[:, :SG_W], _ln(uv[:, SG_W:], g, b)


def _outnorm(yc, ys, yb, g):
    return jnp.concatenate([_rms(yc, g[:, :CONV_W]), _rms(ys, g[:, CONV_W:CONV_W + SG_W]),
                            _rms(yb, g[:, CONV_W + SG_W:])], axis=-1)


def _qkv_fn(pq, pk, pv, gq, gk):
    outs = []
    for p, g, sc in ((pq, gq, HEAD_DIM ** -0.5), (pk, gk, 1.0)):
        for s in range(SB_W // 128):
            x = p[:, 128 * s:128 * (s + 1)]
            seg = _iota(x.shape, 1) < HEAD_DIM
            x2 = x * x
            s0 = jnp.sum(jnp.where(seg, x2, 0.0), axis=-1, keepdims=True)
            s1 = jnp.sum(jnp.where(seg, 0.0, x2), axis=-1, keepdims=True)
            ms = jnp.where(seg, s0, s1) * (1.0 / HEAD_DIM)
            outs.append(x * lax.rsqrt(ms + RMS_EPS) * (g * sc))
    outs.append(pv)
    return jnp.concatenate(outs, axis=-1)


def _swiglu(g, u):
    return jax.nn.silu(g) * u


def _softplus(z):
    return jnp.maximum(z, 0.0) + jnp.log(1.0 + jnp.exp(-jnp.abs(z)))


def mm_colblk(a, wb, out_dtype, name, tm=512):
    S, K = a.shape
    nb, _, C = wb.shape

    def body(a_ref, w_ref, o_ref):
        o_ref[...] = _dot(a_ref[...], w_ref[0], "nn").astype(o_ref.dtype)

    return pl.pallas_call(
        body, grid=(nb, S // tm),
        in_specs=[pl.BlockSpec((tm, K), lambda k, i: (i, 0)), pl.BlockSpec((1, K, C), lambda k, i: (k, 0, 0))],
        out_specs=pl.BlockSpec((tm, C), lambda k, i: (i, k)),
        out_shape=jax.ShapeDtypeStruct((S, nb * C), out_dtype), name=name, compiler_params=_cp())(a, wb)


def mm_res(a, w, res, name, tm=256):
    S, K = a.shape
    N = w.shape[1]

    def body(a_ref, w_ref, r_ref, o_ref):
        o_ref[...] = r_ref[...] + _dot(a_ref[...], w_ref[...], "nn")

    return pl.pallas_call(
        body, grid=(S // tm,),
        in_specs=[pl.BlockSpec((tm, K), lambda i: (i, 0)), pl.BlockSpec((K, N), lambda i: (0, 0)),
                  pl.BlockSpec((tm, N), lambda i: (i, 0))],
        out_specs=pl.BlockSpec((tm, N), lambda i: (i, 0)),
        out_shape=jax.ShapeDtypeStruct((S, N), F32), name=name, compiler_params=_cp())(a, w, res)


def ffn_up(h2, wgu, name, tm=256):
    S, K = h2.shape
    C = wgu.shape[2]

    def body(h_ref, wg_ref, wu_ref, g_ref, u_ref, a_ref):
        h = h_ref[...]
        g = _dot(h, wg_ref[0], "nn")
        u = _dot(h, wu_ref[0], "nn")
        g_ref[...] = g.astype(BF16)
        u_ref[...] = u.astype(BF16)
        a_ref[...] = _swiglu(g, u).astype(BF16)

    o = pl.BlockSpec((tm, C), lambda j, i: (i, j))
    sh = jax.ShapeDtypeStruct((S, 2 * C), BF16)
    return pl.pallas_call(
        body, grid=(2, S // tm),
        in_specs=[pl.BlockSpec((tm, K), lambda j, i: (i, 0)), pl.BlockSpec((1, K, C), lambda j, i: (j, 0, 0)),
                  pl.BlockSpec((1, K, C), lambda j, i: (2 + j, 0, 0))],
        out_specs=[o, o, o], out_shape=[sh, sh, sh], name=name, compiler_params=_cp())(h2, wgu, wgu)


def ffn_down_bwd(dx2, wdown, g, u, name, tm=256):
    S, N = dx2.shape
    C = FFN // 2

    def body(d_ref, w_ref, g_ref, u_ref, o_ref):
        d = d_ref[...]
        for j in range(2):
            cols = slice(j * C, (j + 1) * C)
            dact = _dot(d, w_ref[cols, :], "nt")
            _, vjp = jax.vjp(_swiglu, g_ref[:, cols].astype(F32), u_ref[:, cols].astype(F32))
            dg, du = vjp(dact)
            o_ref[:, cols] = dg.astype(BF16)
            o_ref[:, FFN + j * C:FFN + (j + 1) * C] = du.astype(BF16)

    row = pl.BlockSpec((tm, FFN), lambda i: (i, 0))
    return pl.pallas_call(
        body, grid=(S // tm,),
        in_specs=[pl.BlockSpec((tm, N), lambda i: (i, 0)), pl.BlockSpec((FFN, N), lambda i: (0, 0)), row, row],
        out_specs=pl.BlockSpec((tm, 2 * FFN), lambda i: (i, 0)),
        out_shape=jax.ShapeDtypeStruct((S, 2 * FFN), BF16), name=name, compiler_params=_cp())(dx2, wdown, g, u)


def mm_dgrad_colblk(do, wb, name, tm=512):
    S = do.shape[0]
    nb, K, C = wb.shape

    def body(d_ref, w_ref, o_ref):
        k = pl.program_id(1)
        r = _dot(d_ref[...], w_ref[0], "nt")

        @pl.when(k == 0)
        def _():
            o_ref[...] = r

        @pl.when(k != 0)
        def _():
            o_ref[...] += r

    return pl.pallas_call(
        body, grid=(S // tm, nb),
        in_specs=[pl.BlockSpec((tm, C), lambda i, k: (i, k)), pl.BlockSpec((1, K, C), lambda i, k: (k, 0, 0))],
        out_specs=pl.BlockSpec((tm, K), lambda i, k: (i, 0)),
        out_shape=jax.ShapeDtypeStruct((S, K), F32), name=name, compiler_params=_cp())(do, wb)


def mm_dgrad(do, w, name, tm=256):
    S, N = do.shape
    K = w.shape[0]

    def body(d_ref, w_ref, o_ref):
        o_ref[...] = _dot(d_ref[...], w_ref[...], "nt")

    return pl.pallas_call(
        body, grid=(S // tm,),
        in_specs=[pl.BlockSpec((tm, N), lambda i: (i, 0)), pl.BlockSpec((K, N), lambda i: (0, 0))],
        out_specs=pl.BlockSpec((tm, K), lambda i: (i, 0)),
        out_shape=jax.ShapeDtypeStruct((S, K), F32), name=name, compiler_params=_cp())(do, w)


def mm_wgrad(a, do, tk, tn, name, blocked):
    S, K = a.shape
    N = do.shape[1]

    def body(a_ref, d_ref, o_ref):
        r = _dot(a_ref[...], d_ref[...], "tn")
        if blocked:
            o_ref[0] = r
        else:
            o_ref[...] = r

    if blocked:
        out_spec = pl.BlockSpec((1, tk, tn), lambda n, j: (n, j, 0))
        out_shape = jax.ShapeDtypeStruct((N // tn, K, tn), F32)
    else:
        out_spec = pl.BlockSpec((tk, tn), lambda n, j: (j, n))
        out_shape = jax.ShapeDtypeStruct((K, N), F32)
    return pl.pallas_call(
        body, grid=(N // tn, K // tk),
        in_specs=[pl.BlockSpec((S, tk), lambda n, j: (0, j)), pl.BlockSpec((S, tn), lambda n, j: (0, n))],
        out_specs=out_spec, out_shape=out_shape, name=name, compiler_params=_cp())(a, do)


def rms_fwd(x, g3, l, name, tm=512):
    S, N = x.shape

    def body(x_ref, g_ref, o_ref):
        o_ref[...] = _rms(x_ref[...], g_ref[0]).astype(BF16)

    return pl.pallas_call(
        body, grid=(S // tm,),
        in_specs=[pl.BlockSpec((tm, N), lambda i: (i, 0)), pl.BlockSpec((1, 1, N), lambda i: (l, 0, 0))],
        out_specs=pl.BlockSpec((tm, N), lambda i: (i, 0)),
        out_shape=jax.ShapeDtypeStruct((S, N), BF16), name=name, compiler_params=_cp())(x, g3)


def rms_bwd(dh, x, g3, l, dres, name, tm=512):
    S, N = x.shape

    def body(dh_ref, x_ref, g_ref, r_ref, dx_ref, dg_ref):
        _, vjp = jax.vjp(_rms, x_ref[...], g_ref[0])
        dx, dg = vjp(dh_ref[...])
        dx_ref[...] = r_ref[...] + dx

        @pl.when(pl.program_id(0) == 0)
        def _():
            dg_ref[...] = jnp.zeros_like(dg_ref)

        dg_ref[...] += dg

    row = pl.BlockSpec((tm, N), lambda i: (i, 0))
    return pl.pallas_call(
        body, grid=(S // tm,),
        in_specs=[row, row, pl.BlockSpec((1, 1, N), lambda i: (l, 0, 0)), row],
        out_specs=[row, pl.BlockSpec((1, N), lambda i: (0, 0))],
        out_shape=[jax.ShapeDtypeStruct((S, N), F32), jax.ShapeDtypeStruct((1, N), F32)],
        name=name, compiler_params=_cp())(dh, x, g3, dres)


def outnorm_fwd(yc, ys, yb, g3, l, name, tm=512):
    S = yc.shape[0]

    def body(c_ref, s_ref, b_ref, g_ref, o_ref):
        o_ref[...] = _outnorm(c_ref[...], s_ref[...], b_ref[...], g_ref[0]).astype(BF16)

    return pl.pallas_call(
        body, grid=(S // tm,),
        in_specs=[pl.BlockSpec((tm, CONV_W), lambda i: (i, 0)), pl.BlockSpec((tm, SG_W), lambda i: (i, 0)),
                  pl.BlockSpec((tm, SB_W), lambda i: (i, 0)), pl.BlockSpec((1, 1, D_MODEL), lambda i: (l, 0, 0))],
        out_specs=pl.BlockSpec((tm, D_MODEL), lambda i: (i, 0)),
        out_shape=jax.ShapeDtypeStruct((S, D_MODEL), BF16), name=name, compiler_params=_cp())(yc, ys, yb, g3)


def outnorm_bwd(dyn, yc, ys, yb, g3, l, name, tm=512):
    S = yc.shape[0]

    def body(d_ref, c_ref, s_ref, b_ref, g_ref, dc_ref, ds_ref, db_ref, dg_ref):
        _, vjp = jax.vjp(_outnorm, c_ref[...], s_ref[...], b_ref[...], g_ref[0])
        dc, ds, db, dg = vjp(d_ref[...])
        dc_ref[...] = dc
        ds_ref[...] = ds
        db_ref[...] = db

        @pl.when(pl.program_id(0) == 0)
        def _():
            dg_ref[...] = jnp.zeros_like(dg_ref)

        dg_ref[...] += dg

    sc = pl.BlockSpec((tm, CONV_W), lambda i: (i, 0))
    ss = pl.BlockSpec((tm, SG_W), lambda i: (i, 0))
    sb = pl.BlockSpec((tm, SB_W), lambda i: (i, 0))
    return pl.pallas_call(
        body, grid=(S // tm,),
        in_specs=[pl.BlockSpec((tm, D_MODEL), lambda i: (i, 0)), sc, ss, sb,
                  pl.BlockSpec((1, 1, D_MODEL), lambda i: (l, 0, 0))],
        out_specs=[sc, ss, sb, pl.BlockSpec((1, D_MODEL), lambda i: (0, 0))],
        out_shape=[jax.ShapeDtypeStruct((S, CONV_W), F32), jax.ShapeDtypeStruct((S, SG_W), F32),
                   jax.ShapeDtypeStruct((S, SB_W), F32), jax.ShapeDtypeStruct((1, D_MODEL), F32)],
        name=name, compiler_params=_cp())(dyn, yc, ys, yb, g3)


def loss_head(y, t, name, tm=512):
    S, N = y.shape

    def body(y_ref, t_ref, dy_ref, l_ref):
        e = y_ref[...] - t_ref[...]
        dy_ref[...] = e * (1.0 / N)

        @pl.when(pl.program_id(0) == 0)
        def _():
            l_ref[...] = jnp.zeros_like(l_ref)

        l_ref[...] += (0.5 / N) * jnp.sum(jnp.sum(e * e, axis=-1, keepdims=True), axis=0, keepdims=True)

    row = pl.BlockSpec((tm, N), lambda i: (i, 0))
    return pl.pallas_call(
        body, grid=(S // tm,), in_specs=[row, row],
        out_specs=[row, pl.BlockSpec((1, 1), lambda i: (0, 0))],
        out_shape=[jax.ShapeDtypeStruct((S, N), F32), jax.ShapeDtypeStruct((1, 1), F32)],
        name=name, compiler_params=_cp())(y, t)


def _conv_taps(a, w_ref, T):
    acc = jnp.zeros((T, CONV_W), F32)
    for j in range(CONV_K):
        sh = CONV_K - 1 - j
        r = a if sh == 0 else pltpu.roll(a, sh, 0)
        acc = acc + r[HALO:HALO + T] * w_ref[pl.ds(j, 1), :]
    return acc


def conv_fwd(proj, cw, cb, lg, lb, l, name, T=256):
    S = proj.shape[0]
    nt = S // T

    def body(p_ref, w_ref, cb_ref, lg_ref, lb_ref, y_ref, hc_s):
        hc_s[0:HALO, :] = jnp.zeros((HALO, CONV_W), F32)

        def fill(i, _):
            r0 = pl.multiple_of(i * T, T)
            pc = p_ref[pl.ds(r0, T), :]
            hc_s[pl.ds(r0 + HALO, T), :] = _glu(pc[:, :CONV_W], pc[:, CONV_W:])
            return 0

        lax.fori_loop(0, nt, fill, 0)

        def tile(i, _):
            r0 = pl.multiple_of(i * T, T)
            c = _conv_taps(hc_s[pl.ds(r0, T + HALO), :], w_ref.at[0], T) + cb_ref[0]
            y_ref[pl.ds(r0, T), :] = _ln_silu(c, lg_ref[0], lb_ref[0])
            return 0

        lax.fori_loop(0, nt, tile, 0)

    vec = pl.BlockSpec((1, 1, CONV_W), lambda i: (l, 0, 0))
    return pl.pallas_call(
        body, grid=(1,),
        in_specs=[pl.BlockSpec((S, 2 * CONV_W), lambda i: (0, 0)), pl.BlockSpec((1, 32, CONV_W), lambda i: (l, 0, 0)),
                  vec, vec, vec],
        out_specs=pl.BlockSpec((S, CONV_W), lambda i: (0, 0)),
        out_shape=jax.ShapeDtypeStruct((S, CONV_W), F32),
        scratch_shapes=[pltpu.VMEM((S + HALO, CONV_W), F32)], name=name, compiler_params=_cp())(proj, cw, cb, lg, lb)


def conv_bwd(proj, dy, cw, cb, lg, lb, l, name, T=256):
    S = proj.shape[0]
    nt = S // T

    def body(p_ref, dy_ref, w_ref, cb_ref, lg_ref, lb_ref, dp_ref, dw_ref, dcb_ref, dlg_ref, dlb_ref, hc_s, dc_s):
        hc_s[0:HALO, :] = jnp.zeros((HALO, CONV_W), F32)
        dc_s[S:S + HALO, :] = jnp.zeros((HALO, CONV_W), F32)
        dw_ref[...] = jnp.zeros_like(dw_ref)
        dcb_ref[...] = jnp.zeros_like(dcb_ref)
        dlg_ref[...] = jnp.zeros_like(dlg_ref)
        dlb_ref[...] = jnp.zeros_like(dlb_ref)

        def fill(i, _):
            r0 = pl.multiple_of(i * T, T)
            pc = p_ref[pl.ds(r0, T), :]
            hc_s[pl.ds(r0 + HALO, T), :] = _glu(pc[:, :CONV_W], pc[:, CONV_W:])
            return 0

        lax.fori_loop(0, nt, fill, 0)

        def tile(i, _):
            r0 = pl.multiple_of(i * T, T)
            a = hc_s[pl.ds(r0, T + HALO), :]
            c = _conv_taps(a, w_ref.at[0], T) + cb_ref[0]
            _, vjp = jax.vjp(_ln_silu, c, lg_ref[0], lb_ref[0])
            dc, dlg, dlb = vjp(dy_ref[pl.ds(r0, T), :])
            dc_s[pl.ds(r0, T), :] = dc
            dcb_ref[...] += jnp.sum(dc, axis=0, keepdims=True)
            dlg_ref[...] += dlg
            dlb_ref[...] += dlb
            for j in range(CONV_K):
                sh = CONV_K - 1 - j
                r = a if sh == 0 else pltpu.roll(a, sh, 0)
                dw_ref[pl.ds(j, 1), :] += jnp.sum(dc * r[HALO:HALO + T], axis=0, keepdims=True)
            return 0

        lax.fori_loop(0, nt, tile, 0)

        def back(i, _):
            r0 = pl.multiple_of(i * T, T)
            de = dc_s[pl.ds(r0, T + HALO), :]
            n = T + HALO
            dh = jnp.zeros((T, CONV_W), F32)
            for j in range(CONV_K):
                sh = CONV_K - 1 - j
                r = de if sh == 0 else pltpu.roll(de, n - sh, 0)
                dh = dh + r[0:T] * w_ref[0, pl.ds(j, 1), :]
            pc = p_ref[pl.ds(r0, T), :]
            _, vjp = jax.vjp(_glu, pc[:, :CONV_W], pc[:, CONV_W:])
            dval, dgate = vjp(dh)
            dp_ref[pl.ds(r0, T), :] = jnp.concatenate([dval, dgate], axis=-1).astype(BF16)
            return 0

        lax.fori_loop(0, nt, back, 0)

    vec = pl.BlockSpec((1, 1, CONV_W), lambda i: (l, 0, 0))
    ovec = pl.BlockSpec((1, CONV_W), lambda i: (0, 0))
    vsh = jax.ShapeDtypeStruct((1, CONV_W), F32)
    return pl.pallas_call(
        body, grid=(1,),
        in_specs=[pl.BlockSpec((S, 2 * CONV_W), lambda i: (0, 0)), pl.BlockSpec((S, CONV_W), lambda i: (0, 0)),
                  pl.BlockSpec((1, 32, CONV_W), lambda i: (l, 0, 0)), vec, vec, vec],
        out_specs=[pl.BlockSpec((S, 2 * CONV_W), lambda i: (0, 0)), pl.BlockSpec((32, CONV_W), lambda i: (0, 0)),
                   ovec, ovec, ovec],
        out_shape=[jax.ShapeDtypeStruct((S, 2 * CONV_W), BF16), jax.ShapeDtypeStruct((32, CONV_W), F32), vsh, vsh, vsh],
        scratch_shapes=[pltpu.VMEM((S + HALO, CONV_W), F32), pltpu.VMEM((S + HALO, CONV_W), F32)],
        name=name, compiler_params=_cp())(proj, dy, cw, cb, lg, lb)


def _sg_mix(wt_ref, v, bt):
    slabs = []
    for s in range(2):
        vs = v[:, 128 * s:128 * (s + 1)]
        seg = _iota(vs.shape, 1) < HEAD_DIM
        p0 = _dot(wt_ref[2 * s], vs, "nn") + bt[:, 2 * s:2 * s + 1]
        p1 = _dot(wt_ref[2 * s + 1], vs, "nn") + bt[:, 2 * s + 1:2 * s + 2]
        slabs.append(jnp.where(seg, p0, p1))
    return jnp.concatenate(slabs, axis=-1)


def sg_fwd(proj, lg, lb, wt, bt, l, name):
    S = proj.shape[0]

    def body(p_ref, lg_ref, lb_ref, w_ref, b_ref, y_ref):
        u, v = _sg_pre(p_ref[...], lg_ref[0], lb_ref[0])
        y_ref[...] = u * _sg_mix(w_ref.at[0], v, b_ref[0])

    vec = pl.BlockSpec((1, 1, SG_W), lambda i: (l, 0, 0))
    return pl.pallas_call(
        body, grid=(S // CHUNK,),
        in_specs=[pl.BlockSpec((CHUNK, 2 * SG_W), lambda i: (i, 1)), vec, vec,
                  pl.BlockSpec((1, 4, CHUNK, CHUNK), lambda i: (l, 0, 0, 0)),
                  pl.BlockSpec((1, CHUNK, 4), lambda i: (l, 0, 0))],
        out_specs=pl.BlockSpec((CHUNK, SG_W), lambda i: (i, 0)),
        out_shape=jax.ShapeDtypeStruct((S, SG_W), F32), name=name, compiler_params=_cp())(proj, lg, lb, wt, bt)


def sg_bwd(proj, dy, lg, lb, wt, wtt, bt, l, name):
    S = proj.shape[0]

    def body(p_ref, dy_ref, lg_ref, lb_ref, w_ref, wt_ref, b_ref, dp_ref, dlg_ref, dlb_ref, dw_ref, db_ref):
        @pl.when(pl.program_id(0) == 0)
        def _():
            dlg_ref[...] = jnp.zeros_like(dlg_ref)
            dlb_ref[...] = jnp.zeros_like(dlb_ref)
            dw_ref[...] = jnp.zeros_like(dw_ref)
            db_ref[...] = jnp.zeros_like(db_ref)

        (u, v), vjp = jax.vjp(_sg_pre, p_ref[...], lg_ref[0], lb_ref[0])
        dy = dy_ref[...]
        mixed = _sg_mix(w_ref.at[0], v, b_ref[0])
        du = dy * mixed
        dm = dy * u
        tril = _iota((CHUNK, CHUNK), 1) <= _iota((CHUNK, CHUNK), 0)
        dvs = []
        for s in range(2):
            dms = dm[:, 128 * s:128 * (s + 1)]
            vs = v[:, 128 * s:128 * (s + 1)]
            seg = _iota(dms.shape, 1) < HEAD_DIM
            halves = (jnp.where(seg, dms, 0.0), jnp.where(seg, 0.0, dms))
            dv_h = []
            for e in range(2):
                h = 2 * s + e
                db_ref[:, h:h + 1] += jnp.sum(halves[e], axis=-1, keepdims=True)
                dw_ref[h] += jnp.where(tril, _dot(halves[e], vs, "nt"), 0.0)
                dv_h.append(_dot(wt_ref[0, h], halves[e], "nn"))
            dvs.append(dv_h[0] + dv_h[1])
        dp, dlg, dlb = vjp((du, jnp.concatenate(dvs, axis=-1)))
        dp_ref[...] = dp.astype(BF16)
        dlg_ref[...] += dlg
        dlb_ref[...] += dlb

    vec = pl.BlockSpec((1, 1, SG_W), lambda i: (l, 0, 0))
    wsp = pl.BlockSpec((1, 4, CHUNK, CHUNK), lambda i: (l, 0, 0, 0))
    ovec = pl.BlockSpec((1, SG_W), lambda i: (0, 0))
    return pl.pallas_call(
        body, grid=(S // CHUNK,),
        in_specs=[pl.BlockSpec((CHUNK, 2 * SG_W), lambda i: (i, 1)), pl.BlockSpec((CHUNK, SG_W), lambda i: (i, 0)),
                  vec, vec, wsp, wsp, pl.BlockSpec((1, CHUNK, 4), lambda i: (l, 0, 0))],
        out_specs=[pl.BlockSpec((CHUNK, 2 * SG_W), lambda i: (i, 0)), ovec, ovec,
                   pl.BlockSpec((4, CHUNK, CHUNK), lambda i: (0, 0, 0)), pl.BlockSpec((CHUNK, 4), lambda i: (0, 0))],
        out_shape=[jax.ShapeDtypeStruct((S, 2 * SG_W), BF16), jax.ShapeDtypeStruct((1, SG_W), F32),
                   jax.ShapeDtypeStruct((1, SG_W), F32), jax.ShapeDtypeStruct((4, CHUNK, CHUNK), F32),
                   jax.ShapeDtypeStruct((CHUNK, 4), F32)],
        name=name, compiler_params=_cp())(proj, dy, lg, lb, wt, wtt, bt)


def qkv_fwd(proj, gq, gk, l, name, tm=512):
    S = proj.shape[0]

    def body(pq_ref, pk_ref, pv_ref, gq_ref, gk_ref, o_ref):
        o_ref[...] = _qkv_fn(pq_ref[...], pk_ref[...], pv_ref[...], gq_ref[0], gk_ref[0]).astype(BF16)

    vec = pl.BlockSpec((1, 1, 128), lambda i: (l, 0, 0))
    qb = OFF_SB // SB_W
    return pl.pallas_call(
        body, grid=(S // tm,),
        in_specs=[pl.BlockSpec((tm, SB_W), lambda i: (i, qb)), pl.BlockSpec((tm, SB_W), lambda i: (i, qb + 1)),
                  pl.BlockSpec((tm, SB_W), lambda i: (i, qb + 2)), vec, vec],
        out_specs=pl.BlockSpec((tm, 3 * SB_W), lambda i: (i, 0)),
        out_shape=jax.ShapeDtypeStruct((S, 3 * SB_W), BF16), name=name, compiler_params=_cp())(proj, proj, proj, gq, gk)


def qkv_bwd(proj, dq, dk, dv, gq, gk, l, name, tm=512):
    S = proj.shape[0]

    def body(pq_ref, pk_ref, pv_ref, dq_ref, dk_ref, dv_ref, gq_ref, gk_ref, dp_ref, dgq_ref, dgk_ref):
        _, vjp = jax.vjp(_qkv_fn, pq_ref[...], pk_ref[...], pv_ref[...], gq_ref[0], gk_ref[0])
        dpq, dpk, dpv, dgq, dgk = vjp(jnp.concatenate([dq_ref[...], dk_ref[...], dv_ref[...]], axis=-1))
        dp_ref[...] = jnp.concatenate([dpq, dpk, dpv], axis=-1).astype(BF16)

        @pl.when(pl.program_id(0) == 0)
        def _():
            dgq_ref[...] = jnp.zeros_like(dgq_ref)
            dgk_ref[...] = jnp.zeros_like(dgk_ref)

        dgq_ref[...] += dgq
        dgk_ref[...] += dgk

    vec = pl.BlockSpec((1, 1, 128), lambda i: (l, 0, 0))
    part = pl.BlockSpec((tm, SB_W), lambda i: (i, 0))
    ovec = pl.BlockSpec((1, 128), lambda i: (0, 0))
    qb = OFF_SB // SB_W
    return pl.pallas_call(
        body, grid=(S // tm,),
        in_specs=[pl.BlockSpec((tm, SB_W), lambda i: (i, qb)), pl.BlockSpec((tm, SB_W), lambda i: (i, qb + 1)),
                  pl.BlockSpec((tm, SB_W), lambda i: (i, qb + 2)), part, part, part, vec, vec],
        out_specs=[pl.BlockSpec((tm, 3 * SB_W), lambda i: (i, 0)), ovec, ovec],
        out_shape=[jax.ShapeDtypeStruct((S, 3 * SB_W), BF16), jax.ShapeDtypeStruct((1, 128), F32),
                   jax.ShapeDtypeStruct((1, 128), F32)],
        name=name, compiler_params=_cp())(proj, proj, proj, dq, dk, dv, gq, gk)


def _sb_block(qh, kblk, lim, dcol, m_gt, r):
    z = _dot(qh, kblk, "nt")
    allowed = dcol < lim
    sp = _softplus(z)
    lnb = jnp.where(allowed, -sp, 0.0)
    btw = _cumdot(lnb, m_gt)
    att = jnp.where(allowed, jnp.exp(z - sp + btw + r), 0.0)
    return z, sp, lnb, att, allowed


def attn_fwd(qkv, name):
    S = qkv.shape[0]
    nq = S // CHUNK
    npair = SB_W // 128

    def body(q_ref, k_ref, v_ref, o_ref, rt_ref):
        qi = pl.program_id(1)
        row = _iota((CHUNK, CHUNK), 0)
        col = _iota((CHUNK, CHUNK), 1)
        seg = col < HEAD_DIM
        dcol = col - row
        m_gt = (row > col).astype(MXU_DT)
        q = q_ref[...]
        qh = (jnp.where(seg, q, jnp.zeros_like(q)), jnp.where(seg, jnp.zeros_like(q), q))

        def step(it, carry):
            kb = qi - it
            off = pl.multiple_of(kb * CHUNK, CHUNK)
            kblk = k_ref[pl.ds(off, CHUNK), :]
            vblk = v_ref[pl.ds(off, CHUNK), :]
            lim = jnp.where(kb < qi, 2 * CHUNK, 0)
            new = []
            for h in range(2):
                acc, r = carry[2 * h], carry[2 * h + 1]
                _, _, lnb, att, _ = _sb_block(qh[h], kblk, lim, dcol, m_gt, r)
                new.append(acc + _dot(att, vblk, "nn"))
                new.append(r + jnp.sum(lnb, axis=-1, keepdims=True))
            return tuple(new)

        z0 = jnp.zeros((CHUNK, CHUNK), F32)
        r0 = jnp.zeros((CHUNK, 1), F32)
        res = lax.fori_loop(0, qi + 1, step, (z0, r0, z0, r0))
        o_ref[...] = jnp.where(seg, res[0], res[2])
        rt_ref[...] = jnp.where(seg, res[1], res[3])

    blk = pl.BlockSpec((CHUNK, 128), lambda p, i: (i, p))
    sh = jax.ShapeDtypeStruct((S, SB_W), F32)
    return pl.pallas_call(
        body, grid=(npair, nq),
        in_specs=[blk, pl.BlockSpec((S, 128), lambda p, i: (0, npair + p)),
                  pl.BlockSpec((S, 128), lambda p, i: (0, 2 * npair + p))],
        out_specs=[blk, blk], out_shape=[sh, sh], name=name, compiler_params=_cp())(qkv, qkv, qkv)


def attn_bwd(qkv, rt, do, name):
    S = qkv.shape[0]
    nq = S // CHUNK
    npair = SB_W // 128

    def body(q_ref, k_ref, v_ref, rt_ref, do_ref, dq_ref, dk_ref, dv_ref):
        qi = pl.program_id(1)

        @pl.when(qi == 0)
        def _():
            dk_ref[...] = jnp.zeros_like(dk_ref)
            dv_ref[...] = jnp.zeros_like(dv_ref)

        row = _iota((CHUNK, CHUNK), 0)
        col = _iota((CHUNK, CHUNK), 1)
        seg = col < HEAD_DIM
        dcol = col - row
        m_gt = (row > col).astype(MXU_DT)
        m_lt = (row < col).astype(MXU_DT)
        q = q_ref[...]
        zq = jnp.zeros_like(q)
        qh = (jnp.where(seg, q, zq), jnp.where(seg, zq, q))
        do = do_ref[...]
        doh = (jnp.where(seg, do, 0.0), jnp.where(seg, 0.0, do))
        rt = rt_ref[...]
        rtot = (rt[:, 0:1], rt[:, HEAD_DIM:HEAD_DIM + 1])

        def step(kb, carry):
            off = pl.multiple_of(kb * CHUNK, CHUNK)
            kblk = k_ref[pl.ds(off, CHUNK), :]
            vblk = v_ref[pl.ds(off, CHUNK), :]
            zk = jnp.zeros_like(kblk)
            kh = (jnp.where(seg, kblk, zk), jnp.where(seg, zk, kblk))
            allowed = dcol < jnp.where(kb < qi, 2 * CHUNK, 0)
            new = []
            dkb = jnp.zeros((CHUNK, 128), F32)
            dvb = jnp.zeros((CHUNK, 128), F32)
            for h in range(2):
                dq, lc, gc = carry[3 * h], carry[3 * h + 1], carry[3 * h + 2]
                z = _dot(qh[h], kblk, "nt")
                sp = _softplus(z)
                lnb = jnp.where(allowed, -sp, 0.0)
                lc = lc + jnp.sum(lnb, axis=-1, keepdims=True)
                att = jnp.where(allowed, jnp.exp(z - sp + _cumdot(lnb, m_gt) + (rtot[h] - lc)), 0.0)
                g = _dot(doh[h], vblk, "nt") * att
                c = gc + _cumdot(g, m_lt)
                sig = jnp.exp(z - sp)
                dz = jnp.where(allowed, g * (1.0 - sig) - c * sig, 0.0)
                new.append(dq + _dot(dz, kh[h], "nn"))
                new.append(lc)
                new.append(gc + jnp.sum(g, axis=-1, keepdims=True))
                dkb = dkb + _dot(dz, qh[h], "tn")
                dvb = dvb + _dot(att, doh[h], "tn")
            dk_ref[pl.ds(off, CHUNK), :] += dkb
            dv_ref[pl.ds(off, CHUNK), :] += dvb
            return tuple(new)

        z0 = jnp.zeros((CHUNK, 128), F32)
        c0 = jnp.zeros((CHUNK, 1), F32)
        res = lax.fori_loop(0, qi + 1, step, (z0, c0, c0, z0, c0, c0))
        dq_ref[...] = res[0] + res[3]

    blk = pl.BlockSpec((CHUNK, 128), lambda p, i: (i, p))
    full = pl.BlockSpec((S, 128), lambda p, i: (0, p))
    sh = jax.ShapeDtypeStruct((S, SB_W), F32)
    return pl.pallas_call(
        body, grid=(npair, nq),
        in_specs=[blk, pl.BlockSpec((S, 128), lambda p, i: (0, npair + p)),
                  pl.BlockSpec((S, 128), lambda p, i: (0, 2 * npair + p)), blk, blk],
        out_specs=[blk, full, full], out_shape=[sh, sh, sh], name=name, compiler_params=_cp())(qkv, qkv, qkv, rt, do)


def _adamw_math(w, g, m, v):
    m = ADAM_B1 * m + (1.0 - ADAM_B1) * g
    v = ADAM_B2 * v + (1.0 - ADAM_B2) * (g * g)
    m_hat = m / BC1
    v_hat = v / BC2
    delta = -ADAM_LR * (m_hat / (jnp.sqrt(v_hat) + ADAM_EPS) + ADAM_WD * w)
    return delta, m, v


def adamw_layer(w4, m4, v4, g, outs, l, name, tr):
    L, R, C = w4.shape

    def body(w_ref, m_ref, v_ref, g_ref, a0, a1, a2, a3, go_ref, d_ref, mo_ref, vo_ref):
        g = g_ref[...]
        d, m, v = _adamw_math(w_ref[0], g, m_ref[0], v_ref[0])
        go_ref[0] = g
        d_ref[0] = d
        mo_ref[0] = m
        vo_ref[0] = v

    st = pl.BlockSpec((1, tr, C), lambda i: (l, i, 0))
    sh = jax.ShapeDtypeStruct((L, R, C), F32)
    return pl.pallas_call(
        body, grid=(R // tr,),
        in_specs=[st, st, st, pl.BlockSpec((tr, C), lambda i: (i, 0)), ANY, ANY, ANY, ANY],
        out_specs=[st, st, st, st], out_shape=[sh, sh, sh, sh],
        input_output_aliases={4: 0, 5: 1, 6: 2, 7: 3}, name=name, compiler_params=_cp())(w4, m4, v4, g, *outs)


def adamw_flat(w, m, v, g, name):
    R, C = w.shape

    def body(w_ref, m_ref, v_ref, g_ref, d_ref, mo_ref, vo_ref):
        d, m2, v2 = _adamw_math(w_ref[...], g_ref[...], m_ref[...], v_ref[...])
        d_ref[...] = d
        mo_ref[...] = m2
        vo_ref[...] = v2

    full = pl.BlockSpec((R, C), lambda i: (0, 0))
    sh = jax.ShapeDtypeStruct((R, C), F32)
    return pl.pallas_call(body, grid=(1,), in_specs=[full] * 4, out_specs=[full] * 3, out_shape=[sh] * 3,
                          name=name, compiler_params=_cp())(w, m, v, g)


def add_half(g, l1, c, name, tr):
    nk, R, C = g.shape
    Rh = R // 2
    nt = Rh // tr

    def body(c_ref, g_ref, l_ref, o_ref):
        o_ref[...] = g_ref[...] + l_ref[...]

    gs = pltpu.PrefetchScalarGridSpec(
        num_scalar_prefetch=1, grid=(nk, nt),
        in_specs=[pl.BlockSpec((1, tr, C), lambda k, t, c_ref: (k, c_ref[0] * nt + t, 0)),
                  pl.BlockSpec((1, tr, C), lambda k, t, c_ref: (k, t, 0))],
        out_specs=pl.BlockSpec((1, tr, C), lambda k, t, c_ref: (k, t, 0)))
    return pl.pallas_call(body, grid_spec=gs, out_shape=jax.ShapeDtypeStruct((nk, Rh, C), F32), name=name,
                          compiler_params=_cp())(c, g, l1)


def sum_chips(p, l2, kme, name, tr):
    nk, Rh, C = p.shape

    def body(k_ref, p_ref, a_ref, b_ref, c_ref, o_ref):
        o_ref[...] = ((p_ref[0] + a_ref[0]) + b_ref[0]) + c_ref[0]

    def slot(j):
        return pl.BlockSpec((1, tr, C), lambda t, k_ref: (j, t, 0))

    gs = pltpu.PrefetchScalarGridSpec(
        num_scalar_prefetch=1, grid=(Rh // tr,),
        in_specs=[pl.BlockSpec((1, tr, C), lambda t, k_ref: (k_ref[0], t, 0)), slot(0), slot(1), slot(2)],
        out_specs=pl.BlockSpec((tr, C), lambda t, k_ref: (t, 0)))
    return pl.pallas_call(body, grid_spec=gs, out_shape=jax.ShapeDtypeStruct((Rh, C), F32), name=name,
                          compiler_params=_cp())(kme, p, l2, l2, l2)


def _place():
    x, y, c = lax.axis_index("x"), lax.axis_index("y"), lax.axis_index("c")
    chips = [(1 - x, y), (x, 1 - y), (1 - x, 1 - y)]
    return x, y, c, chips


def all_gather_chips(arrs, name):
    n = len(arrs)

    def body(*refs):
        ins, outs = refs[:n], refs[n:2 * n]
        send_sems, recv_sems, local_sems = refs[2 * n:]
        x, y, c, chips = _place()
        kme = 2 * x + y
        sib = (x, y, 1 - c)

        def half(ref, i, hc):
            rh = arrs[i].shape[0] // 2
            return ref.at[pl.ds(hc * rh, rh), :]

        def rcopy(i, s, src, dst, to):
            return pltpu.make_async_remote_copy(src_ref=src, dst_ref=dst, send_sem=send_sems.at[6 * i + s],
                                                recv_sem=recv_sems.at[6 * i + s], device_id=to, device_id_type=MESH)

        local = [pltpu.make_async_copy(ins[i], outs[i].at[kme], local_sems.at[i]) for i in range(n)]
        for cp in local:
            cp.start()
        first = [rcopy(i, j, half(ins[i], i, c), half(outs[i].at[kme], i, c), (*chip, c))
                 for i in range(n) for j, chip in enumerate(chips)]
        for cp in first:
            cp.start()
        passed = []
        for i in range(n):
            for j, chip in enumerate(chips):
                kj = 2 * chip[0] + chip[1]
                landed = half(outs[i].at[kj], i, c)
                rcopy(i, j, landed, landed, (*chip, c)).wait_recv()
                fw = rcopy(i, 3 + j, landed, landed, sib)
                fw.start()
                passed.append(fw)
        for i in range(n):
            for j, chip in enumerate(chips):
                kj = 2 * chip[0] + chip[1]
                other = half(outs[i].at[kj], i, 1 - c)
                rcopy(i, 3 + j, other, other, sib).wait_recv()
        for cp in first + passed:
            cp.wait_send()
        for cp in local:
            cp.wait()

    return pl.pallas_call(
        body, in_specs=[ANY] * n, out_specs=[ANY] * n,
        out_shape=[jax.ShapeDtypeStruct((N_CHIP,) + a.shape, a.dtype) for a in arrs],
        scratch_shapes=[pltpu.SemaphoreType.DMA((6 * n,)), pltpu.SemaphoreType.DMA((6 * n,)),
                        pltpu.SemaphoreType.DMA((n,))],
        name=name)(*arrs)


def exchange_sibling_half(gs, name):
    n = len(gs)

    def body(*refs):
        ins, outs = refs[:n], refs[n:2 * n]
        send_sems, recv_sems = refs[2 * n:]
        x, y, c, _ = _place()
        cps = []
        for i in range(n):
            rh = gs[i].shape[1] // 2
            cp = pltpu.make_async_remote_copy(
                src_ref=ins[i].at[:, pl.ds((1 - c) * rh, rh), :], dst_ref=outs[i], send_sem=send_sems.at[i],
                recv_sem=recv_sems.at[i], device_id=(x, y, 1 - c), device_id_type=MESH)
            cp.start()
            cps.append(cp)
        for cp in cps:
            cp.wait()

    return pl.pallas_call(
        body, in_specs=[ANY] * n, out_specs=[ANY] * n,
        out_shape=[jax.ShapeDtypeStruct((g.shape[0], g.shape[1] // 2, g.shape[2]), g.dtype) for g in gs],
        scratch_shapes=[pltpu.SemaphoreType.DMA((n,)), pltpu.SemaphoreType.DMA((n,))],
        name=name)(*gs)


def scatter_to_chips(ps, name):
    n = len(ps)

    def body(*refs):
        ins, outs = refs[:n], refs[n:2 * n]
        send_sems, recv_sems = refs[2 * n:]
        x, y, c, chips = _place()
        cps = []
        for i in range(n):
            for j, chip in enumerate(chips):
                cp = pltpu.make_async_remote_copy(
                    src_ref=ins[i].at[2 * chip[0] + chip[1]], dst_ref=outs[i].at[j], send_sem=send_sems.at[3 * i + j],
                    recv_sem=recv_sems.at[3 * i + j], device_id=(*chip, c), device_id_type=MESH)
                cp.start()
                cps.append(cp)
        for cp in cps:
            cp.wait()

    return pl.pallas_call(
        body, in_specs=[ANY] * n, out_specs=[ANY] * n,
        out_shape=[jax.ShapeDtypeStruct((3,) + p.shape[1:], p.dtype) for p in ps],
        scratch_shapes=[pltpu.SemaphoreType.DMA((3 * n,)), pltpu.SemaphoreType.DMA((3 * n,))],
        name=name)(*ps)


def join_sibling_halves(fs, name):
    n = len(fs)

    def body(*refs):
        ins, outs = refs[:n], refs[n:2 * n]
        send_sems, recv_sems, local_sems = refs[2 * n:]
        x, y, c, _ = _place()
        cps = []
        for i in range(n):
            lc = pltpu.make_async_copy(ins[i], outs[i].at[c], local_sems.at[i])
            lc.start()
            cp = pltpu.make_async_remote_copy(
                src_ref=ins[i], dst_ref=outs[i].at[c], send_sem=send_sems.at[i], recv_sem=recv_sems.at[i],
                device_id=(x, y, 1 - c), device_id_type=MESH)
            cp.start()
            cps.append((lc, cp))
        for i, (lc, cp) in enumerate(cps):
            lc.wait()
            cp.wait_send()
            other = outs[i].at[1 - c]
            pltpu.make_async_remote_copy(src_ref=other, dst_ref=other, send_sem=send_sems.at[i],
                                         recv_sem=recv_sems.at[i], device_id=(x, y, 1 - c),
                                         device_id_type=MESH).wait_recv()

    return pl.pallas_call(
        body, in_specs=[ANY] * n, out_specs=[ANY] * n,
        out_shape=[jax.ShapeDtypeStruct((2,) + f.shape, f.dtype) for f in fs],
        scratch_shapes=[pltpu.SemaphoreType.DMA((n,)), pltpu.SemaphoreType.DMA((n,)), pltpu.SemaphoreType.DMA((n,))],
        name=name)(*fs)


def all_reduce_small(v, name):
    R, C = v.shape

    def body(v_ref, sum_ref, all_ref, send_sems, recv_sems, local_sem):
        x, y, c, chips = _place()
        me, sib = (x, y, c), (x, y, 1 - c)

        def slab(px, py, pc):
            return all_ref.at[4 * px + 2 * py + pc]

        def copy(k, block, to, src=None):
            return pltpu.make_async_remote_copy(
                src_ref=slab(*block) if src is None else src, dst_ref=slab(*block), send_sem=send_sems.at[k],
                recv_sem=recv_sems.at[k], device_id=to, device_id_type=MESH)

        mine = pltpu.make_async_copy(v_ref, slab(*me), local_sem)
        mine.start()
        first = [copy(0, me, sib, src=v_ref)] + [copy(1 + j, me, (*chip, c), src=v_ref) for j, chip in enumerate(chips)]
        for cp in first:
            cp.start()
        passed = [copy(4 + j, (*chip, c), sib) for j, chip in enumerate(chips)]
        for j, chip in enumerate(chips):
            copy(1 + j, (*chip, c), me).wait_recv()
            passed[j].start()
        copy(0, sib, me).wait_recv()
        for j, chip in enumerate(chips):
            copy(4 + j, (*chip, 1 - c), me).wait_recv()
        for cp in first + passed:
            cp.wait_send()
        mine.wait()
        acc = all_ref[0]
        for d in range(1, 8):
            acc = acc + all_ref[d]
        sum_ref[...] = acc

    vm = pl.BlockSpec(memory_space=pltpu.VMEM)
    return pl.pallas_call(
        body, in_specs=[vm], out_specs=[vm, vm],
        out_shape=[jax.ShapeDtypeStruct((R, C), F32), jax.ShapeDtypeStruct((8, R, C), F32)],
        scratch_shapes=[pltpu.SemaphoreType.DMA((7,)), pltpu.SemaphoreType.DMA((7,)), pltpu.SemaphoreType.DMA],
        name=name, compiler_params=_cp())(v)[0]


SMALL = ("mix_norm_g", "conv_w", "conv_b", "conv_ln_g", "conv_ln_b", "sg_ln_g", "sg_ln_b", "sg_w", "sg_b",
         "q_norm_g", "k_norm_g", "out_norm_g", "ffn_norm_g")
BIG = ("w_in", "w_out", "w_gate_up", "w_down")
WEIGHTS = ("mix_norm_g", "w_in", "conv_w", "conv_b", "conv_ln_g", "conv_ln_b", "sg_ln_g", "sg_ln_b", "sg_w", "sg_b",
           "q_norm_g", "k_norm_g", "out_norm_g", "w_out", "ffn_norm_g", "w_gate_up", "w_down")
ADAM_ROWS = {"w_in": 256, "w_out": 128, "w_gate_up": 128, "w_down": 176}
RS_ROWS = {"w_in": 256, "w_out": 128, "w_gate_up": 256, "w_down": 176}


def _pack(parts):
    flat = jnp.concatenate([p.reshape(-1) for p in parts])
    n = flat.shape[0]
    rows = -(-n // (8 * 128)) * 8
    return jnp.pad(flat, (0, rows * 128 - n)).reshape(rows, 128)


def _unpack(buf, shapes):
    flat = buf.reshape(-1)
    out, off = [], 0
    for s in shapes:
        n = 1
        for d in s:
            n *= d
        out.append(flat[off:off + n].reshape(s))
        off += n
    return out


def layer_forward(x, l, P, W):
    sv = {"x": x}
    sv["h"] = rms_fwd(x, P["mix_norm_g"], l, f"rms_mix_{l}")
    sv["proj"] = mm_colblk(sv["h"], W["w_in"], F32, f"mm_in_{l}")
    sv["yc"] = conv_fwd(sv["proj"], P["conv_w"], P["conv_b"], P["conv_ln_g"], P["conv_ln_b"], l, f"conv_fwd_{l}")
    sv["ys"] = sg_fwd(sv["proj"], P["sg_ln_g"], P["sg_ln_b"], P["sg_wt"], P["sg_bt"], l, f"sg_fwd_{l}")
    sv["qkv"] = qkv_fwd(sv["proj"], P["q_norm_g"], P["k_norm_g"], l, f"qkv_fwd_{l}")
    sv["yb"], sv["rt"] = attn_fwd(sv["qkv"], f"attn_fwd_{l}")
    sv["yn"] = outnorm_fwd(sv["yc"], sv["ys"], sv["yb"], P["out_norm_g"], l, f"outnorm_fwd_{l}")
    sv["x1"] = mm_res(sv["yn"], W["w_out"].reshape(D_MODEL, D_MODEL), x, f"mm_out_{l}")
    sv["h2"] = rms_fwd(sv["x1"], P["ffn_norm_g"], l, f"rms_ffn_{l}")
    sv["g"], sv["u"], sv["act"] = ffn_up(sv["h2"], W["w_gate_up"], f"ffn_up_{l}")
    x2 = mm_res(sv["act"], W["w_down"].reshape(FFN, D_MODEL), sv["x1"], f"mm_down_{l}")
    return x2, sv


def layer_backward(dx2, l, P, W, sv):
    gb, gs = {}, {}
    wdown = W["w_down"].reshape(FFN, D_MODEL)
    dgu = ffn_down_bwd(dx2, wdown, sv["g"], sv["u"], f"ffn_down_bwd_{l}")
    gb["w_down"] = mm_wgrad(sv["act"], dx2, 1408, 512, f"wgrad_down_{l}", False).reshape(N_CHIP, FFN // N_CHIP, D_MODEL)
    dh2 = mm_dgrad_colblk(dgu, W["w_gate_up"], f"dgrad_gu_{l}")
    gb["w_gate_up"] = mm_wgrad(sv["h2"], dgu, 512, 1408, f"wgrad_gu_{l}", True)
    dx1, gs["ffn_norm_g"] = rms_bwd(dh2, sv["x1"], P["ffn_norm_g"], l, dx2, f"rms_ffn_bwd_{l}")
    dyn = mm_dgrad(dx1, W["w_out"].reshape(D_MODEL, D_MODEL), f"dgrad_out_{l}")
    gb["w_out"] = mm_wgrad(sv["yn"], dx1, 512, 512, f"wgrad_out_{l}", False).reshape(N_CHIP, D_MODEL // N_CHIP, D_MODEL)
    dyc, dys, dyb, gs["out_norm_g"] = outnorm_bwd(dyn, sv["yc"], sv["ys"], sv["yb"], P["out_norm_g"], l,
                                                  f"outnorm_bwd_{l}")
    dq, dk, dv = attn_bwd(sv["qkv"], sv["rt"], dyb, f"attn_bwd_{l}")
    dpb, dgq, dgk = qkv_bwd(sv["proj"], dq, dk, dv, P["q_norm_g"], P["k_norm_g"], l, f"qkv_bwd_{l}")
    gs["q_norm_g"] = dgq[0, :HEAD_DIM] + dgq[0, HEAD_DIM:]
    gs["k_norm_g"] = dgk[0, :HEAD_DIM] + dgk[0, HEAD_DIM:]
    dps, gs["sg_ln_g"], gs["sg_ln_b"], gs["sg_w"], dbt = sg_bwd(
        sv["proj"], dys, P["sg_ln_g"], P["sg_ln_b"], P["sg_wt"], P["sg_wtt"], P["sg_bt"], l, f"sg_bwd_{l}")
    gs["sg_b"] = dbt.T
    dpc, dcw, gs["conv_b"], gs["conv_ln_g"], gs["conv_ln_b"] = conv_bwd(
        sv["proj"], dyc, P["conv_w"], P["conv_b"], P["conv_ln_g"], P["conv_ln_b"], l, f"conv_bwd_{l}")
    gs["conv_w"] = dcw[:CONV_K]
    dproj = jnp.concatenate([dpc, dps, dpb], axis=-1)
    dh = mm_dgrad_colblk(dproj, W["w_in"], f"dgrad_in_{l}")
    gb["w_in"] = mm_wgrad(sv["h"], dproj, 512, IN_W // N_CHIP, f"wgrad_in_{l}", True)
    dx, gs["mix_norm_g"] = rms_bwd(dh, sv["x"], P["mix_norm_g"], l, dx1, f"rms_mix_bwd_{l}")
    return dx, gb, gs


def reduce_big(gb, l, c1, k1):
    gl = [gb[n] for n in BIG]
    l1 = exchange_sibling_half(gl, f"rs_sibling_{l}")
    ps = [add_half(g, a, c1, f"rs_add_{n}_{l}", RS_ROWS[n]) for n, g, a in zip(BIG, gl, l1)]
    l2 = scatter_to_chips(ps, f"rs_chips_{l}")
    fs = [sum_chips(p, a, k1, f"rs_sum_{n}_{l}", RS_ROWS[n]) for n, p, a in zip(BIG, ps, l2)]
    full = join_sibling_halves(fs, f"rs_join_{l}")
    return {n: f.reshape(2 * f.shape[1], f.shape[2]) for n, f in zip(BIG, full)}


def kernel(x, mix_norm_g, w_in, conv_w, conv_b, conv_ln_g, conv_ln_b, sg_ln_g, sg_ln_b, sg_w, sg_b, q_norm_g, k_norm_g, out_norm_g, w_out, ffn_norm_g, w_gate_up, w_down, loss_target, m_mix_norm_g, m_w_in, m_conv_w, m_conv_b, m_conv_ln_g, m_conv_ln_b, m_sg_ln_g, m_sg_ln_b, m_sg_w, m_sg_b, m_q_norm_g, m_k_norm_g, m_out_norm_g, m_w_out, m_ffn_norm_g, m_w_gate_up, m_w_down, v_mix_norm_g, v_w_in, v_conv_w, v_conv_b, v_conv_ln_g, v_conv_ln_b, v_sg_ln_g, v_sg_ln_b, v_sg_w, v_sg_b, v_q_norm_g, v_k_norm_g, v_out_norm_g, v_w_out, v_ffn_norm_g, v_w_gate_up, v_w_down):
    w = dict(mix_norm_g=mix_norm_g, w_in=w_in, conv_w=conv_w, conv_b=conv_b, conv_ln_g=conv_ln_g, conv_ln_b=conv_ln_b,
             sg_ln_g=sg_ln_g, sg_ln_b=sg_ln_b, sg_w=sg_w, sg_b=sg_b, q_norm_g=q_norm_g, k_norm_g=k_norm_g,
             out_norm_g=out_norm_g, w_out=w_out, ffn_norm_g=ffn_norm_g, w_gate_up=w_gate_up, w_down=w_down)
    m = dict(mix_norm_g=m_mix_norm_g, w_in=m_w_in, conv_w=m_conv_w, conv_b=m_conv_b, conv_ln_g=m_conv_ln_g,
             conv_ln_b=m_conv_ln_b, sg_ln_g=m_sg_ln_g, sg_ln_b=m_sg_ln_b, sg_w=m_sg_w, sg_b=m_sg_b,
             q_norm_g=m_q_norm_g, k_norm_g=m_k_norm_g, out_norm_g=m_out_norm_g, w_out=m_w_out,
             ffn_norm_g=m_ffn_norm_g, w_gate_up=m_w_gate_up, w_down=m_w_down)
    v = dict(mix_norm_g=v_mix_norm_g, w_in=v_w_in, conv_w=v_conv_w, conv_b=v_conv_b, conv_ln_g=v_conv_ln_g,
             conv_ln_b=v_conv_ln_b, sg_ln_g=v_sg_ln_g, sg_ln_b=v_sg_ln_b, sg_w=v_sg_w, sg_b=v_sg_b,
             q_norm_g=v_q_norm_g, k_norm_g=v_k_norm_g, out_norm_g=v_out_norm_g, w_out=v_w_out,
             ffn_norm_g=v_ffn_norm_g, w_gate_up=v_w_gate_up, w_down=v_w_down)
    L = DEPTH
    xi, yi, ci = lax.axis_index("x"), lax.axis_index("y"), lax.axis_index("c")
    kme = 2 * xi + yi
    c1 = ci.astype(jnp.int32).reshape(1)
    k1 = kme.astype(jnp.int32).reshape(1)

    W = [dict(zip(BIG, all_gather_chips([w[n][l].astype(BF16) for n in BIG], f"ag_{l}"))) for l in range(L)]
    cw_sh = jnp.pad(conv_w.reshape(L * CONV_K, CONV_W // N_CHIP), ((0, 128 - L * CONV_K), (0, 0)))
    cw_all = all_gather_chips([cw_sh], "ag_conv")[0][:, :L * CONV_K].reshape(N_CHIP, L, CONV_K, CONV_W // N_CHIP)
    cw_full = jnp.transpose(cw_all, (1, 2, 0, 3)).reshape(L, CONV_K, CONV_W)

    tril = jnp.tril(jnp.ones((CHUNK, CHUNK), bool))
    sg_wt = jnp.where(tril, sg_w, 0.0)
    P = {
        "mix_norm_g": mix_norm_g.reshape(L, 1, D_MODEL), "ffn_norm_g": ffn_norm_g.reshape(L, 1, D_MODEL),
        "out_norm_g": out_norm_g.reshape(L, 1, D_MODEL),
        "conv_w": jnp.pad(cw_full, ((0, 0), (0, 1), (0, 0))), "conv_b": conv_b.reshape(L, 1, CONV_W),
        "conv_ln_g": conv_ln_g.reshape(L, 1, CONV_W), "conv_ln_b": conv_ln_b.reshape(L, 1, CONV_W),
        "sg_ln_g": sg_ln_g.reshape(L, 1, SG_W), "sg_ln_b": sg_ln_b.reshape(L, 1, SG_W),
        "sg_wt": sg_wt.astype(MXU_DT), "sg_wtt": jnp.swapaxes(sg_wt, 2, 3).astype(MXU_DT),
        "sg_bt": jnp.swapaxes(sg_b, 1, 2),
        "q_norm_g": jnp.tile(q_norm_g, (1, 2)).reshape(L, 1, 128), "k_norm_g": jnp.tile(k_norm_g, (1, 2)).reshape(L, 1, 128),
    }

    h = x[0]
    saved = []
    for l in range(L):
        h, sv = layer_forward(h, l, P, W[l])
        saved.append(sv)
    dy, loss_part = loss_head(h, loss_target[0], "loss_head")
    loss = lax.psum(loss_part[0, 0], ("x", "y", "c"))

    outs = {n: [lax.empty(w[n].shape, F32) for _ in range(4)] for n in BIG}
    small_grads = [None] * L
    for l in reversed(range(L)):
        dy, gb, small_grads[l] = layer_backward(dy, l, P, W[l], saved[l])
        red = reduce_big(gb, l, c1, k1)
        for n in BIG:
            outs[n] = adamw_layer(w[n], m[n], v[n], red[n], outs[n], l, f"adamw_{n}_{l}", ADAM_ROWS[n])

    shapes = [(L,) + small_grads[0][n].shape for n in SMALL]
    packed = _pack([jnp.stack([small_grads[l][n] for l in range(L)]) for n in SMALL])
    gsum = dict(zip(SMALL, _unpack(all_reduce_small(packed, "ar_small"), shapes)))
    gsum["conv_w"] = lax.dynamic_slice_in_dim(gsum["conv_w"], kme * (CONV_W // N_CHIP), CONV_W // N_CHIP, axis=2)
    gsum = {n: gsum[n].reshape(w[n].shape) for n in SMALL}
    lshapes = [w[n].shape for n in SMALL]
    d_p, m_p, v_p = adamw_flat(_pack([w[n] for n in SMALL]), _pack([m[n] for n in SMALL]),
                               _pack([v[n] for n in SMALL]), _pack([gsum[n] for n in SMALL]), "adamw_small")
    d_s = dict(zip(SMALL, _unpack(d_p, lshapes)))
    m_s = dict(zip(SMALL, _unpack(m_p, lshapes)))
    v_s = dict(zip(SMALL, _unpack(v_p, lshapes)))

    grads = {n: (outs[n][0] if n in BIG else gsum[n]) for n in WEIGHTS}
    delta = {n: (outs[n][1] if n in BIG else d_s[n]) for n in WEIGHTS}
    new_m = {n: (outs[n][2] if n in BIG else m_s[n]) for n in WEIGHTS}
    new_v = {n: (outs[n][3] if n in BIG else v_s[n]) for n in WEIGHTS}
    return (loss, dy[None], *[grads[n] for n in WEIGHTS], *[delta[n] for n in WEIGHTS],
            *[new_m[n] for n in WEIGHTS], *[new_v[n] for n in WEIGHTS])
```

```python
import functools

import jax
import jax.numpy as jnp
from jax import lax
from jax.experimental import pallas as pl
from jax.experimental.pallas import tpu as pltpu

F32 = jnp.float32
BF16 = jnp.bfloat16
MXU_DT = jnp.bfloat16
GRAD_WIRE_DT = jnp.bfloat16

D_MODEL = 1024
DEPTH = 4
HEAD_DIM = 64
CONV_W = 256
SG_W = 256
SB_W = 512
CONV_K = 31
CHUNK = 128
OFF_SG = 2 * CONV_W
OFF_SB = OFF_SG + 2 * SG_W
IN_W = OFF_SB + 3 * SB_W
FFN = 2816
N_CHIP = 4
RMS_EPS = 1e-6
LN_EPS = 1e-5
ADAM_LR = 0.001
ADAM_B1 = 0.9
ADAM_B2 = 0.999
ADAM_EPS = 1e-08
ADAM_WD = 0.01
ADAM_STEP = 10
BC1 = 1.0 - ADAM_B1 ** ADAM_STEP
BC2 = 1.0 - ADAM_B2 ** ADAM_STEP
HALO = 32
MESH = pl.DeviceIdType.MESH
ANY = pl.BlockSpec(memory_space=pl.ANY)
VMEM_LIMIT = 56 * 1024 * 1024


def _cp(**kw):
    return pltpu.CompilerParams(vmem_limit_bytes=VMEM_LIMIT, **kw)


def _dot(a, b, dims):
    dn = {"nn": (((1,), (0,)), ((), ())), "nt": (((1,), (1,)), ((), ())), "tn": (((0,), (0,)), ((), ()))}[dims]
    return lax.dot_general(a.astype(MXU_DT), b.astype(MXU_DT), dn, preferred_element_type=F32)


def _cumdot(x, m):
    hi = x.astype(MXU_DT)
    lo = (x - hi.astype(F32)).astype(MXU_DT)
    dn = (((1,), (0,)), ((), ()))
    return (lax.dot_general(hi, m, dn, preferred_element_type=F32)
            + lax.dot_general(lo, m, dn, preferred_element_type=F32))


def _iota(shape, axis):
    return lax.broadcasted_iota(jnp.int32, shape, axis)


def _rms(x, g):
    return x * lax.rsqrt(jnp.mean(x * x, axis=-1, keepdims=True) + RMS_EPS) * g


def _ln(x, g, b):
    mu = jnp.mean(x, axis=-1, keepdims=True)
    xc = x - mu
    var = jnp.mean(xc * xc, axis=-1, keepdims=True)
    return xc * lax.rsqrt(var + LN_EPS) * g + b


def _glu(val, gate):
    return val * jax.nn.sigmoid(gate)


def _ln_silu(c, g, b):
    return jax.nn.silu(_ln(c, g, b))


_ERF_ALPHA = (-2.72614225801306e-10, 2.77068142495902e-08, -2.10102402082508e-06, -5.69250639462346e-05,
              -7.34990630326855e-04, -2.95459980854025e-03, -1.60960333262415e-02)
_ERF_BETA = (-1.45660718464996e-05, -2.13374055278905e-04, -1.68282697438203e-03, -7.37332916720468e-03,
             -1.42647390514189e-02)


def _erf(x):
    x = jnp.clip(x, -4.0, 4.0)
    x2 = x * x
    p = jnp.full_like(x, _ERF_ALPHA[0])
    for a in _ERF_ALPHA[1:]:
        p = p * x2 + a
    q = jnp.full_like(x, _ERF_BETA[0])
    for b in _ERF_BETA[1:]:
        q = q * x2 + b
    return x * p / q


@jax.custom_jvp
def _gelu(x):
    return 0.5 * x * (1.0 + _erf(x * (2.0 ** -0.5)))


@_gelu.defjvp
def _gelu_jvp(primals, tangents):
    (x,), (t,) = primals, tangents
    cdf = 0.5 * (1.0 + _erf(x * (2.0 ** -0.5)))
    pdf = jnp.exp(-0.5 * x * x) * ((2.0 * jnp.pi) ** -0.5)
    return x * cdf, t * (cdf + x * pdf)


def _sg_pre(uvp, g, b):
    uv = _gelu(uvp)
    return uv[:, :SG_W], _ln(uv[:, SG_W:], g, b)


def _outnorm(yc, ys, yb, g):
    return jnp.concatenate([_rms(yc, g[:, :CONV_W]), _rms(ys, g[:, CONV_W:CONV_W + SG_W]),
                            _rms(yb, g[:, CONV_W + SG_W:])], axis=-1)


def _qkv_fn(pq, pk, pv, gq, gk):
    outs = []
    for p, g, sc in ((pq, gq, HEAD_DIM ** -0.5), (pk, gk, 1.0)):
        for s in range(SB_W // 128):
            x = p[:, 128 * s:128 * (s + 1)]
            seg = _iota(x.shape, 1) < HEAD_DIM
            x2 = x * x
            s0 = jnp.sum(jnp.where(seg, x2, 0.0), axis=-1, keepdims=True)
            s1 = jnp.sum(jnp.where(seg, 0.0, x2), axis=-1, keepdims=True)
            ms = jnp.where(seg, s0, s1) * (1.0 / HEAD_DIM)
            outs.append(x * lax.rsqrt(ms + RMS_EPS) * (g * sc))
    outs.append(pv)
    return jnp.concatenate(outs, axis=-1)


def _swiglu(g, u):
    return jax.nn.silu(g) * u


def _softplus(z):
    return jnp.maximum(z, 0.0) + jnp.log(1.0 + jnp.exp(-jnp.abs(z)))


def mm_colblk(a, wb, out_dtype, name, tm=512):
    S, K = a.shape
    nb, _, C = wb.shape

    def body(a_ref, w_ref, o_ref):
        o_ref[...] = _dot(a_ref[...], w_ref[0], "nn").astype(o_ref.dtype)

    return pl.pallas_call(
        body, grid=(nb, S // tm),
        in_specs=[pl.BlockSpec((tm, K), lambda k, i: (i, 0)), pl.BlockSpec((1, K, C), lambda k, i: (k, 0, 0))],
        out_specs=pl.BlockSpec((tm, C), lambda k, i: (i, k)),
        out_shape=jax.ShapeDtypeStruct((S, nb * C), out_dtype), name=name, compiler_params=_cp())(a, wb)


def mm_res(a, w, res, name, tm=256):
    S, K = a.shape
    N = w.shape[1]

    def body(a_ref, w_ref, r_ref, o_ref):
        o_ref[...] = r_ref[...] + _dot(a_ref[...], w_ref[...], "nn")

    return pl.pallas_call(
        body, grid=(S // tm,),
        in_specs=[pl.BlockSpec((tm, K), lambda i: (i, 0)), pl.BlockSpec((K, N), lambda i: (0, 0)),
                  pl.BlockSpec((tm, N), lambda i: (i, 0))],
        out_specs=pl.BlockSpec((tm, N), lambda i: (i, 0)),
        out_shape=jax.ShapeDtypeStruct((S, N), F32), name=name, compiler_params=_cp())(a, w, res)


def ffn_up(h2, wgu, name, tm=256):
    S, K = h2.shape
    C = wgu.shape[2]

    def body(h_ref, wg_ref, wu_ref, g_ref, u_ref, a_ref):
        h = h_ref[...]
        g = _dot(h, wg_ref[0], "nn")
        u = _dot(h, wu_ref[0], "nn")
        g_ref[...] = g.astype(BF16)
        u_ref[...] = u.astype(BF16)
        a_ref[...] = _swiglu(g, u).astype(BF16)

    o = pl.BlockSpec((tm, C), lambda j, i: (i, j))
    sh = jax.ShapeDtypeStruct((S, 2 * C), BF16)
    return pl.pallas_call(
        body, grid=(2, S // tm),
        in_specs=[pl.BlockSpec((tm, K), lambda j, i: (i, 0)), pl.BlockSpec((1, K, C), lambda j, i: (j, 0, 0)),
                  pl.BlockSpec((1, K, C), lambda j, i: (2 + j, 0, 0))],
        out_specs=[o, o, o], out_shape=[sh, sh, sh], name=name, compiler_params=_cp())(h2, wgu, wgu)


def ffn_down_bwd(dx2, wdown, g, u, name, tm=256):
    S, N = dx2.shape
    C = FFN // 2

    def body(d_ref, w_ref, g_ref, u_ref, o_ref):
        d = d_ref[...]
        for j in range(2):
            cols = slice(j * C, (j + 1) * C)
            dact = _dot(d, w_ref[cols, :], "nt")
            _, vjp = jax.vjp(_swiglu, g_ref[:, cols].astype(F32), u_ref[:, cols].astype(F32))
            dg, du = vjp(dact)
            o_ref[:, cols] = dg.astype(BF16)
            o_ref[:, FFN + j * C:FFN + (j + 1) * C] = du.astype(BF16)

    row = pl.BlockSpec((tm, FFN), lambda i: (i, 0))
    return pl.pallas_call(
        body, grid=(S // tm,),
        in_specs=[pl.BlockSpec((tm, N), lambda i: (i, 0)), pl.BlockSpec((FFN, N), lambda i: (0, 0)), row, row],
        out_specs=pl.BlockSpec((tm, 2 * FFN), lambda i: (i, 0)),
        out_shape=jax.ShapeDtypeStruct((S, 2 * FFN), BF16), name=name, compiler_params=_cp())(dx2, wdown, g, u)


def mm_dgrad_colblk(do, wb, name, tm=512):
    S = do.shape[0]
    nb, K, C = wb.shape

    def body(d_ref, w_ref, o_ref):
        k = pl.program_id(1)
        r = _dot(d_ref[...], w_ref[0], "nt")

        @pl.when(k == 0)
        def _():
            o_ref[...] = r

        @pl.when(k != 0)
        def _():
            o_ref[...] += r

    return pl.pallas_call(
        body, grid=(S // tm, nb),
        in_specs=[pl.BlockSpec((tm, C), lambda i, k: (i, k)), pl.BlockSpec((1, K, C), lambda i, k: (k, 0, 0))],
        out_specs=pl.BlockSpec((tm, K), lambda i, k: (i, 0)),
        out_shape=jax.ShapeDtypeStruct((S, K), F32), name=name, compiler_params=_cp())(do, wb)


def mm_dgrad(do, w, name, tm=256):
    S, N = do.shape
    K = w.shape[0]

    def body(d_ref, w_ref, o_ref):
        o_ref[...] = _dot(d_ref[...], w_ref[...], "nt")

    return pl.pallas_call(
        body, grid=(S // tm,),
        in_specs=[pl.BlockSpec((tm, N), lambda i: (i, 0)), pl.BlockSpec((K, N), lambda i: (0, 0))],
        out_specs=pl.BlockSpec((tm, K), lambda i: (i, 0)),
        out_shape=jax.ShapeDtypeStruct((S, K), F32), name=name, compiler_params=_cp())(do, w)


def mm_wgrad(a, do, tk, tn, name, blocked):
    S, K = a.shape
    N = do.shape[1]

    def body(a_ref, d_ref, o_ref):
        r = _dot(a_ref[...], d_ref[...], "tn").astype(GRAD_WIRE_DT)
        if blocked:
            o_ref[0] = r
        else:
            o_ref[...] = r

    if blocked:
        out_spec = pl.BlockSpec((1, tk, tn), lambda n, j: (n, j, 0))
        out_shape = jax.ShapeDtypeStruct((N // tn, K, tn), GRAD_WIRE_DT)
    else:
        out_spec = pl.BlockSpec((tk, tn), lambda n, j: (j, n))
        out_shape = jax.ShapeDtypeStruct((K, N), GRAD_WIRE_DT)
    return pl.pallas_call(
        body, grid=(N // tn, K // tk),
        in_specs=[pl.BlockSpec((S, tk), lambda n, j: (0, j)), pl.BlockSpec((S, tn), lambda n, j: (0, n))],
        out_specs=out_spec, out_shape=out_shape, name=name, compiler_params=_cp())(a, do)


def rms_fwd(x, g3, l, name, tm=512):
    S, N = x.shape

    def body(x_ref, g_ref, o_ref):
        o_ref[...] = _rms(x_ref[...], g_ref[0]).astype(BF16)

    return pl.pallas_call(
        body, grid=(S // tm,),
        in_specs=[pl.BlockSpec((tm, N), lambda i: (i, 0)), pl.BlockSpec((1, 1, N), lambda i: (l, 0, 0))],
        out_specs=pl.BlockSpec((tm, N), lambda i: (i, 0)),
        out_shape=jax.ShapeDtypeStruct((S, N), BF16), name=name, compiler_params=_cp())(x, g3)


def rms_bwd(dh, x, g3, l, dres, name, tm=512):
    S, N = x.shape

    def body(dh_ref, x_ref, g_ref, r_ref, dx_ref, dg_ref):
        _, vjp = jax.vjp(_rms, x_ref[...], g_ref[0])
        dx, dg = vjp(dh_ref[...])
        dx_ref[...] = r_ref[...] + dx

        @pl.when(pl.program_id(0) == 0)
        def _():
            dg_ref[...] = jnp.zeros_like(dg_ref)

        dg_ref[...] += dg

    row = pl.BlockSpec((tm, N), lambda i: (i, 0))
    return pl.pallas_call(
        body, grid=(S // tm,),
        in_specs=[row, row, pl.BlockSpec((1, 1, N), lambda i: (l, 0, 0)), row],
        out_specs=[row, pl.BlockSpec((1, N), lambda i: (0, 0))],
        out_shape=[jax.ShapeDtypeStruct((S, N), F32), jax.ShapeDtypeStruct((1, N), F32)],
        name=name, compiler_params=_cp())(dh, x, g3, dres)


def outnorm_fwd(yc, ys, yb, g3, l, name, tm=512):
    S = yc.shape[0]

    def body(c_ref, s_ref, b_ref, g_ref, o_ref):
        o_ref[...] = _outnorm(c_ref[...], s_ref[...], b_ref[...], g_ref[0]).astype(BF16)

    return pl.pallas_call(
        body, grid=(S // tm,),
        in_specs=[pl.BlockSpec((tm, CONV_W), lambda i: (i, 0)), pl.BlockSpec((tm, SG_W), lambda i: (i, 0)),
                  pl.BlockSpec((tm, SB_W), lambda i: (i, 0)), pl.BlockSpec((1, 1, D_MODEL), lambda i: (l, 0, 0))],
        out_specs=pl.BlockSpec((tm, D_MODEL), lambda i: (i, 0)),
        out_shape=jax.ShapeDtypeStruct((S, D_MODEL), BF16), name=name, compiler_params=_cp())(yc, ys, yb, g3)


def outnorm_bwd(dyn, yc, ys, yb, g3, l, name, tm=512):
    S = yc.shape[0]

    def body(d_ref, c_ref, s_ref, b_ref, g_ref, dc_ref, ds_ref, db_ref, dg_ref):
        _, vjp = jax.vjp(_outnorm, c_ref[...], s_ref[...], b_ref[...], g_ref[0])
        dc, ds, db, dg = vjp(d_ref[...])
        dc_ref[...] = dc
        ds_ref[...] = ds
        db_ref[...] = db

        @pl.when(pl.program_id(0) == 0)
        def _():
            dg_ref[...] = jnp.zeros_like(dg_ref)

        dg_ref[...] += dg

    sc = pl.BlockSpec((tm, CONV_W), lambda i: (i, 0))
    ss = pl.BlockSpec((tm, SG_W), lambda i: (i, 0))
    sb = pl.BlockSpec((tm, SB_W), lambda i: (i, 0))
    return pl.pallas_call(
        body, grid=(S // tm,),
        in_specs=[pl.BlockSpec((tm, D_MODEL), lambda i: (i, 0)), sc, ss, sb,
                  pl.BlockSpec((1, 1, D_MODEL), lambda i: (l, 0, 0))],
        out_specs=[sc, ss, sb, pl.BlockSpec((1, D_MODEL), lambda i: (0, 0))],
        out_shape=[jax.ShapeDtypeStruct((S, CONV_W), F32), jax.ShapeDtypeStruct((S, SG_W), F32),
                   jax.ShapeDtypeStruct((S, SB_W), F32), jax.ShapeDtypeStruct((1, D_MODEL), F32)],
        name=name, compiler_params=_cp())(dyn, yc, ys, yb, g3)


def loss_head(y, t, name, tm=512):
    S, N = y.shape

    def body(y_ref, t_ref, dy_ref, l_ref):
        e = y_ref[...] - t_ref[...]
        dy_ref[...] = e * (1.0 / N)

        @pl.when(pl.program_id(0) == 0)
        def _():
            l_ref[...] = jnp.zeros_like(l_ref)

        l_ref[...] += (0.5 / N) * jnp.sum(jnp.sum(e * e, axis=-1, keepdims=True), axis=0, keepdims=True)

    row = pl.BlockSpec((tm, N), lambda i: (i, 0))
    return pl.pallas_call(
        body, grid=(S // tm,), in_specs=[row, row],
        out_specs=[row, pl.BlockSpec((1, 1), lambda i: (0, 0))],
        out_shape=[jax.ShapeDtypeStruct((S, N), F32), jax.ShapeDtypeStruct((1, 1), F32)],
        name=name, compiler_params=_cp())(y, t)


def _conv_taps(a, w_ref, T):
    acc = jnp.zeros((T, CONV_W), F32)
    for j in range(CONV_K):
        sh = CONV_K - 1 - j
        r = a if sh == 0 else pltpu.roll(a, sh, 0)
        acc = acc + r[HALO:HALO + T] * w_ref[pl.ds(j, 1), :]
    return acc


def conv_fwd(proj, cw, cb, lg, lb, l, name, T=256):
    S = proj.shape[0]
    nt = S // T

    def body(p_ref, w_ref, cb_ref, lg_ref, lb_ref, y_ref, hc_s):
        hc_s[0:HALO, :] = jnp.zeros((HALO, CONV_W), F32)

        def fill(i, _):
            r0 = pl.multiple_of(i * T, T)
            pc = p_ref[pl.ds(r0, T), :]
            hc_s[pl.ds(r0 + HALO, T), :] = _glu(pc[:, :CONV_W], pc[:, CONV_W:])
            return 0

        lax.fori_loop(0, nt, fill, 0)

        def tile(i, _):
            r0 = pl.multiple_of(i * T, T)
            c = _conv_taps(hc_s[pl.ds(r0, T + HALO), :], w_ref.at[0], T) + cb_ref[0]
            y_ref[pl.ds(r0, T), :] = _ln_silu(c, lg_ref[0], lb_ref[0])
            return 0

        lax.fori_loop(0, nt, tile, 0)

    vec = pl.BlockSpec((1, 1, CONV_W), lambda i: (l, 0, 0))
    return pl.pallas_call(
        body, grid=(1,),
        in_specs=[pl.BlockSpec((S, 2 * CONV_W), lambda i: (0, 0)), pl.BlockSpec((1, 32, CONV_W), lambda i: (l, 0, 0)),
                  vec, vec, vec],
        out_specs=pl.BlockSpec((S, CONV_W), lambda i: (0, 0)),
        out_shape=jax.ShapeDtypeStruct((S, CONV_W), F32),
        scratch_shapes=[pltpu.VMEM((S + HALO, CONV_W), F32)], name=name, compiler_params=_cp())(proj, cw, cb, lg, lb)


def conv_bwd(proj, dy, cw, cb, lg, lb, l, name, T=256):
    S = proj.shape[0]
    nt = S // T

    def body(p_ref, dy_ref, w_ref, cb_ref, lg_ref, lb_ref, dp_ref, dw_ref, dcb_ref, dlg_ref, dlb_ref, hc_s, dc_s):
        hc_s[0:HALO, :] = jnp.zeros((HALO, CONV_W), F32)
        dc_s[S:S + HALO, :] = jnp.zeros((HALO, CONV_W), F32)
        dw_ref[...] = jnp.zeros_like(dw_ref)
        dcb_ref[...] = jnp.zeros_like(dcb_ref)
        dlg_ref[...] = jnp.zeros_like(dlg_ref)
        dlb_ref[...] = jnp.zeros_like(dlb_ref)

        def fill(i, _):
            r0 = pl.multiple_of(i * T, T)
            pc = p_ref[pl.ds(r0, T), :]
            hc_s[pl.ds(r0 + HALO, T), :] = _glu(pc[:, :CONV_W], pc[:, CONV_W:])
            return 0

        lax.fori_loop(0, nt, fill, 0)

        def tile(i, _):
            r0 = pl.multiple_of(i * T, T)
            a = hc_s[pl.ds(r0, T + HALO), :]
            c = _conv_taps(a, w_ref.at[0], T) + cb_ref[0]
            _, vjp = jax.vjp(_ln_silu, c, lg_ref[0], lb_ref[0])
            dc, dlg, dlb = vjp(dy_ref[pl.ds(r0, T), :])
            dc_s[pl.ds(r0, T), :] = dc
            dcb_ref[...] += jnp.sum(dc, axis=0, keepdims=True)
            dlg_ref[...] += dlg
            dlb_ref[...] += dlb
            for j in range(CONV_K):
                sh = CONV_K - 1 - j
                r = a if sh == 0 else pltpu.roll(a, sh, 0)
                dw_ref[pl.ds(j, 1), :] += jnp.sum(dc * r[HALO:HALO + T], axis=0, keepdims=True)
            return 0

        lax.fori_loop(0, nt, tile, 0)

        def back(i, _):
            r0 = pl.multiple_of(i * T, T)
            de = dc_s[pl.ds(r0, T + HALO), :]
            n = T + HALO
            dh = jnp.zeros((T, CONV_W), F32)
            for j in range(CONV_K):
                sh = CONV_K - 1 - j
                r = de if sh == 0 else pltpu.roll(de, n - sh, 0)
                dh = dh + r[0:T] * w_ref[0, pl.ds(j, 1), :]
            pc = p_ref[pl.ds(r0, T), :]
            _, vjp = jax.vjp(_glu, pc[:, :CONV_W], pc[:, CONV_W:])
            dval, dgate = vjp(dh)
            dp_ref[pl.ds(r0, T), :] = jnp.concatenate([dval, dgate], axis=-1).astype(BF16)
            return 0

        lax.fori_loop(0, nt, back, 0)

    vec = pl.BlockSpec((1, 1, CONV_W), lambda i: (l, 0, 0))
    ovec = pl.BlockSpec((1, CONV_W), lambda i: (0, 0))
    vsh = jax.ShapeDtypeStruct((1, CONV_W), F32)
    return pl.pallas_call(
        body, grid=(1,),
        in_specs=[pl.BlockSpec((S, 2 * CONV_W), lambda i: (0, 0)), pl.BlockSpec((S, CONV_W), lambda i: (0, 0)),
                  pl.BlockSpec((1, 32, CONV_W), lambda i: (l, 0, 0)), vec, vec, vec],
        out_specs=[pl.BlockSpec((S, 2 * CONV_W), lambda i: (0, 0)), pl.BlockSpec((32, CONV_W), lambda i: (0, 0)),
                   ovec, ovec, ovec],
        out_shape=[jax.ShapeDtypeStruct((S, 2 * CONV_W), BF16), jax.ShapeDtypeStruct((32, CONV_W), F32), vsh, vsh, vsh],
        scratch_shapes=[pltpu.VMEM((S + HALO, CONV_W), F32), pltpu.VMEM((S + HALO, CONV_W), F32)],
        name=name, compiler_params=_cp())(proj, dy, cw, cb, lg, lb)


def _sg_mix(wt_ref, v, bt):
    slabs = []
    for s in range(2):
        vs = v[:, 128 * s:128 * (s + 1)]
        seg = _iota(vs.shape, 1) < HEAD_DIM
        p0 = _dot(wt_ref[2 * s], vs, "nn") + bt[:, 2 * s:2 * s + 1]
        p1 = _dot(wt_ref[2 * s + 1], vs, "nn") + bt[:, 2 * s + 1:2 * s + 2]
        slabs.append(jnp.where(seg, p0, p1))
    return jnp.concatenate(slabs, axis=-1)


def sg_fwd(proj, lg, lb, wt, bt, l, name):
    S = proj.shape[0]

    def body(p_ref, lg_ref, lb_ref, w_ref, b_ref, y_ref):
        u, v = _sg_pre(p_ref[...], lg_ref[0], lb_ref[0])
        y_ref[...] = u * _sg_mix(w_ref.at[0], v, b_ref[0])

    vec = pl.BlockSpec((1, 1, SG_W), lambda i: (l, 0, 0))
    return pl.pallas_call(
        body, grid=(S // CHUNK,),
        in_specs=[pl.BlockSpec((CHUNK, 2 * SG_W), lambda i: (i, 1)), vec, vec,
                  pl.BlockSpec((1, 4, CHUNK, CHUNK), lambda i: (l, 0, 0, 0)),
                  pl.BlockSpec((1, CHUNK, 4), lambda i: (l, 0, 0))],
        out_specs=pl.BlockSpec((CHUNK, SG_W), lambda i: (i, 0)),
        out_shape=jax.ShapeDtypeStruct((S, SG_W), F32), name=name, compiler_params=_cp())(proj, lg, lb, wt, bt)


def sg_bwd(proj, dy, lg, lb, wt, wtt, bt, l, name):
    S = proj.shape[0]

    def body(p_ref, dy_ref, lg_ref, lb_ref, w_ref, wt_ref, b_ref, dp_ref, dlg_ref, dlb_ref, dw_ref, db_ref):
        @pl.when(pl.program_id(0) == 0)
        def _():
            dlg_ref[...] = jnp.zeros_like(dlg_ref)
            dlb_ref[...] = jnp.zeros_like(dlb_ref)
            dw_ref[...] = jnp.zeros_like(dw_ref)
            db_ref[...] = jnp.zeros_like(db_ref)

        (u, v), vjp = jax.vjp(_sg_pre, p_ref[...], lg_ref[0], lb_ref[0])
        dy = dy_ref[...]
        mixed = _sg_mix(w_ref.at[0], v, b_ref[0])
        du = dy * mixed
        dm = dy * u
        tril = _iota((CHUNK, CHUNK), 1) <= _iota((CHUNK, CHUNK), 0)
        dvs = []
        for s in range(2):
            dms = dm[:, 128 * s:128 * (s + 1)]
            vs = v[:, 128 * s:128 * (s + 1)]
            seg = _iota(dms.shape, 1) < HEAD_DIM
            halves = (jnp.where(seg, dms, 0.0), jnp.where(seg, 0.0, dms))
            dv_h = []
            for e in range(2):
                h = 2 * s + e
                db_ref[:, h:h + 1] += jnp.sum(halves[e], axis=-1, keepdims=True)
                dw_ref[h] += jnp.where(tril, _dot(halves[e], vs, "nt"), 0.0)
                dv_h.append(_dot(wt_ref[0, h], halves[e], "nn"))
            dvs.append(dv_h[0] + dv_h[1])
        dp, dlg, dlb = vjp((du, jnp.concatenate(dvs, axis=-1)))
        dp_ref[...] = dp.astype(BF16)
        dlg_ref[...] += dlg
        dlb_ref[...] += dlb

    vec = pl.BlockSpec((1, 1, SG_W), lambda i: (l, 0, 0))
    wsp = pl.BlockSpec((1, 4, CHUNK, CHUNK), lambda i: (l, 0, 0, 0))
    ovec = pl.BlockSpec((1, SG_W), lambda i: (0, 0))
    return pl.pallas_call(
        body, grid=(S // CHUNK,),
        in_specs=[pl.BlockSpec((CHUNK, 2 * SG_W), lambda i: (i, 1)), pl.BlockSpec((CHUNK, SG_W), lambda i: (i, 0)),
                  vec, vec, wsp, wsp, pl.BlockSpec((1, CHUNK, 4), lambda i: (l, 0, 0))],
        out_specs=[pl.BlockSpec((CHUNK, 2 * SG_W), lambda i: (i, 0)), ovec, ovec,
                   pl.BlockSpec((4, CHUNK, CHUNK), lambda i: (0, 0, 0)), pl.BlockSpec((CHUNK, 4), lambda i: (0, 0))],
        out_shape=[jax.ShapeDtypeStruct((S, 2 * SG_W), BF16), jax.ShapeDtypeStruct((1, SG_W), F32),
                   jax.ShapeDtypeStruct((1, SG_W), F32), jax.ShapeDtypeStruct((4, CHUNK, CHUNK), F32),
                   jax.ShapeDtypeStruct((CHUNK, 4), F32)],
        name=name, compiler_params=_cp())(proj, dy, lg, lb, wt, wtt, bt)


def qkv_fwd(proj, gq, gk, l, name, tm=512):
    S = proj.shape[0]

    def body(pq_ref, pk_ref, pv_ref, gq_ref, gk_ref, o_ref):
        o_ref[...] = _qkv_fn(pq_ref[...], pk_ref[...], pv_ref[...], gq_ref[0], gk_ref[0]).astype(BF16)

    vec = pl.BlockSpec((1, 1, 128), lambda i: (l, 0, 0))
    qb = OFF_SB // SB_W
    return pl.pallas_call(
        body, grid=(S // tm,),
        in_specs=[pl.BlockSpec((tm, SB_W), lambda i: (i, qb)), pl.BlockSpec((tm, SB_W), lambda i: (i, qb + 1)),
                  pl.BlockSpec((tm, SB_W), lambda i: (i, qb + 2)), vec, vec],
        out_specs=pl.BlockSpec((tm, 3 * SB_W), lambda i: (i, 0)),
        out_shape=jax.ShapeDtypeStruct((S, 3 * SB_W), BF16), name=name, compiler_params=_cp())(proj, proj, proj, gq, gk)


def qkv_bwd(proj, dq, dk, dv, gq, gk, l, name, tm=512):
    S = proj.shape[0]

    def body(pq_ref, pk_ref, pv_ref, dq_ref, dk_ref, dv_ref, gq_ref, gk_ref, dp_ref, dgq_ref, dgk_ref):
        _, vjp = jax.vjp(_qkv_fn, pq_ref[...], pk_ref[...], pv_ref[...], gq_ref[0], gk_ref[0])
        dpq, dpk, dpv, dgq, dgk = vjp(jnp.concatenate([dq_ref[...], dk_ref[...], dv_ref[...]], axis=-1))
        dp_ref[...] = jnp.concatenate([dpq, dpk, dpv], axis=-1).astype(BF16)

        @pl.when(pl.program_id(0) == 0)
        def _():
            dgq_ref[...] = jnp.zeros_like(dgq_ref)
            dgk_ref[...] = jnp.zeros_like(dgk_ref)

        dgq_ref[...] += dgq
        dgk_ref[...] += dgk

    vec = pl.BlockSpec((1, 1, 128), lambda i: (l, 0, 0))
    part = pl.BlockSpec((tm, SB_W), lambda i: (i, 0))
    ovec = pl.BlockSpec((1, 128), lambda i: (0, 0))
    qb = OFF_SB // SB_W
    return pl.pallas_call(
        body, grid=(S // tm,),
        in_specs=[pl.BlockSpec((tm, SB_W), lambda i: (i, qb)), pl.BlockSpec((tm, SB_W), lambda i: (i, qb + 1)),
                  pl.BlockSpec((tm, SB_W), lambda i: (i, qb + 2)), part, part, part, vec, vec],
        out_specs=[pl.BlockSpec((tm, 3 * SB_W), lambda i: (i, 0)), ovec, ovec],
        out_shape=[jax.ShapeDtypeStruct((S, 3 * SB_W), BF16), jax.ShapeDtypeStruct((1, 128), F32),
                   jax.ShapeDtypeStruct((1, 128), F32)],
        name=name, compiler_params=_cp())(proj, proj, proj, dq, dk, dv, gq, gk)


QSUB = 4
QT = QSUB * CHUNK


def _sb_consts():
    row = _iota((CHUNK, CHUNK), 0)
    col = _iota((CHUNK, CHUNK), 1)
    ones = jnp.ones((CHUNK, CHUNK), MXU_DT)
    m_gt = jnp.concatenate([(row > col).astype(MXU_DT), ones], axis=1)
    m_lt = jnp.concatenate([(row < col).astype(MXU_DT), ones], axis=1)
    return col < HEAD_DIM, col - row, m_gt, m_lt


def _split_heads(x, seg):
    z = jnp.zeros_like(x)
    return (jnp.where(seg, x, z), jnp.where(seg, z, x))


def attn_fwd(qkv, name):
    S = qkv.shape[0]
    npair = SB_W // 128

    def body(q_ref, k_ref, v_ref, o_ref, rt_ref):
        qi = pl.program_id(1)
        seg, dcol, m_gt, _ = _sb_consts()
        qh = [_split_heads(q_ref[a * CHUNK:(a + 1) * CHUNK, :], seg) for a in range(QSUB)]
        nkb = QSUB * qi + QSUB

        def step(it, carry):
            kb = nkb - 1 - it
            off = pl.multiple_of(kb * CHUNK, CHUNK)
            kblk = k_ref[pl.ds(off, CHUNK), :]
            vblk = v_ref[pl.ds(off, CHUNK), :]
            chains = [(a, h) for a in range(QSUB) for h in range(2)]
            allowed = [dcol < (QSUB * qi + a - kb) * CHUNK for a in range(QSUB)]
            z = [_dot(qh[a][h], kblk, "nt") for a, h in chains]
            sp = [_softplus(zc) for zc in z]
            lnb = [jnp.where(allowed[a], -s, 0.0) for (a, h), s in zip(chains, sp)]
            cs = [_cumdot(x, m_gt) for x in lnb]
            att = [jnp.where(allowed[a], jnp.exp(z[c] - sp[c] + cs[c][:, :CHUNK] + carry[2 * c + 1]), 0.0)
                   for c, (a, h) in enumerate(chains)]
            pv = [_dot(x, vblk, "nn") for x in att]
            new = []
            for c in range(len(chains)):
                new.append(carry[2 * c] + pv[c])
                new.append(carry[2 * c + 1] + cs[c][:, CHUNK:])
            return tuple(new)

        z0 = jnp.zeros((CHUNK, CHUNK), F32)
        res = lax.fori_loop(0, nkb, step, (z0,) * (4 * QSUB))
        for a in range(QSUB):
            rows = slice(a * CHUNK, (a + 1) * CHUNK)
            o_ref[rows, :] = jnp.where(seg, res[4 * a], res[4 * a + 2])
            rt_ref[rows, :] = jnp.concatenate([res[4 * a + 1], res[4 * a + 3]], axis=1)

    return pl.pallas_call(
        body, grid=(npair, S // QT),
        in_specs=[pl.BlockSpec((QT, 128), lambda p, i: (i, p)), pl.BlockSpec((S, 128), lambda p, i: (0, npair + p)),
                  pl.BlockSpec((S, 128), lambda p, i: (0, 2 * npair + p))],
        out_specs=[pl.BlockSpec((QT, 128), lambda p, i: (i, p)), pl.BlockSpec((QT, 256), lambda p, i: (i, p))],
        out_shape=[jax.ShapeDtypeStruct((S, SB_W), F32), jax.ShapeDtypeStruct((S, 2 * SB_W), F32)],
        name=name, compiler_params=_cp())(qkv, qkv, qkv)


def attn_bwd(qkv, rt, do, name):
    S = qkv.shape[0]
    npair = SB_W // 128

    def body(q_ref, k_ref, v_ref, rt_ref, do_ref, dq_ref, dk_ref, dv_ref):
        qi = pl.program_id(1)

        @pl.when(qi == 0)
        def _():
            dk_ref[...] = jnp.zeros_like(dk_ref)
            dv_ref[...] = jnp.zeros_like(dv_ref)

        seg, dcol, m_gt, m_lt = _sb_consts()
        qh, doh, rtot = [], [], []
        for a in range(QSUB):
            rows = slice(a * CHUNK, (a + 1) * CHUNK)
            qh.append(_split_heads(q_ref[rows, :], seg))
            doh.append(_split_heads(do_ref[rows, :], seg))
            rtot.append((rt_ref[rows, 0:CHUNK], rt_ref[rows, CHUNK:2 * CHUNK]))
        nkb = QSUB * qi + QSUB

        def step(kb, carry):
            off = pl.multiple_of(kb * CHUNK, CHUNK)
            kblk = k_ref[pl.ds(off, CHUNK), :]
            vblk = v_ref[pl.ds(off, CHUNK), :]
            kh = _split_heads(kblk, seg)
            chains = [(a, h) for a in range(QSUB) for h in range(2)]
            nc = len(chains)
            allowed = [dcol < (QSUB * qi + a - kb) * CHUNK for a in range(QSUB)]
            z = [_dot(qh[a][h], kblk, "nt") for a, h in chains]
            da = [_dot(doh[a][h], vblk, "nt") for a, h in chains]
            sp = [_softplus(zc) for zc in z]
            lnb = [jnp.where(allowed[a], -s, 0.0) for (a, h), s in zip(chains, sp)]
            cs = [_cumdot(x, m_gt) for x in lnb]
            lc = [carry[3 * c + 1] + cs[c][:, CHUNK:] for c in range(nc)]
            att = [jnp.where(allowed[a], jnp.exp(z[c] - sp[c] + cs[c][:, :CHUNK] + (rtot[a][h] - lc[c])), 0.0)
                   for c, (a, h) in enumerate(chains)]
            g = [da[c] * att[c] for c in range(nc)]
            cg = [_cumdot(x, m_lt) for x in g]
            dz = []
            for c, (a, h) in enumerate(chains):
                sig = jnp.exp(z[c] - sp[c])
                pre = carry[3 * c + 2] + cg[c][:, :CHUNK]
                dz.append(jnp.where(allowed[a], g[c] * (1.0 - sig) - pre * sig, 0.0))
            dqc = [_dot(dz[c], kh[h], "nn") for c, (a, h) in enumerate(chains)]
            dkc = [_dot(dz[c], qh[a][h], "tn") for c, (a, h) in enumerate(chains)]
            dvc = [_dot(att[c], doh[a][h], "tn") for c, (a, h) in enumerate(chains)]
            dk_ref[pl.ds(off, CHUNK), :] += functools.reduce(lambda x, y: x + y, dkc)
            dv_ref[pl.ds(off, CHUNK), :] += functools.reduce(lambda x, y: x + y, dvc)
            new = []
            for c in range(nc):
                new += [carry[3 * c] + dqc[c], lc[c], carry[3 * c + 2] + cg[c][:, CHUNK:]]
            return tuple(new)

        z0 = jnp.zeros((CHUNK, 128), F32)
        res = lax.fori_loop(0, nkb, step, (z0,) * (6 * QSUB))
        for a in range(QSUB):
            dq_ref[a * CHUNK:(a + 1) * CHUNK, :] = res[6 * a] + res[6 * a + 3]

    blk = pl.BlockSpec((QT, 128), lambda p, i: (i, p))
    full = pl.BlockSpec((S, 128), lambda p, i: (0, p))
    sh = jax.ShapeDtypeStruct((S, SB_W), F32)
    return pl.pallas_call(
        body, grid=(npair, S // QT),
        in_specs=[blk, pl.BlockSpec((S, 128), lambda p, i: (0, npair + p)),
                  pl.BlockSpec((S, 128), lambda p, i: (0, 2 * npair + p)),
                  pl.BlockSpec((QT, 256), lambda p, i: (i, p)), blk],
        out_specs=[blk, full, full], out_shape=[sh, sh, sh], name=name, compiler_params=_cp())(qkv, qkv, qkv, rt, do)


def _adamw_math(w, g, m, v):
    m = ADAM_B1 * m + (1.0 - ADAM_B1) * g
    v = ADAM_B2 * v + (1.0 - ADAM_B2) * (g * g)
    m_hat = m / BC1
    v_hat = v / BC2
    delta = -ADAM_LR * (m_hat / (jnp.sqrt(v_hat) + ADAM_EPS) + ADAM_WD * w)
    return delta, m, v


def adamw_layer(w4, m4, v4, g, outs, l, name, tr):
    L, R, C = w4.shape

    def body(w_ref, m_ref, v_ref, g_ref, a0, a1, a2, a3, go_ref, d_ref, mo_ref, vo_ref):
        g = g_ref[...]
        d, m, v = _adamw_math(w_ref[0], g, m_ref[0], v_ref[0])
        go_ref[0] = g
        d_ref[0] = d
        mo_ref[0] = m
        vo_ref[0] = v

    st = pl.BlockSpec((1, tr, C), lambda i: (l, i, 0))
    sh = jax.ShapeDtypeStruct((L, R, C), F32)
    return pl.pallas_call(
        body, grid=(R // tr,),
        in_specs=[st, st, st, pl.BlockSpec((tr, C), lambda i: (i, 0)), ANY, ANY, ANY, ANY],
        out_specs=[st, st, st, st], out_shape=[sh, sh, sh, sh],
        input_output_aliases={4: 0, 5: 1, 6: 2, 7: 3}, name=name, compiler_params=_cp())(w4, m4, v4, g, *outs)


def adamw_flat(w, m, v, g, name):
    R, C = w.shape

    def body(w_ref, m_ref, v_ref, g_ref, d_ref, mo_ref, vo_ref):
        d, m2, v2 = _adamw_math(w_ref[...], g_ref[...], m_ref[...], v_ref[...])
        d_ref[...] = d
        mo_ref[...] = m2
        vo_ref[...] = v2

    full = pl.BlockSpec((R, C), lambda i: (0, 0))
    sh = jax.ShapeDtypeStruct((R, C), F32)
    return pl.pallas_call(body, grid=(1,), in_specs=[full] * 4, out_specs=[full] * 3, out_shape=[sh] * 3,
                          name=name, compiler_params=_cp())(w, m, v, g)


def add_half(g, l1, c, name, tr):
    nk, R, C = g.shape
    Rh = R // 2
    nt = Rh // tr

    def body(c_ref, g_ref, l_ref, o_ref):
        o_ref[...] = (g_ref[...].astype(F32) + l_ref[...].astype(F32)).astype(o_ref.dtype)

    gs = pltpu.PrefetchScalarGridSpec(
        num_scalar_prefetch=1, grid=(nk, nt),
        in_specs=[pl.BlockSpec((1, tr, C), lambda k, t, c_ref: (k, c_ref[0] * nt + t, 0)),
                  pl.BlockSpec((1, tr, C), lambda k, t, c_ref: (k, t, 0))],
        out_specs=pl.BlockSpec((1, tr, C), lambda k, t, c_ref: (k, t, 0)))
    return pl.pallas_call(body, grid_spec=gs, out_shape=jax.ShapeDtypeStruct((nk, Rh, C), g.dtype), name=name,
                          compiler_params=_cp())(c, g, l1)


def sum_chips(p, l2, kme, name, tr):
    nk, Rh, C = p.shape

    def body(k_ref, p_ref, a_ref, b_ref, c_ref, o_ref):
        o_ref[...] = ((p_ref[0].astype(F32) + a_ref[0].astype(F32)) + b_ref[0].astype(F32)) + c_ref[0].astype(F32)

    def slot(j):
        return pl.BlockSpec((1, tr, C), lambda t, k_ref: (j, t, 0))

    gs = pltpu.PrefetchScalarGridSpec(
        num_scalar_prefetch=1, grid=(Rh // tr,),
        in_specs=[pl.BlockSpec((1, tr, C), lambda t, k_ref: (k_ref[0], t, 0)), slot(0), slot(1), slot(2)],
        out_specs=pl.BlockSpec((tr, C), lambda t, k_ref: (t, 0)))
    return pl.pallas_call(body, grid_spec=gs, out_shape=jax.ShapeDtypeStruct((Rh, C), F32), name=name,
                          compiler_params=_cp())(kme, p, l2, l2, l2)


def _place():
    x, y, c = lax.axis_index("x"), lax.axis_index("y"), lax.axis_index("c")
    chips = [(1 - x, y), (x, 1 - y), (1 - x, 1 - y)]
    return x, y, c, chips


def all_gather_chips(arrs, name):
    n = len(arrs)

    def body(*refs):
        ins, outs = refs[:n], refs[n:2 * n]
        send_sems, recv_sems, local_sems = refs[2 * n:]
        x, y, c, chips = _place()
        kme = 2 * x + y
        sib = (x, y, 1 - c)

        def half(ref, i, hc):
            rh = arrs[i].shape[0] // 2
            return ref.at[pl.ds(hc * rh, rh), :]

        def rcopy(i, s, src, dst, to):
            return pltpu.make_async_remote_copy(src_ref=src, dst_ref=dst, send_sem=send_sems.at[6 * i + s],
                                                recv_sem=recv_sems.at[6 * i + s], device_id=to, device_id_type=MESH)

        local = [pltpu.make_async_copy(ins[i], outs[i].at[kme], local_sems.at[i]) for i in range(n)]
        for cp in local:
            cp.start()
        first = [rcopy(i, j, half(ins[i], i, c), half(outs[i].at[kme], i, c), (*chip, c))
                 for i in range(n) for j, chip in enumerate(chips)]
        for cp in first:
            cp.start()
        passed = []
        for i in range(n):
            for j, chip in enumerate(chips):
                kj = 2 * chip[0] + chip[1]
                landed = half(outs[i].at[kj], i, c)
                rcopy(i, j, landed, landed, (*chip, c)).wait_recv()
                fw = rcopy(i, 3 + j, landed, landed, sib)
                fw.start()
                passed.append(fw)
        for i in range(n):
            for j, chip in enumerate(chips):
                kj = 2 * chip[0] + chip[1]
                other = half(outs[i].at[kj], i, 1 - c)
                rcopy(i, 3 + j, other, other, sib).wait_recv()
        for cp in first + passed:
            cp.wait_send()
        for cp in local:
            cp.wait()

    return pl.pallas_call(
        body, in_specs=[ANY] * n, out_specs=[ANY] * n,
        out_shape=[jax.ShapeDtypeStruct((N_CHIP,) + a.shape, a.dtype) for a in arrs],
        scratch_shapes=[pltpu.SemaphoreType.DMA((6 * n,)), pltpu.SemaphoreType.DMA((6 * n,)),
                        pltpu.SemaphoreType.DMA((n,))],
        name=name)(*arrs)


def exchange_sibling_half(gs, name):
    n = len(gs)

    def body(*refs):
        ins, outs = refs[:n], refs[n:2 * n]
        send_sems, recv_sems = refs[2 * n:]
        x, y, c, _ = _place()
        cps = []
        for i in range(n):
            rh = gs[i].shape[1] // 2
            cp = pltpu.make_async_remote_copy(
                src_ref=ins[i].at[:, pl.ds((1 - c) * rh, rh), :], dst_ref=outs[i], send_sem=send_sems.at[i],
                recv_sem=recv_sems.at[i], device_id=(x, y, 1 - c), device_id_type=MESH)
            cp.start()
            cps.append(cp)
        for cp in cps:
            cp.wait()

    return pl.pallas_call(
        body, in_specs=[ANY] * n, out_specs=[ANY] * n,
        out_shape=[jax.ShapeDtypeStruct((g.shape[0], g.shape[1] // 2, g.shape[2]), g.dtype) for g in gs],
        scratch_shapes=[pltpu.SemaphoreType.DMA((n,)), pltpu.SemaphoreType.DMA((n,))],
        name=name)(*gs)


def scatter_to_chips(ps, name):
    n = len(ps)

    def body(*refs):
        ins, outs = refs[:n], refs[n:2 * n]
        send_sems, recv_sems = refs[2 * n:]
        x, y, c, chips = _place()
        cps = []
        for i in range(n):
            for j, chip in enumerate(chips):
                cp = pltpu.make_async_remote_copy(
                    src_ref=ins[i].at[2 * chip[0] + chip[1]], dst_ref=outs[i].at[j], send_sem=send_sems.at[3 * i + j],
                    recv_sem=recv_sems.at[3 * i + j], device_id=(*chip, c), device_id_type=MESH)
                cp.start()
                cps.append(cp)
        for cp in cps:
            cp.wait()

    return pl.pallas_call(
        body, in_specs=[ANY] * n, out_specs=[ANY] * n,
        out_shape=[jax.ShapeDtypeStruct((3,) + p.shape[1:], p.dtype) for p in ps],
        scratch_shapes=[pltpu.SemaphoreType.DMA((3 * n,)), pltpu.SemaphoreType.DMA((3 * n,))],
        name=name)(*ps)


def join_sibling_halves(fs, name):
    n = len(fs)

    def body(*refs):
        ins, outs = refs[:n], refs[n:2 * n]
        send_sems, recv_sems, local_sems = refs[2 * n:]
        x, y, c, _ = _place()
        cps = []
        for i in range(n):
            lc = pltpu.make_async_copy(ins[i], outs[i].at[c], local_sems.at[i])
            lc.start()
            cp = pltpu.make_async_remote_copy(
                src_ref=ins[i], dst_ref=outs[i].at[c], send_sem=send_sems.at[i], recv_sem=recv_sems.at[i],
                device_id=(x, y, 1 - c), device_id_type=MESH)
            cp.start()
            cps.append((lc, cp))
        for i, (lc, cp) in enumerate(cps):
            lc.wait()
            cp.wait_send()
            other = outs[i].at[1 - c]
            pltpu.make_async_remote_copy(src_ref=other, dst_ref=other, send_sem=send_sems.at[i],
                                         recv_sem=recv_sems.at[i], device_id=(x, y, 1 - c),
                                         device_id_type=MESH).wait_recv()

    return pl.pallas_call(
        body, in_specs=[ANY] * n, out_specs=[ANY] * n,
        out_shape=[jax.ShapeDtypeStruct((2,) + f.shape, f.dtype) for f in fs],
        scratch_shapes=[pltpu.SemaphoreType.DMA((n,)), pltpu.SemaphoreType.DMA((n,)), pltpu.SemaphoreType.DMA((n,))],
        name=name)(*fs)


def all_reduce_small(v, name):
    R, C = v.shape

    def body(v_ref, sum_ref, all_ref, send_sems, recv_sems, local_sem):
        x, y, c, chips = _place()
        me, sib = (x, y, c), (x, y, 1 - c)

        def slab(px, py, pc):
            return all_ref.at[4 * px + 2 * py + pc]

        def copy(k, block, to, src=None):
            return pltpu.make_async_remote_copy(
                src_ref=slab(*block) if src is None else src, dst_ref=slab(*block), send_sem=send_sems.at[k],
                recv_sem=recv_sems.at[k], device_id=to, device_id_type=MESH)

        mine = pltpu.make_async_copy(v_ref, slab(*me), local_sem)
        mine.start()
        first = [copy(0, me, sib, src=v_ref)] + [copy(1 + j, me, (*chip, c), src=v_ref) for j, chip in enumerate(chips)]
        for cp in first:
            cp.start()
        passed = [copy(4 + j, (*chip, c), sib) for j, chip in enumerate(chips)]
        for j, chip in enumerate(chips):
            copy(1 + j, (*chip, c), me).wait_recv()
            passed[j].start()
        copy(0, sib, me).wait_recv()
        for j, chip in enumerate(chips):
            copy(4 + j, (*chip, 1 - c), me).wait_recv()
        for cp in first + passed:
            cp.wait_send()
        mine.wait()
        acc = all_ref[0]
        for d in range(1, 8):
            acc = acc + all_ref[d]
        sum_ref[...] = acc

    vm = pl.BlockSpec(memory_space=pltpu.VMEM)
    return pl.pallas_call(
        body, in_specs=[vm], out_specs=[vm, vm],
        out_shape=[jax.ShapeDtypeStruct((R, C), F32), jax.ShapeDtypeStruct((8, R, C), F32)],
        scratch_shapes=[pltpu.SemaphoreType.DMA((7,)), pltpu.SemaphoreType.DMA((7,)), pltpu.SemaphoreType.DMA],
        name=name, compiler_params=_cp())(v)[0]


SMALL = ("mix_norm_g", "conv_w", "conv_b", "conv_ln_g", "conv_ln_b", "sg_ln_g", "sg_ln_b", "sg_w", "sg_b",
         "q_norm_g", "k_norm_g", "out_norm_g", "ffn_norm_g")
BIG = ("w_in", "w_out", "w_gate_up", "w_down")
WEIGHTS = ("mix_norm_g", "w_in", "conv_w", "conv_b", "conv_ln_g", "conv_ln_b", "sg_ln_g", "sg_ln_b", "sg_w", "sg_b",
           "q_norm_g", "k_norm_g", "out_norm_g", "w_out", "ffn_norm_g", "w_gate_up", "w_down")
ADAM_ROWS = {"w_in": 256, "w_out": 128, "w_gate_up": 128, "w_down": 176}
RS_ROWS = {"w_in": 256, "w_out": 128, "w_gate_up": 256, "w_down": 176}


def _pack(parts):
    flat = jnp.concatenate([p.reshape(-1) for p in parts])
    n = flat.shape[0]
    rows = -(-n // (8 * 128)) * 8
    return jnp.pad(flat, (0, rows * 128 - n)).reshape(rows, 128)


def _unpack(buf, shapes):
    flat = buf.reshape(-1)
    out, off = [], 0
    for s in shapes:
        n = 1
        for d in s:
            n *= d
        out.append(flat[off:off + n].reshape(s))
        off += n
    return out


def layer_forward(x, l, P, W):
    sv = {"x": x}
    sv["h"] = rms_fwd(x, P["mix_norm_g"], l, f"rms_mix_{l}")
    sv["proj"] = mm_colblk(sv["h"], W["w_in"], F32, f"mm_in_{l}")
    sv["yc"] = conv_fwd(sv["proj"], P["conv_w"], P["conv_b"], P["conv_ln_g"], P["conv_ln_b"], l, f"conv_fwd_{l}")
    sv["ys"] = sg_fwd(sv["proj"], P["sg_ln_g"], P["sg_ln_b"], P["sg_wt"], P["sg_bt"], l, f"sg_fwd_{l}")
    sv["qkv"] = qkv_fwd(sv["proj"], P["q_norm_g"], P["k_norm_g"], l, f"qkv_fwd_{l}")
    sv["yb"], sv["rt"] = attn_fwd(sv["qkv"], f"attn_fwd_{l}")
    sv["yn"] = outnorm_fwd(sv["yc"], sv["ys"], sv["yb"], P["out_norm_g"], l, f"outnorm_fwd_{l}")
    sv["x1"] = mm_res(sv["yn"], W["w_out"].reshape(D_MODEL, D_MODEL), x, f"mm_out_{l}")
    sv["h2"] = rms_fwd(sv["x1"], P["ffn_norm_g"], l, f"rms_ffn_{l}")
    sv["g"], sv["u"], sv["act"] = ffn_up(sv["h2"], W["w_gate_up"], f"ffn_up_{l}")
    x2 = mm_res(sv["act"], W["w_down"].reshape(FFN, D_MODEL), sv["x1"], f"mm_down_{l}")
    return x2, sv


def layer_backward(dx2, l, P, W, sv):
    gb, gs = {}, {}
    wdown = W["w_down"].reshape(FFN, D_MODEL)
    dgu = ffn_down_bwd(dx2, wdown, sv["g"], sv["u"], f"ffn_down_bwd_{l}")
    gb["w_down"] = mm_wgrad(sv["act"], dx2, 1408, 512, f"wgrad_down_{l}", False).reshape(N_CHIP, FFN // N_CHIP, D_MODEL)
    dh2 = mm_dgrad_colblk(dgu, W["w_gate_up"], f"dgrad_gu_{l}")
    gb["w_gate_up"] = mm_wgrad(sv["h2"], dgu, 512, 1408, f"wgrad_gu_{l}", True)
    dx1, gs["ffn_norm_g"] = rms_bwd(dh2, sv["x1"], P["ffn_norm_g"], l, dx2, f"rms_ffn_bwd_{l}")
    dyn = mm_dgrad(dx1, W["w_out"].reshape(D_MODEL, D_MODEL), f"dgrad_out_{l}")
    gb["w_out"] = mm_wgrad(sv["yn"], dx1, 512, 512, f"wgrad_out_{l}", False).reshape(N_CHIP, D_MODEL // N_CHIP, D_MODEL)
    dyc, dys, dyb, gs["out_norm_g"] = outnorm_bwd(dyn, sv["yc"], sv["ys"], sv["yb"], P["out_norm_g"], l,
                                                  f"outnorm_bwd_{l}")
    dq, dk, dv = attn_bwd(sv["qkv"], sv["rt"], dyb, f"attn_bwd_{l}")
    dpb, dgq, dgk = qkv_bwd(sv["proj"], dq, dk, dv, P["q_norm_g"], P["k_norm_g"], l, f"qkv_bwd_{l}")
    gs["q_norm_g"] = dgq[0, :HEAD_DIM] + dgq[0, HEAD_DIM:]
    gs["k_norm_g"] = dgk[0, :HEAD_DIM] + dgk[0, HEAD_DIM:]
    dps, gs["sg_ln_g"], gs["sg_ln_b"], gs["sg_w"], dbt = sg_bwd(
        sv["proj"], dys, P["sg_ln_g"], P["sg_ln_b"], P["sg_wt"], P["sg_wtt"], P["sg_bt"], l, f"sg_bwd_{l}")
    gs["sg_b"] = dbt.T
    dpc, dcw, gs["conv_b"], gs["conv_ln_g"], gs["conv_ln_b"] = conv_bwd(
        sv["proj"], dyc, P["conv_w"], P["conv_b"], P["conv_ln_g"], P["conv_ln_b"], l, f"conv_bwd_{l}")
    gs["conv_w"] = dcw[:CONV_K]
    dproj = jnp.concatenate([dpc, dps, dpb], axis=-1)
    dh = mm_dgrad_colblk(dproj, W["w_in"], f"dgrad_in_{l}")
    gb["w_in"] = mm_wgrad(sv["h"], dproj, 512, IN_W // N_CHIP, f"wgrad_in_{l}", True)
    dx, gs["mix_norm_g"] = rms_bwd(dh, sv["x"], P["mix_norm_g"], l, dx1, f"rms_mix_bwd_{l}")
    return dx, gb, gs


def reduce_big(gb, l, c1, k1):
    gl = [gb[n] for n in BIG]
    l1 = exchange_sibling_half(gl, f"rs_sibling_{l}")
    ps = [add_half(g, a, c1, f"rs_add_{n}_{l}", RS_ROWS[n]) for n, g, a in zip(BIG, gl, l1)]
    l2 = scatter_to_chips(ps, f"rs_chips_{l}")
    fs = [sum_chips(p, a, k1, f"rs_sum_{n}_{l}", RS_ROWS[n]) for n, p, a in zip(BIG, ps, l2)]
    full = join_sibling_halves(fs, f"rs_join_{l}")
    return {n: f.reshape(2 * f.shape[1], f.shape[2]) for n, f in zip(BIG, full)}


def kernel(x, mix_norm_g, w_in, conv_w, conv_b, conv_ln_g, conv_ln_b, sg_ln_g, sg_ln_b, sg_w, sg_b, q_norm_g, k_norm_g, out_norm_g, w_out, ffn_norm_g, w_gate_up, w_down, loss_target, m_mix_norm_g, m_w_in, m_conv_w, m_conv_b, m_conv_ln_g, m_conv_ln_b, m_sg_ln_g, m_sg_ln_b, m_sg_w, m_sg_b, m_q_norm_g, m_k_norm_g, m_out_norm_g, m_w_out, m_ffn_norm_g, m_w_gate_up, m_w_down, v_mix_norm_g, v_w_in, v_conv_w, v_conv_b, v_conv_ln_g, v_conv_ln_b, v_sg_ln_g, v_sg_ln_b, v_sg_w, v_sg_b, v_q_norm_g, v_k_norm_g, v_out_norm_g, v_w_out, v_ffn_norm_g, v_w_gate_up, v_w_down):
    w = dict(mix_norm_g=mix_norm_g, w_in=w_in, conv_w=conv_w, conv_b=conv_b, conv_ln_g=conv_ln_g, conv_ln_b=conv_ln_b,
             sg_ln_g=sg_ln_g, sg_ln_b=sg_ln_b, sg_w=sg_w, sg_b=sg_b, q_norm_g=q_norm_g, k_norm_g=k_norm_g,
             out_norm_g=out_norm_g, w_out=w_out, ffn_norm_g=ffn_norm_g, w_gate_up=w_gate_up, w_down=w_down)
    m = dict(mix_norm_g=m_mix_norm_g, w_in=m_w_in, conv_w=m_conv_w, conv_b=m_conv_b, conv_ln_g=m_conv_ln_g,
             conv_ln_b=m_conv_ln_b, sg_ln_g=m_sg_ln_g, sg_ln_b=m_sg_ln_b, sg_w=m_sg_w, sg_b=m_sg_b,
             q_norm_g=m_q_norm_g, k_norm_g=m_k_norm_g, out_norm_g=m_out_norm_g, w_out=m_w_out,
             ffn_norm_g=m_ffn_norm_g, w_gate_up=m_w_gate_up, w_down=m_w_down)
    v = dict(mix_norm_g=v_mix_norm_g, w_in=v_w_in, conv_w=v_conv_w, conv_b=v_conv_b, conv_ln_g=v_conv_ln_g,
             conv_ln_b=v_conv_ln_b, sg_ln_g=v_sg_ln_g, sg_ln_b=v_sg_ln_b, sg_w=v_sg_w, sg_b=v_sg_b,
             q_norm_g=v_q_norm_g, k_norm_g=v_k_norm_g, out_norm_g=v_out_norm_g, w_out=v_w_out,
             ffn_norm_g=v_ffn_norm_g, w_gate_up=v_w_gate_up, w_down=v_w_down)
    L = DEPTH
    xi, yi, ci = lax.axis_index("x"), lax.axis_index("y"), lax.axis_index("c")
    kme = 2 * xi + yi
    c1 = ci.astype(jnp.int32).reshape(1)
    k1 = kme.astype(jnp.int32).reshape(1)

    W = [dict(zip(BIG, all_gather_chips([w[n][l].astype(BF16) for n in BIG], f"ag_{l}"))) for l in range(L)]
    cw_sh = jnp.pad(conv_w.reshape(L * CONV_K, CONV_W // N_CHIP), ((0, 128 - L * CONV_K), (0, 0)))
    cw_all = all_gather_chips([cw_sh], "ag_conv")[0][:, :L * CONV_K].reshape(N_CHIP, L, CONV_K, CONV_W // N_CHIP)
    cw_full = jnp.transpose(cw_all, (1, 2, 0, 3)).reshape(L, CONV_K, CONV_W)

    tril = jnp.tril(jnp.ones((CHUNK, CHUNK), bool))
    sg_wt = jnp.where(tril, sg_w, 0.0)
    P = {
        "mix_norm_g": mix_norm_g.reshape(L, 1, D_MODEL), "ffn_norm_g": ffn_norm_g.reshape(L, 1, D_MODEL),
        "out_norm_g": out_norm_g.reshape(L, 1, D_MODEL),
        "conv_w": jnp.pad(cw_full, ((0, 0), (0, 1), (0, 0))), "conv_b": conv_b.reshape(L, 1, CONV_W),
        "conv_ln_g": conv_ln_g.reshape(L, 1, CONV_W), "conv_ln_b": conv_ln_b.reshape(L, 1, CONV_W),
        "sg_ln_g": sg_ln_g.reshape(L, 1, SG_W), "sg_ln_b": sg_ln_b.reshape(L, 1, SG_W),
        "sg_wt": sg_wt.astype(MXU_DT), "sg_wtt": jnp.swapaxes(sg_wt, 2, 3).astype(MXU_DT),
        "sg_bt": jnp.swapaxes(sg_b, 1, 2),
        "q_norm_g": jnp.tile(q_norm_g, (1, 2)).reshape(L, 1, 128), "k_norm_g": jnp.tile(k_norm_g, (1, 2)).reshape(L, 1, 128),
    }

    h = x[0]
    saved = []
    for l in range(L):
        h, sv = layer_forward(h, l, P, W[l])
        saved.append(sv)
    dy, loss_part = loss_head(h, loss_target[0], "loss_head")
    loss = lax.psum(loss_part[0, 0], ("x", "y", "c"))

    outs = {n: [lax.empty(w[n].shape, F32) for _ in range(4)] for n in BIG}
    small_grads = [None] * L
    for l in reversed(range(L)):
        dy, gb, small_grads[l] = layer_backward(dy, l, P, W[l], saved[l])
        red = reduce_big(gb, l, c1, k1)
        for n in BIG:
            outs[n] = adamw_layer(w[n], m[n], v[n], red[n], outs[n], l, f"adamw_{n}_{l}", ADAM_ROWS[n])

    shapes = [(L,) + small_grads[0][n].shape for n in SMALL]
    packed = _pack([jnp.stack([small_grads[l][n] for l in range(L)]) for n in SMALL])
    gsum = dict(zip(SMALL, _unpack(all_reduce_small(packed, "ar_small"), shapes)))
    gsum["conv_w"] = lax.dynamic_slice_in_dim(gsum["conv_w"], kme * (CONV_W // N_CHIP), CONV_W // N_CHIP, axis=2)
    gsum = {n: gsum[n].reshape(w[n].shape) for n in SMALL}
    lshapes = [w[n].shape for n in SMALL]
    d_p, m_p, v_p = adamw_flat(_pack([w[n] for n in SMALL]), _pack([m[n] for n in SMALL]),
                               _pack([v[n] for n in SMALL]), _pack([gsum[n] for n in SMALL]), "adamw_small")
    d_s = dict(zip(SMALL, _unpack(d_p, lshapes)))
    m_s = dict(zip(SMALL, _unpack(m_p, lshapes)))
    v_s = dict(zip(SMALL, _unpack(v_p, lshapes)))

    grads = {n: (outs[n][0] if n in BIG else gsum[n]) for n in WEIGHTS}
    delta = {n: (outs[n][1] if n in BIG else d_s[n]) for n in WEIGHTS}
    new_m = {n: (outs[n][2] if n in BIG else m_s[n]) for n in WEIGHTS}
    new_v = {n: (outs[n][3] if n in BIG else v_s[n]) for n in WEIGHTS}
    return (loss, dy[None], *[grads[n] for n in WEIGHTS], *[delta[n] for n in WEIGHTS],
            *[new_m[n] for n in WEIGHTS], *[new_v[n] for n in WEIGHTS])
```

```python
import functools

import jax
import jax.numpy as jnp
from jax import lax
from jax.experimental import pallas as pl
from jax.experimental.pallas import tpu as pltpu

F32 = jnp.float32
BF16 = jnp.bfloat16
MXU_DT = jnp.bfloat16
GRAD_WIRE_DT = jnp.bfloat16

D_MODEL = 1024
DEPTH = 4
HEAD_DIM = 64
CONV_W = 256
SG_W = 256
SB_W = 512
CONV_K = 31
CHUNK = 128
OFF_SG = 2 * CONV_W
OFF_SB = OFF_SG + 2 * SG_W
IN_W = OFF_SB + 3 * SB_W
FFN = 2816
N_CHIP = 4
RMS_EPS = 1e-6
LN_EPS = 1e-5
ADAM_LR = 0.001
ADAM_B1 = 0.9
ADAM_B2 = 0.999
ADAM_EPS = 1e-08
ADAM_WD = 0.01
ADAM_STEP = 10
BC1 = 1.0 - ADAM_B1 ** ADAM_STEP
BC2 = 1.0 - ADAM_B2 ** ADAM_STEP
HALO = 32
MESH = pl.DeviceIdType.MESH
ANY = pl.BlockSpec(memory_space=pl.ANY)
VMEM_LIMIT = 56 * 1024 * 1024


def _cp(**kw):
    return pltpu.CompilerParams(vmem_limit_bytes=VMEM_LIMIT, **kw)


def _dot(a, b, dims):
    dn = {"nn": (((1,), (0,)), ((), ())), "nt": (((1,), (1,)), ((), ())), "tn": (((0,), (0,)), ((), ()))}[dims]
    return lax.dot_general(a.astype(MXU_DT), b.astype(MXU_DT), dn, preferred_element_type=F32)


def _cumdot(x, m):
    hi = x.astype(MXU_DT)
    lo = (x - hi.astype(F32)).astype(MXU_DT)
    dn = (((1,), (0,)), ((), ()))
    return (lax.dot_general(hi, m, dn, preferred_element_type=F32)
            + lax.dot_general(lo, m, dn, preferred_element_type=F32))


def _iota(shape, axis):
    return lax.broadcasted_iota(jnp.int32, shape, axis)


def _rms(x, g):
    return x * lax.rsqrt(jnp.mean(x * x, axis=-1, keepdims=True) + RMS_EPS) * g


def _ln(x, g, b):
    mu = jnp.mean(x, axis=-1, keepdims=True)
    xc = x - mu
    var = jnp.mean(xc * xc, axis=-1, keepdims=True)
    return xc * lax.rsqrt(var + LN_EPS) * g + b


def _glu(val, gate):
    return val * jax.nn.sigmoid(gate)


def _ln_silu(c, g, b):
    return jax.nn.silu(_ln(c, g, b))


_ERF_ALPHA = (-2.72614225801306e-10, 2.77068142495902e-08, -2.10102402082508e-06, -5.69250639462346e-05,
              -7.34990630326855e-04, -2.95459980854025e-03, -1.60960333262415e-02)
_ERF_BETA = (-1.45660718464996e-05, -2.13374055278905e-04, -1.68282697438203e-03, -7.37332916720468e-03,
             -1.42647390514189e-02)


def _erf(x):
    x = jnp.clip(x, -4.0, 4.0)
    x2 = x * x
    p = jnp.full_like(x, _ERF_ALPHA[0])
    for a in _ERF_ALPHA[1:]:
        p = p * x2 + a
    q = jnp.full_like(x, _ERF_BETA[0])
    for b in _ERF_BETA[1:]:
        q = q * x2 + b
    return x * p / q


@jax.custom_jvp
def _gelu(x):
    return 0.5 * x * (1.0 + _erf(x * (2.0 ** -0.5)))


@_gelu.defjvp
def _gelu_jvp(primals, tangents):
    (x,), (t,) = primals, tangents
    cdf = 0.5 * (1.0 + _erf(x * (2.0 ** -0.5)))
    pdf = jnp.exp(-0.5 * x * x) * ((2.0 * jnp.pi) ** -0.5)
    return x * cdf, t * (cdf + x * pdf)


def _sg_pre(uvp, g, b):
    uv = _gelu(uvp)
    return uv[:, :SG_W], _ln(uv[:, SG_W:], g, b)


def _outnorm(yc, ys, yb, g):
    return jnp.concatenate([_rms(yc, g[:, :CONV_W]), _rms(ys, g[:, CONV_W:CONV_W + SG_W]),
                            _rms(yb, g[:, CONV_W + SG_W:])], axis=-1)


def _qkv_fn(pq, pk, pv, gq, gk):
    outs = []
    for p, g, sc in ((pq, gq, HEAD_DIM ** -0.5), (pk, gk, 1.0)):
        for s in range(SB_W // 128):
            x = p[:, 128 * s:128 * (s + 1)]
            seg = _iota(x.shape, 1) < HEAD_DIM
            x2 = x * x
            s0 = jnp.sum(jnp.where(seg, x2, 0.0), axis=-1, keepdims=True)
            s1 = jnp.sum(jnp.where(seg, 0.0, x2), axis=-1, keepdims=True)
            ms = jnp.where(seg, s0, s1) * (1.0 / HEAD_DIM)
            outs.append(x * lax.rsqrt(ms + RMS_EPS) * (g * sc))
    outs.append(pv)
    return jnp.concatenate(outs, axis=-1)


def _swiglu(g, u):
    return jax.nn.silu(g) * u


def _softplus(z):
    return jnp.maximum(z, 0.0) + jnp.log(1.0 + jnp.exp(-jnp.abs(z)))


def mm_colblk(a, wb, out_dtype, name, tm=512):
    S, K = a.shape
    nb, _, C = wb.shape

    def body(a_ref, w_ref, o_ref):
        o_ref[...] = _dot(a_ref[...], w_ref[0], "nn").astype(o_ref.dtype)

    return pl.pallas_call(
        body, grid=(nb, S // tm),
        in_specs=[pl.BlockSpec((tm, K), lambda k, i: (i, 0)), pl.BlockSpec((1, K, C), lambda k, i: (k, 0, 0))],
        out_specs=pl.BlockSpec((tm, C), lambda k, i: (i, k)),
        out_shape=jax.ShapeDtypeStruct((S, nb * C), out_dtype), name=name, compiler_params=_cp())(a, wb)


def mm_res(a, w, res, name, tm=256):
    S, K = a.shape
    N = w.shape[1]

    def body(a_ref, w_ref, r_ref, o_ref):
        o_ref[...] = r_ref[...] + _dot(a_ref[...], w_ref[...], "nn")

    return pl.pallas_call(
        body, grid=(S // tm,),
        in_specs=[pl.BlockSpec((tm, K), lambda i: (i, 0)), pl.BlockSpec((K, N), lambda i: (0, 0)),
                  pl.BlockSpec((tm, N), lambda i: (i, 0))],
        out_specs=pl.BlockSpec((tm, N), lambda i: (i, 0)),
        out_shape=jax.ShapeDtypeStruct((S, N), F32), name=name, compiler_params=_cp())(a, w, res)


def ffn_up(h2, wgu, name, tm=256):
    S, K = h2.shape
    C = wgu.shape[2]

    def body(h_ref, wg_ref, wu_ref, g_ref, u_ref, a_ref):
        h = h_ref[...]
        g = _dot(h, wg_ref[0], "nn")
        u = _dot(h, wu_ref[0], "nn")
        g_ref[...] = g.astype(BF16)
        u_ref[...] = u.astype(BF16)
        a_ref[...] = _swiglu(g, u).astype(BF16)

    o = pl.BlockSpec((tm, C), lambda j, i: (i, j))
    sh = jax.ShapeDtypeStruct((S, 2 * C), BF16)
    return pl.pallas_call(
        body, grid=(2, S // tm),
        in_specs=[pl.BlockSpec((tm, K), lambda j, i: (i, 0)), pl.BlockSpec((1, K, C), lambda j, i: (j, 0, 0)),
                  pl.BlockSpec((1, K, C), lambda j, i: (2 + j, 0, 0))],
        out_specs=[o, o, o], out_shape=[sh, sh, sh], name=name, compiler_params=_cp())(h2, wgu, wgu)


def ffn_down_bwd(dx2, wdown, g, u, name, tm=256):
    S, N = dx2.shape
    C = FFN // 2

    def body(d_ref, w_ref, g_ref, u_ref, o_ref):
        d = d_ref[...]
        for j in range(2):
            cols = slice(j * C, (j + 1) * C)
            dact = _dot(d, w_ref[cols, :], "nt")
            _, vjp = jax.vjp(_swiglu, g_ref[:, cols].astype(F32), u_ref[:, cols].astype(F32))
            dg, du = vjp(dact)
            o_ref[:, cols] = dg.astype(BF16)
            o_ref[:, FFN + j * C:FFN + (j + 1) * C] = du.astype(BF16)

    row = pl.BlockSpec((tm, FFN), lambda i: (i, 0))
    return pl.pallas_call(
        body, grid=(S // tm,),
        in_specs=[pl.BlockSpec((tm, N), lambda i: (i, 0)), pl.BlockSpec((FFN, N), lambda i: (0, 0)), row, row],
        out_specs=pl.BlockSpec((tm, 2 * FFN), lambda i: (i, 0)),
        out_shape=jax.ShapeDtypeStruct((S, 2 * FFN), BF16), name=name, compiler_params=_cp())(dx2, wdown, g, u)


def mm_dgrad_colblk(do, wb, name, tm=512):
    S = do.shape[0]
    nb, K, C = wb.shape

    def body(d_ref, w_ref, o_ref):
        k = pl.program_id(1)
        r = _dot(d_ref[...], w_ref[0], "nt")

        @pl.when(k == 0)
        def _():
            o_ref[...] = r

        @pl.when(k != 0)
        def _():
            o_ref[...] += r

    return pl.pallas_call(
        body, grid=(S // tm, nb),
        in_specs=[pl.BlockSpec((tm, C), lambda i, k: (i, k)), pl.BlockSpec((1, K, C), lambda i, k: (k, 0, 0))],
        out_specs=pl.BlockSpec((tm, K), lambda i, k: (i, 0)),
        out_shape=jax.ShapeDtypeStruct((S, K), F32), name=name, compiler_params=_cp())(do, wb)


def mm_dgrad(do, w, name, tm=256):
    S, N = do.shape
    K = w.shape[0]

    def body(d_ref, w_ref, o_ref):
        o_ref[...] = _dot(d_ref[...], w_ref[...], "nt")

    return pl.pallas_call(
        body, grid=(S // tm,),
        in_specs=[pl.BlockSpec((tm, N), lambda i: (i, 0)), pl.BlockSpec((K, N), lambda i: (0, 0))],
        out_specs=pl.BlockSpec((tm, K), lambda i: (i, 0)),
        out_shape=jax.ShapeDtypeStruct((S, K), F32), name=name, compiler_params=_cp())(do, w)


def mm_wgrad(a, do, tk, tn, name, blocked):
    S, K = a.shape
    N = do.shape[1]

    def body(a_ref, d_ref, o_ref):
        r = _dot(a_ref[...], d_ref[...], "tn").astype(GRAD_WIRE_DT)
        if blocked:
            o_ref[0] = r
        else:
            o_ref[...] = r

    if blocked:
        out_spec = pl.BlockSpec((1, tk, tn), lambda n, j: (n, j, 0))
        out_shape = jax.ShapeDtypeStruct((N // tn, K, tn), GRAD_WIRE_DT)
    else:
        out_spec = pl.BlockSpec((tk, tn), lambda n, j: (j, n))
        out_shape = jax.ShapeDtypeStruct((K, N), GRAD_WIRE_DT)
    return pl.pallas_call(
        body, grid=(N // tn, K // tk),
        in_specs=[pl.BlockSpec((S, tk), lambda n, j: (0, j)), pl.BlockSpec((S, tn), lambda n, j: (0, n))],
        out_specs=out_spec, out_shape=out_shape, name=name, compiler_params=_cp())(a, do)


def rms_fwd(x, g3, l, name, tm=512):
    S, N = x.shape

    def body(x_ref, g_ref, o_ref):
        o_ref[...] = _rms(x_ref[...], g_ref[0]).astype(BF16)

    return pl.pallas_call(
        body, grid=(S // tm,),
        in_specs=[pl.BlockSpec((tm, N), lambda i: (i, 0)), pl.BlockSpec((1, 1, N), lambda i: (l, 0, 0))],
        out_specs=pl.BlockSpec((tm, N), lambda i: (i, 0)),
        out_shape=jax.ShapeDtypeStruct((S, N), BF16), name=name, compiler_params=_cp())(x, g3)


def rms_bwd(dh, x, g3, l, dres, name, tm=512):
    S, N = x.shape

    def body(dh_ref, x_ref, g_ref, r_ref, dx_ref, dg_ref):
        _, vjp = jax.vjp(_rms, x_ref[...], g_ref[0])
        dx, dg = vjp(dh_ref[...])
        dx_ref[...] = r_ref[...] + dx

        @pl.when(pl.program_id(0) == 0)
        def _():
            dg_ref[...] = jnp.zeros_like(dg_ref)

        dg_ref[...] += dg

    row = pl.BlockSpec((tm, N), lambda i: (i, 0))
    return pl.pallas_call(
        body, grid=(S // tm,),
        in_specs=[row, row, pl.BlockSpec((1, 1, N), lambda i: (l, 0, 0)), row],
        out_specs=[row, pl.BlockSpec((1, N), lambda i: (0, 0))],
        out_shape=[jax.ShapeDtypeStruct((S, N), F32), jax.ShapeDtypeStruct((1, N), F32)],
        name=name, compiler_params=_cp())(dh, x, g3, dres)


def outnorm_fwd(yc, ys, yb, g3, l, name, tm=512):
    S = yc.shape[0]

    def body(c_ref, s_ref, b_ref, g_ref, o_ref):
        o_ref[...] = _outnorm(c_ref[...], s_ref[...], b_ref[...], g_ref[0]).astype(BF16)

    return pl.pallas_call(
        body, grid=(S // tm,),
        in_specs=[pl.BlockSpec((tm, CONV_W), lambda i: (i, 0)), pl.BlockSpec((tm, SG_W), lambda i: (i, 0)),
                  pl.BlockSpec((tm, SB_W), lambda i: (i, 0)), pl.BlockSpec((1, 1, D_MODEL), lambda i: (l, 0, 0))],
        out_specs=pl.BlockSpec((tm, D_MODEL), lambda i: (i, 0)),
        out_shape=jax.ShapeDtypeStruct((S, D_MODEL), BF16), name=name, compiler_params=_cp())(yc, ys, yb, g3)


def outnorm_bwd(dyn, yc, ys, yb, g3, l, name, tm=512):
    S = yc.shape[0]

    def body(d_ref, c_ref, s_ref, b_ref, g_ref, dc_ref, ds_ref, db_ref, dg_ref):
        _, vjp = jax.vjp(_outnorm, c_ref[...], s_ref[...], b_ref[...], g_ref[0])
        dc, ds, db, dg = vjp(d_ref[...])
        dc_ref[...] = dc
        ds_ref[...] = ds
        db_ref[...] = db

        @pl.when(pl.program_id(0) == 0)
        def _():
            dg_ref[...] = jnp.zeros_like(dg_ref)

        dg_ref[...] += dg

    sc = pl.BlockSpec((tm, CONV_W), lambda i: (i, 0))
    ss = pl.BlockSpec((tm, SG_W), lambda i: (i, 0))
    sb = pl.BlockSpec((tm, SB_W), lambda i: (i, 0))
    return pl.pallas_call(
        body, grid=(S // tm,),
        in_specs=[pl.BlockSpec((tm, D_MODEL), lambda i: (i, 0)), sc, ss, sb,
                  pl.BlockSpec((1, 1, D_MODEL), lambda i: (l, 0, 0))],
        out_specs=[sc, ss, sb, pl.BlockSpec((1, D_MODEL), lambda i: (0, 0))],
        out_shape=[jax.ShapeDtypeStruct((S, CONV_W), F32), jax.ShapeDtypeStruct((S, SG_W), F32),
                   jax.ShapeDtypeStruct((S, SB_W), F32), jax.ShapeDtypeStruct((1, D_MODEL), F32)],
        name=name, compiler_params=_cp())(dyn, yc, ys, yb, g3)


def loss_head(y, t, name, tm=512):
    S, N = y.shape

    def body(y_ref, t_ref, dy_ref, l_ref):
        e = y_ref[...] - t_ref[...]
        dy_ref[...] = e * (1.0 / N)

        @pl.when(pl.program_id(0) == 0)
        def _():
            l_ref[...] = jnp.zeros_like(l_ref)

        l_ref[...] += (0.5 / N) * jnp.sum(jnp.sum(e * e, axis=-1, keepdims=True), axis=0, keepdims=True)

    row = pl.BlockSpec((tm, N), lambda i: (i, 0))
    return pl.pallas_call(
        body, grid=(S // tm,), in_specs=[row, row],
        out_specs=[row, pl.BlockSpec((1, 1), lambda i: (0, 0))],
        out_shape=[jax.ShapeDtypeStruct((S, N), F32), jax.ShapeDtypeStruct((1, 1), F32)],
        name=name, compiler_params=_cp())(y, t)


def _conv_taps(a, w_ref, T):
    acc = jnp.zeros((T, CONV_W), F32)
    for j in range(CONV_K):
        sh = CONV_K - 1 - j
        r = a if sh == 0 else pltpu.roll(a, sh, 0)
        acc = acc + r[HALO:HALO + T] * w_ref[pl.ds(j, 1), :]
    return acc


def conv_fwd(proj, cw, cb, lg, lb, l, name, T=256):
    S = proj.shape[0]
    nt = S // T

    def body(p_ref, w_ref, cb_ref, lg_ref, lb_ref, y_ref, hc_s):
        hc_s[0:HALO, :] = jnp.zeros((HALO, CONV_W), F32)

        def fill(i, _):
            r0 = pl.multiple_of(i * T, T)
            pc = p_ref[pl.ds(r0, T), :]
            hc_s[pl.ds(r0 + HALO, T), :] = _glu(pc[:, :CONV_W], pc[:, CONV_W:])
            return 0

        lax.fori_loop(0, nt, fill, 0)

        def tile(i, _):
            r0 = pl.multiple_of(i * T, T)
            c = _conv_taps(hc_s[pl.ds(r0, T + HALO), :], w_ref.at[0], T) + cb_ref[0]
            y_ref[pl.ds(r0, T), :] = _ln_silu(c, lg_ref[0], lb_ref[0])
            return 0

        lax.fori_loop(0, nt, tile, 0)

    vec = pl.BlockSpec((1, 1, CONV_W), lambda i: (l, 0, 0))
    return pl.pallas_call(
        body, grid=(1,),
        in_specs=[pl.BlockSpec((S, 2 * CONV_W), lambda i: (0, 0)), pl.BlockSpec((1, 32, CONV_W), lambda i: (l, 0, 0)),
                  vec, vec, vec],
        out_specs=pl.BlockSpec((S, CONV_W), lambda i: (0, 0)),
        out_shape=jax.ShapeDtypeStruct((S, CONV_W), F32),
        scratch_shapes=[pltpu.VMEM((S + HALO, CONV_W), F32)], name=name, compiler_params=_cp())(proj, cw, cb, lg, lb)


def conv_bwd(proj, dy, cw, cb, lg, lb, l, name, T=256):
    S = proj.shape[0]
    nt = S // T

    def body(p_ref, dy_ref, w_ref, cb_ref, lg_ref, lb_ref, dp_ref, dw_ref, dcb_ref, dlg_ref, dlb_ref, hc_s, dc_s):
        hc_s[0:HALO, :] = jnp.zeros((HALO, CONV_W), F32)
        dc_s[S:S + HALO, :] = jnp.zeros((HALO, CONV_W), F32)
        dw_ref[...] = jnp.zeros_like(dw_ref)
        dcb_ref[...] = jnp.zeros_like(dcb_ref)
        dlg_ref[...] = jnp.zeros_like(dlg_ref)
        dlb_ref[...] = jnp.zeros_like(dlb_ref)

        def fill(i, _):
            r0 = pl.multiple_of(i * T, T)
            pc = p_ref[pl.ds(r0, T), :]
            hc_s[pl.ds(r0 + HALO, T), :] = _glu(pc[:, :CONV_W], pc[:, CONV_W:])
            return 0

        lax.fori_loop(0, nt, fill, 0)

        def tile(i, _):
            r0 = pl.multiple_of(i * T, T)
            a = hc_s[pl.ds(r0, T + HALO), :]
            c = _conv_taps(a, w_ref.at[0], T) + cb_ref[0]
            _, vjp = jax.vjp(_ln_silu, c, lg_ref[0], lb_ref[0])
            dc, dlg, dlb = vjp(dy_ref[pl.ds(r0, T), :])
            dc_s[pl.ds(r0, T), :] = dc
            dcb_ref[...] += jnp.sum(dc, axis=0, keepdims=True)
            dlg_ref[...] += dlg
            dlb_ref[...] += dlb
            for j in range(CONV_K):
                sh = CONV_K - 1 - j
                r = a if sh == 0 else pltpu.roll(a, sh, 0)
                dw_ref[pl.ds(j, 1), :] += jnp.sum(dc * r[HALO:HALO + T], axis=0, keepdims=True)
            return 0

        lax.fori_loop(0, nt, tile, 0)

        def back(i, _):
            r0 = pl.multiple_of(i * T, T)
            de = dc_s[pl.ds(r0, T + HALO), :]
            n = T + HALO
            dh = jnp.zeros((T, CONV_W), F32)
            for j in range(CONV_K):
                sh = CONV_K - 1 - j
                r = de if sh == 0 else pltpu.roll(de, n - sh, 0)
                dh = dh + r[0:T] * w_ref[0, pl.ds(j, 1), :]
            pc = p_ref[pl.ds(r0, T), :]
            _, vjp = jax.vjp(_glu, pc[:, :CONV_W], pc[:, CONV_W:])
            dval, dgate = vjp(dh)
            dp_ref[pl.ds(r0, T), :] = jnp.concatenate([dval, dgate], axis=-1).astype(BF16)
            return 0

        lax.fori_loop(0, nt, back, 0)

    vec = pl.BlockSpec((1, 1, CONV_W), lambda i: (l, 0, 0))
    ovec = pl.BlockSpec((1, CONV_W), lambda i: (0, 0))
    vsh = jax.ShapeDtypeStruct((1, CONV_W), F32)
    return pl.pallas_call(
        body, grid=(1,),
        in_specs=[pl.BlockSpec((S, 2 * CONV_W), lambda i: (0, 0)), pl.BlockSpec((S, CONV_W), lambda i: (0, 0)),
                  pl.BlockSpec((1, 32, CONV_W), lambda i: (l, 0, 0)), vec, vec, vec],
        out_specs=[pl.BlockSpec((S, 2 * CONV_W), lambda i: (0, 0)), pl.BlockSpec((32, CONV_W), lambda i: (0, 0)),
                   ovec, ovec, ovec],
        out_shape=[jax.ShapeDtypeStruct((S, 2 * CONV_W), BF16), jax.ShapeDtypeStruct((32, CONV_W), F32), vsh, vsh, vsh],
        scratch_shapes=[pltpu.VMEM((S + HALO, CONV_W), F32), pltpu.VMEM((S + HALO, CONV_W), F32)],
        name=name, compiler_params=_cp())(proj, dy, cw, cb, lg, lb)


def _sg_mix(wt_ref, v, bt):
    slabs = []
    for s in range(2):
        vs = v[:, 128 * s:128 * (s + 1)]
        seg = _iota(vs.shape, 1) < HEAD_DIM
        p0 = _dot(wt_ref[2 * s], vs, "nn") + bt[:, 2 * s:2 * s + 1]
        p1 = _dot(wt_ref[2 * s + 1], vs, "nn") + bt[:, 2 * s + 1:2 * s + 2]
        slabs.append(jnp.where(seg, p0, p1))
    return jnp.concatenate(slabs, axis=-1)


def sg_fwd(proj, lg, lb, wt, bt, l, name):
    S = proj.shape[0]

    def body(p_ref, lg_ref, lb_ref, w_ref, b_ref, y_ref):
        u, v = _sg_pre(p_ref[...], lg_ref[0], lb_ref[0])
        y_ref[...] = u * _sg_mix(w_ref.at[0], v, b_ref[0])

    vec = pl.BlockSpec((1, 1, SG_W), lambda i: (l, 0, 0))
    return pl.pallas_call(
        body, grid=(S // CHUNK,),
        in_specs=[pl.BlockSpec((CHUNK, 2 * SG_W), lambda i: (i, 1)), vec, vec,
                  pl.BlockSpec((1, 4, CHUNK, CHUNK), lambda i: (l, 0, 0, 0)),
                  pl.BlockSpec((1, CHUNK, 4), lambda i: (l, 0, 0))],
        out_specs=pl.BlockSpec((CHUNK, SG_W), lambda i: (i, 0)),
        out_shape=jax.ShapeDtypeStruct((S, SG_W), F32), name=name, compiler_params=_cp())(proj, lg, lb, wt, bt)


def sg_bwd(proj, dy, lg, lb, wt, wtt, bt, l, name):
    S = proj.shape[0]

    def body(p_ref, dy_ref, lg_ref, lb_ref, w_ref, wt_ref, b_ref, dp_ref, dlg_ref, dlb_ref, dw_ref, db_ref):
        @pl.when(pl.program_id(0) == 0)
        def _():
            dlg_ref[...] = jnp.zeros_like(dlg_ref)
            dlb_ref[...] = jnp.zeros_like(dlb_ref)
            dw_ref[...] = jnp.zeros_like(dw_ref)
            db_ref[...] = jnp.zeros_like(db_ref)

        (u, v), vjp = jax.vjp(_sg_pre, p_ref[...], lg_ref[0], lb_ref[0])
        dy = dy_ref[...]
        mixed = _sg_mix(w_ref.at[0], v, b_ref[0])
        du = dy * mixed
        dm = dy * u
        tril = _iota((CHUNK, CHUNK), 1) <= _iota((CHUNK, CHUNK), 0)
        dvs = []
        for s in range(2):
            dms = dm[:, 128 * s:128 * (s + 1)]
            vs = v[:, 128 * s:128 * (s + 1)]
            seg = _iota(dms.shape, 1) < HEAD_DIM
            halves = (jnp.where(seg, dms, 0.0), jnp.where(seg, 0.0, dms))
            dv_h = []
            for e in range(2):
                h = 2 * s + e
                db_ref[:, h:h + 1] += jnp.sum(halves[e], axis=-1, keepdims=True)
                dw_ref[h] += jnp.where(tril, _dot(halves[e], vs, "nt"), 0.0)
                dv_h.append(_dot(wt_ref[0, h], halves[e], "nn"))
            dvs.append(dv_h[0] + dv_h[1])
        dp, dlg, dlb = vjp((du, jnp.concatenate(dvs, axis=-1)))
        dp_ref[...] = dp.astype(BF16)
        dlg_ref[...] += dlg
        dlb_ref[...] += dlb

    vec = pl.BlockSpec((1, 1, SG_W), lambda i: (l, 0, 0))
    wsp = pl.BlockSpec((1, 4, CHUNK, CHUNK), lambda i: (l, 0, 0, 0))
    ovec = pl.BlockSpec((1, SG_W), lambda i: (0, 0))
    return pl.pallas_call(
        body, grid=(S // CHUNK,),
        in_specs=[pl.BlockSpec((CHUNK, 2 * SG_W), lambda i: (i, 1)), pl.BlockSpec((CHUNK, SG_W), lambda i: (i, 0)),
                  vec, vec, wsp, wsp, pl.BlockSpec((1, CHUNK, 4), lambda i: (l, 0, 0))],
        out_specs=[pl.BlockSpec((CHUNK, 2 * SG_W), lambda i: (i, 0)), ovec, ovec,
                   pl.BlockSpec((4, CHUNK, CHUNK), lambda i: (0, 0, 0)), pl.BlockSpec((CHUNK, 4), lambda i: (0, 0))],
        out_shape=[jax.ShapeDtypeStruct((S, 2 * SG_W), BF16), jax.ShapeDtypeStruct((1, SG_W), F32),
                   jax.ShapeDtypeStruct((1, SG_W), F32), jax.ShapeDtypeStruct((4, CHUNK, CHUNK), F32),
                   jax.ShapeDtypeStruct((CHUNK, 4), F32)],
        name=name, compiler_params=_cp())(proj, dy, lg, lb, wt, wtt, bt)


def qkv_fwd(proj, gq, gk, l, name, tm=512):
    S = proj.shape[0]

    def body(pq_ref, pk_ref, pv_ref, gq_ref, gk_ref, o_ref):
        o_ref[...] = _qkv_fn(pq_ref[...], pk_ref[...], pv_ref[...], gq_ref[0], gk_ref[0]).astype(BF16)

    vec = pl.BlockSpec((1, 1, 128), lambda i: (l, 0, 0))
    qb = OFF_SB // SB_W
    return pl.pallas_call(
        body, grid=(S // tm,),
        in_specs=[pl.BlockSpec((tm, SB_W), lambda i: (i, qb)), pl.BlockSpec((tm, SB_W), lambda i: (i, qb + 1)),
                  pl.BlockSpec((tm, SB_W), lambda i: (i, qb + 2)), vec, vec],
        out_specs=pl.BlockSpec((tm, 3 * SB_W), lambda i: (i, 0)),
        out_shape=jax.ShapeDtypeStruct((S, 3 * SB_W), BF16), name=name, compiler_params=_cp())(proj, proj, proj, gq, gk)


def qkv_bwd(proj, dq, dk, dv, gq, gk, l, name, tm=512):
    S = proj.shape[0]

    def body(pq_ref, pk_ref, pv_ref, dq_ref, dk_ref, dv_ref, gq_ref, gk_ref, dp_ref, dgq_ref, dgk_ref):
        _, vjp = jax.vjp(_qkv_fn, pq_ref[...], pk_ref[...], pv_ref[...], gq_ref[0], gk_ref[0])
        dpq, dpk, dpv, dgq, dgk = vjp(jnp.concatenate([dq_ref[...], dk_ref[...], dv_ref[...]], axis=-1))
        dp_ref[...] = jnp.concatenate([dpq, dpk, dpv], axis=-1).astype(BF16)

        @pl.when(pl.program_id(0) == 0)
        def _():
            dgq_ref[...] = jnp.zeros_like(dgq_ref)
            dgk_ref[...] = jnp.zeros_like(dgk_ref)

        dgq_ref[...] += dgq
        dgk_ref[...] += dgk

    vec = pl.BlockSpec((1, 1, 128), lambda i: (l, 0, 0))
    part = pl.BlockSpec((tm, SB_W), lambda i: (i, 0))
    ovec = pl.BlockSpec((1, 128), lambda i: (0, 0))
    qb = OFF_SB // SB_W
    return pl.pallas_call(
        body, grid=(S // tm,),
        in_specs=[pl.BlockSpec((tm, SB_W), lambda i: (i, qb)), pl.BlockSpec((tm, SB_W), lambda i: (i, qb + 1)),
                  pl.BlockSpec((tm, SB_W), lambda i: (i, qb + 2)), part, part, part, vec, vec],
        out_specs=[pl.BlockSpec((tm, 3 * SB_W), lambda i: (i, 0)), ovec, ovec],
        out_shape=[jax.ShapeDtypeStruct((S, 3 * SB_W), BF16), jax.ShapeDtypeStruct((1, 128), F32),
                   jax.ShapeDtypeStruct((1, 128), F32)],
        name=name, compiler_params=_cp())(proj, proj, proj, dq, dk, dv, gq, gk)


QSUB = 4
QT = QSUB * CHUNK


def _sb_consts():
    row = _iota((CHUNK, CHUNK), 0)
    col = _iota((CHUNK, CHUNK), 1)
    ones = jnp.ones((CHUNK, CHUNK), MXU_DT)
    m_gt = jnp.concatenate([(row > col).astype(MXU_DT), ones], axis=1)
    m_lt = jnp.concatenate([(row < col).astype(MXU_DT), ones], axis=1)
    return col < HEAD_DIM, col - row, m_gt, m_lt


def _split_heads(x, seg):
    z = jnp.zeros_like(x)
    return (jnp.where(seg, x, z), jnp.where(seg, z, x))


class Exchange:
    def __init__(self, ins, bufs, outs, n, build):
        self.ins, self.bufs, self.outs, self.n, self.build = list(ins), list(bufs), list(outs), n, build

    def split(self, refs, n_in, n_out):
        a, b, o = len(self.ins), len(self.bufs), len(self.outs)
        main_in = refs[:n_in]
        c_in = refs[n_in:n_in + a]
        rest = refs[n_in + a + b:]
        main_out = rest[:n_out]
        c_buf = rest[n_out:n_out + b]
        c_out = rest[n_out + b:n_out + b + o]
        send, recv = rest[n_out + b + o:]
        return main_in, main_out, lambda: self.build(c_in, c_buf, c_out, send, recv)


def _with_exchange(ex, n_in, n_out, in_specs, out_specs, out_shape):
    if ex is None:
        return in_specs, out_specs, out_shape, {}, [], []
    a, b = len(ex.ins), len(ex.bufs)
    in_specs = list(in_specs) + [ANY] * (a + b)
    out_specs = list(out_specs) + [ANY] * (b + len(ex.outs))
    out_shape = list(out_shape) + [jax.ShapeDtypeStruct(x.shape, x.dtype) for x in ex.bufs] + list(ex.outs)
    aliases = {n_in + a + i: n_out + i for i in range(b)}
    scratch = [pltpu.SemaphoreType.DMA((ex.n,)), pltpu.SemaphoreType.DMA((ex.n,))]
    return in_specs, out_specs, out_shape, aliases, scratch, ex.ins + ex.bufs


def attn_fwd(qkv, name, ex=None):
    S = qkv.shape[0]
    npair = SB_W // 128
    nq = S // QT

    def body(*refs):
        if ex is None:
            (q_ref, k_ref, v_ref), (o_ref, rt_ref), copies = refs[:3], refs[3:5], None
        else:
            (q_ref, k_ref, v_ref), (o_ref, rt_ref), copies = ex.split(refs, 3, 2)

            @pl.when(jnp.logical_and(pl.program_id(0) == 0, pl.program_id(1) == 0))
            def _():
                for cp in copies():
                    cp.start()

        qi = pl.program_id(1)
        seg, dcol, m_gt, _ = _sb_consts()
        qh = [_split_heads(q_ref[a * CHUNK:(a + 1) * CHUNK, :], seg) for a in range(QSUB)]
        nkb = QSUB * qi + QSUB

        def step(it, carry):
            kb = nkb - 1 - it
            off = pl.multiple_of(kb * CHUNK, CHUNK)
            kblk = k_ref[pl.ds(off, CHUNK), :]
            vblk = v_ref[pl.ds(off, CHUNK), :]
            chains = [(a, h) for a in range(QSUB) for h in range(2)]
            allowed = [dcol < (QSUB * qi + a - kb) * CHUNK for a in range(QSUB)]
            z = [_dot(qh[a][h], kblk, "nt") for a, h in chains]
            sp = [_softplus(zc) for zc in z]
            lnb = [jnp.where(allowed[a], -s, 0.0) for (a, h), s in zip(chains, sp)]
            cs = [_cumdot(x, m_gt) for x in lnb]
            att = [jnp.where(allowed[a], jnp.exp(z[c] - sp[c] + cs[c][:, :CHUNK] + carry[2 * c + 1]), 0.0)
                   for c, (a, h) in enumerate(chains)]
            pv = [_dot(x, vblk, "nn") for x in att]
            new = []
            for c in range(len(chains)):
                new.append(carry[2 * c] + pv[c])
                new.append(carry[2 * c + 1] + cs[c][:, CHUNK:])
            return tuple(new)

        z0 = jnp.zeros((CHUNK, CHUNK), F32)
        res = lax.fori_loop(0, nkb, step, (z0,) * (4 * QSUB))
        for a in range(QSUB):
            rows = slice(a * CHUNK, (a + 1) * CHUNK)
            o_ref[rows, :] = jnp.where(seg, res[4 * a], res[4 * a + 2])
            rt_ref[rows, :] = jnp.concatenate([res[4 * a + 1], res[4 * a + 3]], axis=1)

        if ex is not None:
            @pl.when(jnp.logical_and(pl.program_id(0) == npair - 1, pl.program_id(1) == nq - 1))
            def _():
                for cp in copies():
                    cp.wait()

    in_specs, out_specs, out_shape, aliases, scratch, extra = _with_exchange(
        ex, 3, 2,
        [pl.BlockSpec((QT, 128), lambda p, i: (i, p)), pl.BlockSpec((S, 128), lambda p, i: (0, npair + p)),
         pl.BlockSpec((S, 128), lambda p, i: (0, 2 * npair + p))],
        [pl.BlockSpec((QT, 128), lambda p, i: (i, p)), pl.BlockSpec((QT, 256), lambda p, i: (i, p))],
        [jax.ShapeDtypeStruct((S, SB_W), F32), jax.ShapeDtypeStruct((S, 2 * SB_W), F32)])
    return pl.pallas_call(
        body, grid=(npair, nq), in_specs=in_specs, out_specs=out_specs, out_shape=out_shape,
        input_output_aliases=aliases, scratch_shapes=scratch, name=name, compiler_params=_cp())(qkv, qkv, qkv, *extra)


def attn_bwd(qkv, rt, do, name, ex=None):
    S = qkv.shape[0]
    npair = SB_W // 128
    nq = S // QT

    def body(*refs):
        if ex is None:
            (q_ref, k_ref, v_ref, rt_ref, do_ref), (dq_ref, dk_ref, dv_ref), copies = refs[:5], refs[5:8], None
        else:
            (q_ref, k_ref, v_ref, rt_ref, do_ref), (dq_ref, dk_ref, dv_ref), copies = ex.split(refs, 5, 3)

            @pl.when(jnp.logical_and(pl.program_id(0) == 0, pl.program_id(1) == 0))
            def _():
                for cp in copies():
                    cp.start()

        qi = pl.program_id(1)

        @pl.when(qi == 0)
        def _():
            dk_ref[...] = jnp.zeros_like(dk_ref)
            dv_ref[...] = jnp.zeros_like(dv_ref)

        seg, dcol, m_gt, m_lt = _sb_consts()
        qh, doh, rtot = [], [], []
        for a in range(QSUB):
            rows = slice(a * CHUNK, (a + 1) * CHUNK)
            qh.append(_split_heads(q_ref[rows, :], seg))
            doh.append(_split_heads(do_ref[rows, :], seg))
            rtot.append((rt_ref[rows, 0:CHUNK], rt_ref[rows, CHUNK:2 * CHUNK]))
        nkb = QSUB * qi + QSUB

        def step(kb, carry):
            off = pl.multiple_of(kb * CHUNK, CHUNK)
            kblk = k_ref[pl.ds(off, CHUNK), :]
            vblk = v_ref[pl.ds(off, CHUNK), :]
            kh = _split_heads(kblk, seg)
            chains = [(a, h) for a in range(QSUB) for h in range(2)]
            nc = len(chains)
            allowed = [dcol < (QSUB * qi + a - kb) * CHUNK for a in range(QSUB)]
            z = [_dot(qh[a][h], kblk, "nt") for a, h in chains]
            da = [_dot(doh[a][h], vblk, "nt") for a, h in chains]
            sp = [_softplus(zc) for zc in z]
            lnb = [jnp.where(allowed[a], -s, 0.0) for (a, h), s in zip(chains, sp)]
            cs = [_cumdot(x, m_gt) for x in lnb]
            lc = [carry[3 * c + 1] + cs[c][:, CHUNK:] for c in range(nc)]
            att = [jnp.where(allowed[a], jnp.exp(z[c] - sp[c] + cs[c][:, :CHUNK] + (rtot[a][h] - lc[c])), 0.0)
                   for c, (a, h) in enumerate(chains)]
            g = [da[c] * att[c] for c in range(nc)]
            cg = [_cumdot(x, m_lt) for x in g]
            dz = []
            for c, (a, h) in enumerate(chains):
                sig = jnp.exp(z[c] - sp[c])
                pre = carry[3 * c + 2] + cg[c][:, :CHUNK]
                dz.append(jnp.where(allowed[a], g[c] * (1.0 - sig) - pre * sig, 0.0))
            dqc = [_dot(dz[c], kh[h], "nn") for c, (a, h) in enumerate(chains)]
            dkc = [_dot(dz[c], qh[a][h], "tn") for c, (a, h) in enumerate(chains)]
            dvc = [_dot(att[c], doh[a][h], "tn") for c, (a, h) in enumerate(chains)]
            dk_ref[pl.ds(off, CHUNK), :] += functools.reduce(lambda x, y: x + y, dkc)
            dv_ref[pl.ds(off, CHUNK), :] += functools.reduce(lambda x, y: x + y, dvc)
            new = []
            for c in range(nc):
                new += [carry[3 * c] + dqc[c], lc[c], carry[3 * c + 2] + cg[c][:, CHUNK:]]
            return tuple(new)

        z0 = jnp.zeros((CHUNK, 128), F32)
        res = lax.fori_loop(0, nkb, step, (z0,) * (6 * QSUB))
        for a in range(QSUB):
            dq_ref[a * CHUNK:(a + 1) * CHUNK, :] = res[6 * a] + res[6 * a + 3]

        if ex is not None:
            @pl.when(jnp.logical_and(pl.program_id(0) == npair - 1, pl.program_id(1) == nq - 1))
            def _():
                for cp in copies():
                    cp.wait()

    blk = pl.BlockSpec((QT, 128), lambda p, i: (i, p))
    full = pl.BlockSpec((S, 128), lambda p, i: (0, p))
    sh = jax.ShapeDtypeStruct((S, SB_W), F32)
    in_specs, out_specs, out_shape, aliases, scratch, extra = _with_exchange(
        ex, 5, 3,
        [blk, pl.BlockSpec((S, 128), lambda p, i: (0, npair + p)), pl.BlockSpec((S, 128), lambda p, i: (0, 2 * npair + p)),
         pl.BlockSpec((QT, 256), lambda p, i: (i, p)), blk],
        [blk, full, full], [sh, sh, sh])
    return pl.pallas_call(
        body, grid=(npair, nq), in_specs=in_specs, out_specs=out_specs, out_shape=out_shape,
        input_output_aliases=aliases, scratch_shapes=scratch, name=name,
        compiler_params=_cp())(qkv, qkv, qkv, rt, do, *extra)


def _adamw_math(w, g, m, v):
    m = ADAM_B1 * m + (1.0 - ADAM_B1) * g
    v = ADAM_B2 * v + (1.0 - ADAM_B2) * (g * g)
    m_hat = m / BC1
    v_hat = v / BC2
    delta = -ADAM_LR * (m_hat / (jnp.sqrt(v_hat) + ADAM_EPS) + ADAM_WD * w)
    return delta, m, v


def adamw_layer(w4, m4, v4, g, outs, l, name, tr):
    L, R, C = w4.shape

    def body(w_ref, m_ref, v_ref, g_ref, a0, a1, a2, a3, go_ref, d_ref, mo_ref, vo_ref):
        g = g_ref[...]
        d, m, v = _adamw_math(w_ref[0], g, m_ref[0], v_ref[0])
        go_ref[0] = g
        d_ref[0] = d
        mo_ref[0] = m
        vo_ref[0] = v

    st = pl.BlockSpec((1, tr, C), lambda i: (l, i, 0))
    sh = jax.ShapeDtypeStruct((L, R, C), F32)
    return pl.pallas_call(
        body, grid=(R // tr,),
        in_specs=[st, st, st, pl.BlockSpec((tr, C), lambda i: (i, 0)), ANY, ANY, ANY, ANY],
        out_specs=[st, st, st, st], out_shape=[sh, sh, sh, sh],
        input_output_aliases={4: 0, 5: 1, 6: 2, 7: 3}, name=name, compiler_params=_cp())(w4, m4, v4, g, *outs)


def adamw_flat(w, m, v, g, name):
    R, C = w.shape

    def body(w_ref, m_ref, v_ref, g_ref, d_ref, mo_ref, vo_ref):
        d, m2, v2 = _adamw_math(w_ref[...], g_ref[...], m_ref[...], v_ref[...])
        d_ref[...] = d
        mo_ref[...] = m2
        vo_ref[...] = v2

    full = pl.BlockSpec((R, C), lambda i: (0, 0))
    sh = jax.ShapeDtypeStruct((R, C), F32)
    return pl.pallas_call(body, grid=(1,), in_specs=[full] * 4, out_specs=[full] * 3, out_shape=[sh] * 3,
                          name=name, compiler_params=_cp())(w, m, v, g)


def add_half(g, l1, c, name, tr):
    nk, R, C = g.shape
    Rh = R // 2
    nt = Rh // tr

    def body(c_ref, g_ref, l_ref, o_ref):
        o_ref[...] = (g_ref[...].astype(F32) + l_ref[...].astype(F32)).astype(o_ref.dtype)

    gs = pltpu.PrefetchScalarGridSpec(
        num_scalar_prefetch=1, grid=(nk, nt),
        in_specs=[pl.BlockSpec((1, tr, C), lambda k, t, c_ref: (k, c_ref[0] * nt + t, 0)),
                  pl.BlockSpec((1, tr, C), lambda k, t, c_ref: (k, t, 0))],
        out_specs=pl.BlockSpec((1, tr, C), lambda k, t, c_ref: (k, t, 0)))
    return pl.pallas_call(body, grid_spec=gs, out_shape=jax.ShapeDtypeStruct((nk, Rh, C), g.dtype), name=name,
                          compiler_params=_cp())(c, g, l1)


def sum_chips(p, l2, kc, name, tr):
    nk, Rh, C = p.shape

    def body(k_ref, p_ref, a_ref, b_ref, c_ref, o_ref):
        o_ref[0] = ((p_ref[0].astype(F32) + a_ref[0].astype(F32)) + b_ref[0].astype(F32)) + c_ref[0].astype(F32)

    def slot(j):
        return pl.BlockSpec((1, tr, C), lambda t, k_ref: (j, t, 0))

    gs = pltpu.PrefetchScalarGridSpec(
        num_scalar_prefetch=1, grid=(Rh // tr,),
        in_specs=[pl.BlockSpec((1, tr, C), lambda t, k_ref: (k_ref[0], t, 0)), slot(0), slot(1), slot(2)],
        out_specs=pl.BlockSpec((1, tr, C), lambda t, k_ref: (k_ref[1], t, 0)))
    return pl.pallas_call(body, grid_spec=gs, out_shape=jax.ShapeDtypeStruct((2, Rh, C), F32), name=name,
                          compiler_params=_cp())(kc, p, l2, l2, l2)


def place_slot(a, idx, nslot, name, tr):
    R, C = a.shape

    def body(i_ref, a_ref, o_ref):
        o_ref[0] = a_ref[...]

    gs = pltpu.PrefetchScalarGridSpec(
        num_scalar_prefetch=1, grid=(R // tr,),
        in_specs=[pl.BlockSpec((tr, C), lambda t, i_ref: (t, 0))],
        out_specs=pl.BlockSpec((1, tr, C), lambda t, i_ref: (i_ref[0], t, 0)))
    return pl.pallas_call(body, grid_spec=gs, out_shape=jax.ShapeDtypeStruct((nslot, R, C), a.dtype), name=name,
                          compiler_params=_cp())(idx, a)


def _place():
    x, y, c = lax.axis_index("x"), lax.axis_index("y"), lax.axis_index("c")
    chips = [(1 - x, y), (x, 1 - y), (1 - x, 1 - y)]
    return x, y, c, chips


def _rcopy(src, dst, send_sems, recv_sems, k, to):
    return pltpu.make_async_remote_copy(src_ref=src, dst_ref=dst, send_sem=send_sems.at[k], recv_sem=recv_sems.at[k],
                                        device_id=to, device_id_type=MESH)


def gather_exchange(shards, bufs):
    def build(in_refs, buf_refs, out_refs, send_sems, recv_sems):
        x, y, c, chips = _place()
        cps = []
        for i, ref in enumerate(in_refs):
            rh = shards[i].shape[0] // 2
            for j, chip in enumerate(chips):
                rows = pl.ds(c * rh, rh)
                cps.append(_rcopy(ref.at[rows, :], buf_refs[i].at[2 * x + y].at[rows, :], send_sems, recv_sems,
                                  3 * i + j, (*chip, c)))
        return cps

    return Exchange(shards, bufs, [], 3 * len(shards), build)


def scatter_exchange(ps):
    def build(in_refs, buf_refs, out_refs, send_sems, recv_sems):
        x, y, c, chips = _place()
        return [_rcopy(ref.at[2 * chip[0] + chip[1]], out_refs[i].at[j], send_sems, recv_sems, 3 * i + j, (*chip, c))
                for i, ref in enumerate(in_refs) for j, chip in enumerate(chips)]

    outs = [jax.ShapeDtypeStruct((3,) + p.shape[1:], p.dtype) for p in ps]
    return Exchange(ps, [], outs, 3 * len(ps), build)


def run_exchange(ex, name):
    def body(*refs):
        _, _, copies = ex.split(refs, 0, 0)
        cps = copies()
        for cp in cps:
            cp.start()
        for cp in cps:
            cp.wait()

    in_specs, out_specs, out_shape, aliases, scratch, extra = _with_exchange(ex, 0, 0, [], [], [])
    return pl.pallas_call(body, in_specs=in_specs, out_specs=out_specs, out_shape=out_shape,
                          input_output_aliases=aliases, scratch_shapes=scratch, name=name)(*extra)


def gather_forward(bufs, name):
    n = len(bufs)

    def body(*refs):
        outs = refs[n:2 * n]
        send_sems, recv_sems = refs[2 * n:]
        x, y, c, chips = _place()
        cps = []
        for i in range(n):
            rh = bufs[i].shape[1] // 2
            for j, chip in enumerate(chips):
                landed = outs[i].at[2 * chip[0] + chip[1]].at[pl.ds(c * rh, rh), :]
                cps.append(_rcopy(landed, landed, send_sems, recv_sems, 3 * i + j, (x, y, 1 - c)))
        for cp in cps:
            cp.start()
        for cp in cps:
            cp.wait()

    return pl.pallas_call(
        body, in_specs=[ANY] * n, out_specs=[ANY] * n,
        out_shape=[jax.ShapeDtypeStruct(b.shape, b.dtype) for b in bufs],
        input_output_aliases={i: i for i in range(n)},
        scratch_shapes=[pltpu.SemaphoreType.DMA((3 * n,)), pltpu.SemaphoreType.DMA((3 * n,))], name=name)(*bufs)


def exchange_sibling_half(gs, name):
    n = len(gs)

    def body(*refs):
        ins, outs = refs[:n], refs[n:2 * n]
        send_sems, recv_sems = refs[2 * n:]
        x, y, c, _ = _place()
        cps = []
        for i in range(n):
            rh = gs[i].shape[1] // 2
            cp = pltpu.make_async_remote_copy(
                src_ref=ins[i].at[:, pl.ds((1 - c) * rh, rh), :], dst_ref=outs[i], send_sem=send_sems.at[i],
                recv_sem=recv_sems.at[i], device_id=(x, y, 1 - c), device_id_type=MESH)
            cp.start()
            cps.append(cp)
        for cp in cps:
            cp.wait()

    return pl.pallas_call(
        body, in_specs=[ANY] * n, out_specs=[ANY] * n,
        out_shape=[jax.ShapeDtypeStruct((g.shape[0], g.shape[1] // 2, g.shape[2]), g.dtype) for g in gs],
        scratch_shapes=[pltpu.SemaphoreType.DMA((n,)), pltpu.SemaphoreType.DMA((n,))],
        name=name)(*gs)


def join_sibling_halves(fs, name):
    n = len(fs)

    def body(*refs):
        outs = refs[n:2 * n]
        send_sems, recv_sems = refs[2 * n:]
        x, y, c, _ = _place()
        cps = [_rcopy(outs[i].at[c], outs[i].at[c], send_sems, recv_sems, i, (x, y, 1 - c)) for i in range(n)]
        for cp in cps:
            cp.start()
        for cp in cps:
            cp.wait()

    return pl.pallas_call(
        body, in_specs=[ANY] * n, out_specs=[ANY] * n,
        out_shape=[jax.ShapeDtypeStruct(f.shape, f.dtype) for f in fs],
        input_output_aliases={i: i for i in range(n)},
        scratch_shapes=[pltpu.SemaphoreType.DMA((n,)), pltpu.SemaphoreType.DMA((n,))], name=name)(*fs)


def all_reduce_small(v, name):
    R, C = v.shape

    def body(v_ref, sum_ref, all_ref, send_sems, recv_sems, local_sem):
        x, y, c, chips = _place()
        me, sib = (x, y, c), (x, y, 1 - c)

        def slab(px, py, pc):
            return all_ref.at[4 * px + 2 * py + pc]

        def copy(k, block, to, src=None):
            return pltpu.make_async_remote_copy(
                src_ref=slab(*block) if src is None else src, dst_ref=slab(*block), send_sem=send_sems.at[k],
                recv_sem=recv_sems.at[k], device_id=to, device_id_type=MESH)

        mine = pltpu.make_async_copy(v_ref, slab(*me), local_sem)
        mine.start()
        first = [copy(0, me, sib, src=v_ref)] + [copy(1 + j, me, (*chip, c), src=v_ref) for j, chip in enumerate(chips)]
        for cp in first:
            cp.start()
        passed = [copy(4 + j, (*chip, c), sib) for j, chip in enumerate(chips)]
        for j, chip in enumerate(chips):
            copy(1 + j, (*chip, c), me).wait_recv()
            passed[j].start()
        copy(0, sib, me).wait_recv()
        for j, chip in enumerate(chips):
            copy(4 + j, (*chip, 1 - c), me).wait_recv()
        for cp in first + passed:
            cp.wait_send()
        mine.wait()
        acc = all_ref[0]
        for d in range(1, 8):
            acc = acc + all_ref[d]
        sum_ref[...] = acc

    vm = pl.BlockSpec(memory_space=pltpu.VMEM)
    return pl.pallas_call(
        body, in_specs=[vm], out_specs=[vm, vm],
        out_shape=[jax.ShapeDtypeStruct((R, C), F32), jax.ShapeDtypeStruct((8, R, C), F32)],
        scratch_shapes=[pltpu.SemaphoreType.DMA((7,)), pltpu.SemaphoreType.DMA((7,)), pltpu.SemaphoreType.DMA],
        name=name, compiler_params=_cp())(v)[0]


SMALL = ("mix_norm_g", "conv_w", "conv_b", "conv_ln_g", "conv_ln_b", "sg_ln_g", "sg_ln_b", "sg_w", "sg_b",
         "q_norm_g", "k_norm_g", "out_norm_g", "ffn_norm_g")
BIG = ("w_in", "w_out", "w_gate_up", "w_down")
WEIGHTS = ("mix_norm_g", "w_in", "conv_w", "conv_b", "conv_ln_g", "conv_ln_b", "sg_ln_g", "sg_ln_b", "sg_w", "sg_b",
           "q_norm_g", "k_norm_g", "out_norm_g", "w_out", "ffn_norm_g", "w_gate_up", "w_down")
ADAM_ROWS = {"w_in": 256, "w_out": 128, "w_gate_up": 128, "w_down": 176}
RS_ROWS = {"w_in": 256, "w_out": 128, "w_gate_up": 256, "w_down": 176}
AG_ROWS = {"w_in": 512, "w_out": 256, "w_gate_up": 512, "w_down": 352}


def _pack(parts):
    flat = jnp.concatenate([p.reshape(-1) for p in parts])
    n = flat.shape[0]
    rows = -(-n // (8 * 128)) * 8
    return jnp.pad(flat, (0, rows * 128 - n)).reshape(rows, 128)


def _unpack(buf, shapes):
    flat = buf.reshape(-1)
    out, off = [], 0
    for s in shapes:
        n = 1
        for d in s:
            n *= d
        out.append(flat[off:off + n].reshape(s))
        off += n
    return out


def layer_forward(x, l, P, W, ex=None):
    sv = {"x": x}
    sv["h"] = rms_fwd(x, P["mix_norm_g"], l, f"rms_mix_{l}")
    sv["proj"] = mm_colblk(sv["h"], W["w_in"], F32, f"mm_in_{l}")
    sv["yc"] = conv_fwd(sv["proj"], P["conv_w"], P["conv_b"], P["conv_ln_g"], P["conv_ln_b"], l, f"conv_fwd_{l}")
    sv["ys"] = sg_fwd(sv["proj"], P["sg_ln_g"], P["sg_ln_b"], P["sg_wt"], P["sg_bt"], l, f"sg_fwd_{l}")
    sv["qkv"] = qkv_fwd(sv["proj"], P["q_norm_g"], P["k_norm_g"], l, f"qkv_fwd_{l}")
    sv["yb"], sv["rt"], *moved = attn_fwd(sv["qkv"], f"attn_fwd_{l}", ex)
    sv["yn"] = outnorm_fwd(sv["yc"], sv["ys"], sv["yb"], P["out_norm_g"], l, f"outnorm_fwd_{l}")
    sv["x1"] = mm_res(sv["yn"], W["w_out"].reshape(D_MODEL, D_MODEL), x, f"mm_out_{l}")
    sv["h2"] = rms_fwd(sv["x1"], P["ffn_norm_g"], l, f"rms_ffn_{l}")
    sv["g"], sv["u"], sv["act"] = ffn_up(sv["h2"], W["w_gate_up"], f"ffn_up_{l}")
    x2 = mm_res(sv["act"], W["w_down"].reshape(FFN, D_MODEL), sv["x1"], f"mm_down_{l}")
    return x2, sv, moved


def layer_backward(dx2, l, P, W, sv, ex=None):
    gb, gs = {}, {}
    wdown = W["w_down"].reshape(FFN, D_MODEL)
    dgu = ffn_down_bwd(dx2, wdown, sv["g"], sv["u"], f"ffn_down_bwd_{l}")
    gb["w_down"] = mm_wgrad(sv["act"], dx2, 1408, 512, f"wgrad_down_{l}", False).reshape(N_CHIP, FFN // N_CHIP, D_MODEL)
    dh2 = mm_dgrad_colblk(dgu, W["w_gate_up"], f"dgrad_gu_{l}")
    gb["w_gate_up"] = mm_wgrad(sv["h2"], dgu, 512, 1408, f"wgrad_gu_{l}", True)
    dx1, gs["ffn_norm_g"] = rms_bwd(dh2, sv["x1"], P["ffn_norm_g"], l, dx2, f"rms_ffn_bwd_{l}")
    dyn = mm_dgrad(dx1, W["w_out"].reshape(D_MODEL, D_MODEL), f"dgrad_out_{l}")
    gb["w_out"] = mm_wgrad(sv["yn"], dx1, 512, 512, f"wgrad_out_{l}", False).reshape(N_CHIP, D_MODEL // N_CHIP, D_MODEL)
    dyc, dys, dyb, gs["out_norm_g"] = outnorm_bwd(dyn, sv["yc"], sv["ys"], sv["yb"], P["out_norm_g"], l,
                                                  f"outnorm_bwd_{l}")
    dq, dk, dv, *moved = attn_bwd(sv["qkv"], sv["rt"], dyb, f"attn_bwd_{l}", ex)
    dpb, dgq, dgk = qkv_bwd(sv["proj"], dq, dk, dv, P["q_norm_g"], P["k_norm_g"], l, f"qkv_bwd_{l}")
    gs["q_norm_g"] = dgq[0, :HEAD_DIM] + dgq[0, HEAD_DIM:]
    gs["k_norm_g"] = dgk[0, :HEAD_DIM] + dgk[0, HEAD_DIM:]
    dps, gs["sg_ln_g"], gs["sg_ln_b"], gs["sg_w"], dbt = sg_bwd(
        sv["proj"], dys, P["sg_ln_g"], P["sg_ln_b"], P["sg_wt"], P["sg_wtt"], P["sg_bt"], l, f"sg_bwd_{l}")
    gs["sg_b"] = dbt.T
    dpc, dcw, gs["conv_b"], gs["conv_ln_g"], gs["conv_ln_b"] = conv_bwd(
        sv["proj"], dyc, P["conv_w"], P["conv_b"], P["conv_ln_g"], P["conv_ln_b"], l, f"conv_bwd_{l}")
    gs["conv_w"] = dcw[:CONV_K]
    dproj = jnp.concatenate([dpc, dps, dpb], axis=-1)
    dh = mm_dgrad_colblk(dproj, W["w_in"], f"dgrad_in_{l}")
    gb["w_in"] = mm_wgrad(sv["h"], dproj, 512, IN_W // N_CHIP, f"wgrad_in_{l}", True)
    dx, gs["mix_norm_g"] = rms_bwd(dh, sv["x"], P["mix_norm_g"], l, dx1, f"rms_mix_bwd_{l}")
    return dx, gb, gs, moved


def reduce_start(gb, l, c1):
    gl = [gb[n] for n in BIG]
    l1 = exchange_sibling_half(gl, f"rs_sibling_{l}")
    return [add_half(g, a, c1, f"rs_add_{n}_{l}", RS_ROWS[n]) for n, g, a in zip(BIG, gl, l1)]


def reduce_finish(ps, l2, l, kc):
    fs = [sum_chips(p, a, kc, f"rs_sum_{n}_{l}", RS_ROWS[n]) for n, p, a in zip(BIG, ps, l2)]
    full = join_sibling_halves(fs, f"rs_join_{l}")
    return {n: f.reshape(2 * f.shape[1], f.shape[2]) for n, f in zip(BIG, full)}


def kernel(x, mix_norm_g, w_in, conv_w, conv_b, conv_ln_g, conv_ln_b, sg_ln_g, sg_ln_b, sg_w, sg_b, q_norm_g, k_norm_g, out_norm_g, w_out, ffn_norm_g, w_gate_up, w_down, loss_target, m_mix_norm_g, m_w_in, m_conv_w, m_conv_b, m_conv_ln_g, m_conv_ln_b, m_sg_ln_g, m_sg_ln_b, m_sg_w, m_sg_b, m_q_norm_g, m_k_norm_g, m_out_norm_g, m_w_out, m_ffn_norm_g, m_w_gate_up, m_w_down, v_mix_norm_g, v_w_in, v_conv_w, v_conv_b, v_conv_ln_g, v_conv_ln_b, v_sg_ln_g, v_sg_ln_b, v_sg_w, v_sg_b, v_q_norm_g, v_k_norm_g, v_out_norm_g, v_w_out, v_ffn_norm_g, v_w_gate_up, v_w_down):
    w = dict(mix_norm_g=mix_norm_g, w_in=w_in, conv_w=conv_w, conv_b=conv_b, conv_ln_g=conv_ln_g, conv_ln_b=conv_ln_b,
             sg_ln_g=sg_ln_g, sg_ln_b=sg_ln_b, sg_w=sg_w, sg_b=sg_b, q_norm_g=q_norm_g, k_norm_g=k_norm_g,
             out_norm_g=out_norm_g, w_out=w_out, ffn_norm_g=ffn_norm_g, w_gate_up=w_gate_up, w_down=w_down)
    m = dict(mix_norm_g=m_mix_norm_g, w_in=m_w_in, conv_w=m_conv_w, conv_b=m_conv_b, conv_ln_g=m_conv_ln_g,
             conv_ln_b=m_conv_ln_b, sg_ln_g=m_sg_ln_g, sg_ln_b=m_sg_ln_b, sg_w=m_sg_w, sg_b=m_sg_b,
             q_norm_g=m_q_norm_g, k_norm_g=m_k_norm_g, out_norm_g=m_out_norm_g, w_out=m_w_out,
             ffn_norm_g=m_ffn_norm_g, w_gate_up=m_w_gate_up, w_down=m_w_down)
    v = dict(mix_norm_g=v_mix_norm_g, w_in=v_w_in, conv_w=v_conv_w, conv_b=v_conv_b, conv_ln_g=v_conv_ln_g,
             conv_ln_b=v_conv_ln_b, sg_ln_g=v_sg_ln_g, sg_ln_b=v_sg_ln_b, sg_w=v_sg_w, sg_b=v_sg_b,
             q_norm_g=v_q_norm_g, k_norm_g=v_k_norm_g, out_norm_g=v_out_norm_g, w_out=v_w_out,
             ffn_norm_g=v_ffn_norm_g, w_gate_up=v_w_gate_up, w_down=v_w_down)
    L = DEPTH
    xi, yi, ci = lax.axis_index("x"), lax.axis_index("y"), lax.axis_index("c")
    kme = 2 * xi + yi
    c1 = ci.astype(jnp.int32).reshape(1)
    k1 = kme.astype(jnp.int32).reshape(1)
    kc = jnp.stack([kme, ci]).astype(jnp.int32)

    def gather_parts(l):
        shards = [w[n][l].astype(BF16) for n in BIG]
        bufs = [place_slot(s, k1, N_CHIP, f"ag_own_{n}_{l}", AG_ROWS[n]) for n, s in zip(BIG, shards)]
        return shards, bufs

    shards, bufs = gather_parts(0)
    cw_sh = jnp.pad(conv_w.reshape(L * CONV_K, CONV_W // N_CHIP), ((0, 128 - L * CONV_K), (0, 0)))
    cw_buf = place_slot(cw_sh, k1, N_CHIP, "ag_own_conv", 128)
    *bufs, cw_buf = run_exchange(gather_exchange(shards + [cw_sh], bufs + [cw_buf]), "ag_chips_0")
    *bufs, cw_buf = gather_forward(bufs + [cw_buf], "ag_sibling_0")
    W = [dict(zip(BIG, bufs))]
    cw_all = cw_buf[:, :L * CONV_K].reshape(N_CHIP, L, CONV_K, CONV_W // N_CHIP)
    cw_full = jnp.transpose(cw_all, (1, 2, 0, 3)).reshape(L, CONV_K, CONV_W)

    tril = jnp.tril(jnp.ones((CHUNK, CHUNK), bool))
    sg_wt = jnp.where(tril, sg_w, 0.0)
    P = {
        "mix_norm_g": mix_norm_g.reshape(L, 1, D_MODEL), "ffn_norm_g": ffn_norm_g.reshape(L, 1, D_MODEL),
        "out_norm_g": out_norm_g.reshape(L, 1, D_MODEL),
        "conv_w": jnp.pad(cw_full, ((0, 0), (0, 1), (0, 0))), "conv_b": conv_b.reshape(L, 1, CONV_W),
        "conv_ln_g": conv_ln_g.reshape(L, 1, CONV_W), "conv_ln_b": conv_ln_b.reshape(L, 1, CONV_W),
        "sg_ln_g": sg_ln_g.reshape(L, 1, SG_W), "sg_ln_b": sg_ln_b.reshape(L, 1, SG_W),
        "sg_wt": sg_wt.astype(MXU_DT), "sg_wtt": jnp.swapaxes(sg_wt, 2, 3).astype(MXU_DT),
        "sg_bt": jnp.swapaxes(sg_b, 1, 2),
        "q_norm_g": jnp.tile(q_norm_g, (1, 2)).reshape(L, 1, 128), "k_norm_g": jnp.tile(k_norm_g, (1, 2)).reshape(L, 1, 128),
    }

    h = x[0]
    saved = []
    for l in range(L):
        ex = gather_exchange(*gather_parts(l + 1)) if l + 1 < L else None
        h, sv, moved = layer_forward(h, l, P, W[l], ex)
        saved.append(sv)
        if ex is not None:
            W.append(dict(zip(BIG, gather_forward(moved, f"ag_sibling_{l + 1}"))))
    dy, loss_part = loss_head(h, loss_target[0], "loss_head")
    loss = lax.psum(loss_part[0, 0], ("x", "y", "c"))

    outs = {n: [lax.empty(w[n].shape, F32) for _ in range(4)] for n in BIG}
    small_grads = [None] * L

    def finish(ps, l2, lyr):
        red = reduce_finish(ps, l2, lyr, kc)
        for n in BIG:
            outs[n] = adamw_layer(w[n], m[n], v[n], red[n], outs[n], lyr, f"adamw_{n}_{lyr}", ADAM_ROWS[n])

    ps = None
    for l in reversed(range(L)):
        ex = scatter_exchange(ps) if ps is not None else None
        dy, gb, small_grads[l], moved = layer_backward(dy, l, P, W[l], saved[l], ex)
        if ps is not None:
            finish(ps, moved, l + 1)
        ps = reduce_start(gb, l, c1)
    finish(ps, run_exchange(scatter_exchange(ps), "rs_chips_0"), 0)

    shapes = [(L,) + small_grads[0][n].shape for n in SMALL]
    packed = _pack([jnp.stack([small_grads[l][n] for l in range(L)]) for n in SMALL])
    gsum = dict(zip(SMALL, _unpack(all_reduce_small(packed, "ar_small"), shapes)))
    gsum["conv_w"] = lax.dynamic_slice_in_dim(gsum["conv_w"], kme * (CONV_W // N_CHIP), CONV_W // N_CHIP, axis=2)
    gsum = {n: gsum[n].reshape(w[n].shape) for n in SMALL}
    lshapes = [w[n].shape for n in SMALL]
    d_p, m_p, v_p = adamw_flat(_pack([w[n] for n in SMALL]), _pack([m[n] for n in SMALL]),
                               _pack([v[n] for n in SMALL]), _pack([gsum[n] for n in SMALL]), "adamw_small")
    d_s = dict(zip(SMALL, _unpack(d_p, lshapes)))
    m_s = dict(zip(SMALL, _unpack(m_p, lshapes)))
    v_s = dict(zip(SMALL, _unpack(v_p, lshapes)))

    grads = {n: (outs[n][0] if n in BIG else gsum[n]) for n in WEIGHTS}
    delta = {n: (outs[n][1] if n in BIG else d_s[n]) for n in WEIGHTS}
    new_m = {n: (outs[n][2] if n in BIG else m_s[n]) for n in WEIGHTS}
    new_v = {n: (outs[n][3] if n in BIG else v_s[n]) for n in WEIGHTS}
    return (loss, dy[None], *[grads[n] for n in WEIGHTS], *[delta[n] for n in WEIGHTS],
            *[new_m[n] for n in WEIGHTS], *[new_v[n] for n in WEIGHTS])
```

```python
import functools

import jax
import jax.numpy as jnp
from jax import lax
from jax.experimental import pallas as pl
from jax.experimental.pallas import tpu as pltpu

F32 = jnp.float32
BF16 = jnp.bfloat16
MXU_DT = jnp.bfloat16
GRAD_WIRE_DT = jnp.bfloat16

D_MODEL = 1024
DEPTH = 4
HEAD_DIM = 64
CONV_W = 256
SG_W = 256
SB_W = 512
CONV_K = 31
CHUNK = 128
OFF_SG = 2 * CONV_W
OFF_SB = OFF_SG + 2 * SG_W
IN_W = OFF_SB + 3 * SB_W
FFN = 2816
N_CHIP = 4
RMS_EPS = 1e-6
LN_EPS = 1e-5
ADAM_LR = 0.001
ADAM_B1 = 0.9
ADAM_B2 = 0.999
ADAM_EPS = 1e-08
ADAM_WD = 0.01
ADAM_STEP = 10
BC1 = 1.0 - ADAM_B1 ** ADAM_STEP
BC2 = 1.0 - ADAM_B2 ** ADAM_STEP
HALO = 32
MESH = pl.DeviceIdType.MESH
ANY = pl.BlockSpec(memory_space=pl.ANY)
VMEM_LIMIT = 56 * 1024 * 1024


def _cp(**kw):
    return pltpu.CompilerParams(vmem_limit_bytes=VMEM_LIMIT, **kw)


def _dot(a, b, dims):
    dn = {"nn": (((1,), (0,)), ((), ())), "nt": (((1,), (1,)), ((), ())), "tn": (((0,), (0,)), ((), ()))}[dims]
    return lax.dot_general(a.astype(MXU_DT), b.astype(MXU_DT), dn, preferred_element_type=F32)


def _cumdot(x, m):
    hi = x.astype(MXU_DT)
    lo = (x - hi.astype(F32)).astype(MXU_DT)
    dn = (((1,), (0,)), ((), ()))
    return (lax.dot_general(hi, m, dn, preferred_element_type=F32)
            + lax.dot_general(lo, m, dn, preferred_element_type=F32))


def _iota(shape, axis):
    return lax.broadcasted_iota(jnp.int32, shape, axis)


def _rms(x, g):
    return x * lax.rsqrt(jnp.mean(x * x, axis=-1, keepdims=True) + RMS_EPS) * g


def _ln(x, g, b):
    mu = jnp.mean(x, axis=-1, keepdims=True)
    xc = x - mu
    var = jnp.mean(xc * xc, axis=-1, keepdims=True)
    return xc * lax.rsqrt(var + LN_EPS) * g + b


def _glu(val, gate):
    return val * jax.nn.sigmoid(gate)


def _ln_silu(c, g, b):
    return jax.nn.silu(_ln(c, g, b))


_ERF_ALPHA = (-2.72614225801306e-10, 2.77068142495902e-08, -2.10102402082508e-06, -5.69250639462346e-05,
              -7.34990630326855e-04, -2.95459980854025e-03, -1.60960333262415e-02)
_ERF_BETA = (-1.45660718464996e-05, -2.13374055278905e-04, -1.68282697438203e-03, -7.37332916720468e-03,
             -1.42647390514189e-02)


def _erf(x):
    x = jnp.clip(x, -4.0, 4.0)
    x2 = x * x
    p = jnp.full_like(x, _ERF_ALPHA[0])
    for a in _ERF_ALPHA[1:]:
        p = p * x2 + a
    q = jnp.full_like(x, _ERF_BETA[0])
    for b in _ERF_BETA[1:]:
        q = q * x2 + b
    return x * p / q


@jax.custom_jvp
def _gelu(x):
    return 0.5 * x * (1.0 + _erf(x * (2.0 ** -0.5)))


@_gelu.defjvp
def _gelu_jvp(primals, tangents):
    (x,), (t,) = primals, tangents
    cdf = 0.5 * (1.0 + _erf(x * (2.0 ** -0.5)))
    pdf = jnp.exp(-0.5 * x * x) * ((2.0 * jnp.pi) ** -0.5)
    return x * cdf, t * (cdf + x * pdf)


def _sg_pre(uvp, g, b):
    uv = _gelu(uvp)
    return uv[:, :SG_W], _ln(uv[:, SG_W:], g, b)


def _outnorm(yc, ys, yb, g):
    return jnp.concatenate([_rms(yc, g[:, :CONV_W]), _rms(ys, g[:, CONV_W:CONV_W + SG_W]),
                            _rms(yb, g[:, CONV_W + SG_W:])], axis=-1)


def _qkv_fn(pq, pk, pv, gq, gk):
    outs = []
    for p, g, sc in ((pq, gq, HEAD_DIM ** -0.5), (pk, gk, 1.0)):
        for s in range(SB_W // 128):
            x = p[:, 128 * s:128 * (s + 1)]
            seg = _iota(x.shape, 1) < HEAD_DIM
            x2 = x * x
            s0 = jnp.sum(jnp.where(seg, x2, 0.0), axis=-1, keepdims=True)
            s1 = jnp.sum(jnp.where(seg, 0.0, x2), axis=-1, keepdims=True)
            ms = jnp.where(seg, s0, s1) * (1.0 / HEAD_DIM)
            outs.append(x * lax.rsqrt(ms + RMS_EPS) * (g * sc))
    outs.append(pv)
    return jnp.concatenate(outs, axis=-1)


def _swiglu(g, u):
    return jax.nn.silu(g) * u


def _softplus(z):
    return jnp.maximum(z, 0.0) + jnp.log(1.0 + jnp.exp(-jnp.abs(z)))


def mm_colblk(a, wb, out_dtype, name, tm=1024):
    S, K = a.shape
    nb, _, C = wb.shape

    def body(a_ref, w_ref, o_ref):
        o_ref[...] = _dot(a_ref[...], w_ref[0], "nn").astype(o_ref.dtype)

    return pl.pallas_call(
        body, grid=(nb, S // tm),
        in_specs=[pl.BlockSpec((tm, K), lambda k, i: (i, 0)), pl.BlockSpec((1, K, C), lambda k, i: (k, 0, 0))],
        out_specs=pl.BlockSpec((tm, C), lambda k, i: (i, k)),
        out_shape=jax.ShapeDtypeStruct((S, nb * C), out_dtype), name=name, compiler_params=_cp())(a, wb)


def mm_res(a, w, res, name, tm=512):
    S, K = a.shape
    N = w.shape[1]

    def body(a_ref, w_ref, r_ref, o_ref):
        o_ref[...] = r_ref[...] + _dot(a_ref[...], w_ref[...], "nn")

    return pl.pallas_call(
        body, grid=(S // tm,),
        in_specs=[pl.BlockSpec((tm, K), lambda i: (i, 0)), pl.BlockSpec((K, N), lambda i: (0, 0)),
                  pl.BlockSpec((tm, N), lambda i: (i, 0))],
        out_specs=pl.BlockSpec((tm, N), lambda i: (i, 0)),
        out_shape=jax.ShapeDtypeStruct((S, N), F32), name=name, compiler_params=_cp())(a, w, res)


def ffn_up(h2, wgu, name, tm=512):
    S, K = h2.shape
    C = wgu.shape[2]

    def body(h_ref, wg_ref, wu_ref, g_ref, u_ref, a_ref):
        h = h_ref[...]
        g = _dot(h, wg_ref[0], "nn")
        u = _dot(h, wu_ref[0], "nn")
        g_ref[...] = g.astype(BF16)
        u_ref[...] = u.astype(BF16)
        a_ref[...] = _swiglu(g, u).astype(BF16)

    o = pl.BlockSpec((tm, C), lambda j, i: (i, j))
    sh = jax.ShapeDtypeStruct((S, 2 * C), BF16)
    return pl.pallas_call(
        body, grid=(2, S // tm),
        in_specs=[pl.BlockSpec((tm, K), lambda j, i: (i, 0)), pl.BlockSpec((1, K, C), lambda j, i: (j, 0, 0)),
                  pl.BlockSpec((1, K, C), lambda j, i: (2 + j, 0, 0))],
        out_specs=[o, o, o], out_shape=[sh, sh, sh], name=name, compiler_params=_cp())(h2, wgu, wgu)


def ffn_down_bwd(dx2, wdown, g, u, name, tm=512):
    S, N = dx2.shape
    C = FFN // 2

    def body(d_ref, w_ref, g_ref, u_ref, o_ref):
        d = d_ref[...]
        for j in range(2):
            cols = slice(j * C, (j + 1) * C)
            dact = _dot(d, w_ref[cols, :], "nt")
            _, vjp = jax.vjp(_swiglu, g_ref[:, cols].astype(F32), u_ref[:, cols].astype(F32))
            dg, du = vjp(dact)
            o_ref[:, cols] = dg.astype(BF16)
            o_ref[:, FFN + j * C:FFN + (j + 1) * C] = du.astype(BF16)

    row = pl.BlockSpec((tm, FFN), lambda i: (i, 0))
    return pl.pallas_call(
        body, grid=(S // tm,),
        in_specs=[pl.BlockSpec((tm, N), lambda i: (i, 0)), pl.BlockSpec((FFN, N), lambda i: (0, 0)), row, row],
        out_specs=pl.BlockSpec((tm, 2 * FFN), lambda i: (i, 0)),
        out_shape=jax.ShapeDtypeStruct((S, 2 * FFN), BF16), name=name, compiler_params=_cp())(dx2, wdown, g, u)


def mm_dgrad_colblk(do, wb, name, tm=1024):
    S = do.shape[0]
    nb, K, C = wb.shape

    def body(d_ref, w_ref, o_ref):
        k = pl.program_id(1)
        r = _dot(d_ref[...], w_ref[0], "nt")

        @pl.when(k == 0)
        def _():
            o_ref[...] = r

        @pl.when(k != 0)
        def _():
            o_ref[...] += r

    return pl.pallas_call(
        body, grid=(S // tm, nb),
        in_specs=[pl.BlockSpec((tm, C), lambda i, k: (i, k)), pl.BlockSpec((1, K, C), lambda i, k: (k, 0, 0))],
        out_specs=pl.BlockSpec((tm, K), lambda i, k: (i, 0)),
        out_shape=jax.ShapeDtypeStruct((S, K), F32), name=name, compiler_params=_cp())(do, wb)


def mm_dgrad(do, w, name, tm=1024):
    S, N = do.shape
    K = w.shape[0]

    def body(d_ref, w_ref, o_ref):
        o_ref[...] = _dot(d_ref[...], w_ref[...], "nt")

    return pl.pallas_call(
        body, grid=(S // tm,),
        in_specs=[pl.BlockSpec((tm, N), lambda i: (i, 0)), pl.BlockSpec((K, N), lambda i: (0, 0))],
        out_specs=pl.BlockSpec((tm, K), lambda i: (i, 0)),
        out_shape=jax.ShapeDtypeStruct((S, K), F32), name=name, compiler_params=_cp())(do, w)


def mm_wgrad(a, do, tk, tn, name, blocked):
    S, K = a.shape
    N = do.shape[1]

    def body(a_ref, d_ref, o_ref):
        r = _dot(a_ref[...], d_ref[...], "tn").astype(GRAD_WIRE_DT)
        if blocked:
            o_ref[0] = r
        else:
            o_ref[...] = r

    if blocked:
        out_spec = pl.BlockSpec((1, tk, tn), lambda n, j: (n, j, 0))
        out_shape = jax.ShapeDtypeStruct((N // tn, K, tn), GRAD_WIRE_DT)
    else:
        out_spec = pl.BlockSpec((tk, tn), lambda n, j: (j, n))
        out_shape = jax.ShapeDtypeStruct((K, N), GRAD_WIRE_DT)
    return pl.pallas_call(
        body, grid=(N // tn, K // tk),
        in_specs=[pl.BlockSpec((S, tk), lambda n, j: (0, j)), pl.BlockSpec((S, tn), lambda n, j: (0, n))],
        out_specs=out_spec, out_shape=out_shape, name=name, compiler_params=_cp())(a, do)


def rms_fwd(x, g3, l, name, tm=512):
    S, N = x.shape

    def body(x_ref, g_ref, o_ref):
        o_ref[...] = _rms(x_ref[...], g_ref[0]).astype(BF16)

    return pl.pallas_call(
        body, grid=(S // tm,),
        in_specs=[pl.BlockSpec((tm, N), lambda i: (i, 0)), pl.BlockSpec((1, 1, N), lambda i: (l, 0, 0))],
        out_specs=pl.BlockSpec((tm, N), lambda i: (i, 0)),
        out_shape=jax.ShapeDtypeStruct((S, N), BF16), name=name, compiler_params=_cp())(x, g3)


def rms_bwd(dh, x, g3, l, dres, name, tm=512):
    S, N = x.shape

    def body(dh_ref, x_ref, g_ref, r_ref, dx_ref, dg_ref):
        _, vjp = jax.vjp(_rms, x_ref[...], g_ref[0])
        dx, dg = vjp(dh_ref[...])
        dx_ref[...] = r_ref[...] + dx

        @pl.when(pl.program_id(0) == 0)
        def _():
            dg_ref[...] = jnp.zeros_like(dg_ref)

        dg_ref[...] += dg

    row = pl.BlockSpec((tm, N), lambda i: (i, 0))
    return pl.pallas_call(
        body, grid=(S // tm,),
        in_specs=[row, row, pl.BlockSpec((1, 1, N), lambda i: (l, 0, 0)), row],
        out_specs=[row, pl.BlockSpec((1, N), lambda i: (0, 0))],
        out_shape=[jax.ShapeDtypeStruct((S, N), F32), jax.ShapeDtypeStruct((1, N), F32)],
        name=name, compiler_params=_cp())(dh, x, g3, dres)


def outnorm_fwd(yc, ys, yb, g3, l, name, tm=512):
    S = yc.shape[0]

    def body(c_ref, s_ref, b_ref, g_ref, o_ref):
        o_ref[...] = _outnorm(c_ref[...], s_ref[...], b_ref[...], g_ref[0]).astype(BF16)

    return pl.pallas_call(
        body, grid=(S // tm,),
        in_specs=[pl.BlockSpec((tm, CONV_W), lambda i: (i, 0)), pl.BlockSpec((tm, SG_W), lambda i: (i, 0)),
                  pl.BlockSpec((tm, SB_W), lambda i: (i, 0)), pl.BlockSpec((1, 1, D_MODEL), lambda i: (l, 0, 0))],
        out_specs=pl.BlockSpec((tm, D_MODEL), lambda i: (i, 0)),
        out_shape=jax.ShapeDtypeStruct((S, D_MODEL), BF16), name=name, compiler_params=_cp())(yc, ys, yb, g3)


def outnorm_bwd(dyn, yc, ys, yb, g3, l, name, tm=512):
    S = yc.shape[0]

    def body(d_ref, c_ref, s_ref, b_ref, g_ref, dc_ref, ds_ref, db_ref, dg_ref):
        _, vjp = jax.vjp(_outnorm, c_ref[...], s_ref[...], b_ref[...], g_ref[0])
        dc, ds, db, dg = vjp(d_ref[...])
        dc_ref[...] = dc
        ds_ref[...] = ds
        db_ref[...] = db

        @pl.when(pl.program_id(0) == 0)
        def _():
            dg_ref[...] = jnp.zeros_like(dg_ref)

        dg_ref[...] += dg

    sc = pl.BlockSpec((tm, CONV_W), lambda i: (i, 0))
    ss = pl.BlockSpec((tm, SG_W), lambda i: (i, 0))
    sb = pl.BlockSpec((tm, SB_W), lambda i: (i, 0))
    return pl.pallas_call(
        body, grid=(S // tm,),
        in_specs=[pl.BlockSpec((tm, D_MODEL), lambda i: (i, 0)), sc, ss, sb,
                  pl.BlockSpec((1, 1, D_MODEL), lambda i: (l, 0, 0))],
        out_specs=[sc, ss, sb, pl.BlockSpec((1, D_MODEL), lambda i: (0, 0))],
        out_shape=[jax.ShapeDtypeStruct((S, CONV_W), F32), jax.ShapeDtypeStruct((S, SG_W), F32),
                   jax.ShapeDtypeStruct((S, SB_W), F32), jax.ShapeDtypeStruct((1, D_MODEL), F32)],
        name=name, compiler_params=_cp())(dyn, yc, ys, yb, g3)


def loss_head(y, t, name, tm=512):
    S, N = y.shape

    def body(y_ref, t_ref, dy_ref, l_ref):
        e = y_ref[...] - t_ref[...]
        dy_ref[...] = e * (1.0 / N)

        @pl.when(pl.program_id(0) == 0)
        def _():
            l_ref[...] = jnp.zeros_like(l_ref)

        l_ref[...] += (0.5 / N) * jnp.sum(jnp.sum(e * e, axis=-1, keepdims=True), axis=0, keepdims=True)

    row = pl.BlockSpec((tm, N), lambda i: (i, 0))
    return pl.pallas_call(
        body, grid=(S // tm,), in_specs=[row, row],
        out_specs=[row, pl.BlockSpec((1, 1), lambda i: (0, 0))],
        out_shape=[jax.ShapeDtypeStruct((S, N), F32), jax.ShapeDtypeStruct((1, 1), F32)],
        name=name, compiler_params=_cp())(y, t)


def _conv_taps(a, w_ref, T):
    acc = jnp.zeros((T, CONV_W), F32)
    for j in range(CONV_K):
        sh = CONV_K - 1 - j
        r = a if sh == 0 else pltpu.roll(a, sh, 0)
        acc = acc + r[HALO:HALO + T] * w_ref[pl.ds(j, 1), :]
    return acc


def conv_fwd(proj, cw, cb, lg, lb, l, name, T=256):
    S = proj.shape[0]
    nt = S // T

    def body(p_ref, w_ref, cb_ref, lg_ref, lb_ref, y_ref, hc_s):
        hc_s[0:HALO, :] = jnp.zeros((HALO, CONV_W), F32)

        def fill(i, _):
            r0 = pl.multiple_of(i * T, T)
            pc = p_ref[pl.ds(r0, T), :]
            hc_s[pl.ds(r0 + HALO, T), :] = _glu(pc[:, :CONV_W], pc[:, CONV_W:])
            return 0

        lax.fori_loop(0, nt, fill, 0)

        def tile(i, _):
            r0 = pl.multiple_of(i * T, T)
            c = _conv_taps(hc_s[pl.ds(r0, T + HALO), :], w_ref.at[0], T) + cb_ref[0]
            y_ref[pl.ds(r0, T), :] = _ln_silu(c, lg_ref[0], lb_ref[0])
            return 0

        lax.fori_loop(0, nt, tile, 0)

    vec = pl.BlockSpec((1, 1, CONV_W), lambda i: (l, 0, 0))
    return pl.pallas_call(
        body, grid=(1,),
        in_specs=[pl.BlockSpec((S, 2 * CONV_W), lambda i: (0, 0)), pl.BlockSpec((1, 32, CONV_W), lambda i: (l, 0, 0)),
                  vec, vec, vec],
        out_specs=pl.BlockSpec((S, CONV_W), lambda i: (0, 0)),
        out_shape=jax.ShapeDtypeStruct((S, CONV_W), F32),
        scratch_shapes=[pltpu.VMEM((S + HALO, CONV_W), F32)], name=name, compiler_params=_cp())(proj, cw, cb, lg, lb)


def conv_bwd(proj, dy, cw, cb, lg, lb, l, name, T=256):
    S = proj.shape[0]
    nt = S // T

    def body(p_ref, dy_ref, w_ref, cb_ref, lg_ref, lb_ref, dp_ref, dw_ref, dcb_ref, dlg_ref, dlb_ref, hc_s, dc_s):
        hc_s[0:HALO, :] = jnp.zeros((HALO, CONV_W), F32)
        dc_s[S:S + HALO, :] = jnp.zeros((HALO, CONV_W), F32)
        dw_ref[...] = jnp.zeros_like(dw_ref)
        dcb_ref[...] = jnp.zeros_like(dcb_ref)
        dlg_ref[...] = jnp.zeros_like(dlg_ref)
        dlb_ref[...] = jnp.zeros_like(dlb_ref)

        def fill(i, _):
            r0 = pl.multiple_of(i * T, T)
            pc = p_ref[pl.ds(r0, T), :]
            hc_s[pl.ds(r0 + HALO, T), :] = _glu(pc[:, :CONV_W], pc[:, CONV_W:])
            return 0

        lax.fori_loop(0, nt, fill, 0)

        def tile(i, _):
            r0 = pl.multiple_of(i * T, T)
            a = hc_s[pl.ds(r0, T + HALO), :]
            c = _conv_taps(a, w_ref.at[0], T) + cb_ref[0]
            _, vjp = jax.vjp(_ln_silu, c, lg_ref[0], lb_ref[0])
            dc, dlg, dlb = vjp(dy_ref[pl.ds(r0, T), :])
            dc_s[pl.ds(r0, T), :] = dc
            dcb_ref[...] += jnp.sum(dc, axis=0, keepdims=True)
            dlg_ref[...] += dlg
            dlb_ref[...] += dlb
            for j in range(CONV_K):
                sh = CONV_K - 1 - j
                r = a if sh == 0 else pltpu.roll(a, sh, 0)
                dw_ref[pl.ds(j, 1), :] += jnp.sum(dc * r[HALO:HALO + T], axis=0, keepdims=True)
            return 0

        lax.fori_loop(0, nt, tile, 0)

        def back(i, _):
            r0 = pl.multiple_of(i * T, T)
            de = dc_s[pl.ds(r0, T + HALO), :]
            n = T + HALO
            dh = jnp.zeros((T, CONV_W), F32)
            for j in range(CONV_K):
                sh = CONV_K - 1 - j
                r = de if sh == 0 else pltpu.roll(de, n - sh, 0)
                dh = dh + r[0:T] * w_ref[0, pl.ds(j, 1), :]
            pc = p_ref[pl.ds(r0, T), :]
            _, vjp = jax.vjp(_glu, pc[:, :CONV_W], pc[:, CONV_W:])
            dval, dgate = vjp(dh)
            dp_ref[pl.ds(r0, T), :] = jnp.concatenate([dval, dgate], axis=-1).astype(BF16)
            return 0

        lax.fori_loop(0, nt, back, 0)

    vec = pl.BlockSpec((1, 1, CONV_W), lambda i: (l, 0, 0))
    ovec = pl.BlockSpec((1, CONV_W), lambda i: (0, 0))
    vsh = jax.ShapeDtypeStruct((1, CONV_W), F32)
    return pl.pallas_call(
        body, grid=(1,),
        in_specs=[pl.BlockSpec((S, 2 * CONV_W), lambda i: (0, 0)), pl.BlockSpec((S, CONV_W), lambda i: (0, 0)),
                  pl.BlockSpec((1, 32, CONV_W), lambda i: (l, 0, 0)), vec, vec, vec],
        out_specs=[pl.BlockSpec((S, 2 * CONV_W), lambda i: (0, 0)), pl.BlockSpec((32, CONV_W), lambda i: (0, 0)),
                   ovec, ovec, ovec],
        out_shape=[jax.ShapeDtypeStruct((S, 2 * CONV_W), BF16), jax.ShapeDtypeStruct((32, CONV_W), F32), vsh, vsh, vsh],
        scratch_shapes=[pltpu.VMEM((S + HALO, CONV_W), F32), pltpu.VMEM((S + HALO, CONV_W), F32)],
        name=name, compiler_params=_cp())(proj, dy, cw, cb, lg, lb)


def _sg_mix(wt_ref, v, bt):
    slabs = []
    for s in range(2):
        vs = v[:, 128 * s:128 * (s + 1)]
        seg = _iota(vs.shape, 1) < HEAD_DIM
        p0 = _dot(wt_ref[2 * s], vs, "nn") + bt[:, 2 * s:2 * s + 1]
        p1 = _dot(wt_ref[2 * s + 1], vs, "nn") + bt[:, 2 * s + 1:2 * s + 2]
        slabs.append(jnp.where(seg, p0, p1))
    return jnp.concatenate(slabs, axis=-1)


def sg_fwd(proj, lg, lb, wt, bt, l, name):
    S = proj.shape[0]

    def body(p_ref, lg_ref, lb_ref, w_ref, b_ref, y_ref):
        u, v = _sg_pre(p_ref[...], lg_ref[0], lb_ref[0])
        y_ref[...] = u * _sg_mix(w_ref.at[0], v, b_ref[0])

    vec = pl.BlockSpec((1, 1, SG_W), lambda i: (l, 0, 0))
    return pl.pallas_call(
        body, grid=(S // CHUNK,),
        in_specs=[pl.BlockSpec((CHUNK, 2 * SG_W), lambda i: (i, 1)), vec, vec,
                  pl.BlockSpec((1, 4, CHUNK, CHUNK), lambda i: (l, 0, 0, 0)),
                  pl.BlockSpec((1, CHUNK, 4), lambda i: (l, 0, 0))],
        out_specs=pl.BlockSpec((CHUNK, SG_W), lambda i: (i, 0)),
        out_shape=jax.ShapeDtypeStruct((S, SG_W), F32), name=name, compiler_params=_cp())(proj, lg, lb, wt, bt)


def sg_bwd(proj, dy, lg, lb, wt, wtt, bt, l, name):
    S = proj.shape[0]

    def body(p_ref, dy_ref, lg_ref, lb_ref, w_ref, wt_ref, b_ref, dp_ref, dlg_ref, dlb_ref, dw_ref, db_ref):
        @pl.when(pl.program_id(0) == 0)
        def _():
            dlg_ref[...] = jnp.zeros_like(dlg_ref)
            dlb_ref[...] = jnp.zeros_like(dlb_ref)
            dw_ref[...] = jnp.zeros_like(dw_ref)
            db_ref[...] = jnp.zeros_like(db_ref)

        (u, v), vjp = jax.vjp(_sg_pre, p_ref[...], lg_ref[0], lb_ref[0])
        dy = dy_ref[...]
        mixed = _sg_mix(w_ref.at[0], v, b_ref[0])
        du = dy * mixed
        dm = dy * u
        tril = _iota((CHUNK, CHUNK), 1) <= _iota((CHUNK, CHUNK), 0)
        dvs = []
        for s in range(2):
            dms = dm[:, 128 * s:128 * (s + 1)]
            vs = v[:, 128 * s:128 * (s + 1)]
            seg = _iota(dms.shape, 1) < HEAD_DIM
            halves = (jnp.where(seg, dms, 0.0), jnp.where(seg, 0.0, dms))
            dv_h = []
            for e in range(2):
                h = 2 * s + e
                db_ref[:, h:h + 1] += jnp.sum(halves[e], axis=-1, keepdims=True)
                dw_ref[h] += jnp.where(tril, _dot(halves[e], vs, "nt"), 0.0)
                dv_h.append(_dot(wt_ref[0, h], halves[e], "nn"))
            dvs.append(dv_h[0] + dv_h[1])
        dp, dlg, dlb = vjp((du, jnp.concatenate(dvs, axis=-1)))
        dp_ref[...] = dp.astype(BF16)
        dlg_ref[...] += dlg
        dlb_ref[...] += dlb

    vec = pl.BlockSpec((1, 1, SG_W), lambda i: (l, 0, 0))
    wsp = pl.BlockSpec((1, 4, CHUNK, CHUNK), lambda i: (l, 0, 0, 0))
    ovec = pl.BlockSpec((1, SG_W), lambda i: (0, 0))
    return pl.pallas_call(
        body, grid=(S // CHUNK,),
        in_specs=[pl.BlockSpec((CHUNK, 2 * SG_W), lambda i: (i, 1)), pl.BlockSpec((CHUNK, SG_W), lambda i: (i, 0)),
                  vec, vec, wsp, wsp, pl.BlockSpec((1, CHUNK, 4), lambda i: (l, 0, 0))],
        out_specs=[pl.BlockSpec((CHUNK, 2 * SG_W), lambda i: (i, 0)), ovec, ovec,
                   pl.BlockSpec((4, CHUNK, CHUNK), lambda i: (0, 0, 0)), pl.BlockSpec((CHUNK, 4), lambda i: (0, 0))],
        out_shape=[jax.ShapeDtypeStruct((S, 2 * SG_W), BF16), jax.ShapeDtypeStruct((1, SG_W), F32),
                   jax.ShapeDtypeStruct((1, SG_W), F32), jax.ShapeDtypeStruct((4, CHUNK, CHUNK), F32),
                   jax.ShapeDtypeStruct((CHUNK, 4), F32)],
        name=name, compiler_params=_cp())(proj, dy, lg, lb, wt, wtt, bt)


def qkv_fwd(proj, gq, gk, l, name, tm=512):
    S = proj.shape[0]

    def body(pq_ref, pk_ref, pv_ref, gq_ref, gk_ref, o_ref):
        o_ref[...] = _qkv_fn(pq_ref[...], pk_ref[...], pv_ref[...], gq_ref[0], gk_ref[0]).astype(BF16)

    vec = pl.BlockSpec((1, 1, 128), lambda i: (l, 0, 0))
    qb = OFF_SB // SB_W
    return pl.pallas_call(
        body, grid=(S // tm,),
        in_specs=[pl.BlockSpec((tm, SB_W), lambda i: (i, qb)), pl.BlockSpec((tm, SB_W), lambda i: (i, qb + 1)),
                  pl.BlockSpec((tm, SB_W), lambda i: (i, qb + 2)), vec, vec],
        out_specs=pl.BlockSpec((tm, 3 * SB_W), lambda i: (i, 0)),
        out_shape=jax.ShapeDtypeStruct((S, 3 * SB_W), BF16), name=name, compiler_params=_cp())(proj, proj, proj, gq, gk)


def qkv_bwd(proj, dq, dk, dv, gq, gk, l, name, tm=512):
    S = proj.shape[0]

    def body(pq_ref, pk_ref, pv_ref, dq_ref, dk_ref, dv_ref, gq_ref, gk_ref, dp_ref, dgq_ref, dgk_ref):
        _, vjp = jax.vjp(_qkv_fn, pq_ref[...], pk_ref[...], pv_ref[...], gq_ref[0], gk_ref[0])
        dpq, dpk, dpv, dgq, dgk = vjp(jnp.concatenate([dq_ref[...], dk_ref[...], dv_ref[...]], axis=-1))
        dp_ref[...] = jnp.concatenate([dpq, dpk, dpv], axis=-1).astype(BF16)

        @pl.when(pl.program_id(0) == 0)
        def _():
            dgq_ref[...] = jnp.zeros_like(dgq_ref)
            dgk_ref[...] = jnp.zeros_like(dgk_ref)

        dgq_ref[...] += dgq
        dgk_ref[...] += dgk

    vec = pl.BlockSpec((1, 1, 128), lambda i: (l, 0, 0))
    part = pl.BlockSpec((tm, SB_W), lambda i: (i, 0))
    ovec = pl.BlockSpec((1, 128), lambda i: (0, 0))
    qb = OFF_SB // SB_W
    return pl.pallas_call(
        body, grid=(S // tm,),
        in_specs=[pl.BlockSpec((tm, SB_W), lambda i: (i, qb)), pl.BlockSpec((tm, SB_W), lambda i: (i, qb + 1)),
                  pl.BlockSpec((tm, SB_W), lambda i: (i, qb + 2)), part, part, part, vec, vec],
        out_specs=[pl.BlockSpec((tm, 3 * SB_W), lambda i: (i, 0)), ovec, ovec],
        out_shape=[jax.ShapeDtypeStruct((S, 3 * SB_W), BF16), jax.ShapeDtypeStruct((1, 128), F32),
                   jax.ShapeDtypeStruct((1, 128), F32)],
        name=name, compiler_params=_cp())(proj, proj, proj, dq, dk, dv, gq, gk)


QSUB = 4
QT = QSUB * CHUNK


def _sb_consts():
    row = _iota((CHUNK, CHUNK), 0)
    col = _iota((CHUNK, CHUNK), 1)
    ones = jnp.ones((CHUNK, CHUNK), MXU_DT)
    m_gt = jnp.concatenate([(row > col).astype(MXU_DT), ones], axis=1)
    m_lt = jnp.concatenate([(row < col).astype(MXU_DT), ones], axis=1)
    return col < HEAD_DIM, col - row, m_gt, m_lt


def _split_heads(x, seg):
    z = jnp.zeros_like(x)
    return (jnp.where(seg, x, z), jnp.where(seg, z, x))


class Exchange:
    def __init__(self, ins, bufs, outs, n, build):
        self.ins, self.bufs, self.outs, self.n, self.build = list(ins), list(bufs), list(outs), n, build

    def split(self, refs, n_in, n_out):
        a, b, o = len(self.ins), len(self.bufs), len(self.outs)
        main_in = refs[:n_in]
        c_in = refs[n_in:n_in + a]
        rest = refs[n_in + a + b:]
        main_out = rest[:n_out]
        c_buf = rest[n_out:n_out + b]
        c_out = rest[n_out + b:n_out + b + o]
        send, recv = rest[n_out + b + o:]
        return main_in, main_out, lambda: self.build(c_in, c_buf, c_out, send, recv)


def _with_exchange(ex, n_in, n_out, in_specs, out_specs, out_shape):
    if ex is None:
        return in_specs, out_specs, out_shape, {}, [], []
    a, b = len(ex.ins), len(ex.bufs)
    in_specs = list(in_specs) + [ANY] * (a + b)
    out_specs = list(out_specs) + [ANY] * (b + len(ex.outs))
    out_shape = list(out_shape) + [jax.ShapeDtypeStruct(x.shape, x.dtype) for x in ex.bufs] + list(ex.outs)
    aliases = {n_in + a + i: n_out + i for i in range(b)}
    scratch = [pltpu.SemaphoreType.DMA((ex.n,)), pltpu.SemaphoreType.DMA((ex.n,))]
    return in_specs, out_specs, out_shape, aliases, scratch, ex.ins + ex.bufs


def attn_fwd(qkv, name, ex=None):
    S = qkv.shape[0]
    npair = SB_W // 128
    nq = S // QT

    def body(*refs):
        if ex is None:
            (q_ref, k_ref, v_ref), (o_ref, rt_ref), copies = refs[:3], refs[3:5], None
        else:
            (q_ref, k_ref, v_ref), (o_ref, rt_ref), copies = ex.split(refs, 3, 2)

            @pl.when(jnp.logical_and(pl.program_id(0) == 0, pl.program_id(1) == 0))
            def _():
                for cp in copies():
                    cp.start()

        qi = pl.program_id(1)
        seg, dcol, m_gt, _ = _sb_consts()
        qh = [_split_heads(q_ref[a * CHUNK:(a + 1) * CHUNK, :], seg) for a in range(QSUB)]
        causal = dcol < 0

        def step(kb, carry, diag):
            off = pl.multiple_of(kb * CHUNK, CHUNK)
            kblk = k_ref[pl.ds(off, CHUNK), :]
            vblk = v_ref[pl.ds(off, CHUNK), :]
            chains = [(a, h) for a in range(QSUB) for h in range(2) if diag is None or a >= diag]
            idx = [2 * a + h for a, h in chains]
            z = [_dot(qh[a][h], kblk, "nt") for a, h in chains]
            sp = [_softplus(zc) for zc in z]
            lnb = [jnp.where(causal, -s, 0.0) if a == diag else -s for (a, h), s in zip(chains, sp)]
            cs = [_cumdot(x, m_gt) for x in lnb]
            att = []
            for n, (a, h) in enumerate(chains):
                e = jnp.exp(z[n] - sp[n] + cs[n][:, :CHUNK] + carry[2 * idx[n] + 1])
                att.append(jnp.where(causal, e, 0.0) if a == diag else e)
            pv = [_dot(x, vblk, "nn") for x in att]
            new = list(carry)
            for n, c in enumerate(idx):
                new[2 * c] = carry[2 * c] + pv[n]
                new[2 * c + 1] = carry[2 * c + 1] + cs[n][:, CHUNK:]
            return tuple(new)

        z0 = jnp.zeros((CHUNK, CHUNK), F32)
        res = (z0,) * (4 * QSUB)
        for j in reversed(range(QSUB)):
            res = step(QSUB * qi + j, res, j)
        res = lax.fori_loop(0, QSUB * qi, lambda it, c: step(QSUB * qi - 1 - it, c, None), res)
        for a in range(QSUB):
            rows = slice(a * CHUNK, (a + 1) * CHUNK)
            o_ref[rows, :] = jnp.where(seg, res[4 * a], res[4 * a + 2])
            rt_ref[rows, :] = jnp.concatenate([res[4 * a + 1], res[4 * a + 3]], axis=1)

        if ex is not None:
            @pl.when(jnp.logical_and(pl.program_id(0) == npair - 1, pl.program_id(1) == nq - 1))
            def _():
                for cp in copies():
                    cp.wait()

    in_specs, out_specs, out_shape, aliases, scratch, extra = _with_exchange(
        ex, 3, 2,
        [pl.BlockSpec((QT, 128), lambda p, i: (i, p)), pl.BlockSpec((S, 128), lambda p, i: (0, npair + p)),
         pl.BlockSpec((S, 128), lambda p, i: (0, 2 * npair + p))],
        [pl.BlockSpec((QT, 128), lambda p, i: (i, p)), pl.BlockSpec((QT, 256), lambda p, i: (i, p))],
        [jax.ShapeDtypeStruct((S, SB_W), F32), jax.ShapeDtypeStruct((S, 2 * SB_W), F32)])
    return pl.pallas_call(
        body, grid=(npair, nq), in_specs=in_specs, out_specs=out_specs, out_shape=out_shape,
        input_output_aliases=aliases, scratch_shapes=scratch, name=name, compiler_params=_cp())(qkv, qkv, qkv, *extra)


def attn_bwd(qkv, rt, do, name, ex=None):
    S = qkv.shape[0]
    npair = SB_W // 128
    nq = S // QT

    def body(*refs):
        if ex is None:
            (q_ref, k_ref, v_ref, rt_ref, do_ref), (dq_ref, dk_ref, dv_ref), copies = refs[:5], refs[5:8], None
        else:
            (q_ref, k_ref, v_ref, rt_ref, do_ref), (dq_ref, dk_ref, dv_ref), copies = ex.split(refs, 5, 3)

            @pl.when(jnp.logical_and(pl.program_id(0) == 0, pl.program_id(1) == 0))
            def _():
                for cp in copies():
                    cp.start()

        qi = pl.program_id(1)

        @pl.when(qi == 0)
        def _():
            dk_ref[...] = jnp.zeros_like(dk_ref)
            dv_ref[...] = jnp.zeros_like(dv_ref)

        seg, dcol, m_gt, m_lt = _sb_consts()
        qh, doh, rtot = [], [], []
        for a in range(QSUB):
            rows = slice(a * CHUNK, (a + 1) * CHUNK)
            qh.append(_split_heads(q_ref[rows, :], seg))
            doh.append(_split_heads(do_ref[rows, :], seg))
            rtot.append((rt_ref[rows, 0:CHUNK], rt_ref[rows, CHUNK:2 * CHUNK]))
        causal = dcol < 0

        def step(kb, carry, diag):
            off = pl.multiple_of(kb * CHUNK, CHUNK)
            kblk = k_ref[pl.ds(off, CHUNK), :]
            vblk = v_ref[pl.ds(off, CHUNK), :]
            kh = _split_heads(kblk, seg)
            chains = [(a, h) for a in range(QSUB) for h in range(2) if diag is None or a >= diag]
            idx = [2 * a + h for a, h in chains]
            nc = len(chains)
            z = [_dot(qh[a][h], kblk, "nt") for a, h in chains]
            da = [_dot(doh[a][h], vblk, "nt") for a, h in chains]
            sp = [_softplus(zc) for zc in z]
            lnb = [jnp.where(causal, -s, 0.0) if a == diag else -s for (a, h), s in zip(chains, sp)]
            cs = [_cumdot(x, m_gt) for x in lnb]
            lc = [carry[3 * idx[n] + 1] + cs[n][:, CHUNK:] for n in range(nc)]
            att = []
            for n, (a, h) in enumerate(chains):
                e = jnp.exp(z[n] - sp[n] + cs[n][:, :CHUNK] + (rtot[a][h] - lc[n]))
                att.append(jnp.where(causal, e, 0.0) if a == diag else e)
            g = [da[n] * att[n] for n in range(nc)]
            cg = [_cumdot(x, m_lt) for x in g]
            dz = []
            for n, (a, h) in enumerate(chains):
                sig = jnp.exp(z[n] - sp[n])
                pre = carry[3 * idx[n] + 2] + cg[n][:, :CHUNK]
                d = g[n] * (1.0 - sig) - pre * sig
                dz.append(jnp.where(causal, d, 0.0) if a == diag else d)
            dqc = [_dot(dz[n], kh[h], "nn") for n, (a, h) in enumerate(chains)]
            dkc = [_dot(dz[n], qh[a][h], "tn") for n, (a, h) in enumerate(chains)]
            dvc = [_dot(att[n], doh[a][h], "tn") for n, (a, h) in enumerate(chains)]
            dk_ref[pl.ds(off, CHUNK), :] += functools.reduce(lambda x, y: x + y, dkc)
            dv_ref[pl.ds(off, CHUNK), :] += functools.reduce(lambda x, y: x + y, dvc)
            new = list(carry)
            for n, c in enumerate(idx):
                new[3 * c] = carry[3 * c] + dqc[n]
                new[3 * c + 1] = lc[n]
                new[3 * c + 2] = carry[3 * c + 2] + cg[n][:, CHUNK:]
            return tuple(new)

        z0 = jnp.zeros((CHUNK, 128), F32)
        res = lax.fori_loop(0, QSUB * qi, lambda kb, c: step(kb, c, None), (z0,) * (6 * QSUB))
        for j in range(QSUB):
            res = step(QSUB * qi + j, res, j)
        for a in range(QSUB):
            dq_ref[a * CHUNK:(a + 1) * CHUNK, :] = res[6 * a] + res[6 * a + 3]

        if ex is not None:
            @pl.when(jnp.logical_and(pl.program_id(0) == npair - 1, pl.program_id(1) == nq - 1))
            def _():
                for cp in copies():
                    cp.wait()

    blk = pl.BlockSpec((QT, 128), lambda p, i: (i, p))
    full = pl.BlockSpec((S, 128), lambda p, i: (0, p))
    sh = jax.ShapeDtypeStruct((S, SB_W), F32)
    in_specs, out_specs, out_shape, aliases, scratch, extra = _with_exchange(
        ex, 5, 3,
        [blk, pl.BlockSpec((S, 128), lambda p, i: (0, npair + p)), pl.BlockSpec((S, 128), lambda p, i: (0, 2 * npair + p)),
         pl.BlockSpec((QT, 256), lambda p, i: (i, p)), blk],
        [blk, full, full], [sh, sh, sh])
    return pl.pallas_call(
        body, grid=(npair, nq), in_specs=in_specs, out_specs=out_specs, out_shape=out_shape,
        input_output_aliases=aliases, scratch_shapes=scratch, name=name,
        compiler_params=_cp())(qkv, qkv, qkv, rt, do, *extra)


def _adamw_math(w, g, m, v):
    m = ADAM_B1 * m + (1.0 - ADAM_B1) * g
    v = ADAM_B2 * v + (1.0 - ADAM_B2) * (g * g)
    m_hat = m / BC1
    v_hat = v / BC2
    delta = -ADAM_LR * (m_hat / (jnp.sqrt(v_hat) + ADAM_EPS) + ADAM_WD * w)
    return delta, m, v


def adamw_layer(w4, m4, v4, g, outs, l, name, tr):
    L, R, C = w4.shape
    n_alias = 0 if outs is None else 4

    def body(*refs):
        w_ref, m_ref, v_ref, g_ref = refs[:4]
        go_ref, d_ref, mo_ref, vo_ref = refs[4 + n_alias:]
        g = g_ref[...]
        d, m, v = _adamw_math(w_ref[0], g, m_ref[0], v_ref[0])
        go_ref[0] = g
        d_ref[0] = d
        mo_ref[0] = m
        vo_ref[0] = v

    st = pl.BlockSpec((1, tr, C), lambda i: (l, i, 0))
    sh = jax.ShapeDtypeStruct((L, R, C), F32)
    return pl.pallas_call(
        body, grid=(R // tr,),
        in_specs=[st, st, st, pl.BlockSpec((tr, C), lambda i: (i, 0))] + [ANY] * n_alias,
        out_specs=[st, st, st, st], out_shape=[sh, sh, sh, sh],
        input_output_aliases={4 + i: i for i in range(n_alias)}, name=name,
        compiler_params=_cp())(w4, m4, v4, g, *(outs or ()))


def adamw_flat(w, m, v, g, name):
    R, C = w.shape

    def body(w_ref, m_ref, v_ref, g_ref, d_ref, mo_ref, vo_ref):
        d, m2, v2 = _adamw_math(w_ref[...], g_ref[...], m_ref[...], v_ref[...])
        d_ref[...] = d
        mo_ref[...] = m2
        vo_ref[...] = v2

    full = pl.BlockSpec((R, C), lambda i: (0, 0))
    sh = jax.ShapeDtypeStruct((R, C), F32)
    return pl.pallas_call(body, grid=(1,), in_specs=[full] * 4, out_specs=[full] * 3, out_shape=[sh] * 3,
                          name=name, compiler_params=_cp())(w, m, v, g)


def add_half(g, l1, c, name, tr):
    nk, R, C = g.shape
    Rh = R // 2
    nt = Rh // tr

    def body(c_ref, g_ref, l_ref, o_ref):
        o_ref[...] = (g_ref[...].astype(F32) + l_ref[...].astype(F32)).astype(o_ref.dtype)

    gs = pltpu.PrefetchScalarGridSpec(
        num_scalar_prefetch=1, grid=(nk, nt),
        in_specs=[pl.BlockSpec((1, tr, C), lambda k, t, c_ref: (k, c_ref[0] * nt + t, 0)),
                  pl.BlockSpec((1, tr, C), lambda k, t, c_ref: (k, t, 0))],
        out_specs=pl.BlockSpec((1, tr, C), lambda k, t, c_ref: (k, t, 0)))
    return pl.pallas_call(body, grid_spec=gs, out_shape=jax.ShapeDtypeStruct((nk, Rh, C), g.dtype), name=name,
                          compiler_params=_cp())(c, g, l1)


def sum_chips(p, l2, kc, name, tr):
    nk, Rh, C = p.shape

    def body(k_ref, p_ref, a_ref, b_ref, c_ref, o_ref):
        o_ref[0] = ((p_ref[0].astype(F32) + a_ref[0].astype(F32)) + b_ref[0].astype(F32)) + c_ref[0].astype(F32)

    def slot(j):
        return pl.BlockSpec((1, tr, C), lambda t, k_ref: (j, t, 0))

    gs = pltpu.PrefetchScalarGridSpec(
        num_scalar_prefetch=1, grid=(Rh // tr,),
        in_specs=[pl.BlockSpec((1, tr, C), lambda t, k_ref: (k_ref[0], t, 0)), slot(0), slot(1), slot(2)],
        out_specs=pl.BlockSpec((1, tr, C), lambda t, k_ref: (k_ref[1], t, 0)))
    return pl.pallas_call(body, grid_spec=gs, out_shape=jax.ShapeDtypeStruct((2, Rh, C), F32), name=name,
                          compiler_params=_cp())(kc, p, l2, l2, l2)


def place_slot(a, idx, nslot, name, tr):
    R, C = a.shape

    def body(i_ref, a_ref, o_ref):
        o_ref[0] = a_ref[...]

    gs = pltpu.PrefetchScalarGridSpec(
        num_scalar_prefetch=1, grid=(R // tr,),
        in_specs=[pl.BlockSpec((tr, C), lambda t, i_ref: (t, 0))],
        out_specs=pl.BlockSpec((1, tr, C), lambda t, i_ref: (i_ref[0], t, 0)))
    return pl.pallas_call(body, grid_spec=gs, out_shape=jax.ShapeDtypeStruct((nslot, R, C), a.dtype), name=name,
                          compiler_params=_cp())(idx, a)


def _place():
    x, y, c = lax.axis_index("x"), lax.axis_index("y"), lax.axis_index("c")
    chips = [(1 - x, y), (x, 1 - y), (1 - x, 1 - y)]
    return x, y, c, chips


def _rcopy(src, dst, send_sems, recv_sems, k, to):
    return pltpu.make_async_remote_copy(src_ref=src, dst_ref=dst, send_sem=send_sems.at[k], recv_sem=recv_sems.at[k],
                                        device_id=to, device_id_type=MESH)


def gather_exchange(shards, bufs):
    def build(in_refs, buf_refs, out_refs, send_sems, recv_sems):
        x, y, c, chips = _place()
        cps = []
        for i, ref in enumerate(in_refs):
            rh = shards[i].shape[0] // 2
            for j, chip in enumerate(chips):
                rows = pl.ds(c * rh, rh)
                cps.append(_rcopy(ref.at[rows, :], buf_refs[i].at[2 * x + y].at[rows, :], send_sems, recv_sems,
                                  3 * i + j, (*chip, c)))
        return cps

    return Exchange(shards, bufs, [], 3 * len(shards), build)


def scatter_exchange(ps):
    def build(in_refs, buf_refs, out_refs, send_sems, recv_sems):
        x, y, c, chips = _place()
        return [_rcopy(ref.at[2 * chip[0] + chip[1]], out_refs[i].at[j], send_sems, recv_sems, 3 * i + j, (*chip, c))
                for i, ref in enumerate(in_refs) for j, chip in enumerate(chips)]

    outs = [jax.ShapeDtypeStruct((3,) + p.shape[1:], p.dtype) for p in ps]
    return Exchange(ps, [], outs, 3 * len(ps), build)


def run_exchange(ex, name):
    def body(*refs):
        _, _, copies = ex.split(refs, 0, 0)
        cps = copies()
        for cp in cps:
            cp.start()
        for cp in cps:
            cp.wait()

    in_specs, out_specs, out_shape, aliases, scratch, extra = _with_exchange(ex, 0, 0, [], [], [])
    return pl.pallas_call(body, in_specs=in_specs, out_specs=out_specs, out_shape=out_shape,
                          input_output_aliases=aliases, scratch_shapes=scratch, name=name)(*extra)


def gather_forward(bufs, name):
    n = len(bufs)

    def body(*refs):
        outs = refs[n:2 * n]
        send_sems, recv_sems = refs[2 * n:]
        x, y, c, chips = _place()
        cps = []
        for i in range(n):
            rh = bufs[i].shape[1] // 2
            for j, chip in enumerate(chips):
                landed = outs[i].at[2 * chip[0] + chip[1]].at[pl.ds(c * rh, rh), :]
                cps.append(_rcopy(landed, landed, send_sems, recv_sems, 3 * i + j, (x, y, 1 - c)))
        for cp in cps:
            cp.start()
        for cp in cps:
            cp.wait()

    return pl.pallas_call(
        body, in_specs=[ANY] * n, out_specs=[ANY] * n,
        out_shape=[jax.ShapeDtypeStruct(b.shape, b.dtype) for b in bufs],
        input_output_aliases={i: i for i in range(n)},
        scratch_shapes=[pltpu.SemaphoreType.DMA((3 * n,)), pltpu.SemaphoreType.DMA((3 * n,))], name=name)(*bufs)


def exchange_sibling_half(gs, name):
    n = len(gs)

    def body(*refs):
        ins, outs = refs[:n], refs[n:2 * n]
        send_sems, recv_sems = refs[2 * n:]
        x, y, c, _ = _place()
        cps = []
        for i in range(n):
            rh = gs[i].shape[1] // 2
            cp = pltpu.make_async_remote_copy(
                src_ref=ins[i].at[:, pl.ds((1 - c) * rh, rh), :], dst_ref=outs[i], send_sem=send_sems.at[i],
                recv_sem=recv_sems.at[i], device_id=(x, y, 1 - c), device_id_type=MESH)
            cp.start()
            cps.append(cp)
        for cp in cps:
            cp.wait()

    return pl.pallas_call(
        body, in_specs=[ANY] * n, out_specs=[ANY] * n,
        out_shape=[jax.ShapeDtypeStruct((g.shape[0], g.shape[1] // 2, g.shape[2]), g.dtype) for g in gs],
        scratch_shapes=[pltpu.SemaphoreType.DMA((n,)), pltpu.SemaphoreType.DMA((n,))],
        name=name)(*gs)


def join_sibling_halves(fs, name):
    n = len(fs)

    def body(*refs):
        outs = refs[n:2 * n]
        send_sems, recv_sems = refs[2 * n:]
        x, y, c, _ = _place()
        cps = [_rcopy(outs[i].at[c], outs[i].at[c], send_sems, recv_sems, i, (x, y, 1 - c)) for i in range(n)]
        for cp in cps:
            cp.start()
        for cp in cps:
            cp.wait()

    return pl.pallas_call(
        body, in_specs=[ANY] * n, out_specs=[ANY] * n,
        out_shape=[jax.ShapeDtypeStruct(f.shape, f.dtype) for f in fs],
        input_output_aliases={i: i for i in range(n)},
        scratch_shapes=[pltpu.SemaphoreType.DMA((n,)), pltpu.SemaphoreType.DMA((n,))], name=name)(*fs)


def all_reduce_small(v, name):
    R, C = v.shape

    def body(v_ref, sum_ref, all_ref, send_sems, recv_sems, local_sem):
        x, y, c, chips = _place()
        me, sib = (x, y, c), (x, y, 1 - c)

        def slab(px, py, pc):
            return all_ref.at[4 * px + 2 * py + pc]

        def copy(k, block, to, src=None):
            return pltpu.make_async_remote_copy(
                src_ref=slab(*block) if src is None else src, dst_ref=slab(*block), send_sem=send_sems.at[k],
                recv_sem=recv_sems.at[k], device_id=to, device_id_type=MESH)

        mine = pltpu.make_async_copy(v_ref, slab(*me), local_sem)
        mine.start()
        first = [copy(0, me, sib, src=v_ref)] + [copy(1 + j, me, (*chip, c), src=v_ref) for j, chip in enumerate(chips)]
        for cp in first:
            cp.start()
        passed = [copy(4 + j, (*chip, c), sib) for j, chip in enumerate(chips)]
        for j, chip in enumerate(chips):
            copy(1 + j, (*chip, c), me).wait_recv()
            passed[j].start()
        copy(0, sib, me).wait_recv()
        for j, chip in enumerate(chips):
            copy(4 + j, (*chip, 1 - c), me).wait_recv()
        for cp in first + passed:
            cp.wait_send()
        mine.wait()
        acc = all_ref[0]
        for d in range(1, 8):
            acc = acc + all_ref[d]
        sum_ref[...] = acc

    vm = pl.BlockSpec(memory_space=pltpu.VMEM)
    return pl.pallas_call(
        body, in_specs=[vm], out_specs=[vm, vm],
        out_shape=[jax.ShapeDtypeStruct((R, C), F32), jax.ShapeDtypeStruct((8, R, C), F32)],
        scratch_shapes=[pltpu.SemaphoreType.DMA((7,)), pltpu.SemaphoreType.DMA((7,)), pltpu.SemaphoreType.DMA],
        name=name, compiler_params=_cp())(v)[0]


SMALL = ("mix_norm_g", "conv_w", "conv_b", "conv_ln_g", "conv_ln_b", "sg_ln_g", "sg_ln_b", "sg_w", "sg_b",
         "q_norm_g", "k_norm_g", "out_norm_g", "ffn_norm_g")
BIG = ("w_in", "w_out", "w_gate_up", "w_down")
WEIGHTS = ("mix_norm_g", "w_in", "conv_w", "conv_b", "conv_ln_g", "conv_ln_b", "sg_ln_g", "sg_ln_b", "sg_w", "sg_b",
           "q_norm_g", "k_norm_g", "out_norm_g", "w_out", "ffn_norm_g", "w_gate_up", "w_down")
ADAM_ROWS = {"w_in": 256, "w_out": 128, "w_gate_up": 128, "w_down": 176}
RS_ROWS = {"w_in": 256, "w_out": 128, "w_gate_up": 256, "w_down": 176}
AG_ROWS = {"w_in": 512, "w_out": 256, "w_gate_up": 512, "w_down": 352}


def _pack(parts):
    flat = jnp.concatenate([p.reshape(-1) for p in parts])
    n = flat.shape[0]
    rows = -(-n // (8 * 128)) * 8
    return jnp.pad(flat, (0, rows * 128 - n)).reshape(rows, 128)


def _unpack(buf, shapes):
    flat = buf.reshape(-1)
    out, off = [], 0
    for s in shapes:
        n = 1
        for d in s:
            n *= d
        out.append(flat[off:off + n].reshape(s))
        off += n
    return out


def layer_forward(x, l, P, W, ex=None):
    sv = {"x": x}
    sv["h"] = rms_fwd(x, P["mix_norm_g"], l, f"rms_mix_{l}")
    sv["proj"] = mm_colblk(sv["h"], W["w_in"], F32, f"mm_in_{l}")
    sv["yc"] = conv_fwd(sv["proj"], P["conv_w"], P["conv_b"], P["conv_ln_g"], P["conv_ln_b"], l, f"conv_fwd_{l}")
    sv["ys"] = sg_fwd(sv["proj"], P["sg_ln_g"], P["sg_ln_b"], P["sg_wt"], P["sg_bt"], l, f"sg_fwd_{l}")
    sv["qkv"] = qkv_fwd(sv["proj"], P["q_norm_g"], P["k_norm_g"], l, f"qkv_fwd_{l}")
    sv["yb"], sv["rt"], *moved = attn_fwd(sv["qkv"], f"attn_fwd_{l}", ex)
    sv["yn"] = outnorm_fwd(sv["yc"], sv["ys"], sv["yb"], P["out_norm_g"], l, f"outnorm_fwd_{l}")
    sv["x1"] = mm_res(sv["yn"], W["w_out"].reshape(D_MODEL, D_MODEL), x, f"mm_out_{l}")
    sv["h2"] = rms_fwd(sv["x1"], P["ffn_norm_g"], l, f"rms_ffn_{l}")
    sv["g"], sv["u"], sv["act"] = ffn_up(sv["h2"], W["w_gate_up"], f"ffn_up_{l}")
    x2 = mm_res(sv["act"], W["w_down"].reshape(FFN, D_MODEL), sv["x1"], f"mm_down_{l}")
    return x2, sv, moved


def layer_backward(dx2, l, P, W, sv, ex=None):
    gb, gs = {}, {}
    wdown = W["w_down"].reshape(FFN, D_MODEL)
    dgu = ffn_down_bwd(dx2, wdown, sv["g"], sv["u"], f"ffn_down_bwd_{l}")
    gb["w_down"] = mm_wgrad(sv["act"], dx2, 1408, 512, f"wgrad_down_{l}", False).reshape(N_CHIP, FFN // N_CHIP, D_MODEL)
    dh2 = mm_dgrad_colblk(dgu, W["w_gate_up"], f"dgrad_gu_{l}")
    gb["w_gate_up"] = mm_wgrad(sv["h2"], dgu, 512, 1408, f"wgrad_gu_{l}", True)
    dx1, gs["ffn_norm_g"] = rms_bwd(dh2, sv["x1"], P["ffn_norm_g"], l, dx2, f"rms_ffn_bwd_{l}")
    dyn = mm_dgrad(dx1, W["w_out"].reshape(D_MODEL, D_MODEL), f"dgrad_out_{l}")
    gb["w_out"] = mm_wgrad(sv["yn"], dx1, 512, 1024, f"wgrad_out_{l}", False).reshape(N_CHIP, D_MODEL // N_CHIP, D_MODEL)
    dyc, dys, dyb, gs["out_norm_g"] = outnorm_bwd(dyn, sv["yc"], sv["ys"], sv["yb"], P["out_norm_g"], l,
                                                  f"outnorm_bwd_{l}")
    dq, dk, dv, *moved = attn_bwd(sv["qkv"], sv["rt"], dyb, f"attn_bwd_{l}", ex)
    dpb, dgq, dgk = qkv_bwd(sv["proj"], dq, dk, dv, P["q_norm_g"], P["k_norm_g"], l, f"qkv_bwd_{l}")
    gs["q_norm_g"] = dgq[0, :HEAD_DIM] + dgq[0, HEAD_DIM:]
    gs["k_norm_g"] = dgk[0, :HEAD_DIM] + dgk[0, HEAD_DIM:]
    dps, gs["sg_ln_g"], gs["sg_ln_b"], gs["sg_w"], dbt = sg_bwd(
        sv["proj"], dys, P["sg_ln_g"], P["sg_ln_b"], P["sg_wt"], P["sg_wtt"], P["sg_bt"], l, f"sg_bwd_{l}")
    gs["sg_b"] = dbt.T
    dpc, dcw, gs["conv_b"], gs["conv_ln_g"], gs["conv_ln_b"] = conv_bwd(
        sv["proj"], dyc, P["conv_w"], P["conv_b"], P["conv_ln_g"], P["conv_ln_b"], l, f"conv_bwd_{l}")
    gs["conv_w"] = dcw[:CONV_K]
    dproj = jnp.concatenate([dpc, dps, dpb], axis=-1)
    dh = mm_dgrad_colblk(dproj, W["w_in"], f"dgrad_in_{l}")
    gb["w_in"] = mm_wgrad(sv["h"], dproj, 1024, IN_W // N_CHIP, f"wgrad_in_{l}", True)
    dx, gs["mix_norm_g"] = rms_bwd(dh, sv["x"], P["mix_norm_g"], l, dx1, f"rms_mix_bwd_{l}")
    return dx, gb, gs, moved


def reduce_start(gb, l, c1):
    gl = [gb[n] for n in BIG]
    l1 = exchange_sibling_half(gl, f"rs_sibling_{l}")
    return [add_half(g, a, c1, f"rs_add_{n}_{l}", RS_ROWS[n]) for n, g, a in zip(BIG, gl, l1)]


def reduce_finish(ps, l2, l, kc):
    fs = [sum_chips(p, a, kc, f"rs_sum_{n}_{l}", RS_ROWS[n]) for n, p, a in zip(BIG, ps, l2)]
    full = join_sibling_halves(fs, f"rs_join_{l}")
    return {n: f.reshape(2 * f.shape[1], f.shape[2]) for n, f in zip(BIG, full)}


def kernel(x, mix_norm_g, w_in, conv_w, conv_b, conv_ln_g, conv_ln_b, sg_ln_g, sg_ln_b, sg_w, sg_b, q_norm_g, k_norm_g, out_norm_g, w_out, ffn_norm_g, w_gate_up, w_down, loss_target, m_mix_norm_g, m_w_in, m_conv_w, m_conv_b, m_conv_ln_g, m_conv_ln_b, m_sg_ln_g, m_sg_ln_b, m_sg_w, m_sg_b, m_q_norm_g, m_k_norm_g, m_out_norm_g, m_w_out, m_ffn_norm_g, m_w_gate_up, m_w_down, v_mix_norm_g, v_w_in, v_conv_w, v_conv_b, v_conv_ln_g, v_conv_ln_b, v_sg_ln_g, v_sg_ln_b, v_sg_w, v_sg_b, v_q_norm_g, v_k_norm_g, v_out_norm_g, v_w_out, v_ffn_norm_g, v_w_gate_up, v_w_down):
    w = dict(mix_norm_g=mix_norm_g, w_in=w_in, conv_w=conv_w, conv_b=conv_b, conv_ln_g=conv_ln_g, conv_ln_b=conv_ln_b,
             sg_ln_g=sg_ln_g, sg_ln_b=sg_ln_b, sg_w=sg_w, sg_b=sg_b, q_norm_g=q_norm_g, k_norm_g=k_norm_g,
             out_norm_g=out_norm_g, w_out=w_out, ffn_norm_g=ffn_norm_g, w_gate_up=w_gate_up, w_down=w_down)
    m = dict(mix_norm_g=m_mix_norm_g, w_in=m_w_in, conv_w=m_conv_w, conv_b=m_conv_b, conv_ln_g=m_conv_ln_g,
             conv_ln_b=m_conv_ln_b, sg_ln_g=m_sg_ln_g, sg_ln_b=m_sg_ln_b, sg_w=m_sg_w, sg_b=m_sg_b,
             q_norm_g=m_q_norm_g, k_norm_g=m_k_norm_g, out_norm_g=m_out_norm_g, w_out=m_w_out,
             ffn_norm_g=m_ffn_norm_g, w_gate_up=m_w_gate_up, w_down=m_w_down)
    v = dict(mix_norm_g=v_mix_norm_g, w_in=v_w_in, conv_w=v_conv_w, conv_b=v_conv_b, conv_ln_g=v_conv_ln_g,
             conv_ln_b=v_conv_ln_b, sg_ln_g=v_sg_ln_g, sg_ln_b=v_sg_ln_b, sg_w=v_sg_w, sg_b=v_sg_b,
             q_norm_g=v_q_norm_g, k_norm_g=v_k_norm_g, out_norm_g=v_out_norm_g, w_out=v_w_out,
             ffn_norm_g=v_ffn_norm_g, w_gate_up=v_w_gate_up, w_down=v_w_down)
    L = DEPTH
    xi, yi, ci = lax.axis_index("x"), lax.axis_index("y"), lax.axis_index("c")
    kme = 2 * xi + yi
    c1 = ci.astype(jnp.int32).reshape(1)
    k1 = kme.astype(jnp.int32).reshape(1)
    kc = jnp.stack([kme, ci]).astype(jnp.int32)

    def gather_parts(l):
        shards = [w[n][l].astype(BF16) for n in BIG]
        bufs = [place_slot(s, k1, N_CHIP, f"ag_own_{n}_{l}", AG_ROWS[n]) for n, s in zip(BIG, shards)]
        return shards, bufs

    shards, bufs = gather_parts(0)
    cw_sh = jnp.pad(conv_w.reshape(L * CONV_K, CONV_W // N_CHIP), ((0, 128 - L * CONV_K), (0, 0)))
    cw_buf = place_slot(cw_sh, k1, N_CHIP, "ag_own_conv", 128)
    *bufs, cw_buf = run_exchange(gather_exchange(shards + [cw_sh], bufs + [cw_buf]), "ag_chips_0")
    *bufs, cw_buf = gather_forward(bufs + [cw_buf], "ag_sibling_0")
    W = [dict(zip(BIG, bufs))]
    cw_all = cw_buf[:, :L * CONV_K].reshape(N_CHIP, L, CONV_K, CONV_W // N_CHIP)
    cw_full = jnp.transpose(cw_all, (1, 2, 0, 3)).reshape(L, CONV_K, CONV_W)

    tril = jnp.tril(jnp.ones((CHUNK, CHUNK), bool))
    sg_wt = jnp.where(tril, sg_w, 0.0)
    P = {
        "mix_norm_g": mix_norm_g.reshape(L, 1, D_MODEL), "ffn_norm_g": ffn_norm_g.reshape(L, 1, D_MODEL),
        "out_norm_g": out_norm_g.reshape(L, 1, D_MODEL),
        "conv_w": jnp.pad(cw_full, ((0, 0), (0, 1), (0, 0))), "conv_b": conv_b.reshape(L, 1, CONV_W),
        "conv_ln_g": conv_ln_g.reshape(L, 1, CONV_W), "conv_ln_b": conv_ln_b.reshape(L, 1, CONV_W),
        "sg_ln_g": sg_ln_g.reshape(L, 1, SG_W), "sg_ln_b": sg_ln_b.reshape(L, 1, SG_W),
        "sg_wt": sg_wt.astype(MXU_DT), "sg_wtt": jnp.swapaxes(sg_wt, 2, 3).astype(MXU_DT),
        "sg_bt": jnp.swapaxes(sg_b, 1, 2),
        "q_norm_g": jnp.tile(q_norm_g, (1, 2)).reshape(L, 1, 128), "k_norm_g": jnp.tile(k_norm_g, (1, 2)).reshape(L, 1, 128),
    }

    h = x[0]
    saved = []
    for l in range(L):
        ex = gather_exchange(*gather_parts(l + 1)) if l + 1 < L else None
        h, sv, moved = layer_forward(h, l, P, W[l], ex)
        saved.append(sv)
        if ex is not None:
            W.append(dict(zip(BIG, gather_forward(moved, f"ag_sibling_{l + 1}"))))
    dy, loss_part = loss_head(h, loss_target[0], "loss_head")
    loss = lax.psum(loss_part[0, 0], ("x", "y", "c"))

    outs = {n: None for n in BIG}
    small_grads = [None] * L

    def finish(ps, l2, lyr):
        red = reduce_finish(ps, l2, lyr, kc)
        for n in BIG:
            outs[n] = adamw_layer(w[n], m[n], v[n], red[n], outs[n], lyr, f"adamw_{n}_{lyr}", ADAM_ROWS[n])

    ps = None
    for l in reversed(range(L)):
        ex = scatter_exchange(ps) if ps is not None else None
        dy, gb, small_grads[l], moved = layer_backward(dy, l, P, W[l], saved[l], ex)
        if ps is not None:
            finish(ps, moved, l + 1)
        ps = reduce_start(gb, l, c1)
    finish(ps, run_exchange(scatter_exchange(ps), "rs_chips_0"), 0)

    shapes = [(L,) + small_grads[0][n].shape for n in SMALL]
    packed = _pack([jnp.stack([small_grads[l][n] for l in range(L)]) for n in SMALL])
    gsum = dict(zip(SMALL, _unpack(all_reduce_small(packed, "ar_small"), shapes)))
    gsum["conv_w"] = lax.dynamic_slice_in_dim(gsum["conv_w"], kme * (CONV_W // N_CHIP), CONV_W // N_CHIP, axis=2)
    gsum = {n: gsum[n].reshape(w[n].shape) for n in SMALL}
    lshapes = [w[n].shape for n in SMALL]
    d_p, m_p, v_p = adamw_flat(_pack([w[n] for n in SMALL]), _pack([m[n] for n in SMALL]),
                               _pack([v[n] for n in SMALL]), _pack([gsum[n] for n in SMALL]), "adamw_small")
    d_s = dict(zip(SMALL, _unpack(d_p, lshapes)))
    m_s = dict(zip(SMALL, _unpack(m_p, lshapes)))
    v_s = dict(zip(SMALL, _unpack(v_p, lshapes)))

    grads = {n: (outs[n][0] if n in BIG else gsum[n]) for n in WEIGHTS}
    delta = {n: (outs[n][1] if n in BIG else d_s[n]) for n in WEIGHTS}
    new_m = {n: (outs[n][2] if n in BIG else m_s[n]) for n in WEIGHTS}
    new_v = {n: (outs[n][3] if n in BIG else v_s[n]) for n in WEIGHTS}
    return (loss, dy[None], *[grads[n] for n in WEIGHTS], *[delta[n] for n in WEIGHTS],
            *[new_m[n] for n in WEIGHTS], *[new_v[n] for n in WEIGHTS])
```

```python
import functools

import jax
import jax.numpy as jnp
from jax import lax
from jax.experimental import pallas as pl
from jax.experimental.pallas import tpu as pltpu

F32 = jnp.float32
BF16 = jnp.bfloat16
MXU_DT = jnp.bfloat16
GRAD_WIRE_DT = jnp.bfloat16

D_MODEL = 1024
DEPTH = 4
HEAD_DIM = 64
CONV_W = 256
SG_W = 256
SB_W = 512
CONV_K = 31
CHUNK = 128
OFF_SG = 2 * CONV_W
OFF_SB = OFF_SG + 2 * SG_W
IN_W = OFF_SB + 3 * SB_W
FFN = 2816
N_CHIP = 4
RMS_EPS = 1e-6
LN_EPS = 1e-5
ADAM_LR = 0.001
ADAM_B1 = 0.9
ADAM_B2 = 0.999
ADAM_EPS = 1e-08
ADAM_WD = 0.01
ADAM_STEP = 10
BC1 = 1.0 - ADAM_B1 ** ADAM_STEP
BC2 = 1.0 - ADAM_B2 ** ADAM_STEP
HALO = 32
MESH = pl.DeviceIdType.MESH
ANY = pl.BlockSpec(memory_space=pl.ANY)
VMEM_LIMIT = 56 * 1024 * 1024


def _cp(**kw):
    return pltpu.CompilerParams(vmem_limit_bytes=VMEM_LIMIT, **kw)


def _dot(a, b, dims):
    dn = {"nn": (((1,), (0,)), ((), ())), "nt": (((1,), (1,)), ((), ())), "tn": (((0,), (0,)), ((), ()))}[dims]
    return lax.dot_general(a.astype(MXU_DT), b.astype(MXU_DT), dn, preferred_element_type=F32)


def _cumdot(x, m):
    hi = x.astype(MXU_DT)
    lo = (x - hi.astype(F32)).astype(MXU_DT)
    dn = (((1,), (0,)), ((), ()))
    return (lax.dot_general(hi, m, dn, preferred_element_type=F32)
            + lax.dot_general(lo, m, dn, preferred_element_type=F32))


def _iota(shape, axis):
    return lax.broadcasted_iota(jnp.int32, shape, axis)


def _rms(x, g):
    return x * lax.rsqrt(jnp.mean(x * x, axis=-1, keepdims=True) + RMS_EPS) * g


def _ln(x, g, b):
    mu = jnp.mean(x, axis=-1, keepdims=True)
    xc = x - mu
    var = jnp.mean(xc * xc, axis=-1, keepdims=True)
    return xc * lax.rsqrt(var + LN_EPS) * g + b


def _glu(val, gate):
    return val * jax.nn.sigmoid(gate)


def _ln_silu(c, g, b):
    return jax.nn.silu(_ln(c, g, b))


_ERF_ALPHA = (-2.72614225801306e-10, 2.77068142495902e-08, -2.10102402082508e-06, -5.69250639462346e-05,
              -7.34990630326855e-04, -2.95459980854025e-03, -1.60960333262415e-02)
_ERF_BETA = (-1.45660718464996e-05, -2.13374055278905e-04, -1.68282697438203e-03, -7.37332916720468e-03,
             -1.42647390514189e-02)


def _erf(x):
    x = jnp.clip(x, -4.0, 4.0)
    x2 = x * x
    p = jnp.full_like(x, _ERF_ALPHA[0])
    for a in _ERF_ALPHA[1:]:
        p = p * x2 + a
    q = jnp.full_like(x, _ERF_BETA[0])
    for b in _ERF_BETA[1:]:
        q = q * x2 + b
    return x * p / q


@jax.custom_jvp
def _gelu(x):
    return 0.5 * x * (1.0 + _erf(x * (2.0 ** -0.5)))


@_gelu.defjvp
def _gelu_jvp(primals, tangents):
    (x,), (t,) = primals, tangents
    cdf = 0.5 * (1.0 + _erf(x * (2.0 ** -0.5)))
    pdf = jnp.exp(-0.5 * x * x) * ((2.0 * jnp.pi) ** -0.5)
    return x * cdf, t * (cdf + x * pdf)


def _sg_pre(uvp, g, b):
    uv = _gelu(uvp)
    return uv[:, :SG_W], _ln(uv[:, SG_W:], g, b)


def _outnorm(yc, ys, yb, g):
    return jnp.concatenate([_rms(yc, g[:, :CONV_W]), _rms(ys, g[:, CONV_W:CONV_W + SG_W]),
                            _rms(yb, g[:, CONV_W + SG_W:])], axis=-1)


def _qkv_fn(pq, pk, pv, gq, gk):
    outs = []
    for p, g, sc in ((pq, gq, HEAD_DIM ** -0.5), (pk, gk, 1.0)):
        for s in range(SB_W // 128):
            x = p[:, 128 * s:128 * (s + 1)]
            seg = _iota(x.shape, 1) < HEAD_DIM
            x2 = x * x
            s0 = jnp.sum(jnp.where(seg, x2, 0.0), axis=-1, keepdims=True)
            s1 = jnp.sum(jnp.where(seg, 0.0, x2), axis=-1, keepdims=True)
            ms = jnp.where(seg, s0, s1) * (1.0 / HEAD_DIM)
            outs.append(x * lax.rsqrt(ms + RMS_EPS) * (g * sc))
    outs.append(pv)
    return jnp.concatenate(outs, axis=-1)


def _swiglu(g, u):
    return jax.nn.silu(g) * u


def _softplus(z):
    return jnp.maximum(z, 0.0) + jnp.log(1.0 + jnp.exp(-jnp.abs(z)))


def mm_colblk(a, wb, out_dtype, name, tm=1024):
    S, K = a.shape
    nb, _, C = wb.shape

    def body(a_ref, w_ref, o_ref):
        o_ref[...] = _dot(a_ref[...], w_ref[0], "nn").astype(o_ref.dtype)

    return pl.pallas_call(
        body, grid=(nb, S // tm),
        in_specs=[pl.BlockSpec((tm, K), lambda k, i: (i, 0)), pl.BlockSpec((1, K, C), lambda k, i: (k, 0, 0))],
        out_specs=pl.BlockSpec((tm, C), lambda k, i: (i, k)),
        out_shape=jax.ShapeDtypeStruct((S, nb * C), out_dtype), name=name, compiler_params=_cp())(a, wb)


def mm_res(a, w, res, name, tm=512):
    S, K = a.shape
    N = w.shape[1]

    def body(a_ref, w_ref, r_ref, o_ref):
        o_ref[...] = r_ref[...] + _dot(a_ref[...], w_ref[...], "nn")

    return pl.pallas_call(
        body, grid=(S // tm,),
        in_specs=[pl.BlockSpec((tm, K), lambda i: (i, 0)), pl.BlockSpec((K, N), lambda i: (0, 0)),
                  pl.BlockSpec((tm, N), lambda i: (i, 0))],
        out_specs=pl.BlockSpec((tm, N), lambda i: (i, 0)),
        out_shape=jax.ShapeDtypeStruct((S, N), F32), name=name, compiler_params=_cp())(a, w, res)


def ffn_up(h2, wgu, name, tm=512):
    S, K = h2.shape
    C = wgu.shape[2]

    def body(h_ref, wg_ref, wu_ref, g_ref, u_ref, a_ref):
        h = h_ref[...]
        g = _dot(h, wg_ref[0], "nn")
        u = _dot(h, wu_ref[0], "nn")
        g_ref[...] = g.astype(BF16)
        u_ref[...] = u.astype(BF16)
        a_ref[...] = _swiglu(g, u).astype(BF16)

    o = pl.BlockSpec((tm, C), lambda j, i: (i, j))
    sh = jax.ShapeDtypeStruct((S, 2 * C), BF16)
    return pl.pallas_call(
        body, grid=(2, S // tm),
        in_specs=[pl.BlockSpec((tm, K), lambda j, i: (i, 0)), pl.BlockSpec((1, K, C), lambda j, i: (j, 0, 0)),
                  pl.BlockSpec((1, K, C), lambda j, i: (2 + j, 0, 0))],
        out_specs=[o, o, o], out_shape=[sh, sh, sh], name=name, compiler_params=_cp())(h2, wgu, wgu)


def ffn_down_bwd(dx2, wdown, g, u, name, tm=512):
    S, N = dx2.shape
    C = FFN // 2

    def body(d_ref, w_ref, g_ref, u_ref, o_ref):
        d = d_ref[...]
        for j in range(2):
            cols = slice(j * C, (j + 1) * C)
            dact = _dot(d, w_ref[cols, :], "nt")
            _, vjp = jax.vjp(_swiglu, g_ref[:, cols].astype(F32), u_ref[:, cols].astype(F32))
            dg, du = vjp(dact)
            o_ref[:, cols] = dg.astype(BF16)
            o_ref[:, FFN + j * C:FFN + (j + 1) * C] = du.astype(BF16)

    row = pl.BlockSpec((tm, FFN), lambda i: (i, 0))
    return pl.pallas_call(
        body, grid=(S // tm,),
        in_specs=[pl.BlockSpec((tm, N), lambda i: (i, 0)), pl.BlockSpec((FFN, N), lambda i: (0, 0)), row, row],
        out_specs=pl.BlockSpec((tm, 2 * FFN), lambda i: (i, 0)),
        out_shape=jax.ShapeDtypeStruct((S, 2 * FFN), BF16), name=name, compiler_params=_cp())(dx2, wdown, g, u)


def mm_dgrad_colblk(do, wb, name, tm=1024):
    S = do.shape[0]
    nb, K, C = wb.shape

    def body(d_ref, w_ref, o_ref):
        k = pl.program_id(1)
        r = _dot(d_ref[...], w_ref[0], "nt")

        @pl.when(k == 0)
        def _():
            o_ref[...] = r

        @pl.when(k != 0)
        def _():
            o_ref[...] += r

    return pl.pallas_call(
        body, grid=(S // tm, nb),
        in_specs=[pl.BlockSpec((tm, C), lambda i, k: (i, k)), pl.BlockSpec((1, K, C), lambda i, k: (k, 0, 0))],
        out_specs=pl.BlockSpec((tm, K), lambda i, k: (i, 0)),
        out_shape=jax.ShapeDtypeStruct((S, K), F32), name=name, compiler_params=_cp())(do, wb)


def mm_dgrad(do, w, name, tm=1024):
    S, N = do.shape
    K = w.shape[0]

    def body(d_ref, w_ref, o_ref):
        o_ref[...] = _dot(d_ref[...], w_ref[...], "nt")

    return pl.pallas_call(
        body, grid=(S // tm,),
        in_specs=[pl.BlockSpec((tm, N), lambda i: (i, 0)), pl.BlockSpec((K, N), lambda i: (0, 0))],
        out_specs=pl.BlockSpec((tm, K), lambda i: (i, 0)),
        out_shape=jax.ShapeDtypeStruct((S, K), F32), name=name, compiler_params=_cp())(do, w)


def mm_wgrad(a, do, tk, tn, name, blocked):
    S, K = a.shape
    N = do.shape[1]

    def body(a_ref, d_ref, o_ref):
        r = _dot(a_ref[...], d_ref[...], "tn").astype(GRAD_WIRE_DT)
        if blocked:
            o_ref[0] = r
        else:
            o_ref[...] = r

    if blocked:
        out_spec = pl.BlockSpec((1, tk, tn), lambda n, j: (n, j, 0))
        out_shape = jax.ShapeDtypeStruct((N // tn, K, tn), GRAD_WIRE_DT)
    else:
        out_spec = pl.BlockSpec((tk, tn), lambda n, j: (j, n))
        out_shape = jax.ShapeDtypeStruct((K, N), GRAD_WIRE_DT)
    return pl.pallas_call(
        body, grid=(N // tn, K // tk),
        in_specs=[pl.BlockSpec((S, tk), lambda n, j: (0, j)), pl.BlockSpec((S, tn), lambda n, j: (0, n))],
        out_specs=out_spec, out_shape=out_shape, name=name, compiler_params=_cp())(a, do)


def rms_fwd(x, g3, l, name, tm=512):
    S, N = x.shape

    def body(x_ref, g_ref, o_ref):
        o_ref[...] = _rms(x_ref[...], g_ref[0]).astype(BF16)

    return pl.pallas_call(
        body, grid=(S // tm,),
        in_specs=[pl.BlockSpec((tm, N), lambda i: (i, 0)), pl.BlockSpec((1, 1, N), lambda i: (l, 0, 0))],
        out_specs=pl.BlockSpec((tm, N), lambda i: (i, 0)),
        out_shape=jax.ShapeDtypeStruct((S, N), BF16), name=name, compiler_params=_cp())(x, g3)


def rms_bwd(dh, x, g3, l, dres, name, tm=512):
    S, N = x.shape

    def body(dh_ref, x_ref, g_ref, r_ref, dx_ref, dg_ref):
        _, vjp = jax.vjp(_rms, x_ref[...], g_ref[0])
        dx, dg = vjp(dh_ref[...])
        dx_ref[...] = r_ref[...] + dx

        @pl.when(pl.program_id(0) == 0)
        def _():
            dg_ref[...] = jnp.zeros_like(dg_ref)

        dg_ref[...] += dg

    row = pl.BlockSpec((tm, N), lambda i: (i, 0))
    return pl.pallas_call(
        body, grid=(S // tm,),
        in_specs=[row, row, pl.BlockSpec((1, 1, N), lambda i: (l, 0, 0)), row],
        out_specs=[row, pl.BlockSpec((1, N), lambda i: (0, 0))],
        out_shape=[jax.ShapeDtypeStruct((S, N), F32), jax.ShapeDtypeStruct((1, N), F32)],
        name=name, compiler_params=_cp())(dh, x, g3, dres)


def outnorm_fwd(yc, ys, yb, g3, l, name, tm=512):
    S = yc.shape[0]

    def body(c_ref, s_ref, b_ref, g_ref, o_ref):
        o_ref[...] = _outnorm(c_ref[...], s_ref[...], b_ref[...], g_ref[0]).astype(BF16)

    return pl.pallas_call(
        body, grid=(S // tm,),
        in_specs=[pl.BlockSpec((tm, CONV_W), lambda i: (i, 0)), pl.BlockSpec((tm, SG_W), lambda i: (i, 0)),
                  pl.BlockSpec((tm, SB_W), lambda i: (i, 0)), pl.BlockSpec((1, 1, D_MODEL), lambda i: (l, 0, 0))],
        out_specs=pl.BlockSpec((tm, D_MODEL), lambda i: (i, 0)),
        out_shape=jax.ShapeDtypeStruct((S, D_MODEL), BF16), name=name, compiler_params=_cp())(yc, ys, yb, g3)


def outnorm_bwd(dyn, yc, ys, yb, g3, l, name, tm=512):
    S = yc.shape[0]

    def body(d_ref, c_ref, s_ref, b_ref, g_ref, dc_ref, ds_ref, db_ref, dg_ref):
        _, vjp = jax.vjp(_outnorm, c_ref[...], s_ref[...], b_ref[...], g_ref[0])
        dc, ds, db, dg = vjp(d_ref[...])
        dc_ref[...] = dc
        ds_ref[...] = ds
        db_ref[...] = db

        @pl.when(pl.program_id(0) == 0)
        def _():
            dg_ref[...] = jnp.zeros_like(dg_ref)

        dg_ref[...] += dg

    sc = pl.BlockSpec((tm, CONV_W), lambda i: (i, 0))
    ss = pl.BlockSpec((tm, SG_W), lambda i: (i, 0))
    sb = pl.BlockSpec((tm, SB_W), lambda i: (i, 0))
    return pl.pallas_call(
        body, grid=(S // tm,),
        in_specs=[pl.BlockSpec((tm, D_MODEL), lambda i: (i, 0)), sc, ss, sb,
                  pl.BlockSpec((1, 1, D_MODEL), lambda i: (l, 0, 0))],
        out_specs=[sc, ss, sb, pl.BlockSpec((1, D_MODEL), lambda i: (0, 0))],
        out_shape=[jax.ShapeDtypeStruct((S, CONV_W), F32), jax.ShapeDtypeStruct((S, SG_W), F32),
                   jax.ShapeDtypeStruct((S, SB_W), F32), jax.ShapeDtypeStruct((1, D_MODEL), F32)],
        name=name, compiler_params=_cp())(dyn, yc, ys, yb, g3)


def loss_head(y, t, name, tm=512):
    S, N = y.shape

    def body(y_ref, t_ref, dy_ref, l_ref):
        e = y_ref[...] - t_ref[...]
        dy_ref[...] = e * (1.0 / N)

        @pl.when(pl.program_id(0) == 0)
        def _():
            l_ref[...] = jnp.zeros_like(l_ref)

        l_ref[...] += (0.5 / N) * jnp.sum(jnp.sum(e * e, axis=-1, keepdims=True), axis=0, keepdims=True)

    row = pl.BlockSpec((tm, N), lambda i: (i, 0))
    return pl.pallas_call(
        body, grid=(S // tm,), in_specs=[row, row],
        out_specs=[row, pl.BlockSpec((1, 1), lambda i: (0, 0))],
        out_shape=[jax.ShapeDtypeStruct((S, N), F32), jax.ShapeDtypeStruct((1, 1), F32)],
        name=name, compiler_params=_cp())(y, t)


def _conv_taps(a, w_ref, T):
    acc = jnp.zeros((T, CONV_W), F32)
    for j in range(CONV_K):
        sh = CONV_K - 1 - j
        r = a if sh == 0 else pltpu.roll(a, sh, 0)
        acc = acc + r[HALO:HALO + T] * w_ref[pl.ds(j, 1), :]
    return acc


def conv_fwd(proj, cw, cb, lg, lb, l, name, T=256):
    S = proj.shape[0]
    nt = S // T

    def body(p_ref, w_ref, cb_ref, lg_ref, lb_ref, y_ref, hc_s):
        hc_s[0:HALO, :] = jnp.zeros((HALO, CONV_W), F32)

        def fill(i, _):
            r0 = pl.multiple_of(i * T, T)
            pc = p_ref[pl.ds(r0, T), :]
            hc_s[pl.ds(r0 + HALO, T), :] = _glu(pc[:, :CONV_W], pc[:, CONV_W:])
            return 0

        lax.fori_loop(0, nt, fill, 0)

        def tile(i, _):
            r0 = pl.multiple_of(i * T, T)
            c = _conv_taps(hc_s[pl.ds(r0, T + HALO), :], w_ref.at[0], T) + cb_ref[0]
            y_ref[pl.ds(r0, T), :] = _ln_silu(c, lg_ref[0], lb_ref[0])
            return 0

        lax.fori_loop(0, nt, tile, 0)

    vec = pl.BlockSpec((1, 1, CONV_W), lambda i: (l, 0, 0))
    return pl.pallas_call(
        body, grid=(1,),
        in_specs=[pl.BlockSpec((S, 2 * CONV_W), lambda i: (0, 0)), pl.BlockSpec((1, 32, CONV_W), lambda i: (l, 0, 0)),
                  vec, vec, vec],
        out_specs=pl.BlockSpec((S, CONV_W), lambda i: (0, 0)),
        out_shape=jax.ShapeDtypeStruct((S, CONV_W), F32),
        scratch_shapes=[pltpu.VMEM((S + HALO, CONV_W), F32)], name=name, compiler_params=_cp())(proj, cw, cb, lg, lb)


def conv_bwd(proj, dy, cw, cb, lg, lb, l, name, T=256):
    S = proj.shape[0]
    nt = S // T

    def body(p_ref, dy_ref, w_ref, cb_ref, lg_ref, lb_ref, dp_ref, dw_ref, dcb_ref, dlg_ref, dlb_ref, hc_s, dc_s):
        hc_s[0:HALO, :] = jnp.zeros((HALO, CONV_W), F32)
        dc_s[S:S + HALO, :] = jnp.zeros((HALO, CONV_W), F32)
        dw_ref[...] = jnp.zeros_like(dw_ref)
        dcb_ref[...] = jnp.zeros_like(dcb_ref)
        dlg_ref[...] = jnp.zeros_like(dlg_ref)
        dlb_ref[...] = jnp.zeros_like(dlb_ref)

        def fill(i, _):
            r0 = pl.multiple_of(i * T, T)
            pc = p_ref[pl.ds(r0, T), :]
            hc_s[pl.ds(r0 + HALO, T), :] = _glu(pc[:, :CONV_W], pc[:, CONV_W:])
            return 0

        lax.fori_loop(0, nt, fill, 0)

        def tile(i, _):
            r0 = pl.multiple_of(i * T, T)
            a = hc_s[pl.ds(r0, T + HALO), :]
            c = _conv_taps(a, w_ref.at[0], T) + cb_ref[0]
            _, vjp = jax.vjp(_ln_silu, c, lg_ref[0], lb_ref[0])
            dc, dlg, dlb = vjp(dy_ref[pl.ds(r0, T), :])
            dc_s[pl.ds(r0, T), :] = dc
            dcb_ref[...] += jnp.sum(dc, axis=0, keepdims=True)
            dlg_ref[...] += dlg
            dlb_ref[...] += dlb
            for j in range(CONV_K):
                sh = CONV_K - 1 - j
                r = a if sh == 0 else pltpu.roll(a, sh, 0)
                dw_ref[pl.ds(j, 1), :] += jnp.sum(dc * r[HALO:HALO + T], axis=0, keepdims=True)
            return 0

        lax.fori_loop(0, nt, tile, 0)

        def back(i, _):
            r0 = pl.multiple_of(i * T, T)
            de = dc_s[pl.ds(r0, T + HALO), :]
            n = T + HALO
            dh = jnp.zeros((T, CONV_W), F32)
            for j in range(CONV_K):
                sh = CONV_K - 1 - j
                r = de if sh == 0 else pltpu.roll(de, n - sh, 0)
                dh = dh + r[0:T] * w_ref[0, pl.ds(j, 1), :]
            pc = p_ref[pl.ds(r0, T), :]
            _, vjp = jax.vjp(_glu, pc[:, :CONV_W], pc[:, CONV_W:])
            dval, dgate = vjp(dh)
            dp_ref[pl.ds(r0, T), :] = jnp.concatenate([dval, dgate], axis=-1).astype(BF16)
            return 0

        lax.fori_loop(0, nt, back, 0)

    vec = pl.BlockSpec((1, 1, CONV_W), lambda i: (l, 0, 0))
    ovec = pl.BlockSpec((1, CONV_W), lambda i: (0, 0))
    vsh = jax.ShapeDtypeStruct((1, CONV_W), F32)
    return pl.pallas_call(
        body, grid=(1,),
        in_specs=[pl.BlockSpec((S, 2 * CONV_W), lambda i: (0, 0)), pl.BlockSpec((S, CONV_W), lambda i: (0, 0)),
                  pl.BlockSpec((1, 32, CONV_W), lambda i: (l, 0, 0)), vec, vec, vec],
        out_specs=[pl.BlockSpec((S, 2 * CONV_W), lambda i: (0, 0)), pl.BlockSpec((32, CONV_W), lambda i: (0, 0)),
                   ovec, ovec, ovec],
        out_shape=[jax.ShapeDtypeStruct((S, 2 * CONV_W), BF16), jax.ShapeDtypeStruct((32, CONV_W), F32), vsh, vsh, vsh],
        scratch_shapes=[pltpu.VMEM((S + HALO, CONV_W), F32), pltpu.VMEM((S + HALO, CONV_W), F32)],
        name=name, compiler_params=_cp())(proj, dy, cw, cb, lg, lb)


def _sg_mix(wt_ref, v, bt):
    slabs = []
    for s in range(2):
        vs = v[:, 128 * s:128 * (s + 1)]
        seg = _iota(vs.shape, 1) < HEAD_DIM
        p0 = _dot(wt_ref[2 * s], vs, "nn") + bt[:, 2 * s:2 * s + 1]
        p1 = _dot(wt_ref[2 * s + 1], vs, "nn") + bt[:, 2 * s + 1:2 * s + 2]
        slabs.append(jnp.where(seg, p0, p1))
    return jnp.concatenate(slabs, axis=-1)


def sg_fwd(proj, lg, lb, wt, bt, l, name):
    S = proj.shape[0]

    def body(p_ref, lg_ref, lb_ref, w_ref, b_ref, y_ref):
        u, v = _sg_pre(p_ref[...], lg_ref[0], lb_ref[0])
        y_ref[...] = u * _sg_mix(w_ref.at[0], v, b_ref[0])

    vec = pl.BlockSpec((1, 1, SG_W), lambda i: (l, 0, 0))
    return pl.pallas_call(
        body, grid=(S // CHUNK,),
        in_specs=[pl.BlockSpec((CHUNK, 2 * SG_W), lambda i: (i, 1)), vec, vec,
                  pl.BlockSpec((1, 4, CHUNK, CHUNK), lambda i: (l, 0, 0, 0)),
                  pl.BlockSpec((1, CHUNK, 4), lambda i: (l, 0, 0))],
        out_specs=pl.BlockSpec((CHUNK, SG_W), lambda i: (i, 0)),
        out_shape=jax.ShapeDtypeStruct((S, SG_W), F32), name=name, compiler_params=_cp())(proj, lg, lb, wt, bt)


def sg_bwd(proj, dy, lg, lb, wt, wtt, bt, l, name):
    S = proj.shape[0]

    def body(p_ref, dy_ref, lg_ref, lb_ref, w_ref, wt_ref, b_ref, dp_ref, dlg_ref, dlb_ref, dw_ref, db_ref):
        @pl.when(pl.program_id(0) == 0)
        def _():
            dlg_ref[...] = jnp.zeros_like(dlg_ref)
            dlb_ref[...] = jnp.zeros_like(dlb_ref)
            dw_ref[...] = jnp.zeros_like(dw_ref)
            db_ref[...] = jnp.zeros_like(db_ref)

        (u, v), vjp = jax.vjp(_sg_pre, p_ref[...], lg_ref[0], lb_ref[0])
        dy = dy_ref[...]
        mixed = _sg_mix(w_ref.at[0], v, b_ref[0])
        du = dy * mixed
        dm = dy * u
        tril = _iota((CHUNK, CHUNK), 1) <= _iota((CHUNK, CHUNK), 0)
        dvs = []
        for s in range(2):
            dms = dm[:, 128 * s:128 * (s + 1)]
            vs = v[:, 128 * s:128 * (s + 1)]
            seg = _iota(dms.shape, 1) < HEAD_DIM
            halves = (jnp.where(seg, dms, 0.0), jnp.where(seg, 0.0, dms))
            dv_h = []
            for e in range(2):
                h = 2 * s + e
                db_ref[:, h:h + 1] += jnp.sum(halves[e], axis=-1, keepdims=True)
                dw_ref[h] += jnp.where(tril, _dot(halves[e], vs, "nt"), 0.0)
                dv_h.append(_dot(wt_ref[0, h], halves[e], "nn"))
            dvs.append(dv_h[0] + dv_h[1])
        dp, dlg, dlb = vjp((du, jnp.concatenate(dvs, axis=-1)))
        dp_ref[...] = dp.astype(BF16)
        dlg_ref[...] += dlg
        dlb_ref[...] += dlb

    vec = pl.BlockSpec((1, 1, SG_W), lambda i: (l, 0, 0))
    wsp = pl.BlockSpec((1, 4, CHUNK, CHUNK), lambda i: (l, 0, 0, 0))
    ovec = pl.BlockSpec((1, SG_W), lambda i: (0, 0))
    return pl.pallas_call(
        body, grid=(S // CHUNK,),
        in_specs=[pl.BlockSpec((CHUNK, 2 * SG_W), lambda i: (i, 1)), pl.BlockSpec((CHUNK, SG_W), lambda i: (i, 0)),
                  vec, vec, wsp, wsp, pl.BlockSpec((1, CHUNK, 4), lambda i: (l, 0, 0))],
        out_specs=[pl.BlockSpec((CHUNK, 2 * SG_W), lambda i: (i, 0)), ovec, ovec,
                   pl.BlockSpec((4, CHUNK, CHUNK), lambda i: (0, 0, 0)), pl.BlockSpec((CHUNK, 4), lambda i: (0, 0))],
        out_shape=[jax.ShapeDtypeStruct((S, 2 * SG_W), BF16), jax.ShapeDtypeStruct((1, SG_W), F32),
                   jax.ShapeDtypeStruct((1, SG_W), F32), jax.ShapeDtypeStruct((4, CHUNK, CHUNK), F32),
                   jax.ShapeDtypeStruct((CHUNK, 4), F32)],
        name=name, compiler_params=_cp())(proj, dy, lg, lb, wt, wtt, bt)


def qkv_fwd(proj, gq, gk, l, name, tm=512):
    S = proj.shape[0]

    def body(pq_ref, pk_ref, pv_ref, gq_ref, gk_ref, o_ref):
        o_ref[...] = _qkv_fn(pq_ref[...], pk_ref[...], pv_ref[...], gq_ref[0], gk_ref[0]).astype(BF16)

    vec = pl.BlockSpec((1, 1, 128), lambda i: (l, 0, 0))
    qb = OFF_SB // SB_W
    return pl.pallas_call(
        body, grid=(S // tm,),
        in_specs=[pl.BlockSpec((tm, SB_W), lambda i: (i, qb)), pl.BlockSpec((tm, SB_W), lambda i: (i, qb + 1)),
                  pl.BlockSpec((tm, SB_W), lambda i: (i, qb + 2)), vec, vec],
        out_specs=pl.BlockSpec((tm, 3 * SB_W), lambda i: (i, 0)),
        out_shape=jax.ShapeDtypeStruct((S, 3 * SB_W), BF16), name=name, compiler_params=_cp())(proj, proj, proj, gq, gk)


def qkv_bwd(proj, dq, dk, dv, gq, gk, dpc, dps, l, name, tm=512):
    S = proj.shape[0]

    def body(pq_ref, pk_ref, pv_ref, dq_ref, dk_ref, dv_ref, gq_ref, gk_ref, dpc_ref, dps_ref, dp_ref, dgq_ref,
             dgk_ref):
        _, vjp = jax.vjp(_qkv_fn, pq_ref[...], pk_ref[...], pv_ref[...], gq_ref[0], gk_ref[0])
        dpq, dpk, dpv, dgq, dgk = vjp(jnp.concatenate([dq_ref[...], dk_ref[...], dv_ref[...]], axis=-1))
        dp_ref[:, :OFF_SG] = dpc_ref[...]
        dp_ref[:, OFF_SG:OFF_SB] = dps_ref[...]
        dp_ref[:, OFF_SB:] = jnp.concatenate([dpq, dpk, dpv], axis=-1).astype(BF16)

        @pl.when(pl.program_id(0) == 0)
        def _():
            dgq_ref[...] = jnp.zeros_like(dgq_ref)
            dgk_ref[...] = jnp.zeros_like(dgk_ref)

        dgq_ref[...] += dgq
        dgk_ref[...] += dgk

    vec = pl.BlockSpec((1, 1, 128), lambda i: (l, 0, 0))
    part = pl.BlockSpec((tm, SB_W), lambda i: (i, 0))
    ovec = pl.BlockSpec((1, 128), lambda i: (0, 0))
    qb = OFF_SB // SB_W
    return pl.pallas_call(
        body, grid=(S // tm,),
        in_specs=[pl.BlockSpec((tm, SB_W), lambda i: (i, qb)), pl.BlockSpec((tm, SB_W), lambda i: (i, qb + 1)),
                  pl.BlockSpec((tm, SB_W), lambda i: (i, qb + 2)), part, part, part, vec, vec, part, part],
        out_specs=[pl.BlockSpec((tm, IN_W), lambda i: (i, 0)), ovec, ovec],
        out_shape=[jax.ShapeDtypeStruct((S, IN_W), BF16), jax.ShapeDtypeStruct((1, 128), F32),
                   jax.ShapeDtypeStruct((1, 128), F32)],
        name=name, compiler_params=_cp())(proj, proj, proj, dq, dk, dv, gq, gk, dpc, dps)


QSUB = 4
QT = QSUB * CHUNK


def _sb_consts():
    row = _iota((CHUNK, CHUNK), 0)
    col = _iota((CHUNK, CHUNK), 1)
    ones = jnp.ones((CHUNK, CHUNK), MXU_DT)
    m_gt = jnp.concatenate([(row > col).astype(MXU_DT), ones], axis=1)
    m_lt = jnp.concatenate([(row < col).astype(MXU_DT), ones], axis=1)
    return (col < HEAD_DIM, col - row, jnp.concatenate([m_gt, m_gt], axis=0), jnp.concatenate([m_lt, m_lt], axis=0))


def _split_heads(x, seg):
    z = jnp.zeros_like(x)
    return (jnp.where(seg, x, z), jnp.where(seg, z, x))


def _stack_heads(x, seg):
    return jnp.concatenate(_split_heads(x, seg), axis=0)


def _cumdot2(x, m2):
    hi = x.astype(MXU_DT)
    lo = (x - hi.astype(F32)).astype(MXU_DT)
    dn = (((1,), (0,)), ((), ()))
    return [lax.dot_general(jnp.concatenate([hi[:, CHUNK * h:CHUNK * (h + 1)], lo[:, CHUNK * h:CHUNK * (h + 1)]], axis=1),
                            m2, dn, preferred_element_type=F32) for h in range(2)]


class Exchange:
    def __init__(self, ins, bufs, outs, n, build):
        self.ins, self.bufs, self.outs, self.n, self.build = list(ins), list(bufs), list(outs), n, build

    def split(self, refs, n_in, n_out):
        a, b, o = len(self.ins), len(self.bufs), len(self.outs)
        main_in = refs[:n_in]
        c_in = refs[n_in:n_in + a]
        rest = refs[n_in + a + b:]
        main_out = rest[:n_out]
        c_buf = rest[n_out:n_out + b]
        c_out = rest[n_out + b:n_out + b + o]
        send, recv = rest[n_out + b + o:]
        return main_in, main_out, lambda: self.build(c_in, c_buf, c_out, send, recv)


def _with_exchange(ex, n_in, n_out, in_specs, out_specs, out_shape):
    if ex is None:
        return in_specs, out_specs, out_shape, {}, [], []
    a, b = len(ex.ins), len(ex.bufs)
    in_specs = list(in_specs) + [ANY] * (a + b)
    out_specs = list(out_specs) + [ANY] * (b + len(ex.outs))
    out_shape = list(out_shape) + [jax.ShapeDtypeStruct(x.shape, x.dtype) for x in ex.bufs] + list(ex.outs)
    aliases = {n_in + a + i: n_out + i for i in range(b)}
    scratch = [pltpu.SemaphoreType.DMA((ex.n,)), pltpu.SemaphoreType.DMA((ex.n,))]
    return in_specs, out_specs, out_shape, aliases, scratch, ex.ins + ex.bufs


def attn_fwd(qkv, name, ex=None):
    S = qkv.shape[0]
    npair = SB_W // 128
    nq = S // QT

    def body(*refs):
        if ex is None:
            (q_ref, k_ref, v_ref), (o_ref, rt_ref), copies = refs[:3], refs[3:5], None
        else:
            (q_ref, k_ref, v_ref), (o_ref, rt_ref), copies = ex.split(refs, 3, 2)

            @pl.when(jnp.logical_and(pl.program_id(0) == 0, pl.program_id(1) == 0))
            def _():
                for cp in copies():
                    cp.start()

        qi = pl.program_id(1)
        seg, dcol, m_gt, _ = _sb_consts()
        qs = [q_ref[a * CHUNK:(a + 1) * CHUNK, :] for a in range(QSUB)]
        causal2 = jnp.concatenate([dcol < 0, dcol < 0], axis=1)

        def step(kb, carry, diag):
            off = pl.multiple_of(kb * CHUNK, CHUNK)
            kk = _stack_heads(k_ref[pl.ds(off, CHUNK), :], seg)
            vv = _stack_heads(v_ref[pl.ds(off, CHUNK), :], seg)
            act = [a for a in range(QSUB) if diag is None or a >= diag]
            z = [_dot(qs[a], kk, "nt") for a in act]
            sp = [_softplus(x) for x in z]
            lnb = [jnp.where(causal2, -s, 0.0) if a == diag else -s for a, s in zip(act, sp)]
            cs = [_cumdot2(x, m_gt) for x in lnb]
            att = []
            for n, a in enumerate(act):
                base = z[n] - sp[n]
                e = jnp.exp(base + jnp.concatenate([cs[n][h][:, :CHUNK] + carry[3 * a + 1 + h] for h in range(2)], axis=1))
                att.append(jnp.where(causal2, e, 0.0) if a == diag else e)
            pv = [_dot(x, vv, "nn") for x in att]
            new = list(carry)
            for n, a in enumerate(act):
                new[3 * a] = carry[3 * a] + pv[n]
                for h in range(2):
                    new[3 * a + 1 + h] = carry[3 * a + 1 + h] + cs[n][h][:, CHUNK:]
            return tuple(new)

        z0 = jnp.zeros((CHUNK, CHUNK), F32)
        res = (z0,) * (3 * QSUB)
        for j in reversed(range(QSUB)):
            res = step(QSUB * qi + j, res, j)
        res = lax.fori_loop(0, QSUB * qi, lambda it, c: step(QSUB * qi - 1 - it, c, None), res)
        for a in range(QSUB):
            rows = slice(a * CHUNK, (a + 1) * CHUNK)
            o_ref[rows, :] = res[3 * a]
            rt_ref[rows, :] = jnp.concatenate([res[3 * a + 1], res[3 * a + 2]], axis=1)

        if ex is not None:
            @pl.when(jnp.logical_and(pl.program_id(0) == npair - 1, pl.program_id(1) == nq - 1))
            def _():
                for cp in copies():
                    cp.wait()

    in_specs, out_specs, out_shape, aliases, scratch, extra = _with_exchange(
        ex, 3, 2,
        [pl.BlockSpec((QT, 128), lambda p, i: (i, p)), pl.BlockSpec((S, 128), lambda p, i: (0, npair + p)),
         pl.BlockSpec((S, 128), lambda p, i: (0, 2 * npair + p))],
        [pl.BlockSpec((QT, 128), lambda p, i: (i, p)), pl.BlockSpec((QT, 256), lambda p, i: (i, p))],
        [jax.ShapeDtypeStruct((S, SB_W), F32), jax.ShapeDtypeStruct((S, 2 * SB_W), F32)])
    return pl.pallas_call(
        body, grid=(npair, nq), in_specs=in_specs, out_specs=out_specs, out_shape=out_shape,
        input_output_aliases=aliases, scratch_shapes=scratch, name=name, compiler_params=_cp())(qkv, qkv, qkv, *extra)


def attn_bwd(qkv, rt, do, name, ex=None):
    S = qkv.shape[0]
    npair = SB_W // 128
    nq = S // QT

    def body(*refs):
        if ex is None:
            (q_ref, k_ref, v_ref, rt_ref, do_ref), (dq_ref, dk_ref, dv_ref), copies = refs[:5], refs[5:8], None
        else:
            (q_ref, k_ref, v_ref, rt_ref, do_ref), (dq_ref, dk_ref, dv_ref), copies = ex.split(refs, 5, 3)

            @pl.when(jnp.logical_and(pl.program_id(0) == 0, pl.program_id(1) == 0))
            def _():
                for cp in copies():
                    cp.start()

        qi = pl.program_id(1)

        @pl.when(qi == 0)
        def _():
            dk_ref[...] = jnp.zeros_like(dk_ref)
            dv_ref[...] = jnp.zeros_like(dv_ref)

        seg, dcol, m_gt, m_lt = _sb_consts()
        qs, dos, qq, dd, rtot = [], [], [], [], []
        for a in range(QSUB):
            rows = slice(a * CHUNK, (a + 1) * CHUNK)
            qs.append(q_ref[rows, :])
            dos.append(do_ref[rows, :].astype(MXU_DT))
            qq.append(_stack_heads(qs[a], seg))
            dd.append(_stack_heads(dos[a], seg))
            rtot.append(rt_ref[rows, :])
        causal2 = jnp.concatenate([dcol < 0, dcol < 0], axis=1)

        def step(kb, carry, diag):
            off = pl.multiple_of(kb * CHUNK, CHUNK)
            kk = _stack_heads(k_ref[pl.ds(off, CHUNK), :], seg)
            vv = _stack_heads(v_ref[pl.ds(off, CHUNK), :], seg)
            act = [a for a in range(QSUB) if diag is None or a >= diag]
            z = [_dot(qs[a], kk, "nt") for a in act]
            da = [_dot(dos[a], vv, "nt") for a in act]
            sp = [_softplus(x) for x in z]
            lnb = [jnp.where(causal2, -s, 0.0) if a == diag else -s for a, s in zip(act, sp)]
            cs = [_cumdot2(x, m_gt) for x in lnb]
            lc = [[carry[5 * a + 1 + h] + cs[n][h][:, CHUNK:] for h in range(2)] for n, a in enumerate(act)]
            att = []
            for n, a in enumerate(act):
                btw = jnp.concatenate([cs[n][h][:, :CHUNK] - lc[n][h] for h in range(2)], axis=1)
                e = jnp.exp(z[n] - sp[n] + btw + rtot[a])
                att.append(jnp.where(causal2, e, 0.0) if a == diag else e)
            g = [da[n] * att[n] for n in range(len(act))]
            cg = [_cumdot2(x, m_lt) for x in g]
            dz = []
            for n, a in enumerate(act):
                sig = jnp.exp(z[n] - sp[n])
                pre = jnp.concatenate([carry[5 * a + 3 + h] + cg[n][h][:, :CHUNK] for h in range(2)], axis=1)
                d = g[n] * (1.0 - sig) - pre * sig
                dz.append((jnp.where(causal2, d, 0.0) if a == diag else d).astype(MXU_DT))
            attb = [x.astype(MXU_DT) for x in att]
            dqc = [_dot(x, kk, "nn") for x in dz]
            dkc = [_dot(jnp.concatenate([dz[n][:, :CHUNK], dz[n][:, CHUNK:]], axis=0), qq[a], "tn")
                   for n, a in enumerate(act)]
            dvc = [_dot(jnp.concatenate([attb[n][:, :CHUNK], attb[n][:, CHUNK:]], axis=0), dd[a], "tn")
                   for n, a in enumerate(act)]
            dk_ref[pl.ds(off, CHUNK), :] += functools.reduce(lambda x, y: x + y, dkc)
            dv_ref[pl.ds(off, CHUNK), :] += functools.reduce(lambda x, y: x + y, dvc)
            new = list(carry)
            for n, a in enumerate(act):
                new[5 * a] = carry[5 * a] + dqc[n]
                for h in range(2):
                    new[5 * a + 1 + h] = lc[n][h]
                    new[5 * a + 3 + h] = carry[5 * a + 3 + h] + cg[n][h][:, CHUNK:]
            return tuple(new)

        z0 = jnp.zeros((CHUNK, 128), F32)
        res = lax.fori_loop(0, QSUB * qi, lambda kb, c: step(kb, c, None), (z0,) * (5 * QSUB))
        for j in range(QSUB):
            res = step(QSUB * qi + j, res, j)
        for a in range(QSUB):
            dq_ref[a * CHUNK:(a + 1) * CHUNK, :] = res[5 * a]

        if ex is not None:
            @pl.when(jnp.logical_and(pl.program_id(0) == npair - 1, pl.program_id(1) == nq - 1))
            def _():
                for cp in copies():
                    cp.wait()

    blk = pl.BlockSpec((QT, 128), lambda p, i: (i, p))
    full = pl.BlockSpec((S, 128), lambda p, i: (0, p))
    sh = jax.ShapeDtypeStruct((S, SB_W), F32)
    in_specs, out_specs, out_shape, aliases, scratch, extra = _with_exchange(
        ex, 5, 3,
        [blk, pl.BlockSpec((S, 128), lambda p, i: (0, npair + p)), pl.BlockSpec((S, 128), lambda p, i: (0, 2 * npair + p)),
         pl.BlockSpec((QT, 256), lambda p, i: (i, p)), blk],
        [blk, full, full], [sh, sh, sh])
    return pl.pallas_call(
        body, grid=(npair, nq), in_specs=in_specs, out_specs=out_specs, out_shape=out_shape,
        input_output_aliases=aliases, scratch_shapes=scratch, name=name,
        compiler_params=_cp())(qkv, qkv, qkv, rt, do, *extra)


def _adamw_math(w, g, m, v):
    m = ADAM_B1 * m + (1.0 - ADAM_B1) * g
    v = ADAM_B2 * v + (1.0 - ADAM_B2) * (g * g)
    m_hat = m / BC1
    v_hat = v / BC2
    delta = -ADAM_LR * (m_hat / (jnp.sqrt(v_hat) + ADAM_EPS) + ADAM_WD * w)
    return delta, m, v


def adamw_layer(w4, m4, v4, g, outs, l, name, tr):
    L, R, C = w4.shape
    n_alias = 0 if outs is None else 4

    def body(*refs):
        w_ref, m_ref, v_ref, g_ref = refs[:4]
        go_ref, d_ref, mo_ref, vo_ref = refs[4 + n_alias:]
        g = g_ref[...]
        d, m, v = _adamw_math(w_ref[0], g, m_ref[0], v_ref[0])
        go_ref[0] = g
        d_ref[0] = d
        mo_ref[0] = m
        vo_ref[0] = v

    st = pl.BlockSpec((1, tr, C), lambda i: (l, i, 0))
    sh = jax.ShapeDtypeStruct((L, R, C), F32)
    return pl.pallas_call(
        body, grid=(R // tr,),
        in_specs=[st, st, st, pl.BlockSpec((tr, C), lambda i: (i, 0))] + [ANY] * n_alias,
        out_specs=[st, st, st, st], out_shape=[sh, sh, sh, sh],
        input_output_aliases={4 + i: i for i in range(n_alias)}, name=name,
        compiler_params=_cp())(w4, m4, v4, g, *(outs or ()))


def adamw_flat(w, m, v, g, name):
    R, C = w.shape

    def body(w_ref, m_ref, v_ref, g_ref, d_ref, mo_ref, vo_ref):
        d, m2, v2 = _adamw_math(w_ref[...], g_ref[...], m_ref[...], v_ref[...])
        d_ref[...] = d
        mo_ref[...] = m2
        vo_ref[...] = v2

    full = pl.BlockSpec((R, C), lambda i: (0, 0))
    sh = jax.ShapeDtypeStruct((R, C), F32)
    return pl.pallas_call(body, grid=(1,), in_specs=[full] * 4, out_specs=[full] * 3, out_shape=[sh] * 3,
                          name=name, compiler_params=_cp())(w, m, v, g)


def add_half(g, l1, c, name, tr):
    nk, R, C = g.shape
    Rh = R // 2
    nt = Rh // tr

    def body(c_ref, g_ref, l_ref, o_ref):
        o_ref[...] = (g_ref[...].astype(F32) + l_ref[...].astype(F32)).astype(o_ref.dtype)

    gs = pltpu.PrefetchScalarGridSpec(
        num_scalar_prefetch=1, grid=(nk, nt),
        in_specs=[pl.BlockSpec((1, tr, C), lambda k, t, c_ref: (k, c_ref[0] * nt + t, 0)),
                  pl.BlockSpec((1, tr, C), lambda k, t, c_ref: (k, t, 0))],
        out_specs=pl.BlockSpec((1, tr, C), lambda k, t, c_ref: (k, t, 0)))
    return pl.pallas_call(body, grid_spec=gs, out_shape=jax.ShapeDtypeStruct((nk, Rh, C), g.dtype), name=name,
                          compiler_params=_cp())(c, g, l1)


def sum_chips(p, l2, kc, name, tr):
    nk, Rh, C = p.shape

    def body(k_ref, p_ref, a_ref, b_ref, c_ref, o_ref):
        o_ref[0] = ((p_ref[0].astype(F32) + a_ref[0].astype(F32)) + b_ref[0].astype(F32)) + c_ref[0].astype(F32)

    def slot(j):
        return pl.BlockSpec((1, tr, C), lambda t, k_ref: (j, t, 0))

    gs = pltpu.PrefetchScalarGridSpec(
        num_scalar_prefetch=1, grid=(Rh // tr,),
        in_specs=[pl.BlockSpec((1, tr, C), lambda t, k_ref: (k_ref[0], t, 0)), slot(0), slot(1), slot(2)],
        out_specs=pl.BlockSpec((1, tr, C), lambda t, k_ref: (k_ref[1], t, 0)))
    return pl.pallas_call(body, grid_spec=gs, out_shape=jax.ShapeDtypeStruct((2, Rh, C), F32), name=name,
                          compiler_params=_cp())(kc, p, l2, l2, l2)


def place_slot(a, idx, nslot, name, tr):
    R, C = a.shape

    def body(i_ref, a_ref, o_ref):
        o_ref[0] = a_ref[...]

    gs = pltpu.PrefetchScalarGridSpec(
        num_scalar_prefetch=1, grid=(R // tr,),
        in_specs=[pl.BlockSpec((tr, C), lambda t, i_ref: (t, 0))],
        out_specs=pl.BlockSpec((1, tr, C), lambda t, i_ref: (i_ref[0], t, 0)))
    return pl.pallas_call(body, grid_spec=gs, out_shape=jax.ShapeDtypeStruct((nslot, R, C), a.dtype), name=name,
                          compiler_params=_cp())(idx, a)


def _place():
    x, y, c = lax.axis_index("x"), lax.axis_index("y"), lax.axis_index("c")
    chips = [(1 - x, y), (x, 1 - y), (1 - x, 1 - y)]
    return x, y, c, chips


def _rcopy(src, dst, send_sems, recv_sems, k, to):
    return pltpu.make_async_remote_copy(src_ref=src, dst_ref=dst, send_sem=send_sems.at[k], recv_sem=recv_sems.at[k],
                                        device_id=to, device_id_type=MESH)


def gather_exchange(shards, bufs):
    def build(in_refs, buf_refs, out_refs, send_sems, recv_sems):
        x, y, c, chips = _place()
        cps = []
        for i, ref in enumerate(in_refs):
            rh = shards[i].shape[0] // 2
            for j, chip in enumerate(chips):
                rows = pl.ds(c * rh, rh)
                cps.append(_rcopy(ref.at[rows, :], buf_refs[i].at[2 * x + y].at[rows, :], send_sems, recv_sems,
                                  3 * i + j, (*chip, c)))
        return cps

    return Exchange(shards, bufs, [], 3 * len(shards), build)


def scatter_exchange(ps):
    def build(in_refs, buf_refs, out_refs, send_sems, recv_sems):
        x, y, c, chips = _place()
        return [_rcopy(ref.at[2 * chip[0] + chip[1]], out_refs[i].at[j], send_sems, recv_sems, 3 * i + j, (*chip, c))
                for i, ref in enumerate(in_refs) for j, chip in enumerate(chips)]

    outs = [jax.ShapeDtypeStruct((3,) + p.shape[1:], p.dtype) for p in ps]
    return Exchange(ps, [], outs, 3 * len(ps), build)


def run_exchange(ex, name):
    def body(*refs):
        _, _, copies = ex.split(refs, 0, 0)
        cps = copies()
        for cp in cps:
            cp.start()
        for cp in cps:
            cp.wait()

    in_specs, out_specs, out_shape, aliases, scratch, extra = _with_exchange(ex, 0, 0, [], [], [])
    return pl.pallas_call(body, in_specs=in_specs, out_specs=out_specs, out_shape=out_shape,
                          input_output_aliases=aliases, scratch_shapes=scratch, name=name)(*extra)


def gather_forward(bufs, name):
    n = len(bufs)

    def body(*refs):
        outs = refs[n:2 * n]
        send_sems, recv_sems = refs[2 * n:]
        x, y, c, chips = _place()
        cps = []
        for i in range(n):
            rh = bufs[i].shape[1] // 2
            for j, chip in enumerate(chips):
                landed = outs[i].at[2 * chip[0] + chip[1]].at[pl.ds(c * rh, rh), :]
                cps.append(_rcopy(landed, landed, send_sems, recv_sems, 3 * i + j, (x, y, 1 - c)))
        for cp in cps:
            cp.start()
        for cp in cps:
            cp.wait()

    return pl.pallas_call(
        body, in_specs=[ANY] * n, out_specs=[ANY] * n,
        out_shape=[jax.ShapeDtypeStruct(b.shape, b.dtype) for b in bufs],
        input_output_aliases={i: i for i in range(n)},
        scratch_shapes=[pltpu.SemaphoreType.DMA((3 * n,)), pltpu.SemaphoreType.DMA((3 * n,))], name=name)(*bufs)


def exchange_sibling_half(gs, name):
    n = len(gs)

    def body(*refs):
        ins, outs = refs[:n], refs[n:2 * n]
        send_sems, recv_sems = refs[2 * n:]
        x, y, c, _ = _place()
        cps = []
        for i in range(n):
            rh = gs[i].shape[1] // 2
            cp = pltpu.make_async_remote_copy(
                src_ref=ins[i].at[:, pl.ds((1 - c) * rh, rh), :], dst_ref=outs[i], send_sem=send_sems.at[i],
                recv_sem=recv_sems.at[i], device_id=(x, y, 1 - c), device_id_type=MESH)
            cp.start()
            cps.append(cp)
        for cp in cps:
            cp.wait()

    return pl.pallas_call(
        body, in_specs=[ANY] * n, out_specs=[ANY] * n,
        out_shape=[jax.ShapeDtypeStruct((g.shape[0], g.shape[1] // 2, g.shape[2]), g.dtype) for g in gs],
        scratch_shapes=[pltpu.SemaphoreType.DMA((n,)), pltpu.SemaphoreType.DMA((n,))],
        name=name)(*gs)


def join_sibling_halves(fs, name):
    n = len(fs)

    def body(*refs):
        outs = refs[n:2 * n]
        send_sems, recv_sems = refs[2 * n:]
        x, y, c, _ = _place()
        cps = [_rcopy(outs[i].at[c], outs[i].at[c], send_sems, recv_sems, i, (x, y, 1 - c)) for i in range(n)]
        for cp in cps:
            cp.start()
        for cp in cps:
            cp.wait()

    return pl.pallas_call(
        body, in_specs=[ANY] * n, out_specs=[ANY] * n,
        out_shape=[jax.ShapeDtypeStruct(f.shape, f.dtype) for f in fs],
        input_output_aliases={i: i for i in range(n)},
        scratch_shapes=[pltpu.SemaphoreType.DMA((n,)), pltpu.SemaphoreType.DMA((n,))], name=name)(*fs)


def all_reduce_small(v, name):
    R, C = v.shape

    def body(v_ref, sum_ref, all_ref, send_sems, recv_sems, local_sem):
        x, y, c, chips = _place()
        me, sib = (x, y, c), (x, y, 1 - c)

        def slab(px, py, pc):
            return all_ref.at[4 * px + 2 * py + pc]

        def copy(k, block, to, src=None):
            return pltpu.make_async_remote_copy(
                src_ref=slab(*block) if src is None else src, dst_ref=slab(*block), send_sem=send_sems.at[k],
                recv_sem=recv_sems.at[k], device_id=to, device_id_type=MESH)

        mine = pltpu.make_async_copy(v_ref, slab(*me), local_sem)
        mine.start()
        first = [copy(0, me, sib, src=v_ref)] + [copy(1 + j, me, (*chip, c), src=v_ref) for j, chip in enumerate(chips)]
        for cp in first:
            cp.start()
        passed = [copy(4 + j, (*chip, c), sib) for j, chip in enumerate(chips)]
        for j, chip in enumerate(chips):
            copy(1 + j, (*chip, c), me).wait_recv()
            passed[j].start()
        copy(0, sib, me).wait_recv()
        for j, chip in enumerate(chips):
            copy(4 + j, (*chip, 1 - c), me).wait_recv()
        for cp in first + passed:
            cp.wait_send()
        mine.wait()
        acc = all_ref[0]
        for d in range(1, 8):
            acc = acc + all_ref[d]
        sum_ref[...] = acc

    vm = pl.BlockSpec(memory_space=pltpu.VMEM)
    return pl.pallas_call(
        body, in_specs=[vm], out_specs=[vm, vm],
        out_shape=[jax.ShapeDtypeStruct((R, C), F32), jax.ShapeDtypeStruct((8, R, C), F32)],
        scratch_shapes=[pltpu.SemaphoreType.DMA((7,)), pltpu.SemaphoreType.DMA((7,)), pltpu.SemaphoreType.DMA],
        name=name, compiler_params=_cp())(v)[0]


SMALL = ("mix_norm_g", "conv_w", "conv_b", "conv_ln_g", "conv_ln_b", "sg_ln_g", "sg_ln_b", "sg_w", "sg_b",
         "q_norm_g", "k_norm_g", "out_norm_g", "ffn_norm_g")
BIG = ("w_in", "w_out", "w_gate_up", "w_down")
WEIGHTS = ("mix_norm_g", "w_in", "conv_w", "conv_b", "conv_ln_g", "conv_ln_b", "sg_ln_g", "sg_ln_b", "sg_w", "sg_b",
           "q_norm_g", "k_norm_g", "out_norm_g", "w_out", "ffn_norm_g", "w_gate_up", "w_down")
ADAM_ROWS = {"w_in": 256, "w_out": 128, "w_gate_up": 128, "w_down": 176}
RS_ROWS = {"w_in": 256, "w_out": 128, "w_gate_up": 256, "w_down": 176}
AG_ROWS = {"w_in": 512, "w_out": 256, "w_gate_up": 512, "w_down": 352}


def _pack(parts):
    flat = jnp.concatenate([p.reshape(-1) for p in parts])
    n = flat.shape[0]
    rows = -(-n // (8 * 128)) * 8
    return jnp.pad(flat, (0, rows * 128 - n)).reshape(rows, 128)


def _unpack(buf, shapes):
    flat = buf.reshape(-1)
    out, off = [], 0
    for s in shapes:
        n = 1
        for d in s:
            n *= d
        out.append(flat[off:off + n].reshape(s))
        off += n
    return out


def layer_forward(x, l, P, W, ex=None, after=None):
    sv = {"x": x}
    sv["h"] = rms_fwd(x, P["mix_norm_g"], l, f"rms_mix_{l}")
    sv["proj"] = mm_colblk(sv["h"], W["w_in"], F32, f"mm_in_{l}")
    sv["yc"] = conv_fwd(sv["proj"], P["conv_w"], P["conv_b"], P["conv_ln_g"], P["conv_ln_b"], l, f"conv_fwd_{l}")
    sv["ys"] = sg_fwd(sv["proj"], P["sg_ln_g"], P["sg_ln_b"], P["sg_wt"], P["sg_bt"], l, f"sg_fwd_{l}")
    sv["qkv"] = qkv_fwd(sv["proj"], P["q_norm_g"], P["k_norm_g"], l, f"qkv_fwd_{l}")
    sv["yb"], sv["rt"], *moved = attn_fwd(sv["qkv"], f"attn_fwd_{l}", ex)
    if after is not None:
        W = after(moved)
    sv["yn"] = outnorm_fwd(sv["yc"], sv["ys"], sv["yb"], P["out_norm_g"], l, f"outnorm_fwd_{l}")
    sv["x1"] = mm_res(sv["yn"], W["w_out"].reshape(D_MODEL, D_MODEL), x, f"mm_out_{l}")
    sv["h2"] = rms_fwd(sv["x1"], P["ffn_norm_g"], l, f"rms_ffn_{l}")
    sv["g"], sv["u"], sv["act"] = ffn_up(sv["h2"], W["w_gate_up"], f"ffn_up_{l}")
    x2 = mm_res(sv["act"], W["w_down"].reshape(FFN, D_MODEL), sv["x1"], f"mm_down_{l}")
    return x2, sv, moved


def layer_backward(dx2, l, P, W, sv, mid=None):
    gb, gs = {}, {}
    wdown = W["w_down"].reshape(FFN, D_MODEL)
    dgu = ffn_down_bwd(dx2, wdown, sv["g"], sv["u"], f"ffn_down_bwd_{l}")
    gb["w_down"] = mm_wgrad(sv["act"], dx2, 1408, 512, f"wgrad_down_{l}", False).reshape(N_CHIP, FFN // N_CHIP, D_MODEL)
    dh2 = mm_dgrad_colblk(dgu, W["w_gate_up"], f"dgrad_gu_{l}")
    gb["w_gate_up"] = mm_wgrad(sv["h2"], dgu, 512, 1408, f"wgrad_gu_{l}", True)
    dx1, gs["ffn_norm_g"] = rms_bwd(dh2, sv["x1"], P["ffn_norm_g"], l, dx2, f"rms_ffn_bwd_{l}")
    dyn = mm_dgrad(dx1, W["w_out"].reshape(D_MODEL, D_MODEL), f"dgrad_out_{l}")
    gb["w_out"] = mm_wgrad(sv["yn"], dx1, 512, 1024, f"wgrad_out_{l}", False).reshape(N_CHIP, D_MODEL // N_CHIP, D_MODEL)
    dyc, dys, dyb, gs["out_norm_g"] = outnorm_bwd(dyn, sv["yc"], sv["ys"], sv["yb"], P["out_norm_g"], l,
                                                  f"outnorm_bwd_{l}")
    ex, done = mid(gb) if mid is not None else (None, None)
    dq, dk, dv, *moved = attn_bwd(sv["qkv"], sv["rt"], dyb, f"attn_bwd_{l}", ex)
    if done is not None:
        done(moved)
    dps, gs["sg_ln_g"], gs["sg_ln_b"], gs["sg_w"], dbt = sg_bwd(
        sv["proj"], dys, P["sg_ln_g"], P["sg_ln_b"], P["sg_wt"], P["sg_wtt"], P["sg_bt"], l, f"sg_bwd_{l}")
    gs["sg_b"] = dbt.T
    dpc, dcw, gs["conv_b"], gs["conv_ln_g"], gs["conv_ln_b"] = conv_bwd(
        sv["proj"], dyc, P["conv_w"], P["conv_b"], P["conv_ln_g"], P["conv_ln_b"], l, f"conv_bwd_{l}")
    gs["conv_w"] = dcw[:CONV_K]
    dproj, dgq, dgk = qkv_bwd(sv["proj"], dq, dk, dv, P["q_norm_g"], P["k_norm_g"], dpc, dps, l, f"qkv_bwd_{l}")
    gs["q_norm_g"] = dgq[0, :HEAD_DIM] + dgq[0, HEAD_DIM:]
    gs["k_norm_g"] = dgk[0, :HEAD_DIM] + dgk[0, HEAD_DIM:]
    dh = mm_dgrad_colblk(dproj, W["w_in"], f"dgrad_in_{l}")
    gb["w_in"] = mm_wgrad(sv["h"], dproj, 1024, IN_W // N_CHIP, f"wgrad_in_{l}", True)
    dx, gs["mix_norm_g"] = rms_bwd(dh, sv["x"], P["mix_norm_g"], l, dx1, f"rms_mix_bwd_{l}")
    return dx, gb, gs


def reduce_start(gb, names, l, c1):
    gl = [gb[n] for n in names]
    l1 = exchange_sibling_half(gl, f"rs_sibling_{names[0]}_{l}")
    return [add_half(g, a, c1, f"rs_add_{n}_{l}", RS_ROWS[n]) for n, g, a in zip(names, gl, l1)]


def reduce_finish(ps, l2, names, l, kc):
    fs = [sum_chips(p, a, kc, f"rs_sum_{n}_{l}", RS_ROWS[n]) for n, p, a in zip(names, ps, l2)]
    full = join_sibling_halves(fs, f"rs_join_{names[0]}_{l}")
    return {n: f.reshape(2 * f.shape[1], f.shape[2]) for n, f in zip(names, full)}


def kernel(x, mix_norm_g, w_in, conv_w, conv_b, conv_ln_g, conv_ln_b, sg_ln_g, sg_ln_b, sg_w, sg_b, q_norm_g, k_norm_g, out_norm_g, w_out, ffn_norm_g, w_gate_up, w_down, loss_target, m_mix_norm_g, m_w_in, m_conv_w, m_conv_b, m_conv_ln_g, m_conv_ln_b, m_sg_ln_g, m_sg_ln_b, m_sg_w, m_sg_b, m_q_norm_g, m_k_norm_g, m_out_norm_g, m_w_out, m_ffn_norm_g, m_w_gate_up, m_w_down, v_mix_norm_g, v_w_in, v_conv_w, v_conv_b, v_conv_ln_g, v_conv_ln_b, v_sg_ln_g, v_sg_ln_b, v_sg_w, v_sg_b, v_q_norm_g, v_k_norm_g, v_out_norm_g, v_w_out, v_ffn_norm_g, v_w_gate_up, v_w_down):
    w = dict(mix_norm_g=mix_norm_g, w_in=w_in, conv_w=conv_w, conv_b=conv_b, conv_ln_g=conv_ln_g, conv_ln_b=conv_ln_b,
             sg_ln_g=sg_ln_g, sg_ln_b=sg_ln_b, sg_w=sg_w, sg_b=sg_b, q_norm_g=q_norm_g, k_norm_g=k_norm_g,
             out_norm_g=out_norm_g, w_out=w_out, ffn_norm_g=ffn_norm_g, w_gate_up=w_gate_up, w_down=w_down)
    m = dict(mix_norm_g=m_mix_norm_g, w_in=m_w_in, conv_w=m_conv_w, conv_b=m_conv_b, conv_ln_g=m_conv_ln_g,
             conv_ln_b=m_conv_ln_b, sg_ln_g=m_sg_ln_g, sg_ln_b=m_sg_ln_b, sg_w=m_sg_w, sg_b=m_sg_b,
             q_norm_g=m_q_norm_g, k_norm_g=m_k_norm_g, out_norm_g=m_out_norm_g, w_out=m_w_out,
             ffn_norm_g=m_ffn_norm_g, w_gate_up=m_w_gate_up, w_down=m_w_down)
    v = dict(mix_norm_g=v_mix_norm_g, w_in=v_w_in, conv_w=v_conv_w, conv_b=v_conv_b, conv_ln_g=v_conv_ln_g,
             conv_ln_b=v_conv_ln_b, sg_ln_g=v_sg_ln_g, sg_ln_b=v_sg_ln_b, sg_w=v_sg_w, sg_b=v_sg_b,
             q_norm_g=v_q_norm_g, k_norm_g=v_k_norm_g, out_norm_g=v_out_norm_g, w_out=v_w_out,
             ffn_norm_g=v_ffn_norm_g, w_gate_up=v_w_gate_up, w_down=v_w_down)
    L = DEPTH
    xi, yi, ci = lax.axis_index("x"), lax.axis_index("y"), lax.axis_index("c")
    kme = 2 * xi + yi
    c1 = ci.astype(jnp.int32).reshape(1)
    k1 = kme.astype(jnp.int32).reshape(1)
    kc = jnp.stack([kme, ci]).astype(jnp.int32)

    def gather_parts(items):
        shards = [w[n][l].astype(BF16) for n, l in items]
        bufs = [place_slot(s, k1, N_CHIP, f"ag_own_{n}_{l}", AG_ROWS[n]) for (n, l), s in zip(items, shards)]
        return shards, bufs

    shards, bufs = gather_parts([("w_in", 0)])
    cw_sh = jnp.pad(conv_w.reshape(L * CONV_K, CONV_W // N_CHIP), ((0, 128 - L * CONV_K), (0, 0)))
    cw_buf = place_slot(cw_sh, k1, N_CHIP, "ag_own_conv", 128)
    *bufs, cw_buf = run_exchange(gather_exchange(shards + [cw_sh], bufs + [cw_buf]), "ag_chips_first")
    *bufs, cw_buf = gather_forward(bufs + [cw_buf], "ag_sibling_first")
    W = [{} for _ in range(L)]
    W[0]["w_in"] = bufs[0]
    cw_all = cw_buf[:, :L * CONV_K].reshape(N_CHIP, L, CONV_K, CONV_W // N_CHIP)
    cw_full = jnp.transpose(cw_all, (1, 2, 0, 3)).reshape(L, CONV_K, CONV_W)

    tril = jnp.tril(jnp.ones((CHUNK, CHUNK), bool))
    sg_wt = jnp.where(tril, sg_w, 0.0)
    P = {
        "mix_norm_g": mix_norm_g.reshape(L, 1, D_MODEL), "ffn_norm_g": ffn_norm_g.reshape(L, 1, D_MODEL),
        "out_norm_g": out_norm_g.reshape(L, 1, D_MODEL),
        "conv_w": jnp.pad(cw_full, ((0, 0), (0, 1), (0, 0))), "conv_b": conv_b.reshape(L, 1, CONV_W),
        "conv_ln_g": conv_ln_g.reshape(L, 1, CONV_W), "conv_ln_b": conv_ln_b.reshape(L, 1, CONV_W),
        "sg_ln_g": sg_ln_g.reshape(L, 1, SG_W), "sg_ln_b": sg_ln_b.reshape(L, 1, SG_W),
        "sg_wt": sg_wt.astype(MXU_DT), "sg_wtt": jnp.swapaxes(sg_wt, 2, 3).astype(MXU_DT),
        "sg_bt": jnp.swapaxes(sg_b, 1, 2),
        "q_norm_g": jnp.tile(q_norm_g, (1, 2)).reshape(L, 1, 128), "k_norm_g": jnp.tile(k_norm_g, (1, 2)).reshape(L, 1, 128),
    }

    h = x[0]
    saved = []
    for l in range(L):
        items = [("w_out", l), ("w_gate_up", l), ("w_down", l)] + ([("w_in", l + 1)] if l + 1 < L else [])

        def after(moved, l=l, items=items):
            for (n, ll), a in zip(items, gather_forward(moved, f"ag_sibling_{l}")):
                W[ll][n] = a
            return W[l]

        h, sv, _ = layer_forward(h, l, P, W[l], gather_exchange(*gather_parts(items)), after)
        saved.append(sv)
    dy, loss_part = loss_head(h, loss_target[0], "loss_head")
    loss = lax.psum(loss_part[0, 0], ("x", "y", "c"))

    outs = {n: None for n in BIG}
    small_grads = [None] * L

    def finish(names, ps, l2, lyr):
        red = reduce_finish(ps, l2, names, lyr, kc)
        for n in names:
            outs[n] = adamw_layer(w[n], m[n], v[n], red[n], outs[n], lyr, f"adamw_{n}_{lyr}", ADAM_ROWS[n])

    early = ["w_down", "w_gate_up", "w_out"]
    ps_in = None
    for l in reversed(range(L)):
        def mid(gb, l=l, ps_in=ps_in):
            ps = reduce_start(gb, early, l, c1)

            def done(moved):
                finish(early, ps, moved[:len(early)], l)
                if ps_in is not None:
                    finish(["w_in"], ps_in, moved[len(early):], l + 1)

            return scatter_exchange(ps + (ps_in or [])), done

        dy, gb, small_grads[l] = layer_backward(dy, l, P, W[l], saved[l], mid)
        ps_in = reduce_start(gb, ["w_in"], l, c1)
    finish(["w_in"], ps_in, run_exchange(scatter_exchange(ps_in), "rs_chips_last"), 0)

    shapes = [(L,) + small_grads[0][n].shape for n in SMALL]
    packed = _pack([jnp.stack([small_grads[l][n] for l in range(L)]) for n in SMALL])
    gsum = dict(zip(SMALL, _unpack(all_reduce_small(packed, "ar_small"), shapes)))
    gsum["conv_w"] = lax.dynamic_slice_in_dim(gsum["conv_w"], kme * (CONV_W // N_CHIP), CONV_W // N_CHIP, axis=2)
    gsum = {n: gsum[n].reshape(w[n].shape) for n in SMALL}
    lshapes = [w[n].shape for n in SMALL]
    d_p, m_p, v_p = adamw_flat(_pack([w[n] for n in SMALL]), _pack([m[n] for n in SMALL]),
                               _pack([v[n] for n in SMALL]), _pack([gsum[n] for n in SMALL]), "adamw_small")
    d_s = dict(zip(SMALL, _unpack(d_p, lshapes)))
    m_s = dict(zip(SMALL, _unpack(m_p, lshapes)))
    v_s = dict(zip(SMALL, _unpack(v_p, lshapes)))

    grads = {n: (outs[n][0] if n in BIG else gsum[n]) for n in WEIGHTS}
    delta = {n: (outs[n][1] if n in BIG else d_s[n]) for n in WEIGHTS}
    new_m = {n: (outs[n][2] if n in BIG else m_s[n]) for n in WEIGHTS}
    new_v = {n: (outs[n][3] if n in BIG else v_s[n]) for n in WEIGHTS}
    return (loss, dy[None], *[grads[n] for n in WEIGHTS], *[delta[n] for n in WEIGHTS],
            *[new_m[n] for n in WEIGHTS], *[new_v[n] for n in WEIGHTS])
```

```python
import functools

import jax
import jax.numpy as jnp
from jax import lax
from jax.experimental import pallas as pl
from jax.experimental.pallas import tpu as pltpu

F32 = jnp.float32
BF16 = jnp.bfloat16
MXU_DT = jnp.bfloat16
GRAD_WIRE_DT = jnp.bfloat16

D_MODEL = 1024
DEPTH = 4
HEAD_DIM = 64
CONV_W = 256
SG_W = 256
SB_W = 512
CONV_K = 31
CHUNK = 128
OFF_SG = 2 * CONV_W
OFF_SB = OFF_SG + 2 * SG_W
IN_W = OFF_SB + 3 * SB_W
FFN = 2816
N_CHIP = 4
RMS_EPS = 1e-6
LN_EPS = 1e-5
ADAM_LR = 0.001
ADAM_B1 = 0.9
ADAM_B2 = 0.999
ADAM_EPS = 1e-08
ADAM_WD = 0.01
ADAM_STEP = 10
BC1 = 1.0 - ADAM_B1 ** ADAM_STEP
BC2 = 1.0 - ADAM_B2 ** ADAM_STEP
HALO = 32
MESH = pl.DeviceIdType.MESH
ANY = pl.BlockSpec(memory_space=pl.ANY)
VMEM_LIMIT = 56 * 1024 * 1024


def _cp(**kw):
    return pltpu.CompilerParams(vmem_limit_bytes=VMEM_LIMIT, **kw)


def _dot(a, b, dims):
    dn = {"nn": (((1,), (0,)), ((), ())), "nt": (((1,), (1,)), ((), ())), "tn": (((0,), (0,)), ((), ()))}[dims]
    return lax.dot_general(a.astype(MXU_DT), b.astype(MXU_DT), dn, preferred_element_type=F32)


def _cumdot(x, m):
    hi = x.astype(MXU_DT)
    lo = (x - hi.astype(F32)).astype(MXU_DT)
    dn = (((1,), (0,)), ((), ()))
    return (lax.dot_general(hi, m, dn, preferred_element_type=F32)
            + lax.dot_general(lo, m, dn, preferred_element_type=F32))


def _iota(shape, axis):
    return lax.broadcasted_iota(jnp.int32, shape, axis)


def _rms(x, g):
    return x * lax.rsqrt(jnp.mean(x * x, axis=-1, keepdims=True) + RMS_EPS) * g


def _ln(x, g, b):
    mu = jnp.mean(x, axis=-1, keepdims=True)
    xc = x - mu
    var = jnp.mean(xc * xc, axis=-1, keepdims=True)
    return xc * lax.rsqrt(var + LN_EPS) * g + b


def _glu(val, gate):
    return val * jax.nn.sigmoid(gate)


def _ln_silu(c, g, b):
    return jax.nn.silu(_ln(c, g, b))


_ERF_ALPHA = (-2.72614225801306e-10, 2.77068142495902e-08, -2.10102402082508e-06, -5.69250639462346e-05,
              -7.34990630326855e-04, -2.95459980854025e-03, -1.60960333262415e-02)
_ERF_BETA = (-1.45660718464996e-05, -2.13374055278905e-04, -1.68282697438203e-03, -7.37332916720468e-03,
             -1.42647390514189e-02)


def _erf(x):
    x = jnp.clip(x, -4.0, 4.0)
    x2 = x * x
    p = jnp.full_like(x, _ERF_ALPHA[0])
    for a in _ERF_ALPHA[1:]:
        p = p * x2 + a
    q = jnp.full_like(x, _ERF_BETA[0])
    for b in _ERF_BETA[1:]:
        q = q * x2 + b
    return x * p / q


@jax.custom_jvp
def _gelu(x):
    return 0.5 * x * (1.0 + _erf(x * (2.0 ** -0.5)))


@_gelu.defjvp
def _gelu_jvp(primals, tangents):
    (x,), (t,) = primals, tangents
    cdf = 0.5 * (1.0 + _erf(x * (2.0 ** -0.5)))
    pdf = jnp.exp(-0.5 * x * x) * ((2.0 * jnp.pi) ** -0.5)
    return x * cdf, t * (cdf + x * pdf)


def _sg_pre(uvp, g, b):
    uv = _gelu(uvp)
    return uv[:, :SG_W], _ln(uv[:, SG_W:], g, b)


def _outnorm(yc, ys, yb, g):
    return jnp.concatenate([_rms(yc, g[:, :CONV_W]), _rms(ys, g[:, CONV_W:CONV_W + SG_W]),
                            _rms(yb, g[:, CONV_W + SG_W:])], axis=-1)


def _qkv_fn(pq, pk, pv, gq, gk):
    outs = []
    for p, g, sc in ((pq, gq, HEAD_DIM ** -0.5), (pk, gk, 1.0)):
        for s in range(SB_W // 128):
            x = p[:, 128 * s:128 * (s + 1)]
            seg = _iota(x.shape, 1) < HEAD_DIM
            x2 = x * x
            s0 = jnp.sum(jnp.where(seg, x2, 0.0), axis=-1, keepdims=True)
            s1 = jnp.sum(jnp.where(seg, 0.0, x2), axis=-1, keepdims=True)
            ms = jnp.where(seg, s0, s1) * (1.0 / HEAD_DIM)
            outs.append(x * lax.rsqrt(ms + RMS_EPS) * (g * sc))
    outs.append(pv)
    return jnp.concatenate(outs, axis=-1)


def _swiglu(g, u):
    return jax.nn.silu(g) * u


def _softplus(z):
    return jnp.maximum(z, 0.0) + jnp.log(1.0 + jnp.exp(-jnp.abs(z)))


def mm_colblk(a, wb, out_dtype, name, tm=1024):
    S, K = a.shape
    nb, _, C = wb.shape

    def body(a_ref, w_ref, o_ref):
        o_ref[...] = _dot(a_ref[...], w_ref[0], "nn").astype(o_ref.dtype)

    return pl.pallas_call(
        body, grid=(nb, S // tm),
        in_specs=[pl.BlockSpec((tm, K), lambda k, i: (i, 0)), pl.BlockSpec((1, K, C), lambda k, i: (k, 0, 0))],
        out_specs=pl.BlockSpec((tm, C), lambda k, i: (i, k)),
        out_shape=jax.ShapeDtypeStruct((S, nb * C), out_dtype), name=name, compiler_params=_cp())(a, wb)


def mm_res(a, w, res, name, tm=512):
    S, K = a.shape
    N = w.shape[1]

    def body(a_ref, w_ref, r_ref, o_ref):
        o_ref[...] = r_ref[...] + _dot(a_ref[...], w_ref[...], "nn")

    return pl.pallas_call(
        body, grid=(S // tm,),
        in_specs=[pl.BlockSpec((tm, K), lambda i: (i, 0)), pl.BlockSpec((K, N), lambda i: (0, 0)),
                  pl.BlockSpec((tm, N), lambda i: (i, 0))],
        out_specs=pl.BlockSpec((tm, N), lambda i: (i, 0)),
        out_shape=jax.ShapeDtypeStruct((S, N), F32), name=name, compiler_params=_cp())(a, w, res)


def ffn_up(h2, wgu, name, tm=512):
    S, K = h2.shape
    C = wgu.shape[2]

    def body(h_ref, wg_ref, wu_ref, g_ref, u_ref, a_ref):
        h = h_ref[...]
        g = _dot(h, wg_ref[0], "nn")
        u = _dot(h, wu_ref[0], "nn")
        g_ref[...] = g.astype(BF16)
        u_ref[...] = u.astype(BF16)
        a_ref[...] = _swiglu(g, u).astype(BF16)

    o = pl.BlockSpec((tm, C), lambda j, i: (i, j))
    sh = jax.ShapeDtypeStruct((S, 2 * C), BF16)
    return pl.pallas_call(
        body, grid=(2, S // tm),
        in_specs=[pl.BlockSpec((tm, K), lambda j, i: (i, 0)), pl.BlockSpec((1, K, C), lambda j, i: (j, 0, 0)),
                  pl.BlockSpec((1, K, C), lambda j, i: (2 + j, 0, 0))],
        out_specs=[o, o, o], out_shape=[sh, sh, sh], name=name, compiler_params=_cp())(h2, wgu, wgu)


def ffn_down_bwd(dx2, wdown, g, u, name, tm=512):
    S, N = dx2.shape
    C = FFN // 2

    def body(d_ref, w_ref, g_ref, u_ref, o_ref):
        d = d_ref[...]
        for j in range(2):
            cols = slice(j * C, (j + 1) * C)
            dact = _dot(d, w_ref[cols, :], "nt")
            _, vjp = jax.vjp(_swiglu, g_ref[:, cols].astype(F32), u_ref[:, cols].astype(F32))
            dg, du = vjp(dact)
            o_ref[:, cols] = dg.astype(BF16)
            o_ref[:, FFN + j * C:FFN + (j + 1) * C] = du.astype(BF16)

    row = pl.BlockSpec((tm, FFN), lambda i: (i, 0))
    return pl.pallas_call(
        body, grid=(S // tm,),
        in_specs=[pl.BlockSpec((tm, N), lambda i: (i, 0)), pl.BlockSpec((FFN, N), lambda i: (0, 0)), row, row],
        out_specs=pl.BlockSpec((tm, 2 * FFN), lambda i: (i, 0)),
        out_shape=jax.ShapeDtypeStruct((S, 2 * FFN), BF16), name=name, compiler_params=_cp())(dx2, wdown, g, u)


def mm_dgrad_colblk(do, wb, name, tm=1024):
    S = do.shape[0]
    nb, K, C = wb.shape

    def body(d_ref, w_ref, o_ref):
        k = pl.program_id(1)
        r = _dot(d_ref[...], w_ref[0], "nt")

        @pl.when(k == 0)
        def _():
            o_ref[...] = r

        @pl.when(k != 0)
        def _():
            o_ref[...] += r

    return pl.pallas_call(
        body, grid=(S // tm, nb),
        in_specs=[pl.BlockSpec((tm, C), lambda i, k: (i, k)), pl.BlockSpec((1, K, C), lambda i, k: (k, 0, 0))],
        out_specs=pl.BlockSpec((tm, K), lambda i, k: (i, 0)),
        out_shape=jax.ShapeDtypeStruct((S, K), F32), name=name, compiler_params=_cp())(do, wb)


def mm_dgrad(do, w, name, tm=1024):
    S, N = do.shape
    K = w.shape[0]

    def body(d_ref, w_ref, o_ref):
        o_ref[...] = _dot(d_ref[...], w_ref[...], "nt")

    return pl.pallas_call(
        body, grid=(S // tm,),
        in_specs=[pl.BlockSpec((tm, N), lambda i: (i, 0)), pl.BlockSpec((K, N), lambda i: (0, 0))],
        out_specs=pl.BlockSpec((tm, K), lambda i: (i, 0)),
        out_shape=jax.ShapeDtypeStruct((S, K), F32), name=name, compiler_params=_cp())(do, w)


def mm_wgrad(a, do, tk, tn, name, blocked):
    S, K = a.shape
    N = do.shape[1]

    def body(a_ref, d_ref, o_ref):
        r = _dot(a_ref[...], d_ref[...], "tn").astype(GRAD_WIRE_DT)
        if blocked:
            o_ref[0] = r
        else:
            o_ref[...] = r

    if blocked:
        out_spec = pl.BlockSpec((1, tk, tn), lambda n, j: (n, j, 0))
        out_shape = jax.ShapeDtypeStruct((N // tn, K, tn), GRAD_WIRE_DT)
    else:
        out_spec = pl.BlockSpec((tk, tn), lambda n, j: (j, n))
        out_shape = jax.ShapeDtypeStruct((K, N), GRAD_WIRE_DT)
    return pl.pallas_call(
        body, grid=(N // tn, K // tk),
        in_specs=[pl.BlockSpec((S, tk), lambda n, j: (0, j)), pl.BlockSpec((S, tn), lambda n, j: (0, n))],
        out_specs=out_spec, out_shape=out_shape, name=name, compiler_params=_cp())(a, do)


def rms_fwd(x, g3, l, name, tm=512):
    S, N = x.shape

    def body(x_ref, g_ref, o_ref):
        o_ref[...] = _rms(x_ref[...], g_ref[0]).astype(BF16)

    return pl.pallas_call(
        body, grid=(S // tm,),
        in_specs=[pl.BlockSpec((tm, N), lambda i: (i, 0)), pl.BlockSpec((1, 1, N), lambda i: (l, 0, 0))],
        out_specs=pl.BlockSpec((tm, N), lambda i: (i, 0)),
        out_shape=jax.ShapeDtypeStruct((S, N), BF16), name=name, compiler_params=_cp())(x, g3)


def rms_bwd(dh, x, g3, l, dres, name, tm=512):
    S, N = x.shape

    def body(dh_ref, x_ref, g_ref, r_ref, dx_ref, dg_ref):
        _, vjp = jax.vjp(_rms, x_ref[...], g_ref[0])
        dx, dg = vjp(dh_ref[...])
        dx_ref[...] = r_ref[...] + dx

        @pl.when(pl.program_id(0) == 0)
        def _():
            dg_ref[...] = jnp.zeros_like(dg_ref)

        dg_ref[...] += dg

    row = pl.BlockSpec((tm, N), lambda i: (i, 0))
    return pl.pallas_call(
        body, grid=(S // tm,),
        in_specs=[row, row, pl.BlockSpec((1, 1, N), lambda i: (l, 0, 0)), row],
        out_specs=[row, pl.BlockSpec((1, N), lambda i: (0, 0))],
        out_shape=[jax.ShapeDtypeStruct((S, N), F32), jax.ShapeDtypeStruct((1, N), F32)],
        name=name, compiler_params=_cp())(dh, x, g3, dres)


def outnorm_fwd(yc, ys, yb, g3, l, name, tm=512):
    S = yc.shape[0]

    def body(c_ref, s_ref, b_ref, g_ref, o_ref):
        o_ref[...] = _outnorm(c_ref[...], s_ref[...], b_ref[...], g_ref[0]).astype(BF16)

    return pl.pallas_call(
        body, grid=(S // tm,),
        in_specs=[pl.BlockSpec((tm, CONV_W), lambda i: (i, 0)), pl.BlockSpec((tm, SG_W), lambda i: (i, 0)),
                  pl.BlockSpec((tm, SB_W), lambda i: (i, 0)), pl.BlockSpec((1, 1, D_MODEL), lambda i: (l, 0, 0))],
        out_specs=pl.BlockSpec((tm, D_MODEL), lambda i: (i, 0)),
        out_shape=jax.ShapeDtypeStruct((S, D_MODEL), BF16), name=name, compiler_params=_cp())(yc, ys, yb, g3)


def outnorm_bwd(dyn, yc, ys, yb, g3, l, name, tm=512):
    S = yc.shape[0]

    def body(d_ref, c_ref, s_ref, b_ref, g_ref, dc_ref, ds_ref, db_ref, dg_ref):
        _, vjp = jax.vjp(_outnorm, c_ref[...], s_ref[...], b_ref[...], g_ref[0])
        dc, ds, db, dg = vjp(d_ref[...])
        dc_ref[...] = dc
        ds_ref[...] = ds
        db_ref[...] = db

        @pl.when(pl.program_id(0) == 0)
        def _():
            dg_ref[...] = jnp.zeros_like(dg_ref)

        dg_ref[...] += dg

    sc = pl.BlockSpec((tm, CONV_W), lambda i: (i, 0))
    ss = pl.BlockSpec((tm, SG_W), lambda i: (i, 0))
    sb = pl.BlockSpec((tm, SB_W), lambda i: (i, 0))
    return pl.pallas_call(
        body, grid=(S // tm,),
        in_specs=[pl.BlockSpec((tm, D_MODEL), lambda i: (i, 0)), sc, ss, sb,
                  pl.BlockSpec((1, 1, D_MODEL), lambda i: (l, 0, 0))],
        out_specs=[sc, ss, sb, pl.BlockSpec((1, D_MODEL), lambda i: (0, 0))],
        out_shape=[jax.ShapeDtypeStruct((S, CONV_W), F32), jax.ShapeDtypeStruct((S, SG_W), F32),
                   jax.ShapeDtypeStruct((S, SB_W), F32), jax.ShapeDtypeStruct((1, D_MODEL), F32)],
        name=name, compiler_params=_cp())(dyn, yc, ys, yb, g3)


def loss_head(y, t, name, tm=512):
    S, N = y.shape

    def body(y_ref, t_ref, dy_ref, l_ref):
        e = y_ref[...] - t_ref[...]
        dy_ref[...] = e * (1.0 / N)

        @pl.when(pl.program_id(0) == 0)
        def _():
            l_ref[...] = jnp.zeros_like(l_ref)

        l_ref[...] += (0.5 / N) * jnp.sum(jnp.sum(e * e, axis=-1, keepdims=True), axis=0, keepdims=True)

    row = pl.BlockSpec((tm, N), lambda i: (i, 0))
    return pl.pallas_call(
        body, grid=(S // tm,), in_specs=[row, row],
        out_specs=[row, pl.BlockSpec((1, 1), lambda i: (0, 0))],
        out_shape=[jax.ShapeDtypeStruct((S, N), F32), jax.ShapeDtypeStruct((1, 1), F32)],
        name=name, compiler_params=_cp())(y, t)


def _conv_taps(a, w_ref, T):
    acc = jnp.zeros((T, CONV_W), F32)
    for j in range(CONV_K):
        sh = CONV_K - 1 - j
        r = a if sh == 0 else pltpu.roll(a, sh, 0)
        acc = acc + r[HALO:HALO + T] * w_ref[pl.ds(j, 1), :]
    return acc


def conv_fwd(proj, cw, cb, lg, lb, l, name, T=256):
    S = proj.shape[0]
    nt = S // T

    def body(p_ref, w_ref, cb_ref, lg_ref, lb_ref, y_ref, hc_s):
        hc_s[0:HALO, :] = jnp.zeros((HALO, CONV_W), F32)

        def fill(i, _):
            r0 = pl.multiple_of(i * T, T)
            pc = p_ref[pl.ds(r0, T), :]
            hc_s[pl.ds(r0 + HALO, T), :] = _glu(pc[:, :CONV_W], pc[:, CONV_W:])
            return 0

        lax.fori_loop(0, nt, fill, 0)

        def tile(i, _):
            r0 = pl.multiple_of(i * T, T)
            c = _conv_taps(hc_s[pl.ds(r0, T + HALO), :], w_ref.at[0], T) + cb_ref[0]
            y_ref[pl.ds(r0, T), :] = _ln_silu(c, lg_ref[0], lb_ref[0])
            return 0

        lax.fori_loop(0, nt, tile, 0)

    vec = pl.BlockSpec((1, 1, CONV_W), lambda i: (l, 0, 0))
    return pl.pallas_call(
        body, grid=(1,),
        in_specs=[pl.BlockSpec((S, 2 * CONV_W), lambda i: (0, 0)), pl.BlockSpec((1, 32, CONV_W), lambda i: (l, 0, 0)),
                  vec, vec, vec],
        out_specs=pl.BlockSpec((S, CONV_W), lambda i: (0, 0)),
        out_shape=jax.ShapeDtypeStruct((S, CONV_W), F32),
        scratch_shapes=[pltpu.VMEM((S + HALO, CONV_W), F32)], name=name, compiler_params=_cp())(proj, cw, cb, lg, lb)


def conv_bwd(proj, dy, cw, cb, lg, lb, l, name, T=256):
    S = proj.shape[0]
    nt = S // T

    def body(p_ref, dy_ref, w_ref, cb_ref, lg_ref, lb_ref, dp_ref, dw_ref, dcb_ref, dlg_ref, dlb_ref, hc_s, dc_s):
        hc_s[0:HALO, :] = jnp.zeros((HALO, CONV_W), F32)
        dc_s[S:S + HALO, :] = jnp.zeros((HALO, CONV_W), F32)
        dw_ref[...] = jnp.zeros_like(dw_ref)
        dcb_ref[...] = jnp.zeros_like(dcb_ref)
        dlg_ref[...] = jnp.zeros_like(dlg_ref)
        dlb_ref[...] = jnp.zeros_like(dlb_ref)

        def fill(i, _):
            r0 = pl.multiple_of(i * T, T)
            pc = p_ref[pl.ds(r0, T), :]
            hc_s[pl.ds(r0 + HALO, T), :] = _glu(pc[:, :CONV_W], pc[:, CONV_W:])
            return 0

        lax.fori_loop(0, nt, fill, 0)

        def tile(i, _):
            r0 = pl.multiple_of(i * T, T)
            a = hc_s[pl.ds(r0, T + HALO), :]
            c = _conv_taps(a, w_ref.at[0], T) + cb_ref[0]
            _, vjp = jax.vjp(_ln_silu, c, lg_ref[0], lb_ref[0])
            dc, dlg, dlb = vjp(dy_ref[pl.ds(r0, T), :])
            dc_s[pl.ds(r0, T), :] = dc
            dcb_ref[...] += jnp.sum(dc, axis=0, keepdims=True)
            dlg_ref[...] += dlg
            dlb_ref[...] += dlb
            for j in range(CONV_K):
                sh = CONV_K - 1 - j
                r = a if sh == 0 else pltpu.roll(a, sh, 0)
                dw_ref[pl.ds(j, 1), :] += jnp.sum(dc * r[HALO:HALO + T], axis=0, keepdims=True)
            return 0

        lax.fori_loop(0, nt, tile, 0)

        def back(i, _):
            r0 = pl.multiple_of(i * T, T)
            de = dc_s[pl.ds(r0, T + HALO), :]
            n = T + HALO
            dh = jnp.zeros((T, CONV_W), F32)
            for j in range(CONV_K):
                sh = CONV_K - 1 - j
                r = de if sh == 0 else pltpu.roll(de, n - sh, 0)
                dh = dh + r[0:T] * w_ref[0, pl.ds(j, 1), :]
            pc = p_ref[pl.ds(r0, T), :]
            _, vjp = jax.vjp(_glu, pc[:, :CONV_W], pc[:, CONV_W:])
            dval, dgate = vjp(dh)
            dp_ref[pl.ds(r0, T), :] = jnp.concatenate([dval, dgate], axis=-1).astype(BF16)
            return 0

        lax.fori_loop(0, nt, back, 0)

    vec = pl.BlockSpec((1, 1, CONV_W), lambda i: (l, 0, 0))
    ovec = pl.BlockSpec((1, CONV_W), lambda i: (0, 0))
    vsh = jax.ShapeDtypeStruct((1, CONV_W), F32)
    return pl.pallas_call(
        body, grid=(1,),
        in_specs=[pl.BlockSpec((S, 2 * CONV_W), lambda i: (0, 0)), pl.BlockSpec((S, CONV_W), lambda i: (0, 0)),
                  pl.BlockSpec((1, 32, CONV_W), lambda i: (l, 0, 0)), vec, vec, vec],
        out_specs=[pl.BlockSpec((S, 2 * CONV_W), lambda i: (0, 0)), pl.BlockSpec((32, CONV_W), lambda i: (0, 0)),
                   ovec, ovec, ovec],
        out_shape=[jax.ShapeDtypeStruct((S, 2 * CONV_W), BF16), jax.ShapeDtypeStruct((32, CONV_W), F32), vsh, vsh, vsh],
        scratch_shapes=[pltpu.VMEM((S + HALO, CONV_W), F32), pltpu.VMEM((S + HALO, CONV_W), F32)],
        name=name, compiler_params=_cp())(proj, dy, cw, cb, lg, lb)


def _sg_mix(wt_ref, v, bt):
    slabs = []
    for s in range(2):
        vs = v[:, 128 * s:128 * (s + 1)]
        seg = _iota(vs.shape, 1) < HEAD_DIM
        p0 = _dot(wt_ref[2 * s], vs, "nn") + bt[:, 2 * s:2 * s + 1]
        p1 = _dot(wt_ref[2 * s + 1], vs, "nn") + bt[:, 2 * s + 1:2 * s + 2]
        slabs.append(jnp.where(seg, p0, p1))
    return jnp.concatenate(slabs, axis=-1)


def sg_fwd(proj, lg, lb, wt, bt, l, name):
    S = proj.shape[0]

    def body(p_ref, lg_ref, lb_ref, w_ref, b_ref, y_ref):
        u, v = _sg_pre(p_ref[...], lg_ref[0], lb_ref[0])
        y_ref[...] = u * _sg_mix(w_ref.at[0], v, b_ref[0])

    vec = pl.BlockSpec((1, 1, SG_W), lambda i: (l, 0, 0))
    return pl.pallas_call(
        body, grid=(S // CHUNK,),
        in_specs=[pl.BlockSpec((CHUNK, 2 * SG_W), lambda i: (i, 1)), vec, vec,
                  pl.BlockSpec((1, 4, CHUNK, CHUNK), lambda i: (l, 0, 0, 0)),
                  pl.BlockSpec((1, CHUNK, 4), lambda i: (l, 0, 0))],
        out_specs=pl.BlockSpec((CHUNK, SG_W), lambda i: (i, 0)),
        out_shape=jax.ShapeDtypeStruct((S, SG_W), F32), name=name, compiler_params=_cp())(proj, lg, lb, wt, bt)


def sg_bwd(proj, dy, lg, lb, wt, wtt, bt, l, name):
    S = proj.shape[0]

    def body(p_ref, dy_ref, lg_ref, lb_ref, w_ref, wt_ref, b_ref, dp_ref, dlg_ref, dlb_ref, dw_ref, db_ref):
        @pl.when(pl.program_id(0) == 0)
        def _():
            dlg_ref[...] = jnp.zeros_like(dlg_ref)
            dlb_ref[...] = jnp.zeros_like(dlb_ref)
            dw_ref[...] = jnp.zeros_like(dw_ref)
            db_ref[...] = jnp.zeros_like(db_ref)

        (u, v), vjp = jax.vjp(_sg_pre, p_ref[...], lg_ref[0], lb_ref[0])
        dy = dy_ref[...]
        mixed = _sg_mix(w_ref.at[0], v, b_ref[0])
        du = dy * mixed
        dm = dy * u
        tril = _iota((CHUNK, CHUNK), 1) <= _iota((CHUNK, CHUNK), 0)
        dvs = []
        for s in range(2):
            dms = dm[:, 128 * s:128 * (s + 1)]
            vs = v[:, 128 * s:128 * (s + 1)]
            seg = _iota(dms.shape, 1) < HEAD_DIM
            halves = (jnp.where(seg, dms, 0.0), jnp.where(seg, 0.0, dms))
            dv_h = []
            for e in range(2):
                h = 2 * s + e
                db_ref[:, h:h + 1] += jnp.sum(halves[e], axis=-1, keepdims=True)
                dw_ref[h] += jnp.where(tril, _dot(halves[e], vs, "nt"), 0.0)
                dv_h.append(_dot(wt_ref[0, h], halves[e], "nn"))
            dvs.append(dv_h[0] + dv_h[1])
        dp, dlg, dlb = vjp((du, jnp.concatenate(dvs, axis=-1)))
        dp_ref[...] = dp.astype(BF16)
        dlg_ref[...] += dlg
        dlb_ref[...] += dlb

    vec = pl.BlockSpec((1, 1, SG_W), lambda i: (l, 0, 0))
    wsp = pl.BlockSpec((1, 4, CHUNK, CHUNK), lambda i: (l, 0, 0, 0))
    ovec = pl.BlockSpec((1, SG_W), lambda i: (0, 0))
    return pl.pallas_call(
        body, grid=(S // CHUNK,),
        in_specs=[pl.BlockSpec((CHUNK, 2 * SG_W), lambda i: (i, 1)), pl.BlockSpec((CHUNK, SG_W), lambda i: (i, 0)),
                  vec, vec, wsp, wsp, pl.BlockSpec((1, CHUNK, 4), lambda i: (l, 0, 0))],
        out_specs=[pl.BlockSpec((CHUNK, 2 * SG_W), lambda i: (i, 0)), ovec, ovec,
                   pl.BlockSpec((4, CHUNK, CHUNK), lambda i: (0, 0, 0)), pl.BlockSpec((CHUNK, 4), lambda i: (0, 0))],
        out_shape=[jax.ShapeDtypeStruct((S, 2 * SG_W), BF16), jax.ShapeDtypeStruct((1, SG_W), F32),
                   jax.ShapeDtypeStruct((1, SG_W), F32), jax.ShapeDtypeStruct((4, CHUNK, CHUNK), F32),
                   jax.ShapeDtypeStruct((CHUNK, 4), F32)],
        name=name, compiler_params=_cp())(proj, dy, lg, lb, wt, wtt, bt)


def qkv_fwd(proj, gq, gk, l, name, tm=512):
    S = proj.shape[0]

    def body(pq_ref, pk_ref, pv_ref, gq_ref, gk_ref, o_ref):
        o_ref[...] = _qkv_fn(pq_ref[...], pk_ref[...], pv_ref[...], gq_ref[0], gk_ref[0]).astype(BF16)

    vec = pl.BlockSpec((1, 1, 128), lambda i: (l, 0, 0))
    qb = OFF_SB // SB_W
    return pl.pallas_call(
        body, grid=(S // tm,),
        in_specs=[pl.BlockSpec((tm, SB_W), lambda i: (i, qb)), pl.BlockSpec((tm, SB_W), lambda i: (i, qb + 1)),
                  pl.BlockSpec((tm, SB_W), lambda i: (i, qb + 2)), vec, vec],
        out_specs=pl.BlockSpec((tm, 3 * SB_W), lambda i: (i, 0)),
        out_shape=jax.ShapeDtypeStruct((S, 3 * SB_W), BF16), name=name, compiler_params=_cp())(proj, proj, proj, gq, gk)


def qkv_bwd(proj, dq, dk, dv, gq, gk, dpc, dps, l, name, tm=512):
    S = proj.shape[0]

    def body(pq_ref, pk_ref, pv_ref, dq_ref, dk_ref, dv_ref, gq_ref, gk_ref, dpc_ref, dps_ref, dp_ref, dgq_ref,
             dgk_ref):
        _, vjp = jax.vjp(_qkv_fn, pq_ref[...], pk_ref[...], pv_ref[...], gq_ref[0], gk_ref[0])
        dpq, dpk, dpv, dgq, dgk = vjp(jnp.concatenate([dq_ref[...], dk_ref[...], dv_ref[...]], axis=-1))
        dp_ref[:, :OFF_SG] = dpc_ref[...]
        dp_ref[:, OFF_SG:OFF_SB] = dps_ref[...]
        dp_ref[:, OFF_SB:] = jnp.concatenate([dpq, dpk, dpv], axis=-1).astype(BF16)

        @pl.when(pl.program_id(0) == 0)
        def _():
            dgq_ref[...] = jnp.zeros_like(dgq_ref)
            dgk_ref[...] = jnp.zeros_like(dgk_ref)

        dgq_ref[...] += dgq
        dgk_ref[...] += dgk

    vec = pl.BlockSpec((1, 1, 128), lambda i: (l, 0, 0))
    part = pl.BlockSpec((tm, SB_W), lambda i: (i, 0))
    ovec = pl.BlockSpec((1, 128), lambda i: (0, 0))
    qb = OFF_SB // SB_W
    return pl.pallas_call(
        body, grid=(S // tm,),
        in_specs=[pl.BlockSpec((tm, SB_W), lambda i: (i, qb)), pl.BlockSpec((tm, SB_W), lambda i: (i, qb + 1)),
                  pl.BlockSpec((tm, SB_W), lambda i: (i, qb + 2)), part, part, part, vec, vec, part, part],
        out_specs=[pl.BlockSpec((tm, IN_W), lambda i: (i, 0)), ovec, ovec],
        out_shape=[jax.ShapeDtypeStruct((S, IN_W), BF16), jax.ShapeDtypeStruct((1, 128), F32),
                   jax.ShapeDtypeStruct((1, 128), F32)],
        name=name, compiler_params=_cp())(proj, proj, proj, dq, dk, dv, gq, gk, dpc, dps)


QSUB_FWD = 4
QSUB_BWD = 8


def _sb_consts():
    row = _iota((CHUNK, CHUNK), 0)
    col = _iota((CHUNK, CHUNK), 1)
    ones = jnp.ones((CHUNK, CHUNK), MXU_DT)
    m_gt = jnp.concatenate([(row > col).astype(MXU_DT), ones], axis=1)
    m_lt = jnp.concatenate([(row < col).astype(MXU_DT), ones], axis=1)
    return (col < HEAD_DIM, col - row, jnp.concatenate([m_gt, m_gt], axis=0), jnp.concatenate([m_lt, m_lt], axis=0))


def _split_heads(x, seg):
    z = jnp.zeros_like(x)
    return (jnp.where(seg, x, z), jnp.where(seg, z, x))


def _stack_heads(x, seg):
    return jnp.concatenate(_split_heads(x, seg), axis=0)


def _cumdot2(x, m2):
    hi = x.astype(MXU_DT)
    lo = (x - hi.astype(F32)).astype(MXU_DT)
    dn = (((1,), (0,)), ((), ()))
    return [lax.dot_general(jnp.concatenate([hi[:, CHUNK * h:CHUNK * (h + 1)], lo[:, CHUNK * h:CHUNK * (h + 1)]], axis=1),
                            m2, dn, preferred_element_type=F32) for h in range(2)]


class Exchange:
    def __init__(self, ins, bufs, outs, n, build):
        self.ins, self.bufs, self.outs, self.n, self.build = list(ins), list(bufs), list(outs), n, build

    def split(self, refs, n_in, n_out):
        a, b, o = len(self.ins), len(self.bufs), len(self.outs)
        main_in = refs[:n_in]
        c_in = refs[n_in:n_in + a]
        rest = refs[n_in + a + b:]
        main_out = rest[:n_out]
        c_buf = rest[n_out:n_out + b]
        c_out = rest[n_out + b:n_out + b + o]
        send, recv = rest[n_out + b + o:]
        return main_in, main_out, lambda: self.build(c_in, c_buf, c_out, send, recv)


def _with_exchange(ex, n_in, n_out, in_specs, out_specs, out_shape):
    if ex is None:
        return in_specs, out_specs, out_shape, {}, [], []
    a, b = len(ex.ins), len(ex.bufs)
    in_specs = list(in_specs) + [ANY] * (a + b)
    out_specs = list(out_specs) + [ANY] * (b + len(ex.outs))
    out_shape = list(out_shape) + [jax.ShapeDtypeStruct(x.shape, x.dtype) for x in ex.bufs] + list(ex.outs)
    aliases = {n_in + a + i: n_out + i for i in range(b)}
    scratch = [pltpu.SemaphoreType.DMA((ex.n,)), pltpu.SemaphoreType.DMA((ex.n,))]
    return in_specs, out_specs, out_shape, aliases, scratch, ex.ins + ex.bufs


def attn_fwd(qkv, name, ex=None):
    S = qkv.shape[0]
    npair = SB_W // 128
    QSUB = QSUB_FWD
    QT = QSUB * CHUNK
    nq = S // QT

    def body(*refs):
        if ex is None:
            (q_ref, k_ref, v_ref), (o_ref, rt_ref), copies = refs[:3], refs[3:5], None
        else:
            (q_ref, k_ref, v_ref), (o_ref, rt_ref), copies = ex.split(refs, 3, 2)

            @pl.when(jnp.logical_and(pl.program_id(0) == 0, pl.program_id(1) == 0))
            def _():
                for cp in copies():
                    cp.start()

        qi = pl.program_id(1)
        seg, dcol, m_gt, _ = _sb_consts()
        qs = [q_ref[a * CHUNK:(a + 1) * CHUNK, :] for a in range(QSUB)]
        causal2 = jnp.concatenate([dcol < 0, dcol < 0], axis=1)

        def step(kb, carry, diag):
            off = pl.multiple_of(kb * CHUNK, CHUNK)
            kk = _stack_heads(k_ref[pl.ds(off, CHUNK), :], seg)
            vv = _stack_heads(v_ref[pl.ds(off, CHUNK), :], seg)
            act = [a for a in range(QSUB) if diag is None or a >= diag]
            z = [_dot(qs[a], kk, "nt") for a in act]
            sp = [_softplus(x) for x in z]
            lnb = [jnp.where(causal2, -s, 0.0) if a == diag else -s for a, s in zip(act, sp)]
            cs = [_cumdot2(x, m_gt) for x in lnb]
            att = []
            for n, a in enumerate(act):
                base = z[n] - sp[n]
                e = jnp.exp(base + jnp.concatenate([cs[n][h][:, :CHUNK] + carry[3 * a + 1 + h] for h in range(2)], axis=1))
                att.append(jnp.where(causal2, e, 0.0) if a == diag else e)
            pv = [_dot(x, vv, "nn") for x in att]
            new = list(carry)
            for n, a in enumerate(act):
                new[3 * a] = carry[3 * a] + pv[n]
                for h in range(2):
                    new[3 * a + 1 + h] = carry[3 * a + 1 + h] + cs[n][h][:, CHUNK:]
            return tuple(new)

        z0 = jnp.zeros((CHUNK, CHUNK), F32)
        res = (z0,) * (3 * QSUB)
        for j in reversed(range(QSUB)):
            res = step(QSUB * qi + j, res, j)
        res = lax.fori_loop(0, QSUB * qi, lambda it, c: step(QSUB * qi - 1 - it, c, None), res)
        for a in range(QSUB):
            rows = slice(a * CHUNK, (a + 1) * CHUNK)
            o_ref[rows, :] = res[3 * a]
            rt_ref[rows, :] = jnp.concatenate([res[3 * a + 1], res[3 * a + 2]], axis=1)

        if ex is not None:
            @pl.when(jnp.logical_and(pl.program_id(0) == npair - 1, pl.program_id(1) == nq - 1))
            def _():
                for cp in copies():
                    cp.wait()

    in_specs, out_specs, out_shape, aliases, scratch, extra = _with_exchange(
        ex, 3, 2,
        [pl.BlockSpec((QT, 128), lambda p, i: (i, p)), pl.BlockSpec((S, 128), lambda p, i: (0, npair + p)),
         pl.BlockSpec((S, 128), lambda p, i: (0, 2 * npair + p))],
        [pl.BlockSpec((QT, 128), lambda p, i: (i, p)), pl.BlockSpec((QT, 256), lambda p, i: (i, p))],
        [jax.ShapeDtypeStruct((S, SB_W), F32), jax.ShapeDtypeStruct((S, 2 * SB_W), F32)])
    return pl.pallas_call(
        body, grid=(npair, nq), in_specs=in_specs, out_specs=out_specs, out_shape=out_shape,
        input_output_aliases=aliases, scratch_shapes=scratch, name=name, compiler_params=_cp())(qkv, qkv, qkv, *extra)


def attn_bwd(qkv, rt, do, name, ex=None):
    S = qkv.shape[0]
    npair = SB_W // 128
    QSUB = QSUB_BWD
    QT = QSUB * CHUNK
    nq = S // QT

    def body(*refs):
        if ex is None:
            (q_ref, k_ref, v_ref, rt_ref, do_ref), (dq_ref, dk_ref, dv_ref), copies = refs[:5], refs[5:8], None
        else:
            (q_ref, k_ref, v_ref, rt_ref, do_ref), (dq_ref, dk_ref, dv_ref), copies = ex.split(refs, 5, 3)

            @pl.when(jnp.logical_and(pl.program_id(0) == 0, pl.program_id(1) == 0))
            def _():
                for cp in copies():
                    cp.start()

        qi = pl.program_id(1)

        @pl.when(qi == 0)
        def _():
            dk_ref[...] = jnp.zeros_like(dk_ref)
            dv_ref[...] = jnp.zeros_like(dv_ref)

        seg, dcol, m_gt, m_lt = _sb_consts()
        qs, dos, qq, dd, rtot = [], [], [], [], []
        for a in range(QSUB):
            rows = slice(a * CHUNK, (a + 1) * CHUNK)
            qs.append(q_ref[rows, :])
            dos.append(do_ref[rows, :].astype(MXU_DT))
            qq.append(_stack_heads(qs[a], seg))
            dd.append(_stack_heads(dos[a], seg))
            rtot.append(rt_ref[rows, :])
        causal2 = jnp.concatenate([dcol < 0, dcol < 0], axis=1)

        def step(kb, carry, diag):
            off = pl.multiple_of(kb * CHUNK, CHUNK)
            kk = _stack_heads(k_ref[pl.ds(off, CHUNK), :], seg)
            vv = _stack_heads(v_ref[pl.ds(off, CHUNK), :], seg)
            act = [a for a in range(QSUB) if diag is None or a >= diag]
            z = [_dot(qs[a], kk, "nt") for a in act]
            da = [_dot(dos[a], vv, "nt") for a in act]
            sp = [_softplus(x) for x in z]
            lnb = [jnp.where(causal2, -s, 0.0) if a == diag else -s for a, s in zip(act, sp)]
            cs = [_cumdot2(x, m_gt) for x in lnb]
            lc = [[carry[5 * a + 1 + h] + cs[n][h][:, CHUNK:] for h in range(2)] for n, a in enumerate(act)]
            att = []
            for n, a in enumerate(act):
                btw = jnp.concatenate([cs[n][h][:, :CHUNK] - lc[n][h] for h in range(2)], axis=1)
                e = jnp.exp(z[n] - sp[n] + btw + rtot[a])
                att.append(jnp.where(causal2, e, 0.0) if a == diag else e)
            g = [da[n] * att[n] for n in range(len(act))]
            cg = [_cumdot2(x, m_lt) for x in g]
            dz = []
            for n, a in enumerate(act):
                sig = jnp.exp(z[n] - sp[n])
                pre = jnp.concatenate([carry[5 * a + 3 + h] + cg[n][h][:, :CHUNK] for h in range(2)], axis=1)
                d = g[n] * (1.0 - sig) - pre * sig
                dz.append((jnp.where(causal2, d, 0.0) if a == diag else d).astype(MXU_DT))
            attb = [x.astype(MXU_DT) for x in att]
            dqc = [_dot(x, kk, "nn") for x in dz]
            dkc = [_dot(jnp.concatenate([dz[n][:, :CHUNK], dz[n][:, CHUNK:]], axis=0), qq[a], "tn")
                   for n, a in enumerate(act)]
            dvc = [_dot(jnp.concatenate([attb[n][:, :CHUNK], attb[n][:, CHUNK:]], axis=0), dd[a], "tn")
                   for n, a in enumerate(act)]
            dk_ref[pl.ds(off, CHUNK), :] += functools.reduce(lambda x, y: x + y, dkc)
            dv_ref[pl.ds(off, CHUNK), :] += functools.reduce(lambda x, y: x + y, dvc)
            new = list(carry)
            for n, a in enumerate(act):
                new[5 * a] = carry[5 * a] + dqc[n]
                for h in range(2):
                    new[5 * a + 1 + h] = lc[n][h]
                    new[5 * a + 3 + h] = carry[5 * a + 3 + h] + cg[n][h][:, CHUNK:]
            return tuple(new)

        z0 = jnp.zeros((CHUNK, 128), F32)
        res = lax.fori_loop(0, QSUB * qi, lambda kb, c: step(kb, c, None), (z0,) * (5 * QSUB))
        for j in range(QSUB):
            res = step(QSUB * qi + j, res, j)
        for a in range(QSUB):
            dq_ref[a * CHUNK:(a + 1) * CHUNK, :] = res[5 * a]

        if ex is not None:
            @pl.when(jnp.logical_and(pl.program_id(0) == npair - 1, pl.program_id(1) == nq - 1))
            def _():
                for cp in copies():
                    cp.wait()

    blk = pl.BlockSpec((QT, 128), lambda p, i: (i, p))
    full = pl.BlockSpec((S, 128), lambda p, i: (0, p))
    sh = jax.ShapeDtypeStruct((S, SB_W), F32)
    in_specs, out_specs, out_shape, aliases, scratch, extra = _with_exchange(
        ex, 5, 3,
        [blk, pl.BlockSpec((S, 128), lambda p, i: (0, npair + p)), pl.BlockSpec((S, 128), lambda p, i: (0, 2 * npair + p)),
         pl.BlockSpec((QT, 256), lambda p, i: (i, p)), blk],
        [blk, full, full], [sh, sh, sh])
    return pl.pallas_call(
        body, grid=(npair, nq), in_specs=in_specs, out_specs=out_specs, out_shape=out_shape,
        input_output_aliases=aliases, scratch_shapes=scratch, name=name,
        compiler_params=_cp())(qkv, qkv, qkv, rt, do, *extra)


def _adamw_math(w, g, m, v):
    m = ADAM_B1 * m + (1.0 - ADAM_B1) * g
    v = ADAM_B2 * v + (1.0 - ADAM_B2) * (g * g)
    m_hat = m / BC1
    v_hat = v / BC2
    delta = -ADAM_LR * (m_hat / (jnp.sqrt(v_hat) + ADAM_EPS) + ADAM_WD * w)
    return delta, m, v


def adamw_layer(w4, m4, v4, g, outs, l, name, tr):
    L, R, C = w4.shape
    n_alias = 0 if outs is None else 4

    def body(*refs):
        w_ref, m_ref, v_ref, g_ref = refs[:4]
        go_ref, d_ref, mo_ref, vo_ref = refs[4 + n_alias:]
        g = g_ref[...]
        d, m, v = _adamw_math(w_ref[0], g, m_ref[0], v_ref[0])
        go_ref[0] = g
        d_ref[0] = d
        mo_ref[0] = m
        vo_ref[0] = v

    st = pl.BlockSpec((1, tr, C), lambda i: (l, i, 0))
    sh = jax.ShapeDtypeStruct((L, R, C), F32)
    return pl.pallas_call(
        body, grid=(R // tr,),
        in_specs=[st, st, st, pl.BlockSpec((tr, C), lambda i: (i, 0))] + [ANY] * n_alias,
        out_specs=[st, st, st, st], out_shape=[sh, sh, sh, sh],
        input_output_aliases={4 + i: i for i in range(n_alias)}, name=name,
        compiler_params=_cp())(w4, m4, v4, g, *(outs or ()))


def adamw_flat(w, m, v, g, name):
    R, C = w.shape

    def body(w_ref, m_ref, v_ref, g_ref, d_ref, mo_ref, vo_ref):
        d, m2, v2 = _adamw_math(w_ref[...], g_ref[...], m_ref[...], v_ref[...])
        d_ref[...] = d
        mo_ref[...] = m2
        vo_ref[...] = v2

    full = pl.BlockSpec((R, C), lambda i: (0, 0))
    sh = jax.ShapeDtypeStruct((R, C), F32)
    return pl.pallas_call(body, grid=(1,), in_specs=[full] * 4, out_specs=[full] * 3, out_shape=[sh] * 3,
                          name=name, compiler_params=_cp())(w, m, v, g)


def add_half(g, l1, c, name, tr):
    nk, R, C = g.shape
    Rh = R // 2
    nt = Rh // tr

    def body(c_ref, g_ref, l_ref, o_ref):
        o_ref[...] = (g_ref[...].astype(F32) + l_ref[...].astype(F32)).astype(o_ref.dtype)

    gs = pltpu.PrefetchScalarGridSpec(
        num_scalar_prefetch=1, grid=(nk, nt),
        in_specs=[pl.BlockSpec((1, tr, C), lambda k, t, c_ref: (k, c_ref[0] * nt + t, 0)),
                  pl.BlockSpec((1, tr, C), lambda k, t, c_ref: (k, t, 0))],
        out_specs=pl.BlockSpec((1, tr, C), lambda k, t, c_ref: (k, t, 0)))
    return pl.pallas_call(body, grid_spec=gs, out_shape=jax.ShapeDtypeStruct((nk, Rh, C), g.dtype), name=name,
                          compiler_params=_cp())(c, g, l1)


def sum_chips(p, l2, kc, name, tr):
    nk, Rh, C = p.shape

    def body(k_ref, p_ref, a_ref, b_ref, c_ref, o_ref):
        o_ref[0] = ((p_ref[0].astype(F32) + a_ref[0].astype(F32)) + b_ref[0].astype(F32)) + c_ref[0].astype(F32)

    def slot(j):
        return pl.BlockSpec((1, tr, C), lambda t, k_ref: (j, t, 0))

    gs = pltpu.PrefetchScalarGridSpec(
        num_scalar_prefetch=1, grid=(Rh // tr,),
        in_specs=[pl.BlockSpec((1, tr, C), lambda t, k_ref: (k_ref[0], t, 0)), slot(0), slot(1), slot(2)],
        out_specs=pl.BlockSpec((1, tr, C), lambda t, k_ref: (k_ref[1], t, 0)))
    return pl.pallas_call(body, grid_spec=gs, out_shape=jax.ShapeDtypeStruct((2, Rh, C), F32), name=name,
                          compiler_params=_cp())(kc, p, l2, l2, l2)


def place_slot(a3, l, idx, nslot, dtype, name, tr):
    _, R, C = a3.shape

    def body(i_ref, a_ref, o_ref):
        o_ref[0] = a_ref[0].astype(dtype)

    gs = pltpu.PrefetchScalarGridSpec(
        num_scalar_prefetch=1, grid=(R // tr,),
        in_specs=[pl.BlockSpec((1, tr, C), lambda t, i_ref: (l, t, 0))],
        out_specs=pl.BlockSpec((1, tr, C), lambda t, i_ref: (i_ref[0], t, 0)))
    return pl.pallas_call(body, grid_spec=gs, out_shape=jax.ShapeDtypeStruct((nslot, R, C), dtype), name=name,
                          compiler_params=_cp())(idx, a3)


def _place():
    x, y, c = lax.axis_index("x"), lax.axis_index("y"), lax.axis_index("c")
    chips = [(1 - x, y), (x, 1 - y), (1 - x, 1 - y)]
    return x, y, c, chips


def _rcopy(src, dst, send_sems, recv_sems, k, to):
    return pltpu.make_async_remote_copy(src_ref=src, dst_ref=dst, send_sem=send_sems.at[k], recv_sem=recv_sems.at[k],
                                        device_id=to, device_id_type=MESH)


def gather_exchange(bufs):
    def build(in_refs, buf_refs, out_refs, send_sems, recv_sems):
        x, y, c, chips = _place()
        cps = []
        for i, ref in enumerate(buf_refs):
            rh = bufs[i].shape[1] // 2
            mine = ref.at[2 * x + y].at[pl.ds(c * rh, rh), :]
            for j, chip in enumerate(chips):
                cps.append(_rcopy(mine, mine, send_sems, recv_sems, 3 * i + j, (*chip, c)))
        return cps

    return Exchange([], bufs, [], 3 * len(bufs), build)


def scatter_exchange(ps):
    def build(in_refs, buf_refs, out_refs, send_sems, recv_sems):
        x, y, c, chips = _place()
        return [_rcopy(ref.at[2 * chip[0] + chip[1]], out_refs[i].at[j], send_sems, recv_sems, 3 * i + j, (*chip, c))
                for i, ref in enumerate(in_refs) for j, chip in enumerate(chips)]

    outs = [jax.ShapeDtypeStruct((3,) + p.shape[1:], p.dtype) for p in ps]
    return Exchange(ps, [], outs, 3 * len(ps), build)


def run_exchange(ex, name):
    def body(*refs):
        _, _, copies = ex.split(refs, 0, 0)
        cps = copies()
        for cp in cps:
            cp.start()
        for cp in cps:
            cp.wait()

    in_specs, out_specs, out_shape, aliases, scratch, extra = _with_exchange(ex, 0, 0, [], [], [])
    return pl.pallas_call(body, in_specs=in_specs, out_specs=out_specs, out_shape=out_shape,
                          input_output_aliases=aliases, scratch_shapes=scratch, name=name)(*extra)


def gather_forward(bufs, name):
    n = len(bufs)

    def body(*refs):
        outs = refs[n:2 * n]
        send_sems, recv_sems = refs[2 * n:]
        x, y, c, chips = _place()
        cps = []
        for i in range(n):
            rh = bufs[i].shape[1] // 2
            for j, chip in enumerate(chips):
                landed = outs[i].at[2 * chip[0] + chip[1]].at[pl.ds(c * rh, rh), :]
                cps.append(_rcopy(landed, landed, send_sems, recv_sems, 3 * i + j, (x, y, 1 - c)))
        for cp in cps:
            cp.start()
        for cp in cps:
            cp.wait()

    return pl.pallas_call(
        body, in_specs=[ANY] * n, out_specs=[ANY] * n,
        out_shape=[jax.ShapeDtypeStruct(b.shape, b.dtype) for b in bufs],
        input_output_aliases={i: i for i in range(n)},
        scratch_shapes=[pltpu.SemaphoreType.DMA((3 * n,)), pltpu.SemaphoreType.DMA((3 * n,))], name=name)(*bufs)


def exchange_sibling_half(gs, name):
    n = len(gs)

    def body(*refs):
        ins, outs = refs[:n], refs[n:2 * n]
        send_sems, recv_sems = refs[2 * n:]
        x, y, c, _ = _place()
        cps = []
        for i in range(n):
            rh = gs[i].shape[1] // 2
            cp = pltpu.make_async_remote_copy(
                src_ref=ins[i].at[:, pl.ds((1 - c) * rh, rh), :], dst_ref=outs[i], send_sem=send_sems.at[i],
                recv_sem=recv_sems.at[i], device_id=(x, y, 1 - c), device_id_type=MESH)
            cp.start()
            cps.append(cp)
        for cp in cps:
            cp.wait()

    return pl.pallas_call(
        body, in_specs=[ANY] * n, out_specs=[ANY] * n,
        out_shape=[jax.ShapeDtypeStruct((g.shape[0], g.shape[1] // 2, g.shape[2]), g.dtype) for g in gs],
        scratch_shapes=[pltpu.SemaphoreType.DMA((n,)), pltpu.SemaphoreType.DMA((n,))],
        name=name)(*gs)


def join_sibling_halves(fs, name):
    n = len(fs)

    def body(*refs):
        outs = refs[n:2 * n]
        send_sems, recv_sems = refs[2 * n:]
        x, y, c, _ = _place()
        cps = [_rcopy(outs[i].at[c], outs[i].at[c], send_sems, recv_sems, i, (x, y, 1 - c)) for i in range(n)]
        for cp in cps:
            cp.start()
        for cp in cps:
            cp.wait()

    return pl.pallas_call(
        body, in_specs=[ANY] * n, out_specs=[ANY] * n,
        out_shape=[jax.ShapeDtypeStruct(f.shape, f.dtype) for f in fs],
        input_output_aliases={i: i for i in range(n)},
        scratch_shapes=[pltpu.SemaphoreType.DMA((n,)), pltpu.SemaphoreType.DMA((n,))], name=name)(*fs)


def all_reduce_small(v, name):
    R, C = v.shape

    def body(v_ref, sum_ref, all_ref, send_sems, recv_sems, local_sem):
        x, y, c, chips = _place()
        me, sib = (x, y, c), (x, y, 1 - c)

        def slab(px, py, pc):
            return all_ref.at[4 * px + 2 * py + pc]

        def copy(k, block, to, src=None):
            return pltpu.make_async_remote_copy(
                src_ref=slab(*block) if src is None else src, dst_ref=slab(*block), send_sem=send_sems.at[k],
                recv_sem=recv_sems.at[k], device_id=to, device_id_type=MESH)

        mine = pltpu.make_async_copy(v_ref, slab(*me), local_sem)
        mine.start()
        first = [copy(0, me, sib, src=v_ref)] + [copy(1 + j, me, (*chip, c), src=v_ref) for j, chip in enumerate(chips)]
        for cp in first:
            cp.start()
        passed = [copy(4 + j, (*chip, c), sib) for j, chip in enumerate(chips)]
        for j, chip in enumerate(chips):
            copy(1 + j, (*chip, c), me).wait_recv()
            passed[j].start()
        copy(0, sib, me).wait_recv()
        for j, chip in enumerate(chips):
            copy(4 + j, (*chip, 1 - c), me).wait_recv()
        for cp in first + passed:
            cp.wait_send()
        mine.wait()
        acc = all_ref[0]
        for d in range(1, 8):
            acc = acc + all_ref[d]
        sum_ref[...] = acc

    vm = pl.BlockSpec(memory_space=pltpu.VMEM)
    return pl.pallas_call(
        body, in_specs=[vm], out_specs=[vm, vm],
        out_shape=[jax.ShapeDtypeStruct((R, C), F32), jax.ShapeDtypeStruct((8, R, C), F32)],
        scratch_shapes=[pltpu.SemaphoreType.DMA((7,)), pltpu.SemaphoreType.DMA((7,)), pltpu.SemaphoreType.DMA],
        name=name, compiler_params=_cp())(v)[0]


SMALL = ("mix_norm_g", "conv_w", "conv_b", "conv_ln_g", "conv_ln_b", "sg_ln_g", "sg_ln_b", "sg_w", "sg_b",
         "q_norm_g", "k_norm_g", "out_norm_g", "ffn_norm_g")
BIG = ("w_in", "w_out", "w_gate_up", "w_down")
WEIGHTS = ("mix_norm_g", "w_in", "conv_w", "conv_b", "conv_ln_g", "conv_ln_b", "sg_ln_g", "sg_ln_b", "sg_w", "sg_b",
           "q_norm_g", "k_norm_g", "out_norm_g", "w_out", "ffn_norm_g", "w_gate_up", "w_down")
ADAM_ROWS = {"w_in": 512, "w_out": 256, "w_gate_up": 256, "w_down": 352}
RS_ROWS = {"w_in": 512, "w_out": 128, "w_gate_up": 512, "w_down": 352}
AG_ROWS = {"w_in": 512, "w_out": 256, "w_gate_up": 512, "w_down": 352}


def _pack(parts):
    flat = jnp.concatenate([p.reshape(-1) for p in parts])
    n = flat.shape[0]
    rows = -(-n // (8 * 128)) * 8
    return jnp.pad(flat, (0, rows * 128 - n)).reshape(rows, 128)


def _unpack(buf, shapes):
    flat = buf.reshape(-1)
    out, off = [], 0
    for s in shapes:
        n = 1
        for d in s:
            n *= d
        out.append(flat[off:off + n].reshape(s))
        off += n
    return out


def layer_forward(x, l, P, W, ex=None, after=None):
    sv = {"x": x}
    sv["h"] = rms_fwd(x, P["mix_norm_g"], l, f"rms_mix_{l}")
    sv["proj"] = mm_colblk(sv["h"], W["w_in"], F32, f"mm_in_{l}")
    sv["yc"] = conv_fwd(sv["proj"], P["conv_w"], P["conv_b"], P["conv_ln_g"], P["conv_ln_b"], l, f"conv_fwd_{l}")
    sv["ys"] = sg_fwd(sv["proj"], P["sg_ln_g"], P["sg_ln_b"], P["sg_wt"], P["sg_bt"], l, f"sg_fwd_{l}")
    sv["qkv"] = qkv_fwd(sv["proj"], P["q_norm_g"], P["k_norm_g"], l, f"qkv_fwd_{l}")
    sv["yb"], sv["rt"], *moved = attn_fwd(sv["qkv"], f"attn_fwd_{l}", ex)
    if after is not None:
        W = after(moved)
    sv["yn"] = outnorm_fwd(sv["yc"], sv["ys"], sv["yb"], P["out_norm_g"], l, f"outnorm_fwd_{l}")
    sv["x1"] = mm_res(sv["yn"], W["w_out"].reshape(D_MODEL, D_MODEL), x, f"mm_out_{l}")
    sv["h2"] = rms_fwd(sv["x1"], P["ffn_norm_g"], l, f"rms_ffn_{l}")
    sv["g"], sv["u"], sv["act"] = ffn_up(sv["h2"], W["w_gate_up"], f"ffn_up_{l}")
    x2 = mm_res(sv["act"], W["w_down"].reshape(FFN, D_MODEL), sv["x1"], f"mm_down_{l}")
    return x2, sv, moved


def layer_backward(dx2, l, P, W, sv, mid=None):
    gb, gs = {}, {}
    wdown = W["w_down"].reshape(FFN, D_MODEL)
    dgu = ffn_down_bwd(dx2, wdown, sv["g"], sv["u"], f"ffn_down_bwd_{l}")
    gb["w_down"] = mm_wgrad(sv["act"], dx2, 1408, 512, f"wgrad_down_{l}", False).reshape(N_CHIP, FFN // N_CHIP, D_MODEL)
    dh2 = mm_dgrad_colblk(dgu, W["w_gate_up"], f"dgrad_gu_{l}")
    gb["w_gate_up"] = mm_wgrad(sv["h2"], dgu, 512, 1408, f"wgrad_gu_{l}", True)
    dx1, gs["ffn_norm_g"] = rms_bwd(dh2, sv["x1"], P["ffn_norm_g"], l, dx2, f"rms_ffn_bwd_{l}")
    dyn = mm_dgrad(dx1, W["w_out"].reshape(D_MODEL, D_MODEL), f"dgrad_out_{l}")
    gb["w_out"] = mm_wgrad(sv["yn"], dx1, 512, 1024, f"wgrad_out_{l}", False).reshape(N_CHIP, D_MODEL // N_CHIP, D_MODEL)
    dyc, dys, dyb, gs["out_norm_g"] = outnorm_bwd(dyn, sv["yc"], sv["ys"], sv["yb"], P["out_norm_g"], l,
                                                  f"outnorm_bwd_{l}")
    ex, done = mid(gb) if mid is not None else (None, None)
    dq, dk, dv, *moved = attn_bwd(sv["qkv"], sv["rt"], dyb, f"attn_bwd_{l}", ex)
    if done is not None:
        done(moved)
    dps, gs["sg_ln_g"], gs["sg_ln_b"], gs["sg_w"], dbt = sg_bwd(
        sv["proj"], dys, P["sg_ln_g"], P["sg_ln_b"], P["sg_wt"], P["sg_wtt"], P["sg_bt"], l, f"sg_bwd_{l}")
    gs["sg_b"] = dbt.T
    dpc, dcw, gs["conv_b"], gs["conv_ln_g"], gs["conv_ln_b"] = conv_bwd(
        sv["proj"], dyc, P["conv_w"], P["conv_b"], P["conv_ln_g"], P["conv_ln_b"], l, f"conv_bwd_{l}")
    gs["conv_w"] = dcw[:CONV_K]
    dproj, dgq, dgk = qkv_bwd(sv["proj"], dq, dk, dv, P["q_norm_g"], P["k_norm_g"], dpc, dps, l, f"qkv_bwd_{l}")
    gs["q_norm_g"] = dgq[0, :HEAD_DIM] + dgq[0, HEAD_DIM:]
    gs["k_norm_g"] = dgk[0, :HEAD_DIM] + dgk[0, HEAD_DIM:]
    dh = mm_dgrad_colblk(dproj, W["w_in"], f"dgrad_in_{l}")
    gb["w_in"] = mm_wgrad(sv["h"], dproj, 1024, IN_W // N_CHIP, f"wgrad_in_{l}", True)
    dx, gs["mix_norm_g"] = rms_bwd(dh, sv["x"], P["mix_norm_g"], l, dx1, f"rms_mix_bwd_{l}")
    return dx, gb, gs


def reduce_start(gl, items, c1):
    n0, l0 = items[0]
    l1 = exchange_sibling_half(gl, f"rs_sibling_{n0}_{l0}")
    return [add_half(g, a, c1, f"rs_add_{n}_{l}", RS_ROWS[n]) for (n, l), g, a in zip(items, gl, l1)]


def reduce_finish(ps, l2, items, kc):
    n0, l0 = items[0]
    fs = [sum_chips(p, a, kc, f"rs_sum_{n}_{l}", RS_ROWS[n]) for (n, l), p, a in zip(items, ps, l2)]
    full = join_sibling_halves(fs, f"rs_join_{n0}_{l0}")
    return [f.reshape(2 * f.shape[1], f.shape[2]) for f in full]


def kernel(x, mix_norm_g, w_in, conv_w, conv_b, conv_ln_g, conv_ln_b, sg_ln_g, sg_ln_b, sg_w, sg_b, q_norm_g, k_norm_g, out_norm_g, w_out, ffn_norm_g, w_gate_up, w_down, loss_target, m_mix_norm_g, m_w_in, m_conv_w, m_conv_b, m_conv_ln_g, m_conv_ln_b, m_sg_ln_g, m_sg_ln_b, m_sg_w, m_sg_b, m_q_norm_g, m_k_norm_g, m_out_norm_g, m_w_out, m_ffn_norm_g, m_w_gate_up, m_w_down, v_mix_norm_g, v_w_in, v_conv_w, v_conv_b, v_conv_ln_g, v_conv_ln_b, v_sg_ln_g, v_sg_ln_b, v_sg_w, v_sg_b, v_q_norm_g, v_k_norm_g, v_out_norm_g, v_w_out, v_ffn_norm_g, v_w_gate_up, v_w_down):
    w = dict(mix_norm_g=mix_norm_g, w_in=w_in, conv_w=conv_w, conv_b=conv_b, conv_ln_g=conv_ln_g, conv_ln_b=conv_ln_b,
             sg_ln_g=sg_ln_g, sg_ln_b=sg_ln_b, sg_w=sg_w, sg_b=sg_b, q_norm_g=q_norm_g, k_norm_g=k_norm_g,
             out_norm_g=out_norm_g, w_out=w_out, ffn_norm_g=ffn_norm_g, w_gate_up=w_gate_up, w_down=w_down)
    m = dict(mix_norm_g=m_mix_norm_g, w_in=m_w_in, conv_w=m_conv_w, conv_b=m_conv_b, conv_ln_g=m_conv_ln_g,
             conv_ln_b=m_conv_ln_b, sg_ln_g=m_sg_ln_g, sg_ln_b=m_sg_ln_b, sg_w=m_sg_w, sg_b=m_sg_b,
             q_norm_g=m_q_norm_g, k_norm_g=m_k_norm_g, out_norm_g=m_out_norm_g, w_out=m_w_out,
             ffn_norm_g=m_ffn_norm_g, w_gate_up=m_w_gate_up, w_down=m_w_down)
    v = dict(mix_norm_g=v_mix_norm_g, w_in=v_w_in, conv_w=v_conv_w, conv_b=v_conv_b, conv_ln_g=v_conv_ln_g,
             conv_ln_b=v_conv_ln_b, sg_ln_g=v_sg_ln_g, sg_ln_b=v_sg_ln_b, sg_w=v_sg_w, sg_b=v_sg_b,
             q_norm_g=v_q_norm_g, k_norm_g=v_k_norm_g, out_norm_g=v_out_norm_g, w_out=v_w_out,
             ffn_norm_g=v_ffn_norm_g, w_gate_up=v_w_gate_up, w_down=v_w_down)
    L = DEPTH
    xi, yi, ci = lax.axis_index("x"), lax.axis_index("y"), lax.axis_index("c")
    kme = 2 * xi + yi
    c1 = ci.astype(jnp.int32).reshape(1)
    k1 = kme.astype(jnp.int32).reshape(1)
    kc = jnp.stack([kme, ci]).astype(jnp.int32)

    def gather_parts(items):
        return [place_slot(w[n], l, k1, N_CHIP, BF16, f"ag_own_{n}_{l}", AG_ROWS[n]) for n, l in items]

    cw_sh = jnp.pad(conv_w.reshape(L * CONV_K, CONV_W // N_CHIP), ((0, 128 - L * CONV_K), (0, 0)))
    cw_buf = place_slot(cw_sh[None], 0, k1, N_CHIP, F32, "ag_own_conv", 128)
    *bufs, cw_buf = run_exchange(gather_exchange(gather_parts([("w_in", 0)]) + [cw_buf]), "ag_chips_first")
    *bufs, cw_buf = gather_forward(bufs + [cw_buf], "ag_sibling_first")
    W = [{} for _ in range(L)]
    W[0]["w_in"] = bufs[0]
    cw_all = cw_buf[:, :L * CONV_K].reshape(N_CHIP, L, CONV_K, CONV_W // N_CHIP)
    cw_full = jnp.transpose(cw_all, (1, 2, 0, 3)).reshape(L, CONV_K, CONV_W)

    tril = jnp.tril(jnp.ones((CHUNK, CHUNK), bool))
    sg_wt = jnp.where(tril, sg_w, 0.0)
    P = {
        "mix_norm_g": mix_norm_g.reshape(L, 1, D_MODEL), "ffn_norm_g": ffn_norm_g.reshape(L, 1, D_MODEL),
        "out_norm_g": out_norm_g.reshape(L, 1, D_MODEL),
        "conv_w": jnp.pad(cw_full, ((0, 0), (0, 1), (0, 0))), "conv_b": conv_b.reshape(L, 1, CONV_W),
        "conv_ln_g": conv_ln_g.reshape(L, 1, CONV_W), "conv_ln_b": conv_ln_b.reshape(L, 1, CONV_W),
        "sg_ln_g": sg_ln_g.reshape(L, 1, SG_W), "sg_ln_b": sg_ln_b.reshape(L, 1, SG_W),
        "sg_wt": sg_wt.astype(MXU_DT), "sg_wtt": jnp.swapaxes(sg_wt, 2, 3).astype(MXU_DT),
        "sg_bt": jnp.swapaxes(sg_b, 1, 2),
        "q_norm_g": jnp.tile(q_norm_g, (1, 2)).reshape(L, 1, 128), "k_norm_g": jnp.tile(k_norm_g, (1, 2)).reshape(L, 1, 128),
    }

    h = x[0]
    saved = []
    for l in range(L):
        items = [("w_out", l), ("w_gate_up", l), ("w_down", l)] + ([("w_in", l + 1)] if l + 1 < L else [])

        def after(moved, l=l, items=items):
            for (n, ll), a in zip(items, gather_forward(moved, f"ag_sibling_{l}")):
                W[ll][n] = a
            return W[l]

        h, sv, _ = layer_forward(h, l, P, W[l], gather_exchange(gather_parts(items)), after)
        saved.append(sv)
    dy, loss_part = loss_head(h, loss_target[0], "loss_head")
    loss = lax.psum(loss_part[0, 0], ("x", "y", "c"))

    outs = {n: None for n in BIG}
    small_grads = [None] * L

    def finish(items, ps, l2):
        for (n, lyr), g in zip(items, reduce_finish(ps, l2, items, kc)):
            outs[n] = adamw_layer(w[n], m[n], v[n], g, outs[n], lyr, f"adamw_{n}_{lyr}", ADAM_ROWS[n])

    g_in = None
    for l in reversed(range(L)):
        def mid(gb, l=l, g_in=g_in):
            items = [("w_down", l), ("w_gate_up", l), ("w_out", l)] + ([("w_in", l + 1)] if g_in is not None else [])
            ps = reduce_start([gb[n] for n, _ in items[:3]] + ([g_in] if g_in is not None else []), items, c1)
            return scatter_exchange(ps), lambda moved: finish(items, ps, moved)

        dy, gb, small_grads[l] = layer_backward(dy, l, P, W[l], saved[l], mid)
        g_in = gb["w_in"]
    ps = reduce_start([g_in], [("w_in", 0)], c1)
    finish([("w_in", 0)], ps, run_exchange(scatter_exchange(ps), "rs_chips_last"))

    shapes = [(L,) + small_grads[0][n].shape for n in SMALL]
    packed = _pack([jnp.stack([small_grads[l][n] for l in range(L)]) for n in SMALL])
    gsum = dict(zip(SMALL, _unpack(all_reduce_small(packed, "ar_small"), shapes)))
    gsum["conv_w"] = lax.dynamic_slice_in_dim(gsum["conv_w"], kme * (CONV_W // N_CHIP), CONV_W // N_CHIP, axis=2)
    gsum = {n: gsum[n].reshape(w[n].shape) for n in SMALL}
    lshapes = [w[n].shape for n in SMALL]
    d_p, m_p, v_p = adamw_flat(_pack([w[n] for n in SMALL]), _pack([m[n] for n in SMALL]),
                               _pack([v[n] for n in SMALL]), _pack([gsum[n] for n in SMALL]), "adamw_small")
    d_s = dict(zip(SMALL, _unpack(d_p, lshapes)))
    m_s = dict(zip(SMALL, _unpack(m_p, lshapes)))
    v_s = dict(zip(SMALL, _unpack(v_p, lshapes)))

    grads = {n: (outs[n][0] if n in BIG else gsum[n]) for n in WEIGHTS}
    delta = {n: (outs[n][1] if n in BIG else d_s[n]) for n in WEIGHTS}
    new_m = {n: (outs[n][2] if n in BIG else m_s[n]) for n in WEIGHTS}
    new_v = {n: (outs[n][3] if n in BIG else v_s[n]) for n in WEIGHTS}
    return (loss, dy[None], *[grads[n] for n in WEIGHTS], *[delta[n] for n in WEIGHTS],
            *[new_m[n] for n in WEIGHTS], *[new_v[n] for n in WEIGHTS])
```

```python
import functools

import jax
import jax.numpy as jnp
from jax import lax
from jax.experimental import pallas as pl
from jax.experimental.pallas import tpu as pltpu

F32 = jnp.float32
BF16 = jnp.bfloat16
MXU_DT = jnp.bfloat16
GRAD_WIRE_DT = jnp.bfloat16

D_MODEL = 1024
DEPTH = 4
HEAD_DIM = 64
CONV_W = 256
SG_W = 256
SB_W = 512
CONV_K = 31
CHUNK = 128
OFF_SG = 2 * CONV_W
OFF_SB = OFF_SG + 2 * SG_W
IN_W = OFF_SB + 3 * SB_W
FFN = 2816
N_CHIP = 4
RMS_EPS = 1e-6
LN_EPS = 1e-5
ADAM_LR = 0.001
ADAM_B1 = 0.9
ADAM_B2 = 0.999
ADAM_EPS = 1e-08
ADAM_WD = 0.01
ADAM_STEP = 10
BC1 = 1.0 - ADAM_B1 ** ADAM_STEP
BC2 = 1.0 - ADAM_B2 ** ADAM_STEP
HALO = 32
MESH = pl.DeviceIdType.MESH
ANY = pl.BlockSpec(memory_space=pl.ANY)
VMEM_LIMIT = 56 * 1024 * 1024


def _cp(**kw):
    return pltpu.CompilerParams(vmem_limit_bytes=VMEM_LIMIT, **kw)


def _dot(a, b, dims):
    dn = {"nn": (((1,), (0,)), ((), ())), "nt": (((1,), (1,)), ((), ())), "tn": (((0,), (0,)), ((), ()))}[dims]
    return lax.dot_general(a.astype(MXU_DT), b.astype(MXU_DT), dn, preferred_element_type=F32)


def _cumdot(x, m):
    hi = x.astype(MXU_DT)
    lo = (x - hi.astype(F32)).astype(MXU_DT)
    dn = (((1,), (0,)), ((), ()))
    return (lax.dot_general(hi, m, dn, preferred_element_type=F32)
            + lax.dot_general(lo, m, dn, preferred_element_type=F32))


def _iota(shape, axis):
    return lax.broadcasted_iota(jnp.int32, shape, axis)


def _rms(x, g):
    return x * lax.rsqrt(jnp.mean(x * x, axis=-1, keepdims=True) + RMS_EPS) * g


def _ln(x, g, b):
    mu = jnp.mean(x, axis=-1, keepdims=True)
    xc = x - mu
    var = jnp.mean(xc * xc, axis=-1, keepdims=True)
    return xc * lax.rsqrt(var + LN_EPS) * g + b


def _glu(val, gate):
    return val * jax.nn.sigmoid(gate)


def _ln_silu(c, g, b):
    return jax.nn.silu(_ln(c, g, b))


_ERF_ALPHA = (-2.72614225801306e-10, 2.77068142495902e-08, -2.10102402082508e-06, -5.69250639462346e-05,
              -7.34990630326855e-04, -2.95459980854025e-03, -1.60960333262415e-02)
_ERF_BETA = (-1.45660718464996e-05, -2.13374055278905e-04, -1.68282697438203e-03, -7.37332916720468e-03,
             -1.42647390514189e-02)


def _erf(x):
    x = jnp.clip(x, -4.0, 4.0)
    x2 = x * x
    p = jnp.full_like(x, _ERF_ALPHA[0])
    for a in _ERF_ALPHA[1:]:
        p = p * x2 + a
    q = jnp.full_like(x, _ERF_BETA[0])
    for b in _ERF_BETA[1:]:
        q = q * x2 + b
    return x * p / q


@jax.custom_jvp
def _gelu(x):
    return 0.5 * x * (1.0 + _erf(x * (2.0 ** -0.5)))


@_gelu.defjvp
def _gelu_jvp(primals, tangents):
    (x,), (t,) = primals, tangents
    cdf = 0.5 * (1.0 + _erf(x * (2.0 ** -0.5)))
    pdf = jnp.exp(-0.5 * x * x) * ((2.0 * jnp.pi) ** -0.5)
    return x * cdf, t * (cdf + x * pdf)


def _sg_pre(uvp, g, b):
    uv = _gelu(uvp)
    return uv[:, :SG_W], _ln(uv[:, SG_W:], g, b)


def _outnorm(yc, ys, yb, g):
    return jnp.concatenate([_rms(yc, g[:, :CONV_W]), _rms(ys, g[:, CONV_W:CONV_W + SG_W]),
                            _rms(yb, g[:, CONV_W + SG_W:])], axis=-1)


def _qkv_fn(pq, pk, pv, gq, gk):
    outs = []
    for p, g, sc in ((pq, gq, HEAD_DIM ** -0.5), (pk, gk, 1.0)):
        for s in range(SB_W // 128):
            x = p[:, 128 * s:128 * (s + 1)]
            seg = _iota(x.shape, 1) < HEAD_DIM
            x2 = x * x
            s0 = jnp.sum(jnp.where(seg, x2, 0.0), axis=-1, keepdims=True)
            s1 = jnp.sum(jnp.where(seg, 0.0, x2), axis=-1, keepdims=True)
            ms = jnp.where(seg, s0, s1) * (1.0 / HEAD_DIM)
            outs.append(x * lax.rsqrt(ms + RMS_EPS) * (g * sc))
    outs.append(pv)
    return jnp.concatenate(outs, axis=-1)


def _swiglu(g, u):
    return jax.nn.silu(g) * u


def _softplus(z):
    return jnp.maximum(z, 0.0) + jnp.log(1.0 + jnp.exp(-jnp.abs(z)))


def mm_colblk(a, wb, out_dtype, name, tm=1024):
    S, K = a.shape
    nb, _, C = wb.shape

    def body(a_ref, w_ref, o_ref):
        o_ref[...] = _dot(a_ref[...], w_ref[0], "nn").astype(o_ref.dtype)

    return pl.pallas_call(
        body, grid=(nb, S // tm),
        in_specs=[pl.BlockSpec((tm, K), lambda k, i: (i, 0)), pl.BlockSpec((1, K, C), lambda k, i: (k, 0, 0))],
        out_specs=pl.BlockSpec((tm, C), lambda k, i: (i, k)),
        out_shape=jax.ShapeDtypeStruct((S, nb * C), out_dtype), name=name, compiler_params=_cp())(a, wb)


def mm_res(a, w, res, name, tm=512):
    S, K = a.shape
    N = w.shape[1]

    def body(a_ref, w_ref, r_ref, o_ref):
        o_ref[...] = r_ref[...] + _dot(a_ref[...], w_ref[...], "nn")

    return pl.pallas_call(
        body, grid=(S // tm,),
        in_specs=[pl.BlockSpec((tm, K), lambda i: (i, 0)), pl.BlockSpec((K, N), lambda i: (0, 0)),
                  pl.BlockSpec((tm, N), lambda i: (i, 0))],
        out_specs=pl.BlockSpec((tm, N), lambda i: (i, 0)),
        out_shape=jax.ShapeDtypeStruct((S, N), F32), name=name, compiler_params=_cp())(a, w, res)


def ffn_up(h2, wgu, name, tm=512, ex=None):
    S, K = h2.shape
    C = wgu.shape[2]
    nm = S // tm

    def body(*refs):
        if ex is None:
            (h_ref, wg_ref, wu_ref), (g_ref, u_ref, a_ref), copies = refs[:3], refs[3:6], None
        else:
            (h_ref, wg_ref, wu_ref), (g_ref, u_ref, a_ref), copies = ex.split(refs, 3, 3)

            @pl.when(jnp.logical_and(pl.program_id(0) == 0, pl.program_id(1) == 0))
            def _():
                for cp in copies():
                    cp.start()

        h = h_ref[...]
        g = _dot(h, wg_ref[0], "nn")
        u = _dot(h, wu_ref[0], "nn")
        g_ref[...] = g.astype(BF16)
        u_ref[...] = u.astype(BF16)
        a_ref[...] = _swiglu(g, u).astype(BF16)

        if ex is not None:
            @pl.when(jnp.logical_and(pl.program_id(0) == 1, pl.program_id(1) == nm - 1))
            def _():
                for cp in copies():
                    cp.wait()

    o = pl.BlockSpec((tm, C), lambda j, i: (i, j))
    sh = jax.ShapeDtypeStruct((S, 2 * C), BF16)
    in_specs, out_specs, out_shape, aliases, scratch, extra = _with_exchange(
        ex, 3, 3,
        [pl.BlockSpec((tm, K), lambda j, i: (i, 0)), pl.BlockSpec((1, K, C), lambda j, i: (j, 0, 0)),
         pl.BlockSpec((1, K, C), lambda j, i: (2 + j, 0, 0))],
        [o, o, o], [sh, sh, sh])
    return pl.pallas_call(
        body, grid=(2, nm), in_specs=in_specs, out_specs=out_specs, out_shape=out_shape,
        input_output_aliases=aliases, scratch_shapes=scratch, name=name, compiler_params=_cp())(h2, wgu, wgu, *extra)


def ffn_down_bwd(dx2, wdown, g, u, name, tm=512):
    S, N = dx2.shape
    C = FFN // 2

    def body(d_ref, w_ref, g_ref, u_ref, o_ref):
        d = d_ref[...]
        for j in range(2):
            cols = slice(j * C, (j + 1) * C)
            dact = _dot(d, w_ref[cols, :], "nt")
            _, vjp = jax.vjp(_swiglu, g_ref[:, cols].astype(F32), u_ref[:, cols].astype(F32))
            dg, du = vjp(dact)
            o_ref[:, cols] = dg.astype(BF16)
            o_ref[:, FFN + j * C:FFN + (j + 1) * C] = du.astype(BF16)

    row = pl.BlockSpec((tm, FFN), lambda i: (i, 0))
    return pl.pallas_call(
        body, grid=(S // tm,),
        in_specs=[pl.BlockSpec((tm, N), lambda i: (i, 0)), pl.BlockSpec((FFN, N), lambda i: (0, 0)), row, row],
        out_specs=pl.BlockSpec((tm, 2 * FFN), lambda i: (i, 0)),
        out_shape=jax.ShapeDtypeStruct((S, 2 * FFN), BF16), name=name, compiler_params=_cp())(dx2, wdown, g, u)


def mm_dgrad_colblk(do, wb, name, tm=1024):
    S = do.shape[0]
    nb, K, C = wb.shape

    def body(d_ref, w_ref, o_ref):
        k = pl.program_id(1)
        r = _dot(d_ref[...], w_ref[0], "nt")

        @pl.when(k == 0)
        def _():
            o_ref[...] = r

        @pl.when(k != 0)
        def _():
            o_ref[...] += r

    return pl.pallas_call(
        body, grid=(S // tm, nb),
        in_specs=[pl.BlockSpec((tm, C), lambda i, k: (i, k)), pl.BlockSpec((1, K, C), lambda i, k: (k, 0, 0))],
        out_specs=pl.BlockSpec((tm, K), lambda i, k: (i, 0)),
        out_shape=jax.ShapeDtypeStruct((S, K), F32), name=name, compiler_params=_cp())(do, wb)


def mm_dgrad(do, w, name, tm=1024):
    S, N = do.shape
    K = w.shape[0]

    def body(d_ref, w_ref, o_ref):
        o_ref[...] = _dot(d_ref[...], w_ref[...], "nt")

    return pl.pallas_call(
        body, grid=(S // tm,),
        in_specs=[pl.BlockSpec((tm, N), lambda i: (i, 0)), pl.BlockSpec((K, N), lambda i: (0, 0))],
        out_specs=pl.BlockSpec((tm, K), lambda i: (i, 0)),
        out_shape=jax.ShapeDtypeStruct((S, K), F32), name=name, compiler_params=_cp())(do, w)


def mm_wgrad(a, do, tk, tn, name, blocked):
    S, K = a.shape
    N = do.shape[1]

    def body(a_ref, d_ref, o_ref):
        r = _dot(a_ref[...], d_ref[...], "tn").astype(GRAD_WIRE_DT)
        if blocked:
            o_ref[0] = r
        else:
            o_ref[...] = r

    if blocked:
        out_spec = pl.BlockSpec((1, tk, tn), lambda n, j: (n, j, 0))
        out_shape = jax.ShapeDtypeStruct((N // tn, K, tn), GRAD_WIRE_DT)
    else:
        out_spec = pl.BlockSpec((tk, tn), lambda n, j: (j, n))
        out_shape = jax.ShapeDtypeStruct((K, N), GRAD_WIRE_DT)
    return pl.pallas_call(
        body, grid=(N // tn, K // tk),
        in_specs=[pl.BlockSpec((S, tk), lambda n, j: (0, j)), pl.BlockSpec((S, tn), lambda n, j: (0, n))],
        out_specs=out_spec, out_shape=out_shape, name=name, compiler_params=_cp())(a, do)


def rms_fwd(x, g3, l, name, tm=512):
    S, N = x.shape

    def body(x_ref, g_ref, o_ref):
        o_ref[...] = _rms(x_ref[...], g_ref[0]).astype(BF16)

    return pl.pallas_call(
        body, grid=(S // tm,),
        in_specs=[pl.BlockSpec((tm, N), lambda i: (i, 0)), pl.BlockSpec((1, 1, N), lambda i: (l, 0, 0))],
        out_specs=pl.BlockSpec((tm, N), lambda i: (i, 0)),
        out_shape=jax.ShapeDtypeStruct((S, N), BF16), name=name, compiler_params=_cp())(x, g3)


def rms_bwd(dh, x, g3, l, dres, name, tm=512):
    S, N = x.shape

    def body(dh_ref, x_ref, g_ref, r_ref, dx_ref, dg_ref):
        _, vjp = jax.vjp(_rms, x_ref[...], g_ref[0])
        dx, dg = vjp(dh_ref[...])
        dx_ref[...] = r_ref[...] + dx

        @pl.when(pl.program_id(0) == 0)
        def _():
            dg_ref[...] = jnp.zeros_like(dg_ref)

        dg_ref[...] += dg

    row = pl.BlockSpec((tm, N), lambda i: (i, 0))
    return pl.pallas_call(
        body, grid=(S // tm,),
        in_specs=[row, row, pl.BlockSpec((1, 1, N), lambda i: (l, 0, 0)), row],
        out_specs=[row, pl.BlockSpec((1, N), lambda i: (0, 0))],
        out_shape=[jax.ShapeDtypeStruct((S, N), F32), jax.ShapeDtypeStruct((1, N), F32)],
        name=name, compiler_params=_cp())(dh, x, g3, dres)


def outnorm_fwd(yc, ys, yb, g3, l, name, tm=512):
    S = yc.shape[0]

    def body(c_ref, s_ref, b_ref, g_ref, o_ref):
        o_ref[...] = _outnorm(c_ref[...], s_ref[...], b_ref[...], g_ref[0]).astype(BF16)

    return pl.pallas_call(
        body, grid=(S // tm,),
        in_specs=[pl.BlockSpec((tm, CONV_W), lambda i: (i, 0)), pl.BlockSpec((tm, SG_W), lambda i: (i, 0)),
                  pl.BlockSpec((tm, SB_W), lambda i: (i, 0)), pl.BlockSpec((1, 1, D_MODEL), lambda i: (l, 0, 0))],
        out_specs=pl.BlockSpec((tm, D_MODEL), lambda i: (i, 0)),
        out_shape=jax.ShapeDtypeStruct((S, D_MODEL), BF16), name=name, compiler_params=_cp())(yc, ys, yb, g3)


def outnorm_bwd(dyn, yc, ys, yb, g3, l, name, tm=512):
    S = yc.shape[0]

    def body(d_ref, c_ref, s_ref, b_ref, g_ref, dc_ref, ds_ref, db_ref, dg_ref):
        _, vjp = jax.vjp(_outnorm, c_ref[...], s_ref[...], b_ref[...], g_ref[0])
        dc, ds, db, dg = vjp(d_ref[...])
        dc_ref[...] = dc
        ds_ref[...] = ds
        db_ref[...] = db

        @pl.when(pl.program_id(0) == 0)
        def _():
            dg_ref[...] = jnp.zeros_like(dg_ref)

        dg_ref[...] += dg

    sc = pl.BlockSpec((tm, CONV_W), lambda i: (i, 0))
    ss = pl.BlockSpec((tm, SG_W), lambda i: (i, 0))
    sb = pl.BlockSpec((tm, SB_W), lambda i: (i, 0))
    return pl.pallas_call(
        body, grid=(S // tm,),
        in_specs=[pl.BlockSpec((tm, D_MODEL), lambda i: (i, 0)), sc, ss, sb,
                  pl.BlockSpec((1, 1, D_MODEL), lambda i: (l, 0, 0))],
        out_specs=[sc, ss, sb, pl.BlockSpec((1, D_MODEL), lambda i: (0, 0))],
        out_shape=[jax.ShapeDtypeStruct((S, CONV_W), F32), jax.ShapeDtypeStruct((S, SG_W), F32),
                   jax.ShapeDtypeStruct((S, SB_W), F32), jax.ShapeDtypeStruct((1, D_MODEL), F32)],
        name=name, compiler_params=_cp())(dyn, yc, ys, yb, g3)


def loss_head(y, t, name, tm=512):
    S, N = y.shape

    def body(y_ref, t_ref, dy_ref, l_ref):
        e = y_ref[...] - t_ref[...]
        dy_ref[...] = e * (1.0 / N)

        @pl.when(pl.program_id(0) == 0)
        def _():
            l_ref[...] = jnp.zeros_like(l_ref)

        l_ref[...] += (0.5 / N) * jnp.sum(jnp.sum(e * e, axis=-1, keepdims=True), axis=0, keepdims=True)

    row = pl.BlockSpec((tm, N), lambda i: (i, 0))
    return pl.pallas_call(
        body, grid=(S // tm,), in_specs=[row, row],
        out_specs=[row, pl.BlockSpec((1, 1), lambda i: (0, 0))],
        out_shape=[jax.ShapeDtypeStruct((S, N), F32), jax.ShapeDtypeStruct((1, 1), F32)],
        name=name, compiler_params=_cp())(y, t)


def _conv_taps(a, w_ref, T):
    acc = jnp.zeros((T, CONV_W), F32)
    for j in range(CONV_K):
        sh = CONV_K - 1 - j
        r = a if sh == 0 else pltpu.roll(a, sh, 0)
        acc = acc + r[HALO:HALO + T] * w_ref[pl.ds(j, 1), :]
    return acc


def conv_fwd(proj, cw, cb, lg, lb, l, name, T=256):
    S = proj.shape[0]
    nt = S // T

    def body(p_ref, w_ref, cb_ref, lg_ref, lb_ref, y_ref, hc_s):
        hc_s[0:HALO, :] = jnp.zeros((HALO, CONV_W), F32)

        def fill(i, _):
            r0 = pl.multiple_of(i * T, T)
            pc = p_ref[pl.ds(r0, T), :]
            hc_s[pl.ds(r0 + HALO, T), :] = _glu(pc[:, :CONV_W], pc[:, CONV_W:])
            return 0

        lax.fori_loop(0, nt, fill, 0)

        def tile(i, _):
            r0 = pl.multiple_of(i * T, T)
            c = _conv_taps(hc_s[pl.ds(r0, T + HALO), :], w_ref.at[0], T) + cb_ref[0]
            y_ref[pl.ds(r0, T), :] = _ln_silu(c, lg_ref[0], lb_ref[0])
            return 0

        lax.fori_loop(0, nt, tile, 0)

    vec = pl.BlockSpec((1, 1, CONV_W), lambda i: (l, 0, 0))
    return pl.pallas_call(
        body, grid=(1,),
        in_specs=[pl.BlockSpec((S, 2 * CONV_W), lambda i: (0, 0)), pl.BlockSpec((1, 32, CONV_W), lambda i: (l, 0, 0)),
                  vec, vec, vec],
        out_specs=pl.BlockSpec((S, CONV_W), lambda i: (0, 0)),
        out_shape=jax.ShapeDtypeStruct((S, CONV_W), F32),
        scratch_shapes=[pltpu.VMEM((S + HALO, CONV_W), F32)], name=name, compiler_params=_cp())(proj, cw, cb, lg, lb)


def conv_bwd(proj, dy, cw, cb, lg, lb, l, name, T=256):
    S = proj.shape[0]
    nt = S // T

    def body(p_ref, dy_ref, w_ref, cb_ref, lg_ref, lb_ref, dp_ref, dw_ref, dcb_ref, dlg_ref, dlb_ref, hc_s, dc_s):
        hc_s[0:HALO, :] = jnp.zeros((HALO, CONV_W), F32)
        dc_s[S:S + HALO, :] = jnp.zeros((HALO, CONV_W), F32)
        dw_ref[...] = jnp.zeros_like(dw_ref)
        dcb_ref[...] = jnp.zeros_like(dcb_ref)
        dlg_ref[...] = jnp.zeros_like(dlg_ref)
        dlb_ref[...] = jnp.zeros_like(dlb_ref)

        def fill(i, _):
            r0 = pl.multiple_of(i * T, T)
            pc = p_ref[pl.ds(r0, T), :]
            hc_s[pl.ds(r0 + HALO, T), :] = _glu(pc[:, :CONV_W], pc[:, CONV_W:])
            return 0

        lax.fori_loop(0, nt, fill, 0)

        def tile(i, _):
            r0 = pl.multiple_of(i * T, T)
            a = hc_s[pl.ds(r0, T + HALO), :]
            c = _conv_taps(a, w_ref.at[0], T) + cb_ref[0]
            _, vjp = jax.vjp(_ln_silu, c, lg_ref[0], lb_ref[0])
            dc, dlg, dlb = vjp(dy_ref[pl.ds(r0, T), :])
            dc_s[pl.ds(r0, T), :] = dc
            dcb_ref[...] += jnp.sum(dc, axis=0, keepdims=True)
            dlg_ref[...] += dlg
            dlb_ref[...] += dlb
            for j in range(CONV_K):
                sh = CONV_K - 1 - j
                r = a if sh == 0 else pltpu.roll(a, sh, 0)
                dw_ref[pl.ds(j, 1), :] += jnp.sum(dc * r[HALO:HALO + T], axis=0, keepdims=True)
            return 0

        lax.fori_loop(0, nt, tile, 0)

        def back(i, _):
            r0 = pl.multiple_of(i * T, T)
            de = dc_s[pl.ds(r0, T + HALO), :]
            n = T + HALO
            dh = jnp.zeros((T, CONV_W), F32)
            for j in range(CONV_K):
                sh = CONV_K - 1 - j
                r = de if sh == 0 else pltpu.roll(de, n - sh, 0)
                dh = dh + r[0:T] * w_ref[0, pl.ds(j, 1), :]
            pc = p_ref[pl.ds(r0, T), :]
            _, vjp = jax.vjp(_glu, pc[:, :CONV_W], pc[:, CONV_W:])
            dval, dgate = vjp(dh)
            dp_ref[pl.ds(r0, T), :] = jnp.concatenate([dval, dgate], axis=-1).astype(BF16)
            return 0

        lax.fori_loop(0, nt, back, 0)

    vec = pl.BlockSpec((1, 1, CONV_W), lambda i: (l, 0, 0))
    ovec = pl.BlockSpec((1, CONV_W), lambda i: (0, 0))
    vsh = jax.ShapeDtypeStruct((1, CONV_W), F32)
    return pl.pallas_call(
        body, grid=(1,),
        in_specs=[pl.BlockSpec((S, 2 * CONV_W), lambda i: (0, 0)), pl.BlockSpec((S, CONV_W), lambda i: (0, 0)),
                  pl.BlockSpec((1, 32, CONV_W), lambda i: (l, 0, 0)), vec, vec, vec],
        out_specs=[pl.BlockSpec((S, 2 * CONV_W), lambda i: (0, 0)), pl.BlockSpec((32, CONV_W), lambda i: (0, 0)),
                   ovec, ovec, ovec],
        out_shape=[jax.ShapeDtypeStruct((S, 2 * CONV_W), BF16), jax.ShapeDtypeStruct((32, CONV_W), F32), vsh, vsh, vsh],
        scratch_shapes=[pltpu.VMEM((S + HALO, CONV_W), F32), pltpu.VMEM((S + HALO, CONV_W), F32)],
        name=name, compiler_params=_cp())(proj, dy, cw, cb, lg, lb)


def _sg_mix(wt_ref, v, bt):
    slabs = []
    for s in range(2):
        vs = v[:, 128 * s:128 * (s + 1)]
        seg = _iota(vs.shape, 1) < HEAD_DIM
        p0 = _dot(wt_ref[2 * s], vs, "nn") + bt[:, 2 * s:2 * s + 1]
        p1 = _dot(wt_ref[2 * s + 1], vs, "nn") + bt[:, 2 * s + 1:2 * s + 2]
        slabs.append(jnp.where(seg, p0, p1))
    return jnp.concatenate(slabs, axis=-1)


def sg_fwd(proj, lg, lb, wt, bt, l, name):
    S = proj.shape[0]

    def body(p_ref, lg_ref, lb_ref, w_ref, b_ref, y_ref):
        u, v = _sg_pre(p_ref[...], lg_ref[0], lb_ref[0])
        y_ref[...] = u * _sg_mix(w_ref.at[0], v, b_ref[0])

    vec = pl.BlockSpec((1, 1, SG_W), lambda i: (l, 0, 0))
    return pl.pallas_call(
        body, grid=(S // CHUNK,),
        in_specs=[pl.BlockSpec((CHUNK, 2 * SG_W), lambda i: (i, 1)), vec, vec,
                  pl.BlockSpec((1, 4, CHUNK, CHUNK), lambda i: (l, 0, 0, 0)),
                  pl.BlockSpec((1, CHUNK, 4), lambda i: (l, 0, 0))],
        out_specs=pl.BlockSpec((CHUNK, SG_W), lambda i: (i, 0)),
        out_shape=jax.ShapeDtypeStruct((S, SG_W), F32), name=name, compiler_params=_cp())(proj, lg, lb, wt, bt)


def sg_bwd(proj, dy, lg, lb, wt, wtt, bt, l, name):
    S = proj.shape[0]

    def body(p_ref, dy_ref, lg_ref, lb_ref, w_ref, wt_ref, b_ref, dp_ref, dlg_ref, dlb_ref, dw_ref, db_ref):
        @pl.when(pl.program_id(0) == 0)
        def _():
            dlg_ref[...] = jnp.zeros_like(dlg_ref)
            dlb_ref[...] = jnp.zeros_like(dlb_ref)
            dw_ref[...] = jnp.zeros_like(dw_ref)
            db_ref[...] = jnp.zeros_like(db_ref)

        (u, v), vjp = jax.vjp(_sg_pre, p_ref[...], lg_ref[0], lb_ref[0])
        dy = dy_ref[...]
        mixed = _sg_mix(w_ref.at[0], v, b_ref[0])
        du = dy * mixed
        dm = dy * u
        tril = _iota((CHUNK, CHUNK), 1) <= _iota((CHUNK, CHUNK), 0)
        dvs = []
        for s in range(2):
            dms = dm[:, 128 * s:128 * (s + 1)]
            vs = v[:, 128 * s:128 * (s + 1)]
            seg = _iota(dms.shape, 1) < HEAD_DIM
            halves = (jnp.where(seg, dms, 0.0), jnp.where(seg, 0.0, dms))
            dv_h = []
            for e in range(2):
                h = 2 * s + e
                db_ref[:, h:h + 1] += jnp.sum(halves[e], axis=-1, keepdims=True)
                dw_ref[h] += jnp.where(tril, _dot(halves[e], vs, "nt"), 0.0)
                dv_h.append(_dot(wt_ref[0, h], halves[e], "nn"))
            dvs.append(dv_h[0] + dv_h[1])
        dp, dlg, dlb = vjp((du, jnp.concatenate(dvs, axis=-1)))
        dp_ref[...] = dp.astype(BF16)
        dlg_ref[...] += dlg
        dlb_ref[...] += dlb

    vec = pl.BlockSpec((1, 1, SG_W), lambda i: (l, 0, 0))
    wsp = pl.BlockSpec((1, 4, CHUNK, CHUNK), lambda i: (l, 0, 0, 0))
    ovec = pl.BlockSpec((1, SG_W), lambda i: (0, 0))
    return pl.pallas_call(
        body, grid=(S // CHUNK,),
        in_specs=[pl.BlockSpec((CHUNK, 2 * SG_W), lambda i: (i, 1)), pl.BlockSpec((CHUNK, SG_W), lambda i: (i, 0)),
                  vec, vec, wsp, wsp, pl.BlockSpec((1, CHUNK, 4), lambda i: (l, 0, 0))],
        out_specs=[pl.BlockSpec((CHUNK, 2 * SG_W), lambda i: (i, 0)), ovec, ovec,
                   pl.BlockSpec((4, CHUNK, CHUNK), lambda i: (0, 0, 0)), pl.BlockSpec((CHUNK, 4), lambda i: (0, 0))],
        out_shape=[jax.ShapeDtypeStruct((S, 2 * SG_W), BF16), jax.ShapeDtypeStruct((1, SG_W), F32),
                   jax.ShapeDtypeStruct((1, SG_W), F32), jax.ShapeDtypeStruct((4, CHUNK, CHUNK), F32),
                   jax.ShapeDtypeStruct((CHUNK, 4), F32)],
        name=name, compiler_params=_cp())(proj, dy, lg, lb, wt, wtt, bt)


def qkv_fwd(proj, gq, gk, l, name, tm=512):
    S = proj.shape[0]

    def body(pq_ref, pk_ref, pv_ref, gq_ref, gk_ref, o_ref):
        o_ref[...] = _qkv_fn(pq_ref[...], pk_ref[...], pv_ref[...], gq_ref[0], gk_ref[0]).astype(BF16)

    vec = pl.BlockSpec((1, 1, 128), lambda i: (l, 0, 0))
    qb = OFF_SB // SB_W
    return pl.pallas_call(
        body, grid=(S // tm,),
        in_specs=[pl.BlockSpec((tm, SB_W), lambda i: (i, qb)), pl.BlockSpec((tm, SB_W), lambda i: (i, qb + 1)),
                  pl.BlockSpec((tm, SB_W), lambda i: (i, qb + 2)), vec, vec],
        out_specs=pl.BlockSpec((tm, 3 * SB_W), lambda i: (i, 0)),
        out_shape=jax.ShapeDtypeStruct((S, 3 * SB_W), BF16), name=name, compiler_params=_cp())(proj, proj, proj, gq, gk)


def qkv_bwd(proj, dq, dk, dv, gq, gk, dpc, dps, l, name, tm=512):
    S = proj.shape[0]

    def body(pq_ref, pk_ref, pv_ref, dq_ref, dk_ref, dv_ref, gq_ref, gk_ref, dpc_ref, dps_ref, dp_ref, dgq_ref,
             dgk_ref):
        _, vjp = jax.vjp(_qkv_fn, pq_ref[...], pk_ref[...], pv_ref[...], gq_ref[0], gk_ref[0])
        dpq, dpk, dpv, dgq, dgk = vjp(jnp.concatenate([dq_ref[...], dk_ref[...], dv_ref[...]], axis=-1))
        dp_ref[:, :OFF_SG] = dpc_ref[...]
        dp_ref[:, OFF_SG:OFF_SB] = dps_ref[...]
        dp_ref[:, OFF_SB:] = jnp.concatenate([dpq, dpk, dpv], axis=-1).astype(BF16)

        @pl.when(pl.program_id(0) == 0)
        def _():
            dgq_ref[...] = jnp.zeros_like(dgq_ref)
            dgk_ref[...] = jnp.zeros_like(dgk_ref)

        dgq_ref[...] += dgq
        dgk_ref[...] += dgk

    vec = pl.BlockSpec((1, 1, 128), lambda i: (l, 0, 0))
    part = pl.BlockSpec((tm, SB_W), lambda i: (i, 0))
    ovec = pl.BlockSpec((1, 128), lambda i: (0, 0))
    qb = OFF_SB // SB_W
    return pl.pallas_call(
        body, grid=(S // tm,),
        in_specs=[pl.BlockSpec((tm, SB_W), lambda i: (i, qb)), pl.BlockSpec((tm, SB_W), lambda i: (i, qb + 1)),
                  pl.BlockSpec((tm, SB_W), lambda i: (i, qb + 2)), part, part, part, vec, vec, part, part],
        out_specs=[pl.BlockSpec((tm, IN_W), lambda i: (i, 0)), ovec, ovec],
        out_shape=[jax.ShapeDtypeStruct((S, IN_W), BF16), jax.ShapeDtypeStruct((1, 128), F32),
                   jax.ShapeDtypeStruct((1, 128), F32)],
        name=name, compiler_params=_cp())(proj, proj, proj, dq, dk, dv, gq, gk, dpc, dps)


QSUB_FWD = 8
QSUB_BWD = 8


def _sb_consts():
    row = _iota((CHUNK, CHUNK), 0)
    col = _iota((CHUNK, CHUNK), 1)
    ones = jnp.ones((CHUNK, CHUNK), MXU_DT)
    m_gt = jnp.concatenate([(row > col).astype(MXU_DT), ones], axis=1)
    m_lt = jnp.concatenate([(row < col).astype(MXU_DT), ones], axis=1)
    return (col < HEAD_DIM, col - row, jnp.concatenate([m_gt, m_gt], axis=0), jnp.concatenate([m_lt, m_lt], axis=0))


def _split_heads(x, seg):
    z = jnp.zeros_like(x)
    return (jnp.where(seg, x, z), jnp.where(seg, z, x))


def _stack_heads(x, seg):
    return jnp.concatenate(_split_heads(x, seg), axis=0)


def _cumdot2(x, m2):
    hi = x.astype(MXU_DT)
    lo = (x - hi.astype(F32)).astype(MXU_DT)
    dn = (((1,), (0,)), ((), ()))
    return [lax.dot_general(jnp.concatenate([hi[:, CHUNK * h:CHUNK * (h + 1)], lo[:, CHUNK * h:CHUNK * (h + 1)]], axis=1),
                            m2, dn, preferred_element_type=F32) for h in range(2)]


class Exchange:
    def __init__(self, ins, bufs, outs, n, build):
        self.ins, self.bufs, self.outs, self.n, self.build = list(ins), list(bufs), list(outs), n, build

    def split(self, refs, n_in, n_out):
        a, b, o = len(self.ins), len(self.bufs), len(self.outs)
        main_in = refs[:n_in]
        c_in = refs[n_in:n_in + a]
        rest = refs[n_in + a + b:]
        main_out = rest[:n_out]
        c_buf = rest[n_out:n_out + b]
        c_out = rest[n_out + b:n_out + b + o]
        send, recv = rest[n_out + b + o:]
        return main_in, main_out, lambda: self.build(c_in, c_buf, c_out, send, recv)


def _with_exchange(ex, n_in, n_out, in_specs, out_specs, out_shape):
    if ex is None:
        return in_specs, out_specs, out_shape, {}, [], []
    a, b = len(ex.ins), len(ex.bufs)
    in_specs = list(in_specs) + [ANY] * (a + b)
    out_specs = list(out_specs) + [ANY] * (b + len(ex.outs))
    out_shape = list(out_shape) + [jax.ShapeDtypeStruct(x.shape, x.dtype) for x in ex.bufs] + list(ex.outs)
    aliases = {n_in + a + i: n_out + i for i in range(b)}
    scratch = [pltpu.SemaphoreType.DMA((ex.n,)), pltpu.SemaphoreType.DMA((ex.n,))]
    return in_specs, out_specs, out_shape, aliases, scratch, ex.ins + ex.bufs


def attn_fwd(qkv, name, ex=None):
    S = qkv.shape[0]
    npair = SB_W // 128
    QSUB = QSUB_FWD
    QT = QSUB * CHUNK
    nq = S // QT

    def body(*refs):
        if ex is None:
            (q_ref, k_ref, v_ref), (o_ref, rt_ref), copies = refs[:3], refs[3:5], None
        else:
            (q_ref, k_ref, v_ref), (o_ref, rt_ref), copies = ex.split(refs, 3, 2)

            @pl.when(jnp.logical_and(pl.program_id(0) == 0, pl.program_id(1) == 0))
            def _():
                for cp in copies():
                    cp.start()

        qi = pl.program_id(1)
        seg, dcol, m_gt, _ = _sb_consts()
        qs = [q_ref[a * CHUNK:(a + 1) * CHUNK, :] for a in range(QSUB)]
        causal2 = jnp.concatenate([dcol < 0, dcol < 0], axis=1)

        def step(kb, carry, diag):
            off = pl.multiple_of(kb * CHUNK, CHUNK)
            kk = _stack_heads(k_ref[pl.ds(off, CHUNK), :], seg)
            vv = _stack_heads(v_ref[pl.ds(off, CHUNK), :], seg)
            act = [a for a in range(QSUB) if diag is None or a >= diag]
            z = [_dot(qs[a], kk, "nt") for a in act]
            sp = [_softplus(x) for x in z]
            lnb = [jnp.where(causal2, -s, 0.0) if a == diag else -s for a, s in zip(act, sp)]
            cs = [_cumdot2(x, m_gt) for x in lnb]
            att = []
            for n, a in enumerate(act):
                base = z[n] - sp[n]
                e = jnp.exp(base + jnp.concatenate([cs[n][h][:, :CHUNK] + carry[3 * a + 1 + h] for h in range(2)], axis=1))
                att.append(jnp.where(causal2, e, 0.0) if a == diag else e)
            pv = [_dot(x, vv, "nn") for x in att]
            new = list(carry)
            for n, a in enumerate(act):
                new[3 * a] = carry[3 * a] + pv[n]
                for h in range(2):
                    new[3 * a + 1 + h] = carry[3 * a + 1 + h] + cs[n][h][:, CHUNK:]
            return tuple(new)

        z0 = jnp.zeros((CHUNK, CHUNK), F32)
        res = (z0,) * (3 * QSUB)
        for j in reversed(range(QSUB)):
            res = step(QSUB * qi + j, res, j)
        res = lax.fori_loop(0, QSUB * qi, lambda it, c: step(QSUB * qi - 1 - it, c, None), res)
        for a in range(QSUB):
            rows = slice(a * CHUNK, (a + 1) * CHUNK)
            o_ref[rows, :] = res[3 * a]
            rt_ref[rows, :] = jnp.concatenate([res[3 * a + 1], res[3 * a + 2]], axis=1)

        if ex is not None:
            @pl.when(jnp.logical_and(pl.program_id(0) == npair - 1, pl.program_id(1) == nq - 1))
            def _():
                for cp in copies():
                    cp.wait()

    in_specs, out_specs, out_shape, aliases, scratch, extra = _with_exchange(
        ex, 3, 2,
        [pl.BlockSpec((QT, 128), lambda p, i: (i, p)), pl.BlockSpec((S, 128), lambda p, i: (0, npair + p)),
         pl.BlockSpec((S, 128), lambda p, i: (0, 2 * npair + p))],
        [pl.BlockSpec((QT, 128), lambda p, i: (i, p)), pl.BlockSpec((QT, 256), lambda p, i: (i, p))],
        [jax.ShapeDtypeStruct((S, SB_W), F32), jax.ShapeDtypeStruct((S, 2 * SB_W), F32)])
    return pl.pallas_call(
        body, grid=(npair, nq), in_specs=in_specs, out_specs=out_specs, out_shape=out_shape,
        input_output_aliases=aliases, scratch_shapes=scratch, name=name, compiler_params=_cp())(qkv, qkv, qkv, *extra)


def attn_bwd(qkv, rt, do, name, ex=None):
    S = qkv.shape[0]
    npair = SB_W // 128
    QSUB = QSUB_BWD
    QT = QSUB * CHUNK
    nq = S // QT

    def body(*refs):
        if ex is None:
            (q_ref, k_ref, v_ref, rt_ref, do_ref), (dq_ref, dk_ref, dv_ref), copies = refs[:5], refs[5:8], None
        else:
            (q_ref, k_ref, v_ref, rt_ref, do_ref), (dq_ref, dk_ref, dv_ref), copies = ex.split(refs, 5, 3)

            @pl.when(jnp.logical_and(pl.program_id(0) == 0, pl.program_id(1) == 0))
            def _():
                for cp in copies():
                    cp.start()

        qi = pl.program_id(1)

        @pl.when(qi == 0)
        def _():
            dk_ref[...] = jnp.zeros_like(dk_ref)
            dv_ref[...] = jnp.zeros_like(dv_ref)

        seg, dcol, m_gt, m_lt = _sb_consts()
        qs, dos, qq, dd, rtot = [], [], [], [], []
        for a in range(QSUB):
            rows = slice(a * CHUNK, (a + 1) * CHUNK)
            qs.append(q_ref[rows, :])
            dos.append(do_ref[rows, :].astype(MXU_DT))
            qq.append(_stack_heads(qs[a], seg))
            dd.append(_stack_heads(dos[a], seg))
            rtot.append(rt_ref[rows, :])
        causal2 = jnp.concatenate([dcol < 0, dcol < 0], axis=1)

        def step(kb, carry, diag):
            off = pl.multiple_of(kb * CHUNK, CHUNK)
            kk = _stack_heads(k_ref[pl.ds(off, CHUNK), :], seg)
            vv = _stack_heads(v_ref[pl.ds(off, CHUNK), :], seg)
            act = [a for a in range(QSUB) if diag is None or a >= diag]
            z = [_dot(qs[a], kk, "nt") for a in act]
            da = [_dot(dos[a], vv, "nt") for a in act]
            sp = [_softplus(x) for x in z]
            lnb = [jnp.where(causal2, -s, 0.0) if a == diag else -s for a, s in zip(act, sp)]
            cs = [_cumdot2(x, m_gt) for x in lnb]
            lc = [[carry[5 * a + 1 + h] + cs[n][h][:, CHUNK:] for h in range(2)] for n, a in enumerate(act)]
            att = []
            for n, a in enumerate(act):
                btw = jnp.concatenate([cs[n][h][:, :CHUNK] - lc[n][h] for h in range(2)], axis=1)
                e = jnp.exp(z[n] - sp[n] + btw + rtot[a])
                att.append(jnp.where(causal2, e, 0.0) if a == diag else e)
            g = [da[n] * att[n] for n in range(len(act))]
            cg = [_cumdot2(x, m_lt) for x in g]
            dz = []
            for n, a in enumerate(act):
                sig = jnp.exp(z[n] - sp[n])
                pre = jnp.concatenate([carry[5 * a + 3 + h] + cg[n][h][:, :CHUNK] for h in range(2)], axis=1)
                d = g[n] * (1.0 - sig) - pre * sig
                dz.append((jnp.where(causal2, d, 0.0) if a == diag else d).astype(MXU_DT))
            attb = [x.astype(MXU_DT) for x in att]
            dqc = [_dot(x, kk, "nn") for x in dz]
            dkc = [_dot(jnp.concatenate([dz[n][:, :CHUNK], dz[n][:, CHUNK:]], axis=0), qq[a], "tn")
                   for n, a in enumerate(act)]
            dvc = [_dot(jnp.concatenate([attb[n][:, :CHUNK], attb[n][:, CHUNK:]], axis=0), dd[a], "tn")
                   for n, a in enumerate(act)]
            dk_ref[pl.ds(off, CHUNK), :] += functools.reduce(lambda x, y: x + y, dkc)
            dv_ref[pl.ds(off, CHUNK), :] += functools.reduce(lambda x, y: x + y, dvc)
            new = list(carry)
            for n, a in enumerate(act):
                new[5 * a] = carry[5 * a] + dqc[n]
                for h in range(2):
                    new[5 * a + 1 + h] = lc[n][h]
                    new[5 * a + 3 + h] = carry[5 * a + 3 + h] + cg[n][h][:, CHUNK:]
            return tuple(new)

        z0 = jnp.zeros((CHUNK, 128), F32)
        res = lax.fori_loop(0, QSUB * qi, lambda kb, c: step(kb, c, None), (z0,) * (5 * QSUB))
        for j in range(QSUB):
            res = step(QSUB * qi + j, res, j)
        for a in range(QSUB):
            dq_ref[a * CHUNK:(a + 1) * CHUNK, :] = res[5 * a]

        if ex is not None:
            @pl.when(jnp.logical_and(pl.program_id(0) == npair - 1, pl.program_id(1) == nq - 1))
            def _():
                for cp in copies():
                    cp.wait()

    blk = pl.BlockSpec((QT, 128), lambda p, i: (i, p))
    full = pl.BlockSpec((S, 128), lambda p, i: (0, p))
    sh = jax.ShapeDtypeStruct((S, SB_W), F32)
    in_specs, out_specs, out_shape, aliases, scratch, extra = _with_exchange(
        ex, 5, 3,
        [blk, pl.BlockSpec((S, 128), lambda p, i: (0, npair + p)), pl.BlockSpec((S, 128), lambda p, i: (0, 2 * npair + p)),
         pl.BlockSpec((QT, 256), lambda p, i: (i, p)), blk],
        [blk, full, full], [sh, sh, sh])
    return pl.pallas_call(
        body, grid=(npair, nq), in_specs=in_specs, out_specs=out_specs, out_shape=out_shape,
        input_output_aliases=aliases, scratch_shapes=scratch, name=name,
        compiler_params=_cp())(qkv, qkv, qkv, rt, do, *extra)


def _adamw_math(w, g, m, v):
    m = ADAM_B1 * m + (1.0 - ADAM_B1) * g
    v = ADAM_B2 * v + (1.0 - ADAM_B2) * (g * g)
    m_hat = m / BC1
    v_hat = v / BC2
    delta = -ADAM_LR * (m_hat / (jnp.sqrt(v_hat) + ADAM_EPS) + ADAM_WD * w)
    return delta, m, v


def adamw_layer(w4, m4, v4, g, outs, l, name, tr):
    L, R, C = w4.shape
    n_alias = 0 if outs is None else 4

    def body(*refs):
        w_ref, m_ref, v_ref, g_ref = refs[:4]
        go_ref, d_ref, mo_ref, vo_ref = refs[4 + n_alias:]
        g = g_ref[...]
        d, m, v = _adamw_math(w_ref[0], g, m_ref[0], v_ref[0])
        go_ref[0] = g
        d_ref[0] = d
        mo_ref[0] = m
        vo_ref[0] = v

    st = pl.BlockSpec((1, tr, C), lambda i: (l, i, 0))
    sh = jax.ShapeDtypeStruct((L, R, C), F32)
    return pl.pallas_call(
        body, grid=(R // tr,),
        in_specs=[st, st, st, pl.BlockSpec((tr, C), lambda i: (i, 0))] + [ANY] * n_alias,
        out_specs=[st, st, st, st], out_shape=[sh, sh, sh, sh],
        input_output_aliases={4 + i: i for i in range(n_alias)}, name=name,
        compiler_params=_cp())(w4, m4, v4, g, *(outs or ()))


def adamw_flat(w, m, v, g, name):
    R, C = w.shape

    def body(w_ref, m_ref, v_ref, g_ref, d_ref, mo_ref, vo_ref):
        d, m2, v2 = _adamw_math(w_ref[...], g_ref[...], m_ref[...], v_ref[...])
        d_ref[...] = d
        mo_ref[...] = m2
        vo_ref[...] = v2

    full = pl.BlockSpec((R, C), lambda i: (0, 0))
    sh = jax.ShapeDtypeStruct((R, C), F32)
    return pl.pallas_call(body, grid=(1,), in_specs=[full] * 4, out_specs=[full] * 3, out_shape=[sh] * 3,
                          name=name, compiler_params=_cp())(w, m, v, g)


def add_half(g, l1, c, name, tr):
    nk, R, C = g.shape
    Rh = R // 2
    nt = Rh // tr

    def body(c_ref, g_ref, l_ref, o_ref):
        o_ref[...] = (g_ref[...].astype(F32) + l_ref[...].astype(F32)).astype(o_ref.dtype)

    gs = pltpu.PrefetchScalarGridSpec(
        num_scalar_prefetch=1, grid=(nk, nt),
        in_specs=[pl.BlockSpec((1, tr, C), lambda k, t, c_ref: (k, c_ref[0] * nt + t, 0)),
                  pl.BlockSpec((1, tr, C), lambda k, t, c_ref: (k, t, 0))],
        out_specs=pl.BlockSpec((1, tr, C), lambda k, t, c_ref: (k, t, 0)))
    return pl.pallas_call(body, grid_spec=gs, out_shape=jax.ShapeDtypeStruct((nk, Rh, C), g.dtype), name=name,
                          compiler_params=_cp())(c, g, l1)


def sum_chips(p, l2, kc, name, tr):
    nk, Rh, C = p.shape

    def body(k_ref, p_ref, a_ref, b_ref, c_ref, o_ref):
        o_ref[0] = ((p_ref[0].astype(F32) + a_ref[0].astype(F32)) + b_ref[0].astype(F32)) + c_ref[0].astype(F32)

    def slot(j):
        return pl.BlockSpec((1, tr, C), lambda t, k_ref: (j, t, 0))

    gs = pltpu.PrefetchScalarGridSpec(
        num_scalar_prefetch=1, grid=(Rh // tr,),
        in_specs=[pl.BlockSpec((1, tr, C), lambda t, k_ref: (k_ref[0], t, 0)), slot(0), slot(1), slot(2)],
        out_specs=pl.BlockSpec((1, tr, C), lambda t, k_ref: (k_ref[1], t, 0)))
    return pl.pallas_call(body, grid_spec=gs, out_shape=jax.ShapeDtypeStruct((2, Rh, C), F32), name=name,
                          compiler_params=_cp())(kc, p, l2, l2, l2)


def place_slot(a3, l, idx, nslot, dtype, name, tr):
    _, R, C = a3.shape

    def body(i_ref, a_ref, o_ref):
        o_ref[0] = a_ref[0].astype(dtype)

    gs = pltpu.PrefetchScalarGridSpec(
        num_scalar_prefetch=1, grid=(R // tr,),
        in_specs=[pl.BlockSpec((1, tr, C), lambda t, i_ref: (l, t, 0))],
        out_specs=pl.BlockSpec((1, tr, C), lambda t, i_ref: (i_ref[0], t, 0)))
    return pl.pallas_call(body, grid_spec=gs, out_shape=jax.ShapeDtypeStruct((nslot, R, C), dtype), name=name,
                          compiler_params=_cp())(idx, a3)


def _place():
    x, y, c = lax.axis_index("x"), lax.axis_index("y"), lax.axis_index("c")
    chips = [(1 - x, y), (x, 1 - y), (1 - x, 1 - y)]
    return x, y, c, chips


def _rcopy(src, dst, send_sems, recv_sems, k, to):
    return pltpu.make_async_remote_copy(src_ref=src, dst_ref=dst, send_sem=send_sems.at[k], recv_sem=recv_sems.at[k],
                                        device_id=to, device_id_type=MESH)


def gather_exchange(bufs):
    def build(in_refs, buf_refs, out_refs, send_sems, recv_sems):
        x, y, c, chips = _place()
        cps = []
        for i, ref in enumerate(buf_refs):
            rh = bufs[i].shape[1] // 2
            mine = ref.at[2 * x + y].at[pl.ds(c * rh, rh), :]
            for j, chip in enumerate(chips):
                cps.append(_rcopy(mine, mine, send_sems, recv_sems, 3 * i + j, (*chip, c)))
        return cps

    return Exchange([], bufs, [], 3 * len(bufs), build)


def scatter_exchange(ps):
    def build(in_refs, buf_refs, out_refs, send_sems, recv_sems):
        x, y, c, chips = _place()
        return [_rcopy(ref.at[2 * chip[0] + chip[1]], out_refs[i].at[j], send_sems, recv_sems, 3 * i + j, (*chip, c))
                for i, ref in enumerate(in_refs) for j, chip in enumerate(chips)]

    outs = [jax.ShapeDtypeStruct((3,) + p.shape[1:], p.dtype) for p in ps]
    return Exchange(ps, [], outs, 3 * len(ps), build)


def run_exchange(ex, name):
    def body(*refs):
        _, _, copies = ex.split(refs, 0, 0)
        cps = copies()
        for cp in cps:
            cp.start()
        for cp in cps:
            cp.wait()

    in_specs, out_specs, out_shape, aliases, scratch, extra = _with_exchange(ex, 0, 0, [], [], [])
    return pl.pallas_call(body, in_specs=in_specs, out_specs=out_specs, out_shape=out_shape,
                          input_output_aliases=aliases, scratch_shapes=scratch, name=name)(*extra)


def gather_forward(bufs, name):
    n = len(bufs)

    def body(*refs):
        outs = refs[n:2 * n]
        send_sems, recv_sems = refs[2 * n:]
        x, y, c, chips = _place()
        cps = []
        for i in range(n):
            rh = bufs[i].shape[1] // 2
            for j, chip in enumerate(chips):
                landed = outs[i].at[2 * chip[0] + chip[1]].at[pl.ds(c * rh, rh), :]
                cps.append(_rcopy(landed, landed, send_sems, recv_sems, 3 * i + j, (x, y, 1 - c)))
        for cp in cps:
            cp.start()
        for cp in cps:
            cp.wait()

    return pl.pallas_call(
        body, in_specs=[ANY] * n, out_specs=[ANY] * n,
        out_shape=[jax.ShapeDtypeStruct(b.shape, b.dtype) for b in bufs],
        input_output_aliases={i: i for i in range(n)},
        scratch_shapes=[pltpu.SemaphoreType.DMA((3 * n,)), pltpu.SemaphoreType.DMA((3 * n,))], name=name)(*bufs)


def exchange_sibling_half(gs, name):
    n = len(gs)

    def body(*refs):
        ins, outs = refs[:n], refs[n:2 * n]
        send_sems, recv_sems = refs[2 * n:]
        x, y, c, _ = _place()
        cps = []
        for i in range(n):
            rh = gs[i].shape[1] // 2
            cp = pltpu.make_async_remote_copy(
                src_ref=ins[i].at[:, pl.ds((1 - c) * rh, rh), :], dst_ref=outs[i], send_sem=send_sems.at[i],
                recv_sem=recv_sems.at[i], device_id=(x, y, 1 - c), device_id_type=MESH)
            cp.start()
            cps.append(cp)
        for cp in cps:
            cp.wait()

    return pl.pallas_call(
        body, in_specs=[ANY] * n, out_specs=[ANY] * n,
        out_shape=[jax.ShapeDtypeStruct((g.shape[0], g.shape[1] // 2, g.shape[2]), g.dtype) for g in gs],
        scratch_shapes=[pltpu.SemaphoreType.DMA((n,)), pltpu.SemaphoreType.DMA((n,))],
        name=name)(*gs)


def join_sibling_halves(fs, name):
    n = len(fs)

    def body(*refs):
        outs = refs[n:2 * n]
        send_sems, recv_sems = refs[2 * n:]
        x, y, c, _ = _place()
        cps = [_rcopy(outs[i].at[c], outs[i].at[c], send_sems, recv_sems, i, (x, y, 1 - c)) for i in range(n)]
        for cp in cps:
            cp.start()
        for cp in cps:
            cp.wait()

    return pl.pallas_call(
        body, in_specs=[ANY] * n, out_specs=[ANY] * n,
        out_shape=[jax.ShapeDtypeStruct(f.shape, f.dtype) for f in fs],
        input_output_aliases={i: i for i in range(n)},
        scratch_shapes=[pltpu.SemaphoreType.DMA((n,)), pltpu.SemaphoreType.DMA((n,))], name=name)(*fs)


def all_reduce_small(v, name):
    R, C = v.shape

    def body(v_ref, sum_ref, all_ref, send_sems, recv_sems, local_sem):
        x, y, c, chips = _place()
        me, sib = (x, y, c), (x, y, 1 - c)

        def slab(px, py, pc):
            return all_ref.at[4 * px + 2 * py + pc]

        def copy(k, block, to, src=None):
            return pltpu.make_async_remote_copy(
                src_ref=slab(*block) if src is None else src, dst_ref=slab(*block), send_sem=send_sems.at[k],
                recv_sem=recv_sems.at[k], device_id=to, device_id_type=MESH)

        mine = pltpu.make_async_copy(v_ref, slab(*me), local_sem)
        mine.start()
        first = [copy(0, me, sib, src=v_ref)] + [copy(1 + j, me, (*chip, c), src=v_ref) for j, chip in enumerate(chips)]
        for cp in first:
            cp.start()
        passed = [copy(4 + j, (*chip, c), sib) for j, chip in enumerate(chips)]
        for j, chip in enumerate(chips):
            copy(1 + j, (*chip, c), me).wait_recv()
            passed[j].start()
        copy(0, sib, me).wait_recv()
        for j, chip in enumerate(chips):
            copy(4 + j, (*chip, 1 - c), me).wait_recv()
        for cp in first + passed:
            cp.wait_send()
        mine.wait()
        acc = all_ref[0]
        for d in range(1, 8):
            acc = acc + all_ref[d]
        sum_ref[...] = acc

    vm = pl.BlockSpec(memory_space=pltpu.VMEM)
    return pl.pallas_call(
        body, in_specs=[vm], out_specs=[vm, vm],
        out_shape=[jax.ShapeDtypeStruct((R, C), F32), jax.ShapeDtypeStruct((8, R, C), F32)],
        scratch_shapes=[pltpu.SemaphoreType.DMA((7,)), pltpu.SemaphoreType.DMA((7,)), pltpu.SemaphoreType.DMA],
        name=name, compiler_params=_cp())(v)[0]


SMALL = ("mix_norm_g", "conv_w", "conv_b", "conv_ln_g", "conv_ln_b", "sg_ln_g", "sg_ln_b", "sg_w", "sg_b",
         "q_norm_g", "k_norm_g", "out_norm_g", "ffn_norm_g")
BIG = ("w_in", "w_out", "w_gate_up", "w_down")
WEIGHTS = ("mix_norm_g", "w_in", "conv_w", "conv_b", "conv_ln_g", "conv_ln_b", "sg_ln_g", "sg_ln_b", "sg_w", "sg_b",
           "q_norm_g", "k_norm_g", "out_norm_g", "w_out", "ffn_norm_g", "w_gate_up", "w_down")
ADAM_ROWS = {"w_in": 512, "w_out": 256, "w_gate_up": 256, "w_down": 352}
RS_ROWS = {"w_in": 512, "w_out": 128, "w_gate_up": 512, "w_down": 352}
AG_ROWS = {"w_in": 512, "w_out": 256, "w_gate_up": 512, "w_down": 352}


def _pack(parts):
    flat = jnp.concatenate([p.reshape(-1) for p in parts])
    n = flat.shape[0]
    rows = -(-n // (8 * 128)) * 8
    return jnp.pad(flat, (0, rows * 128 - n)).reshape(rows, 128)


def _unpack(buf, shapes):
    flat = buf.reshape(-1)
    out, off = [], 0
    for s in shapes:
        n = 1
        for d in s:
            n *= d
        out.append(flat[off:off + n].reshape(s))
        off += n
    return out


def layer_forward(x, l, P, W, ex=None, after=None, ex2=None, after2=None):
    sv = {"x": x}
    sv["h"] = rms_fwd(x, P["mix_norm_g"], l, f"rms_mix_{l}")
    sv["proj"] = mm_colblk(sv["h"], W["w_in"], F32, f"mm_in_{l}")
    sv["yc"] = conv_fwd(sv["proj"], P["conv_w"], P["conv_b"], P["conv_ln_g"], P["conv_ln_b"], l, f"conv_fwd_{l}")
    sv["ys"] = sg_fwd(sv["proj"], P["sg_ln_g"], P["sg_ln_b"], P["sg_wt"], P["sg_bt"], l, f"sg_fwd_{l}")
    sv["qkv"] = qkv_fwd(sv["proj"], P["q_norm_g"], P["k_norm_g"], l, f"qkv_fwd_{l}")
    sv["yb"], sv["rt"], *moved = attn_fwd(sv["qkv"], f"attn_fwd_{l}", ex)
    if after is not None:
        W = after(moved)
    sv["yn"] = outnorm_fwd(sv["yc"], sv["ys"], sv["yb"], P["out_norm_g"], l, f"outnorm_fwd_{l}")
    sv["x1"] = mm_res(sv["yn"], W["w_out"].reshape(D_MODEL, D_MODEL), x, f"mm_out_{l}")
    sv["h2"] = rms_fwd(sv["x1"], P["ffn_norm_g"], l, f"rms_ffn_{l}")
    sv["g"], sv["u"], sv["act"], *moved2 = ffn_up(sv["h2"], W["w_gate_up"], f"ffn_up_{l}", ex=ex2)
    if after2 is not None:
        W = after2(moved2)
    x2 = mm_res(sv["act"], W["w_down"].reshape(FFN, D_MODEL), sv["x1"], f"mm_down_{l}")
    return x2, sv, moved


def layer_backward(dx2, l, P, W, sv, mid=None):
    gb, gs = {}, {}
    wdown = W["w_down"].reshape(FFN, D_MODEL)
    dgu = ffn_down_bwd(dx2, wdown, sv["g"], sv["u"], f"ffn_down_bwd_{l}")
    gb["w_down"] = mm_wgrad(sv["act"], dx2, 1408, 512, f"wgrad_down_{l}", False).reshape(N_CHIP, FFN // N_CHIP, D_MODEL)
    dh2 = mm_dgrad_colblk(dgu, W["w_gate_up"], f"dgrad_gu_{l}")
    gb["w_gate_up"] = mm_wgrad(sv["h2"], dgu, 512, 1408, f"wgrad_gu_{l}", True)
    dx1, gs["ffn_norm_g"] = rms_bwd(dh2, sv["x1"], P["ffn_norm_g"], l, dx2, f"rms_ffn_bwd_{l}")
    dyn = mm_dgrad(dx1, W["w_out"].reshape(D_MODEL, D_MODEL), f"dgrad_out_{l}")
    gb["w_out"] = mm_wgrad(sv["yn"], dx1, 512, 1024, f"wgrad_out_{l}", False).reshape(N_CHIP, D_MODEL // N_CHIP, D_MODEL)
    dyc, dys, dyb, gs["out_norm_g"] = outnorm_bwd(dyn, sv["yc"], sv["ys"], sv["yb"], P["out_norm_g"], l,
                                                  f"outnorm_bwd_{l}")
    ex, done = mid(gb) if mid is not None else (None, None)
    dq, dk, dv, *moved = attn_bwd(sv["qkv"], sv["rt"], dyb, f"attn_bwd_{l}", ex)
    if done is not None:
        done(moved)
    dps, gs["sg_ln_g"], gs["sg_ln_b"], gs["sg_w"], dbt = sg_bwd(
        sv["proj"], dys, P["sg_ln_g"], P["sg_ln_b"], P["sg_wt"], P["sg_wtt"], P["sg_bt"], l, f"sg_bwd_{l}")
    gs["sg_b"] = dbt.T
    dpc, dcw, gs["conv_b"], gs["conv_ln_g"], gs["conv_ln_b"] = conv_bwd(
        sv["proj"], dyc, P["conv_w"], P["conv_b"], P["conv_ln_g"], P["conv_ln_b"], l, f"conv_bwd_{l}")
    gs["conv_w"] = dcw[:CONV_K]
    dproj, dgq, dgk = qkv_bwd(sv["proj"], dq, dk, dv, P["q_norm_g"], P["k_norm_g"], dpc, dps, l, f"qkv_bwd_{l}")
    gs["q_norm_g"] = dgq[0, :HEAD_DIM] + dgq[0, HEAD_DIM:]
    gs["k_norm_g"] = dgk[0, :HEAD_DIM] + dgk[0, HEAD_DIM:]
    dh = mm_dgrad_colblk(dproj, W["w_in"], f"dgrad_in_{l}")
    gb["w_in"] = mm_wgrad(sv["h"], dproj, 1024, IN_W // N_CHIP, f"wgrad_in_{l}", True)
    dx, gs["mix_norm_g"] = rms_bwd(dh, sv["x"], P["mix_norm_g"], l, dx1, f"rms_mix_bwd_{l}")
    return dx, gb, gs


def reduce_start(gl, items, c1):
    n0, l0 = items[0]
    l1 = exchange_sibling_half(gl, f"rs_sibling_{n0}_{l0}")
    return [add_half(g, a, c1, f"rs_add_{n}_{l}", RS_ROWS[n]) for (n, l), g, a in zip(items, gl, l1)]


def reduce_finish(ps, l2, items, kc):
    n0, l0 = items[0]
    fs = [sum_chips(p, a, kc, f"rs_sum_{n}_{l}", RS_ROWS[n]) for (n, l), p, a in zip(items, ps, l2)]
    full = join_sibling_halves(fs, f"rs_join_{n0}_{l0}")
    return [f.reshape(2 * f.shape[1], f.shape[2]) for f in full]


def kernel(x, mix_norm_g, w_in, conv_w, conv_b, conv_ln_g, conv_ln_b, sg_ln_g, sg_ln_b, sg_w, sg_b, q_norm_g, k_norm_g, out_norm_g, w_out, ffn_norm_g, w_gate_up, w_down, loss_target, m_mix_norm_g, m_w_in, m_conv_w, m_conv_b, m_conv_ln_g, m_conv_ln_b, m_sg_ln_g, m_sg_ln_b, m_sg_w, m_sg_b, m_q_norm_g, m_k_norm_g, m_out_norm_g, m_w_out, m_ffn_norm_g, m_w_gate_up, m_w_down, v_mix_norm_g, v_w_in, v_conv_w, v_conv_b, v_conv_ln_g, v_conv_ln_b, v_sg_ln_g, v_sg_ln_b, v_sg_w, v_sg_b, v_q_norm_g, v_k_norm_g, v_out_norm_g, v_w_out, v_ffn_norm_g, v_w_gate_up, v_w_down):
    w = dict(mix_norm_g=mix_norm_g, w_in=w_in, conv_w=conv_w, conv_b=conv_b, conv_ln_g=conv_ln_g, conv_ln_b=conv_ln_b,
             sg_ln_g=sg_ln_g, sg_ln_b=sg_ln_b, sg_w=sg_w, sg_b=sg_b, q_norm_g=q_norm_g, k_norm_g=k_norm_g,
             out_norm_g=out_norm_g, w_out=w_out, ffn_norm_g=ffn_norm_g, w_gate_up=w_gate_up, w_down=w_down)
    m = dict(mix_norm_g=m_mix_norm_g, w_in=m_w_in, conv_w=m_conv_w, conv_b=m_conv_b, conv_ln_g=m_conv_ln_g,
             conv_ln_b=m_conv_ln_b, sg_ln_g=m_sg_ln_g, sg_ln_b=m_sg_ln_b, sg_w=m_sg_w, sg_b=m_sg_b,
             q_norm_g=m_q_norm_g, k_norm_g=m_k_norm_g, out_norm_g=m_out_norm_g, w_out=m_w_out,
             ffn_norm_g=m_ffn_norm_g, w_gate_up=m_w_gate_up, w_down=m_w_down)
    v = dict(mix_norm_g=v_mix_norm_g, w_in=v_w_in, conv_w=v_conv_w, conv_b=v_conv_b, conv_ln_g=v_conv_ln_g,
             conv_ln_b=v_conv_ln_b, sg_ln_g=v_sg_ln_g, sg_ln_b=v_sg_ln_b, sg_w=v_sg_w, sg_b=v_sg_b,
             q_norm_g=v_q_norm_g, k_norm_g=v_k_norm_g, out_norm_g=v_out_norm_g, w_out=v_w_out,
             ffn_norm_g=v_ffn_norm_g, w_gate_up=v_w_gate_up, w_down=v_w_down)
    L = DEPTH
    xi, yi, ci = lax.axis_index("x"), lax.axis_index("y"), lax.axis_index("c")
    kme = 2 * xi + yi
    c1 = ci.astype(jnp.int32).reshape(1)
    k1 = kme.astype(jnp.int32).reshape(1)
    kc = jnp.stack([kme, ci]).astype(jnp.int32)

    def gather_parts(items):
        return [place_slot(w[n], l, k1, N_CHIP, BF16, f"ag_own_{n}_{l}", AG_ROWS[n]) for n, l in items]

    cw_sh = jnp.pad(conv_w.reshape(L * CONV_K, CONV_W // N_CHIP), ((0, 128 - L * CONV_K), (0, 0)))
    cw_buf = place_slot(cw_sh[None], 0, k1, N_CHIP, F32, "ag_own_conv", 128)
    *bufs, cw_buf = run_exchange(gather_exchange(gather_parts([("w_in", 0)]) + [cw_buf]), "ag_chips_first")
    *bufs, cw_buf = gather_forward(bufs + [cw_buf], "ag_sibling_first")
    W = [{} for _ in range(L)]
    W[0]["w_in"] = bufs[0]
    cw_all = cw_buf[:, :L * CONV_K].reshape(N_CHIP, L, CONV_K, CONV_W // N_CHIP)
    cw_full = jnp.transpose(cw_all, (1, 2, 0, 3)).reshape(L, CONV_K, CONV_W)

    tril = jnp.tril(jnp.ones((CHUNK, CHUNK), bool))
    sg_wt = jnp.where(tril, sg_w, 0.0)
    P = {
        "mix_norm_g": mix_norm_g.reshape(L, 1, D_MODEL), "ffn_norm_g": ffn_norm_g.reshape(L, 1, D_MODEL),
        "out_norm_g": out_norm_g.reshape(L, 1, D_MODEL),
        "conv_w": jnp.pad(cw_full, ((0, 0), (0, 1), (0, 0))), "conv_b": conv_b.reshape(L, 1, CONV_W),
        "conv_ln_g": conv_ln_g.reshape(L, 1, CONV_W), "conv_ln_b": conv_ln_b.reshape(L, 1, CONV_W),
        "sg_ln_g": sg_ln_g.reshape(L, 1, SG_W), "sg_ln_b": sg_ln_b.reshape(L, 1, SG_W),
        "sg_wt": sg_wt.astype(MXU_DT), "sg_wtt": jnp.swapaxes(sg_wt, 2, 3).astype(MXU_DT),
        "sg_bt": jnp.swapaxes(sg_b, 1, 2),
        "q_norm_g": jnp.tile(q_norm_g, (1, 2)).reshape(L, 1, 128), "k_norm_g": jnp.tile(k_norm_g, (1, 2)).reshape(L, 1, 128),
    }

    h = x[0]
    saved = []
    for l in range(L):
        items = [("w_out", l), ("w_gate_up", l)] + ([("w_in", l + 1)] if l + 1 < L else [])
        items2 = [("w_down", l)]

        def after(moved, l=l, items=items, tag="a"):
            for (n, ll), a in zip(items, gather_forward(moved, f"ag_sibling_{tag}_{l}")):
                W[ll][n] = a
            return W[l]

        h, sv, _ = layer_forward(h, l, P, W[l], gather_exchange(gather_parts(items)), after,
                                 gather_exchange(gather_parts(items2)),
                                 functools.partial(after, l=l, items=items2, tag="b"))
        saved.append(sv)
    dy, loss_part = loss_head(h, loss_target[0], "loss_head")
    loss = lax.psum(loss_part[0, 0], ("x", "y", "c"))

    outs = {n: None for n in BIG}
    small_grads = [None] * L

    def finish(items, ps, l2):
        for (n, lyr), g in zip(items, reduce_finish(ps, l2, items, kc)):
            outs[n] = adamw_layer(w[n], m[n], v[n], g, outs[n], lyr, f"adamw_{n}_{lyr}", ADAM_ROWS[n])

    g_in = None
    for l in reversed(range(L)):
        def mid(gb, l=l, g_in=g_in):
            items = [("w_down", l), ("w_gate_up", l), ("w_out", l)] + ([("w_in", l + 1)] if g_in is not None else [])
            ps = reduce_start([gb[n] for n, _ in items[:3]] + ([g_in] if g_in is not None else []), items, c1)
            return scatter_exchange(ps), lambda moved: finish(items, ps, moved)

        dy, gb, small_grads[l] = layer_backward(dy, l, P, W[l], saved[l], mid)
        g_in = gb["w_in"]
    ps = reduce_start([g_in], [("w_in", 0)], c1)
    finish([("w_in", 0)], ps, run_exchange(scatter_exchange(ps), "rs_chips_last"))

    shapes = [(L,) + small_grads[0][n].shape for n in SMALL]
    packed = _pack([jnp.stack([small_grads[l][n] for l in range(L)]) for n in SMALL])
    gsum = dict(zip(SMALL, _unpack(all_reduce_small(packed, "ar_small"), shapes)))
    gsum["conv_w"] = lax.dynamic_slice_in_dim(gsum["conv_w"], kme * (CONV_W // N_CHIP), CONV_W // N_CHIP, axis=2)
    gsum = {n: gsum[n].reshape(w[n].shape) for n in SMALL}
    lshapes = [w[n].shape for n in SMALL]
    d_p, m_p, v_p = adamw_flat(_pack([w[n] for n in SMALL]), _pack([m[n] for n in SMALL]),
                               _pack([v[n] for n in SMALL]), _pack([gsum[n] for n in SMALL]), "adamw_small")
    d_s = dict(zip(SMALL, _unpack(d_p, lshapes)))
    m_s = dict(zip(SMALL, _unpack(m_p, lshapes)))
    v_s = dict(zip(SMALL, _unpack(v_p, lshapes)))

    grads = {n: (outs[n][0] if n in BIG else gsum[n]) for n in WEIGHTS}
    delta = {n: (outs[n][1] if n in BIG else d_s[n]) for n in WEIGHTS}
    new_m = {n: (outs[n][2] if n in BIG else m_s[n]) for n in WEIGHTS}
    new_v = {n: (outs[n][3] if n in BIG else v_s[n]) for n in WEIGHTS}
    return (loss, dy[None], *[grads[n] for n in WEIGHTS], *[delta[n] for n in WEIGHTS],
            *[new_m[n] for n in WEIGHTS], *[new_v[n] for n in WEIGHTS])
```

```python
import functools

import jax
import jax.numpy as jnp
from jax import lax
from jax.experimental import pallas as pl
from jax.experimental.pallas import tpu as pltpu

F32 = jnp.float32
BF16 = jnp.bfloat16
MXU_DT = jnp.bfloat16
GRAD_WIRE_DT = jnp.bfloat16

D_MODEL = 1024
DEPTH = 4
HEAD_DIM = 64
CONV_W = 256
SG_W = 256
SB_W = 512
CONV_K = 31
CHUNK = 128
OFF_SG = 2 * CONV_W
OFF_SB = OFF_SG + 2 * SG_W
IN_W = OFF_SB + 3 * SB_W
FFN = 2816
N_CHIP = 4
RMS_EPS = 1e-6
LN_EPS = 1e-5
ADAM_LR = 0.001
ADAM_B1 = 0.9
ADAM_B2 = 0.999
ADAM_EPS = 1e-08
ADAM_WD = 0.01
ADAM_STEP = 10
BC1 = 1.0 - ADAM_B1 ** ADAM_STEP
BC2 = 1.0 - ADAM_B2 ** ADAM_STEP
HALO = 32
MESH = pl.DeviceIdType.MESH
ANY = pl.BlockSpec(memory_space=pl.ANY)
VMEM_LIMIT = 56 * 1024 * 1024


def _cp(**kw):
    return pltpu.CompilerParams(vmem_limit_bytes=VMEM_LIMIT, **kw)


def _dot(a, b, dims):
    dn = {"nn": (((1,), (0,)), ((), ())), "nt": (((1,), (1,)), ((), ())), "tn": (((0,), (0,)), ((), ()))}[dims]
    return lax.dot_general(a.astype(MXU_DT), b.astype(MXU_DT), dn, preferred_element_type=F32)


def _iota(shape, axis):
    return lax.broadcasted_iota(jnp.int32, shape, axis)


def _rms(x, g):
    return x * lax.rsqrt(jnp.mean(x * x, axis=-1, keepdims=True) + RMS_EPS) * g


def _ln(x, g, b):
    mu = jnp.mean(x, axis=-1, keepdims=True)
    xc = x - mu
    var = jnp.mean(xc * xc, axis=-1, keepdims=True)
    return xc * lax.rsqrt(var + LN_EPS) * g + b


def _glu(val, gate):
    return val * jax.nn.sigmoid(gate)


def _ln_silu(c, g, b):
    return jax.nn.silu(_ln(c, g, b))


_ERF_ALPHA = (-2.72614225801306e-10, 2.77068142495902e-08, -2.10102402082508e-06, -5.69250639462346e-05,
              -7.34990630326855e-04, -2.95459980854025e-03, -1.60960333262415e-02)
_ERF_BETA = (-1.45660718464996e-05, -2.13374055278905e-04, -1.68282697438203e-03, -7.37332916720468e-03,
             -1.42647390514189e-02)


def _erf(x):
    x = jnp.clip(x, -4.0, 4.0)
    x2 = x * x
    p = jnp.full_like(x, _ERF_ALPHA[0])
    for a in _ERF_ALPHA[1:]:
        p = p * x2 + a
    q = jnp.full_like(x, _ERF_BETA[0])
    for b in _ERF_BETA[1:]:
        q = q * x2 + b
    return x * p / q


@jax.custom_jvp
def _gelu(x):
    return 0.5 * x * (1.0 + _erf(x * (2.0 ** -0.5)))


@_gelu.defjvp
def _gelu_jvp(primals, tangents):
    (x,), (t,) = primals, tangents
    cdf = 0.5 * (1.0 + _erf(x * (2.0 ** -0.5)))
    pdf = jnp.exp(-0.5 * x * x) * ((2.0 * jnp.pi) ** -0.5)
    return x * cdf, t * (cdf + x * pdf)


def _sg_pre(uvp, g, b):
    uv = _gelu(uvp)
    return uv[:, :SG_W], _ln(uv[:, SG_W:], g, b)


def _outnorm(yc, ys, yb, g):
    return jnp.concatenate([_rms(yc, g[:, :CONV_W]), _rms(ys, g[:, CONV_W:CONV_W + SG_W]),
                            _rms(yb, g[:, CONV_W + SG_W:])], axis=-1)


def _qkv_fn(pq, pk, pv, gq, gk):
    outs = []
    for p, g, sc in ((pq, gq, HEAD_DIM ** -0.5), (pk, gk, 1.0)):
        for s in range(SB_W // 128):
            x = p[:, 128 * s:128 * (s + 1)]
            seg = _iota(x.shape, 1) < HEAD_DIM
            x2 = x * x
            s0 = jnp.sum(jnp.where(seg, x2, 0.0), axis=-1, keepdims=True)
            s1 = jnp.sum(jnp.where(seg, 0.0, x2), axis=-1, keepdims=True)
            ms = jnp.where(seg, s0, s1) * (1.0 / HEAD_DIM)
            outs.append(x * lax.rsqrt(ms + RMS_EPS) * (g * sc))
    outs.append(pv)
    return jnp.concatenate(outs, axis=-1)


def _swiglu(g, u):
    return jax.nn.silu(g) * u


def _softplus(z):
    return jnp.maximum(z, 0.0) + jnp.log(1.0 + jnp.exp(-jnp.abs(z)))


def mm_colblk(a, wb, out_dtype, name, tm=1024):
    S, K = a.shape
    nb, _, C = wb.shape

    def body(a_ref, w_ref, o_ref):
        o_ref[...] = _dot(a_ref[...], w_ref[0], "nn").astype(o_ref.dtype)

    return pl.pallas_call(
        body, grid=(nb, S // tm),
        in_specs=[pl.BlockSpec((tm, K), lambda k, i: (i, 0)), pl.BlockSpec((1, K, C), lambda k, i: (k, 0, 0))],
        out_specs=pl.BlockSpec((tm, C), lambda k, i: (i, k)),
        out_shape=jax.ShapeDtypeStruct((S, nb * C), out_dtype), name=name, compiler_params=_cp())(a, wb)


def mm_res(a, w, res, name, tm=512):
    S, K = a.shape
    N = w.shape[1]

    def body(a_ref, w_ref, r_ref, o_ref):
        o_ref[...] = r_ref[...] + _dot(a_ref[...], w_ref[...], "nn")

    return pl.pallas_call(
        body, grid=(S // tm,),
        in_specs=[pl.BlockSpec((tm, K), lambda i: (i, 0)), pl.BlockSpec((K, N), lambda i: (0, 0)),
                  pl.BlockSpec((tm, N), lambda i: (i, 0))],
        out_specs=pl.BlockSpec((tm, N), lambda i: (i, 0)),
        out_shape=jax.ShapeDtypeStruct((S, N), F32), name=name, compiler_params=_cp())(a, w, res)


def ffn_up(h2, wgu, name, tm=512, ex=None):
    S, K = h2.shape
    C = wgu.shape[2]
    nm = S // tm

    def body(*refs):
        if ex is None:
            (h_ref, wg_ref, wu_ref), (g_ref, u_ref, a_ref), copies = refs[:3], refs[3:6], None
        else:
            (h_ref, wg_ref, wu_ref), (g_ref, u_ref, a_ref), copies = ex.split(refs, 3, 3)

            @pl.when(jnp.logical_and(pl.program_id(0) == 0, pl.program_id(1) == 0))
            def _():
                for cp in copies():
                    cp.start()

        h = h_ref[...]
        g = _dot(h, wg_ref[0], "nn")
        u = _dot(h, wu_ref[0], "nn")
        g_ref[...] = g.astype(BF16)
        u_ref[...] = u.astype(BF16)
        a_ref[...] = _swiglu(g, u).astype(BF16)

        if ex is not None:
            @pl.when(jnp.logical_and(pl.program_id(0) == 1, pl.program_id(1) == nm - 1))
            def _():
                for cp in copies():
                    cp.wait()

    o = pl.BlockSpec((tm, C), lambda j, i: (i, j))
    sh = jax.ShapeDtypeStruct((S, 2 * C), BF16)
    in_specs, out_specs, out_shape, aliases, scratch, extra = _with_exchange(
        ex, 3, 3,
        [pl.BlockSpec((tm, K), lambda j, i: (i, 0)), pl.BlockSpec((1, K, C), lambda j, i: (j, 0, 0)),
         pl.BlockSpec((1, K, C), lambda j, i: (2 + j, 0, 0))],
        [o, o, o], [sh, sh, sh])
    return pl.pallas_call(
        body, grid=(2, nm), in_specs=in_specs, out_specs=out_specs, out_shape=out_shape,
        input_output_aliases=aliases, scratch_shapes=scratch, name=name, compiler_params=_cp())(h2, wgu, wgu, *extra)


def ffn_down_bwd(dx2, wdown, g, u, name, tm=512):
    S, N = dx2.shape
    C = FFN // 2

    def body(d_ref, w_ref, g_ref, u_ref, o_ref):
        d = d_ref[...]
        for j in range(2):
            cols = slice(j * C, (j + 1) * C)
            dact = _dot(d, w_ref[cols, :], "nt")
            g = g_ref[:, cols].astype(F32)
            s = jax.nn.sigmoid(g)
            gs = g * s
            o_ref[:, cols] = (dact * u_ref[:, cols].astype(F32) * (s + gs * (1.0 - s))).astype(BF16)
            o_ref[:, FFN + j * C:FFN + (j + 1) * C] = (dact * gs).astype(BF16)

    row = pl.BlockSpec((tm, FFN), lambda i: (i, 0))
    return pl.pallas_call(
        body, grid=(S // tm,),
        in_specs=[pl.BlockSpec((tm, N), lambda i: (i, 0)), pl.BlockSpec((FFN, N), lambda i: (0, 0)), row, row],
        out_specs=pl.BlockSpec((tm, 2 * FFN), lambda i: (i, 0)),
        out_shape=jax.ShapeDtypeStruct((S, 2 * FFN), BF16), name=name, compiler_params=_cp())(dx2, wdown, g, u)


def mm_dgrad_colblk(do, wb, name, tm=1024):
    S = do.shape[0]
    nb, K, C = wb.shape

    def body(d_ref, w_ref, o_ref):
        k = pl.program_id(1)
        r = _dot(d_ref[...], w_ref[0], "nt")

        @pl.when(k == 0)
        def _():
            o_ref[...] = r

        @pl.when(k != 0)
        def _():
            o_ref[...] += r

    return pl.pallas_call(
        body, grid=(S // tm, nb),
        in_specs=[pl.BlockSpec((tm, C), lambda i, k: (i, k)), pl.BlockSpec((1, K, C), lambda i, k: (k, 0, 0))],
        out_specs=pl.BlockSpec((tm, K), lambda i, k: (i, 0)),
        out_shape=jax.ShapeDtypeStruct((S, K), F32), name=name, compiler_params=_cp())(do, wb)


def mm_dgrad(do, w, name, tm=1024):
    S, N = do.shape
    K = w.shape[0]

    def body(d_ref, w_ref, o_ref):
        o_ref[...] = _dot(d_ref[...], w_ref[...], "nt")

    return pl.pallas_call(
        body, grid=(S // tm,),
        in_specs=[pl.BlockSpec((tm, N), lambda i: (i, 0)), pl.BlockSpec((K, N), lambda i: (0, 0))],
        out_specs=pl.BlockSpec((tm, K), lambda i: (i, 0)),
        out_shape=jax.ShapeDtypeStruct((S, K), F32), name=name, compiler_params=_cp())(do, w)


def mm_wgrad(a, do, tk, tn, name, blocked):
    S, K = a.shape
    N = do.shape[1]

    def body(a_ref, d_ref, o_ref):
        r = _dot(a_ref[...], d_ref[...], "tn").astype(GRAD_WIRE_DT)
        if blocked:
            o_ref[0] = r
        else:
            o_ref[...] = r

    if blocked:
        out_spec = pl.BlockSpec((1, tk, tn), lambda n, j: (n, j, 0))
        out_shape = jax.ShapeDtypeStruct((N // tn, K, tn), GRAD_WIRE_DT)
    else:
        out_spec = pl.BlockSpec((tk, tn), lambda n, j: (j, n))
        out_shape = jax.ShapeDtypeStruct((K, N), GRAD_WIRE_DT)
    return pl.pallas_call(
        body, grid=(N // tn, K // tk),
        in_specs=[pl.BlockSpec((S, tk), lambda n, j: (0, j)), pl.BlockSpec((S, tn), lambda n, j: (0, n))],
        out_specs=out_spec, out_shape=out_shape, name=name, compiler_params=_cp())(a, do)


def rms_fwd(x, g3, l, name, tm=512):
    S, N = x.shape

    def body(x_ref, g_ref, o_ref):
        o_ref[...] = _rms(x_ref[...], g_ref[0]).astype(BF16)

    return pl.pallas_call(
        body, grid=(S // tm,),
        in_specs=[pl.BlockSpec((tm, N), lambda i: (i, 0)), pl.BlockSpec((1, 1, N), lambda i: (l, 0, 0))],
        out_specs=pl.BlockSpec((tm, N), lambda i: (i, 0)),
        out_shape=jax.ShapeDtypeStruct((S, N), BF16), name=name, compiler_params=_cp())(x, g3)


def rms_bwd(dh, x, g3, l, dres, name, tm=512):
    S, N = x.shape

    def body(dh_ref, x_ref, g_ref, r_ref, dx_ref, dg_ref):
        _, vjp = jax.vjp(_rms, x_ref[...], g_ref[0])
        dx, dg = vjp(dh_ref[...])
        dx_ref[...] = r_ref[...] + dx

        @pl.when(pl.program_id(0) == 0)
        def _():
            dg_ref[...] = jnp.zeros_like(dg_ref)

        dg_ref[...] += dg

    row = pl.BlockSpec((tm, N), lambda i: (i, 0))
    return pl.pallas_call(
        body, grid=(S // tm,),
        in_specs=[row, row, pl.BlockSpec((1, 1, N), lambda i: (l, 0, 0)), row],
        out_specs=[row, pl.BlockSpec((1, N), lambda i: (0, 0))],
        out_shape=[jax.ShapeDtypeStruct((S, N), F32), jax.ShapeDtypeStruct((1, N), F32)],
        name=name, compiler_params=_cp())(dh, x, g3, dres)


def outnorm_fwd(yc, ys, yb, g3, l, name, tm=512):
    S = yc.shape[0]

    def body(c_ref, s_ref, b_ref, g_ref, o_ref):
        o_ref[...] = _outnorm(c_ref[...], s_ref[...], b_ref[...], g_ref[0]).astype(BF16)

    return pl.pallas_call(
        body, grid=(S // tm,),
        in_specs=[pl.BlockSpec((tm, CONV_W), lambda i: (i, 0)), pl.BlockSpec((tm, SG_W), lambda i: (i, 0)),
                  pl.BlockSpec((tm, SB_W), lambda i: (i, 0)), pl.BlockSpec((1, 1, D_MODEL), lambda i: (l, 0, 0))],
        out_specs=pl.BlockSpec((tm, D_MODEL), lambda i: (i, 0)),
        out_shape=jax.ShapeDtypeStruct((S, D_MODEL), BF16), name=name, compiler_params=_cp())(yc, ys, yb, g3)


def outnorm_bwd(dyn, yc, ys, yb, g3, l, name, tm=512):
    S = yc.shape[0]

    def body(d_ref, c_ref, s_ref, b_ref, g_ref, dc_ref, ds_ref, db_ref, dg_ref):
        _, vjp = jax.vjp(_outnorm, c_ref[...], s_ref[...], b_ref[...], g_ref[0])
        dc, ds, db, dg = vjp(d_ref[...])
        dc_ref[...] = dc
        ds_ref[...] = ds
        db_ref[...] = db

        @pl.when(pl.program_id(0) == 0)
        def _():
            dg_ref[...] = jnp.zeros_like(dg_ref)

        dg_ref[...] += dg

    sc = pl.BlockSpec((tm, CONV_W), lambda i: (i, 0))
    ss = pl.BlockSpec((tm, SG_W), lambda i: (i, 0))
    sb = pl.BlockSpec((tm, SB_W), lambda i: (i, 0))
    return pl.pallas_call(
        body, grid=(S // tm,),
        in_specs=[pl.BlockSpec((tm, D_MODEL), lambda i: (i, 0)), sc, ss, sb,
                  pl.BlockSpec((1, 1, D_MODEL), lambda i: (l, 0, 0))],
        out_specs=[sc, ss, sb, pl.BlockSpec((1, D_MODEL), lambda i: (0, 0))],
        out_shape=[jax.ShapeDtypeStruct((S, CONV_W), F32), jax.ShapeDtypeStruct((S, SG_W), F32),
                   jax.ShapeDtypeStruct((S, SB_W), F32), jax.ShapeDtypeStruct((1, D_MODEL), F32)],
        name=name, compiler_params=_cp())(dyn, yc, ys, yb, g3)


def loss_head(y, t, name, tm=512):
    S, N = y.shape

    def body(y_ref, t_ref, dy_ref, l_ref):
        e = y_ref[...] - t_ref[...]
        dy_ref[...] = e * (1.0 / N)

        @pl.when(pl.program_id(0) == 0)
        def _():
            l_ref[...] = jnp.zeros_like(l_ref)

        l_ref[...] += (0.5 / N) * jnp.sum(jnp.sum(e * e, axis=-1, keepdims=True), axis=0, keepdims=True)

    row = pl.BlockSpec((tm, N), lambda i: (i, 0))
    return pl.pallas_call(
        body, grid=(S // tm,), in_specs=[row, row],
        out_specs=[row, pl.BlockSpec((1, 1), lambda i: (0, 0))],
        out_shape=[jax.ShapeDtypeStruct((S, N), F32), jax.ShapeDtypeStruct((1, 1), F32)],
        name=name, compiler_params=_cp())(y, t)


def _conv_taps(a, w_ref, T):
    acc = jnp.zeros((T, CONV_W), F32)
    for j in range(CONV_K):
        sh = CONV_K - 1 - j
        r = a if sh == 0 else pltpu.roll(a, sh, 0)
        acc = acc + r[HALO:HALO + T] * w_ref[pl.ds(j, 1), :]
    return acc


def conv_fwd(proj, cw, cb, lg, lb, l, name, T=256):
    S = proj.shape[0]
    nt = S // T

    def body(p_ref, w_ref, cb_ref, lg_ref, lb_ref, y_ref, hc_s):
        hc_s[0:HALO, :] = jnp.zeros((HALO, CONV_W), F32)

        def fill(i, _):
            r0 = pl.multiple_of(i * T, T)
            pc = p_ref[pl.ds(r0, T), :]
            hc_s[pl.ds(r0 + HALO, T), :] = _glu(pc[:, :CONV_W], pc[:, CONV_W:])
            return 0

        lax.fori_loop(0, nt, fill, 0)

        def tile(i, _):
            r0 = pl.multiple_of(i * T, T)
            c = _conv_taps(hc_s[pl.ds(r0, T + HALO), :], w_ref.at[0], T) + cb_ref[0]
            y_ref[pl.ds(r0, T), :] = _ln_silu(c, lg_ref[0], lb_ref[0])
            return 0

        lax.fori_loop(0, nt, tile, 0)

    vec = pl.BlockSpec((1, 1, CONV_W), lambda i: (l, 0, 0))
    return pl.pallas_call(
        body, grid=(1,),
        in_specs=[pl.BlockSpec((S, 2 * CONV_W), lambda i: (0, 0)), pl.BlockSpec((1, 32, CONV_W), lambda i: (l, 0, 0)),
                  vec, vec, vec],
        out_specs=pl.BlockSpec((S, CONV_W), lambda i: (0, 0)),
        out_shape=jax.ShapeDtypeStruct((S, CONV_W), F32),
        scratch_shapes=[pltpu.VMEM((S + HALO, CONV_W), F32)], name=name, compiler_params=_cp())(proj, cw, cb, lg, lb)


def conv_bwd(proj, dy, cw, cb, lg, lb, l, name, T=256):
    S = proj.shape[0]
    nt = S // T

    def body(p_ref, dy_ref, w_ref, cb_ref, lg_ref, lb_ref, dp_ref, dw_ref, dcb_ref, dlg_ref, dlb_ref, hc_s, dc_s):
        hc_s[0:HALO, :] = jnp.zeros((HALO, CONV_W), F32)
        dc_s[S:S + HALO, :] = jnp.zeros((HALO, CONV_W), F32)
        dw_ref[...] = jnp.zeros_like(dw_ref)
        dcb_ref[...] = jnp.zeros_like(dcb_ref)
        dlg_ref[...] = jnp.zeros_like(dlg_ref)
        dlb_ref[...] = jnp.zeros_like(dlb_ref)

        def fill(i, _):
            r0 = pl.multiple_of(i * T, T)
            pc = p_ref[pl.ds(r0, T), :]
            hc_s[pl.ds(r0 + HALO, T), :] = _glu(pc[:, :CONV_W], pc[:, CONV_W:])
            return 0

        lax.fori_loop(0, nt, fill, 0)

        def tile(i, _):
            r0 = pl.multiple_of(i * T, T)
            a = hc_s[pl.ds(r0, T + HALO), :]
            c = _conv_taps(a, w_ref.at[0], T) + cb_ref[0]
            _, vjp = jax.vjp(_ln_silu, c, lg_ref[0], lb_ref[0])
            dc, dlg, dlb = vjp(dy_ref[pl.ds(r0, T), :])
            dc_s[pl.ds(r0, T), :] = dc
            dcb_ref[...] += jnp.sum(dc, axis=0, keepdims=True)
            dlg_ref[...] += dlg
            dlb_ref[...] += dlb
            for j in range(CONV_K):
                sh = CONV_K - 1 - j
                r = a if sh == 0 else pltpu.roll(a, sh, 0)
                dw_ref[pl.ds(j, 1), :] += jnp.sum(dc * r[HALO:HALO + T], axis=0, keepdims=True)
            return 0

        lax.fori_loop(0, nt, tile, 0)

        def back(i, _):
            r0 = pl.multiple_of(i * T, T)
            de = dc_s[pl.ds(r0, T + HALO), :]
            n = T + HALO
            dh = jnp.zeros((T, CONV_W), F32)
            for j in range(CONV_K):
                sh = CONV_K - 1 - j
                r = de if sh == 0 else pltpu.roll(de, n - sh, 0)
                dh = dh + r[0:T] * w_ref[0, pl.ds(j, 1), :]
            pc = p_ref[pl.ds(r0, T), :]
            _, vjp = jax.vjp(_glu, pc[:, :CONV_W], pc[:, CONV_W:])
            dval, dgate = vjp(dh)
            dp_ref[pl.ds(r0, T), :] = jnp.concatenate([dval, dgate], axis=-1).astype(BF16)
            return 0

        lax.fori_loop(0, nt, back, 0)

    vec = pl.BlockSpec((1, 1, CONV_W), lambda i: (l, 0, 0))
    ovec = pl.BlockSpec((1, CONV_W), lambda i: (0, 0))
    vsh = jax.ShapeDtypeStruct((1, CONV_W), F32)
    return pl.pallas_call(
        body, grid=(1,),
        in_specs=[pl.BlockSpec((S, 2 * CONV_W), lambda i: (0, 0)), pl.BlockSpec((S, CONV_W), lambda i: (0, 0)),
                  pl.BlockSpec((1, 32, CONV_W), lambda i: (l, 0, 0)), vec, vec, vec],
        out_specs=[pl.BlockSpec((S, 2 * CONV_W), lambda i: (0, 0)), pl.BlockSpec((32, CONV_W), lambda i: (0, 0)),
                   ovec, ovec, ovec],
        out_shape=[jax.ShapeDtypeStruct((S, 2 * CONV_W), BF16), jax.ShapeDtypeStruct((32, CONV_W), F32), vsh, vsh, vsh],
        scratch_shapes=[pltpu.VMEM((S + HALO, CONV_W), F32), pltpu.VMEM((S + HALO, CONV_W), F32)],
        name=name, compiler_params=_cp())(proj, dy, cw, cb, lg, lb)


def _sg_mix(wt_ref, v, bt):
    slabs = []
    for s in range(2):
        vs = v[:, 128 * s:128 * (s + 1)]
        seg = _iota(vs.shape, 1) < HEAD_DIM
        p0 = _dot(wt_ref[2 * s], vs, "nn") + bt[:, 2 * s:2 * s + 1]
        p1 = _dot(wt_ref[2 * s + 1], vs, "nn") + bt[:, 2 * s + 1:2 * s + 2]
        slabs.append(jnp.where(seg, p0, p1))
    return jnp.concatenate(slabs, axis=-1)


def sg_fwd(proj, lg, lb, wt, bt, l, name):
    S = proj.shape[0]

    def body(p_ref, lg_ref, lb_ref, w_ref, b_ref, y_ref):
        u, v = _sg_pre(p_ref[...], lg_ref[0], lb_ref[0])
        y_ref[...] = u * _sg_mix(w_ref.at[0], v, b_ref[0])

    vec = pl.BlockSpec((1, 1, SG_W), lambda i: (l, 0, 0))
    return pl.pallas_call(
        body, grid=(S // CHUNK,),
        in_specs=[pl.BlockSpec((CHUNK, 2 * SG_W), lambda i: (i, 1)), vec, vec,
                  pl.BlockSpec((1, 4, CHUNK, CHUNK), lambda i: (l, 0, 0, 0)),
                  pl.BlockSpec((1, CHUNK, 4), lambda i: (l, 0, 0))],
        out_specs=pl.BlockSpec((CHUNK, SG_W), lambda i: (i, 0)),
        out_shape=jax.ShapeDtypeStruct((S, SG_W), F32), name=name, compiler_params=_cp())(proj, lg, lb, wt, bt)


def sg_bwd(proj, dy, lg, lb, wt, wtt, bt, l, name):
    S = proj.shape[0]

    def body(p_ref, dy_ref, lg_ref, lb_ref, w_ref, wt_ref, b_ref, dp_ref, dlg_ref, dlb_ref, dw_ref, db_ref):
        @pl.when(pl.program_id(0) == 0)
        def _():
            dlg_ref[...] = jnp.zeros_like(dlg_ref)
            dlb_ref[...] = jnp.zeros_like(dlb_ref)
            dw_ref[...] = jnp.zeros_like(dw_ref)
            db_ref[...] = jnp.zeros_like(db_ref)

        (u, v), vjp = jax.vjp(_sg_pre, p_ref[...], lg_ref[0], lb_ref[0])
        dy = dy_ref[...]
        mixed = _sg_mix(w_ref.at[0], v, b_ref[0])
        du = dy * mixed
        dm = dy * u
        tril = _iota((CHUNK, CHUNK), 1) <= _iota((CHUNK, CHUNK), 0)
        dvs = []
        for s in range(2):
            dms = dm[:, 128 * s:128 * (s + 1)]
            vs = v[:, 128 * s:128 * (s + 1)]
            seg = _iota(dms.shape, 1) < HEAD_DIM
            halves = (jnp.where(seg, dms, 0.0), jnp.where(seg, 0.0, dms))
            dv_h = []
            for e in range(2):
                h = 2 * s + e
                db_ref[:, h:h + 1] += jnp.sum(halves[e], axis=-1, keepdims=True)
                dw_ref[h] += jnp.where(tril, _dot(halves[e], vs, "nt"), 0.0)
                dv_h.append(_dot(wt_ref[0, h], halves[e], "nn"))
            dvs.append(dv_h[0] + dv_h[1])
        dp, dlg, dlb = vjp((du, jnp.concatenate(dvs, axis=-1)))
        dp_ref[...] = dp.astype(BF16)
        dlg_ref[...] += dlg
        dlb_ref[...] += dlb

    vec = pl.BlockSpec((1, 1, SG_W), lambda i: (l, 0, 0))
    wsp = pl.BlockSpec((1, 4, CHUNK, CHUNK), lambda i: (l, 0, 0, 0))
    ovec = pl.BlockSpec((1, SG_W), lambda i: (0, 0))
    return pl.pallas_call(
        body, grid=(S // CHUNK,),
        in_specs=[pl.BlockSpec((CHUNK, 2 * SG_W), lambda i: (i, 1)), pl.BlockSpec((CHUNK, SG_W), lambda i: (i, 0)),
                  vec, vec, wsp, wsp, pl.BlockSpec((1, CHUNK, 4), lambda i: (l, 0, 0))],
        out_specs=[pl.BlockSpec((CHUNK, 2 * SG_W), lambda i: (i, 0)), ovec, ovec,
                   pl.BlockSpec((4, CHUNK, CHUNK), lambda i: (0, 0, 0)), pl.BlockSpec((CHUNK, 4), lambda i: (0, 0))],
        out_shape=[jax.ShapeDtypeStruct((S, 2 * SG_W), BF16), jax.ShapeDtypeStruct((1, SG_W), F32),
                   jax.ShapeDtypeStruct((1, SG_W), F32), jax.ShapeDtypeStruct((4, CHUNK, CHUNK), F32),
                   jax.ShapeDtypeStruct((CHUNK, 4), F32)],
        name=name, compiler_params=_cp())(proj, dy, lg, lb, wt, wtt, bt)


def qkv_fwd(proj, gq, gk, l, name, tm=512):
    S = proj.shape[0]

    def body(pq_ref, pk_ref, pv_ref, gq_ref, gk_ref, o_ref):
        o_ref[...] = _qkv_fn(pq_ref[...], pk_ref[...], pv_ref[...], gq_ref[0], gk_ref[0]).astype(BF16)

    vec = pl.BlockSpec((1, 1, 128), lambda i: (l, 0, 0))
    qb = OFF_SB // SB_W
    return pl.pallas_call(
        body, grid=(S // tm,),
        in_specs=[pl.BlockSpec((tm, SB_W), lambda i: (i, qb)), pl.BlockSpec((tm, SB_W), lambda i: (i, qb + 1)),
                  pl.BlockSpec((tm, SB_W), lambda i: (i, qb + 2)), vec, vec],
        out_specs=pl.BlockSpec((tm, 3 * SB_W), lambda i: (i, 0)),
        out_shape=jax.ShapeDtypeStruct((S, 3 * SB_W), BF16), name=name, compiler_params=_cp())(proj, proj, proj, gq, gk)


def qkv_bwd(proj, dq, dk, dv, gq, gk, dpc, dps, l, name, tm=512):
    S = proj.shape[0]

    def body(pq_ref, pk_ref, pv_ref, dq_ref, dk_ref, dv_ref, gq_ref, gk_ref, dpc_ref, dps_ref, dp_ref, dgq_ref,
             dgk_ref):
        _, vjp = jax.vjp(_qkv_fn, pq_ref[...], pk_ref[...], pv_ref[...], gq_ref[0], gk_ref[0])
        dpq, dpk, dpv, dgq, dgk = vjp(jnp.concatenate([dq_ref[...], dk_ref[...], dv_ref[...]], axis=-1))
        dp_ref[:, :OFF_SG] = dpc_ref[...]
        dp_ref[:, OFF_SG:OFF_SB] = dps_ref[...]
        dp_ref[:, OFF_SB:] = jnp.concatenate([dpq, dpk, dpv], axis=-1).astype(BF16)

        @pl.when(pl.program_id(0) == 0)
        def _():
            dgq_ref[...] = jnp.zeros_like(dgq_ref)
            dgk_ref[...] = jnp.zeros_like(dgk_ref)

        dgq_ref[...] += dgq
        dgk_ref[...] += dgk

    vec = pl.BlockSpec((1, 1, 128), lambda i: (l, 0, 0))
    part = pl.BlockSpec((tm, SB_W), lambda i: (i, 0))
    ovec = pl.BlockSpec((1, 128), lambda i: (0, 0))
    qb = OFF_SB // SB_W
    return pl.pallas_call(
        body, grid=(S // tm,),
        in_specs=[pl.BlockSpec((tm, SB_W), lambda i: (i, qb)), pl.BlockSpec((tm, SB_W), lambda i: (i, qb + 1)),
                  pl.BlockSpec((tm, SB_W), lambda i: (i, qb + 2)), part, part, part, vec, vec, part, part],
        out_specs=[pl.BlockSpec((tm, IN_W), lambda i: (i, 0)), ovec, ovec],
        out_shape=[jax.ShapeDtypeStruct((S, IN_W), BF16), jax.ShapeDtypeStruct((1, 128), F32),
                   jax.ShapeDtypeStruct((1, 128), F32)],
        name=name, compiler_params=_cp())(proj, proj, proj, dq, dk, dv, gq, gk, dpc, dps)


QSUB_FWD = 8
QSUB_BWD = 8


def _sb_consts():
    row = _iota((CHUNK, CHUNK), 0)
    col = _iota((CHUNK, CHUNK), 1)
    ones = jnp.ones((CHUNK, CHUNK), MXU_DT)
    m_gt = jnp.concatenate([(row > col).astype(MXU_DT), ones], axis=1)
    m_lt = jnp.concatenate([(row < col).astype(MXU_DT), ones], axis=1)
    return (col < HEAD_DIM, col - row, jnp.concatenate([m_gt, m_gt], axis=0), jnp.concatenate([m_lt, m_lt], axis=0))


def _split_heads(x, seg):
    z = jnp.zeros_like(x)
    return (jnp.where(seg, x, z), jnp.where(seg, z, x))


def _stack_heads(x, seg):
    return jnp.concatenate(_split_heads(x, seg), axis=0)


def _cumdot2(x, m2):
    hi = x.astype(MXU_DT)
    lo = (x - hi.astype(F32)).astype(MXU_DT)
    dn = (((1,), (0,)), ((), ()))
    return [lax.dot_general(jnp.concatenate([hi[:, CHUNK * h:CHUNK * (h + 1)], lo[:, CHUNK * h:CHUNK * (h + 1)]], axis=1),
                            m2, dn, preferred_element_type=F32) for h in range(2)]


class Exchange:
    def __init__(self, ins, bufs, outs, n, build):
        self.ins, self.bufs, self.outs, self.n, self.build = list(ins), list(bufs), list(outs), n, build

    def split(self, refs, n_in, n_out):
        a, b, o = len(self.ins), len(self.bufs), len(self.outs)
        main_in = refs[:n_in]
        c_in = refs[n_in:n_in + a]
        rest = refs[n_in + a + b:]
        main_out = rest[:n_out]
        c_buf = rest[n_out:n_out + b]
        c_out = rest[n_out + b:n_out + b + o]
        send, recv = rest[n_out + b + o:]
        return main_in, main_out, lambda: self.build(c_in, c_buf, c_out, send, recv)


def _with_exchange(ex, n_in, n_out, in_specs, out_specs, out_shape):
    if ex is None:
        return in_specs, out_specs, out_shape, {}, [], []
    a, b = len(ex.ins), len(ex.bufs)
    in_specs = list(in_specs) + [ANY] * (a + b)
    out_specs = list(out_specs) + [ANY] * (b + len(ex.outs))
    out_shape = list(out_shape) + [jax.ShapeDtypeStruct(x.shape, x.dtype) for x in ex.bufs] + list(ex.outs)
    aliases = {n_in + a + i: n_out + i for i in range(b)}
    scratch = [pltpu.SemaphoreType.DMA((ex.n,)), pltpu.SemaphoreType.DMA((ex.n,))]
    return in_specs, out_specs, out_shape, aliases, scratch, ex.ins + ex.bufs


def attn_fwd(qkv, name, ex=None):
    S = qkv.shape[0]
    npair = SB_W // 128
    QSUB = QSUB_FWD
    QT = QSUB * CHUNK
    nq = S // QT

    def body(*refs):
        if ex is None:
            (q_ref, k_ref, v_ref), (o_ref, rt_ref), copies = refs[:3], refs[3:5], None
        else:
            (q_ref, k_ref, v_ref), (o_ref, rt_ref), copies = ex.split(refs, 3, 2)

            @pl.when(jnp.logical_and(pl.program_id(0) == 0, pl.program_id(1) == 0))
            def _():
                for cp in copies():
                    cp.start()

        qi = pl.program_id(1)
        seg, dcol, m_gt, _ = _sb_consts()
        qs = [q_ref[a * CHUNK:(a + 1) * CHUNK, :] for a in range(QSUB)]
        causal2 = jnp.concatenate([dcol < 0, dcol < 0], axis=1)

        def step(kb, carry, diag):
            off = pl.multiple_of(kb * CHUNK, CHUNK)
            kk = _stack_heads(k_ref[pl.ds(off, CHUNK), :], seg)
            vv = _stack_heads(v_ref[pl.ds(off, CHUNK), :], seg)
            act = [a for a in range(QSUB) if diag is None or a >= diag]
            z = [_dot(qs[a], kk, "nt") for a in act]
            sp = [_softplus(x) for x in z]
            lnb = [jnp.where(causal2, -s, 0.0) if a == diag else -s for a, s in zip(act, sp)]
            cs = [_cumdot2(x, m_gt) for x in lnb]
            att = []
            for n, a in enumerate(act):
                base = z[n] - sp[n]
                e = jnp.exp(base + jnp.concatenate([cs[n][h][:, :CHUNK] + carry[3 * a + 1 + h] for h in range(2)], axis=1))
                att.append(jnp.where(causal2, e, 0.0) if a == diag else e)
            pv = [_dot(x, vv, "nn") for x in att]
            new = list(carry)
            for n, a in enumerate(act):
                new[3 * a] = carry[3 * a] + pv[n]
                for h in range(2):
                    new[3 * a + 1 + h] = carry[3 * a + 1 + h] + cs[n][h][:, CHUNK:]
            return tuple(new)

        z0 = jnp.zeros((CHUNK, CHUNK), F32)
        res = (z0,) * (3 * QSUB)
        for j in reversed(range(QSUB)):
            res = step(QSUB * qi + j, res, j)
        res = lax.fori_loop(0, QSUB * qi, lambda it, c: step(QSUB * qi - 1 - it, c, None), res)
        for a in range(QSUB):
            rows = slice(a * CHUNK, (a + 1) * CHUNK)
            o_ref[rows, :] = res[3 * a]
            rt_ref[rows, :] = jnp.concatenate([res[3 * a + 1], res[3 * a + 2]], axis=1)

        if ex is not None:
            @pl.when(jnp.logical_and(pl.program_id(0) == npair - 1, pl.program_id(1) == nq - 1))
            def _():
                for cp in copies():
                    cp.wait()

    in_specs, out_specs, out_shape, aliases, scratch, extra = _with_exchange(
        ex, 3, 2,
        [pl.BlockSpec((QT, 128), lambda p, i: (i, p)), pl.BlockSpec((S, 128), lambda p, i: (0, npair + p)),
         pl.BlockSpec((S, 128), lambda p, i: (0, 2 * npair + p))],
        [pl.BlockSpec((QT, 128), lambda p, i: (i, p)), pl.BlockSpec((QT, 256), lambda p, i: (i, p))],
        [jax.ShapeDtypeStruct((S, SB_W), F32), jax.ShapeDtypeStruct((S, 2 * SB_W), F32)])
    return pl.pallas_call(
        body, grid=(npair, nq), in_specs=in_specs, out_specs=out_specs, out_shape=out_shape,
        input_output_aliases=aliases, scratch_shapes=scratch, name=name, compiler_params=_cp())(qkv, qkv, qkv, *extra)


def attn_bwd(qkv, rt, do, name, ex=None):
    S = qkv.shape[0]
    npair = SB_W // 128
    QSUB = QSUB_BWD
    QT = QSUB * CHUNK
    nq = S // QT

    def body(*refs):
        if ex is None:
            (q_ref, k_ref, v_ref, rt_ref, do_ref), (dq_ref, dk_ref, dv_ref), copies = refs[:5], refs[5:8], None
        else:
            (q_ref, k_ref, v_ref, rt_ref, do_ref), (dq_ref, dk_ref, dv_ref), copies = ex.split(refs, 5, 3)

            @pl.when(jnp.logical_and(pl.program_id(0) == 0, pl.program_id(1) == 0))
            def _():
                for cp in copies():
                    cp.start()

        qi = pl.program_id(1)

        @pl.when(qi == 0)
        def _():
            dk_ref[...] = jnp.zeros_like(dk_ref)
            dv_ref[...] = jnp.zeros_like(dv_ref)

        seg, dcol, m_gt, m_lt = _sb_consts()
        qs, dos, qq, dd, rtot = [], [], [], [], []
        for a in range(QSUB):
            rows = slice(a * CHUNK, (a + 1) * CHUNK)
            qs.append(q_ref[rows, :])
            dos.append(do_ref[rows, :].astype(MXU_DT))
            qq.append(_stack_heads(qs[a], seg))
            dd.append(_stack_heads(dos[a], seg))
            rtot.append(rt_ref[rows, :])
        causal2 = jnp.concatenate([dcol < 0, dcol < 0], axis=1)

        def step(kb, carry, diag):
            off = pl.multiple_of(kb * CHUNK, CHUNK)
            kk = _stack_heads(k_ref[pl.ds(off, CHUNK), :], seg)
            vv = _stack_heads(v_ref[pl.ds(off, CHUNK), :], seg)
            act = [a for a in range(QSUB) if diag is None or a >= diag]
            z = [_dot(qs[a], kk, "nt") for a in act]
            da = [_dot(dos[a], vv, "nt") for a in act]
            sp = [_softplus(x) for x in z]
            lnb = [jnp.where(causal2, -s, 0.0) if a == diag else -s for a, s in zip(act, sp)]
            cs = [_cumdot2(x, m_gt) for x in lnb]
            lc = [[carry[5 * a + 1 + h] + cs[n][h][:, CHUNK:] for h in range(2)] for n, a in enumerate(act)]
            att = []
            for n, a in enumerate(act):
                btw = jnp.concatenate([cs[n][h][:, :CHUNK] - lc[n][h] for h in range(2)], axis=1)
                e = jnp.exp(z[n] - sp[n] + btw + rtot[a])
                att.append(jnp.where(causal2, e, 0.0) if a == diag else e)
            g = [da[n] * att[n] for n in range(len(act))]
            cg = [_cumdot2(x, m_lt) for x in g]
            dz = []
            for n, a in enumerate(act):
                sig = jnp.exp(z[n] - sp[n])
                pre = jnp.concatenate([carry[5 * a + 3 + h] + cg[n][h][:, :CHUNK] for h in range(2)], axis=1)
                d = g[n] * (1.0 - sig) - pre * sig
                dz.append((jnp.where(causal2, d, 0.0) if a == diag else d).astype(MXU_DT))
            attb = [x.astype(MXU_DT) for x in att]
            dqc = [_dot(x, kk, "nn") for x in dz]
            dkc = [_dot(jnp.concatenate([dz[n][:, :CHUNK], dz[n][:, CHUNK:]], axis=0), qq[a], "tn")
                   for n, a in enumerate(act)]
            dvc = [_dot(jnp.concatenate([attb[n][:, :CHUNK], attb[n][:, CHUNK:]], axis=0), dd[a], "tn")
                   for n, a in enumerate(act)]
            dk_ref[pl.ds(off, CHUNK), :] += functools.reduce(lambda x, y: x + y, dkc)
            dv_ref[pl.ds(off, CHUNK), :] += functools.reduce(lambda x, y: x + y, dvc)
            new = list(carry)
            for n, a in enumerate(act):
                new[5 * a] = carry[5 * a] + dqc[n]
                for h in range(2):
                    new[5 * a + 1 + h] = lc[n][h]
                    new[5 * a + 3 + h] = carry[5 * a + 3 + h] + cg[n][h][:, CHUNK:]
            return tuple(new)

        z0 = jnp.zeros((CHUNK, 128), F32)
        res = lax.fori_loop(0, QSUB * qi, lambda kb, c: step(kb, c, None), (z0,) * (5 * QSUB))
        for j in range(QSUB):
            res = step(QSUB * qi + j, res, j)
        for a in range(QSUB):
            dq_ref[a * CHUNK:(a + 1) * CHUNK, :] = res[5 * a]

        if ex is not None:
            @pl.when(jnp.logical_and(pl.program_id(0) == npair - 1, pl.program_id(1) == nq - 1))
            def _():
                for cp in copies():
                    cp.wait()

    blk = pl.BlockSpec((QT, 128), lambda p, i: (i, p))
    full = pl.BlockSpec((S, 128), lambda p, i: (0, p))
    sh = jax.ShapeDtypeStruct((S, SB_W), F32)
    in_specs, out_specs, out_shape, aliases, scratch, extra = _with_exchange(
        ex, 5, 3,
        [blk, pl.BlockSpec((S, 128), lambda p, i: (0, npair + p)), pl.BlockSpec((S, 128), lambda p, i: (0, 2 * npair + p)),
         pl.BlockSpec((QT, 256), lambda p, i: (i, p)), blk],
        [blk, full, full], [sh, sh, sh])
    return pl.pallas_call(
        body, grid=(npair, nq), in_specs=in_specs, out_specs=out_specs, out_shape=out_shape,
        input_output_aliases=aliases, scratch_shapes=scratch, name=name,
        compiler_params=_cp())(qkv, qkv, qkv, rt, do, *extra)


def _adamw_math(w, g, m, v):
    m = ADAM_B1 * m + (1.0 - ADAM_B1) * g
    v = ADAM_B2 * v + (1.0 - ADAM_B2) * (g * g)
    m_hat = m / BC1
    v_hat = v / BC2
    delta = -ADAM_LR * (m_hat / (jnp.sqrt(v_hat) + ADAM_EPS) + ADAM_WD * w)
    return delta, m, v


def adamw_layer(w4, m4, v4, g, outs, l, name, tr):
    L, R, C = w4.shape
    n_alias = 0 if outs is None else 4

    def body(*refs):
        w_ref, m_ref, v_ref, g_ref = refs[:4]
        go_ref, d_ref, mo_ref, vo_ref = refs[4 + n_alias:]
        g = g_ref[...]
        d, m, v = _adamw_math(w_ref[0], g, m_ref[0], v_ref[0])
        go_ref[0] = g
        d_ref[0] = d
        mo_ref[0] = m
        vo_ref[0] = v

    st = pl.BlockSpec((1, tr, C), lambda i: (l, i, 0))
    sh = jax.ShapeDtypeStruct((L, R, C), F32)
    return pl.pallas_call(
        body, grid=(R // tr,),
        in_specs=[st, st, st, pl.BlockSpec((tr, C), lambda i: (i, 0))] + [ANY] * n_alias,
        out_specs=[st, st, st, st], out_shape=[sh, sh, sh, sh],
        input_output_aliases={4 + i: i for i in range(n_alias)}, name=name,
        compiler_params=_cp())(w4, m4, v4, g, *(outs or ()))


def adamw_flat(w, m, v, g, name):
    R, C = w.shape

    def body(w_ref, m_ref, v_ref, g_ref, d_ref, mo_ref, vo_ref):
        d, m2, v2 = _adamw_math(w_ref[...], g_ref[...], m_ref[...], v_ref[...])
        d_ref[...] = d
        mo_ref[...] = m2
        vo_ref[...] = v2

    full = pl.BlockSpec((R, C), lambda i: (0, 0))
    sh = jax.ShapeDtypeStruct((R, C), F32)
    return pl.pallas_call(body, grid=(1,), in_specs=[full] * 4, out_specs=[full] * 3, out_shape=[sh] * 3,
                          name=name, compiler_params=_cp())(w, m, v, g)


def add_halves(gl, l1s, c, name):
    n = len(gl)
    nk = gl[0].shape[0]

    def body(c_ref, *refs):
        for g_ref, l_ref, o_ref in zip(refs[:n], refs[n:2 * n], refs[2 * n:]):
            o_ref[...] = (g_ref[...].astype(F32) + l_ref[...].astype(F32)).astype(o_ref.dtype)

    def half(g, own):
        blk = (1, g.shape[1] // 2, g.shape[2])
        return pl.BlockSpec(blk, (lambda k, c_ref: (k, c_ref[0], 0)) if own else (lambda k, c_ref: (k, 0, 0)))

    gs = pltpu.PrefetchScalarGridSpec(
        num_scalar_prefetch=1, grid=(nk,),
        in_specs=[half(g, True) for g in gl] + [half(g, False) for g in gl],
        out_specs=[half(g, False) for g in gl])
    return pl.pallas_call(
        body, grid_spec=gs,
        out_shape=[jax.ShapeDtypeStruct((nk, g.shape[1] // 2, g.shape[2]), g.dtype) for g in gl],
        name=name, compiler_params=_cp())(c, *gl, *l1s)


def sum_chips_all(ps, l2s, kc, name):
    n = len(ps)

    def body(k_ref, *refs):
        for i in range(n):
            p_ref, (a_ref, b_ref, c_ref), o_ref = refs[i], refs[n + 3 * i:n + 3 * i + 3], refs[4 * n + i]
            o_ref[0] = ((p_ref[0].astype(F32) + a_ref[0].astype(F32)) + b_ref[0].astype(F32)) + c_ref[0].astype(F32)

    def blk(p):
        return (1,) + p.shape[1:]

    gs = pltpu.PrefetchScalarGridSpec(
        num_scalar_prefetch=1, grid=(1,),
        in_specs=[pl.BlockSpec(blk(p), lambda t, k_ref: (k_ref[0], 0, 0)) for p in ps]
        + [pl.BlockSpec(blk(p), functools.partial(lambda t, k_ref, j: (j, 0, 0), j=j)) for p in ps for j in range(3)],
        out_specs=[pl.BlockSpec(blk(p), lambda t, k_ref: (k_ref[1], 0, 0)) for p in ps])
    return pl.pallas_call(
        body, grid_spec=gs, out_shape=[jax.ShapeDtypeStruct((2,) + p.shape[1:], F32) for p in ps],
        name=name, compiler_params=_cp())(kc, *ps, *[l2 for l2 in l2s for _ in range(3)])


def place_slot(a3, l, idx, nslot, dtype, name, tr):
    _, R, C = a3.shape

    def body(i_ref, a_ref, o_ref):
        o_ref[0] = a_ref[0].astype(dtype)

    gs = pltpu.PrefetchScalarGridSpec(
        num_scalar_prefetch=1, grid=(R // tr,),
        in_specs=[pl.BlockSpec((1, tr, C), lambda t, i_ref: (l, t, 0))],
        out_specs=pl.BlockSpec((1, tr, C), lambda t, i_ref: (i_ref[0], t, 0)))
    return pl.pallas_call(body, grid_spec=gs, out_shape=jax.ShapeDtypeStruct((nslot, R, C), dtype), name=name,
                          compiler_params=_cp())(idx, a3)


def _place():
    x, y, c = lax.axis_index("x"), lax.axis_index("y"), lax.axis_index("c")
    chips = [(1 - x, y), (x, 1 - y), (1 - x, 1 - y)]
    return x, y, c, chips


def _rcopy(src, dst, send_sems, recv_sems, k, to):
    return pltpu.make_async_remote_copy(src_ref=src, dst_ref=dst, send_sem=send_sems.at[k], recv_sem=recv_sems.at[k],
                                        device_id=to, device_id_type=MESH)


def gather_exchange(bufs):
    def build(in_refs, buf_refs, out_refs, send_sems, recv_sems):
        x, y, c, chips = _place()
        cps = []
        for i, ref in enumerate(buf_refs):
            rh = bufs[i].shape[1] // 2
            mine = ref.at[2 * x + y].at[pl.ds(c * rh, rh), :]
            for j, chip in enumerate(chips):
                cps.append(_rcopy(mine, mine, send_sems, recv_sems, 3 * i + j, (*chip, c)))
        return cps

    return Exchange([], bufs, [], 3 * len(bufs), build)


def scatter_exchange(ps):
    def build(in_refs, buf_refs, out_refs, send_sems, recv_sems):
        x, y, c, chips = _place()
        return [_rcopy(ref.at[2 * chip[0] + chip[1]], out_refs[i].at[j], send_sems, recv_sems, 3 * i + j, (*chip, c))
                for i, ref in enumerate(in_refs) for j, chip in enumerate(chips)]

    outs = [jax.ShapeDtypeStruct((3,) + p.shape[1:], p.dtype) for p in ps]
    return Exchange(ps, [], outs, 3 * len(ps), build)


def run_exchange(ex, name):
    def body(*refs):
        _, _, copies = ex.split(refs, 0, 0)
        cps = copies()
        for cp in cps:
            cp.start()
        for cp in cps:
            cp.wait()

    in_specs, out_specs, out_shape, aliases, scratch, extra = _with_exchange(ex, 0, 0, [], [], [])
    return pl.pallas_call(body, in_specs=in_specs, out_specs=out_specs, out_shape=out_shape,
                          input_output_aliases=aliases, scratch_shapes=scratch, name=name)(*extra)


def gather_forward(bufs, name):
    n = len(bufs)

    def body(*refs):
        outs = refs[n:2 * n]
        send_sems, recv_sems = refs[2 * n:]
        x, y, c, chips = _place()
        cps = []
        for i in range(n):
            rh = bufs[i].shape[1] // 2
            for j, chip in enumerate(chips):
                landed = outs[i].at[2 * chip[0] + chip[1]].at[pl.ds(c * rh, rh), :]
                cps.append(_rcopy(landed, landed, send_sems, recv_sems, 3 * i + j, (x, y, 1 - c)))
        for cp in cps:
            cp.start()
        for cp in cps:
            cp.wait()

    return pl.pallas_call(
        body, in_specs=[ANY] * n, out_specs=[ANY] * n,
        out_shape=[jax.ShapeDtypeStruct(b.shape, b.dtype) for b in bufs],
        input_output_aliases={i: i for i in range(n)},
        scratch_shapes=[pltpu.SemaphoreType.DMA((3 * n,)), pltpu.SemaphoreType.DMA((3 * n,))], name=name)(*bufs)


def exchange_sibling_half(gs, name):
    n = len(gs)

    def body(*refs):
        ins, outs = refs[:n], refs[n:2 * n]
        send_sems, recv_sems = refs[2 * n:]
        x, y, c, _ = _place()
        cps = []
        for i in range(n):
            rh = gs[i].shape[1] // 2
            cp = pltpu.make_async_remote_copy(
                src_ref=ins[i].at[:, pl.ds((1 - c) * rh, rh), :], dst_ref=outs[i], send_sem=send_sems.at[i],
                recv_sem=recv_sems.at[i], device_id=(x, y, 1 - c), device_id_type=MESH)
            cp.start()
            cps.append(cp)
        for cp in cps:
            cp.wait()

    return pl.pallas_call(
        body, in_specs=[ANY] * n, out_specs=[ANY] * n,
        out_shape=[jax.ShapeDtypeStruct((g.shape[0], g.shape[1] // 2, g.shape[2]), g.dtype) for g in gs],
        scratch_shapes=[pltpu.SemaphoreType.DMA((n,)), pltpu.SemaphoreType.DMA((n,))],
        name=name)(*gs)


def join_sibling_halves(fs, name):
    n = len(fs)

    def body(*refs):
        outs = refs[n:2 * n]
        send_sems, recv_sems = refs[2 * n:]
        x, y, c, _ = _place()
        cps = [_rcopy(outs[i].at[c], outs[i].at[c], send_sems, recv_sems, i, (x, y, 1 - c)) for i in range(n)]
        for cp in cps:
            cp.start()
        for cp in cps:
            cp.wait()

    return pl.pallas_call(
        body, in_specs=[ANY] * n, out_specs=[ANY] * n,
        out_shape=[jax.ShapeDtypeStruct(f.shape, f.dtype) for f in fs],
        input_output_aliases={i: i for i in range(n)},
        scratch_shapes=[pltpu.SemaphoreType.DMA((n,)), pltpu.SemaphoreType.DMA((n,))], name=name)(*fs)


def all_reduce_small(v, name):
    R, C = v.shape

    def body(v_ref, sum_ref, all_ref, send_sems, recv_sems, local_sem):
        x, y, c, chips = _place()
        me, sib = (x, y, c), (x, y, 1 - c)

        def slab(px, py, pc):
            return all_ref.at[4 * px + 2 * py + pc]

        def copy(k, block, to, src=None):
            return pltpu.make_async_remote_copy(
                src_ref=slab(*block) if src is None else src, dst_ref=slab(*block), send_sem=send_sems.at[k],
                recv_sem=recv_sems.at[k], device_id=to, device_id_type=MESH)

        mine = pltpu.make_async_copy(v_ref, slab(*me), local_sem)
        mine.start()
        first = [copy(0, me, sib, src=v_ref)] + [copy(1 + j, me, (*chip, c), src=v_ref) for j, chip in enumerate(chips)]
        for cp in first:
            cp.start()
        passed = [copy(4 + j, (*chip, c), sib) for j, chip in enumerate(chips)]
        for j, chip in enumerate(chips):
            copy(1 + j, (*chip, c), me).wait_recv()
            passed[j].start()
        copy(0, sib, me).wait_recv()
        for j, chip in enumerate(chips):
            copy(4 + j, (*chip, 1 - c), me).wait_recv()
        for cp in first + passed:
            cp.wait_send()
        mine.wait()
        acc = all_ref[0]
        for d in range(1, 8):
            acc = acc + all_ref[d]
        sum_ref[...] = acc

    vm = pl.BlockSpec(memory_space=pltpu.VMEM)
    return pl.pallas_call(
        body, in_specs=[vm], out_specs=[vm, vm],
        out_shape=[jax.ShapeDtypeStruct((R, C), F32), jax.ShapeDtypeStruct((8, R, C), F32)],
        scratch_shapes=[pltpu.SemaphoreType.DMA((7,)), pltpu.SemaphoreType.DMA((7,)), pltpu.SemaphoreType.DMA],
        name=name, compiler_params=_cp())(v)[0]


SMALL = ("mix_norm_g", "conv_w", "conv_b", "conv_ln_g", "conv_ln_b", "sg_ln_g", "sg_ln_b", "sg_w", "sg_b",
         "q_norm_g", "k_norm_g", "out_norm_g", "ffn_norm_g")
BIG = ("w_in", "w_out", "w_gate_up", "w_down")
WEIGHTS = ("mix_norm_g", "w_in", "conv_w", "conv_b", "conv_ln_g", "conv_ln_b", "sg_ln_g", "sg_ln_b", "sg_w", "sg_b",
           "q_norm_g", "k_norm_g", "out_norm_g", "w_out", "ffn_norm_g", "w_gate_up", "w_down")
ADAM_ROWS = {"w_in": 512, "w_out": 256, "w_gate_up": 256, "w_down": 352}
AG_ROWS = {"w_in": 512, "w_out": 256, "w_gate_up": 512, "w_down": 352}


def _pack(parts):
    flat = jnp.concatenate([p.reshape(-1) for p in parts])
    n = flat.shape[0]
    rows = -(-n // (8 * 128)) * 8
    return jnp.pad(flat, (0, rows * 128 - n)).reshape(rows, 128)


def _unpack(buf, shapes):
    flat = buf.reshape(-1)
    out, off = [], 0
    for s in shapes:
        n = 1
        for d in s:
            n *= d
        out.append(flat[off:off + n].reshape(s))
        off += n
    return out


def layer_forward(x, l, P, W, ex=None, after=None, ex2=None, after2=None):
    sv = {"x": x}
    sv["h"] = rms_fwd(x, P["mix_norm_g"], l, f"rms_mix_{l}")
    sv["proj"] = mm_colblk(sv["h"], W["w_in"], F32, f"mm_in_{l}")
    sv["yc"] = conv_fwd(sv["proj"], P["conv_w"], P["conv_b"], P["conv_ln_g"], P["conv_ln_b"], l, f"conv_fwd_{l}")
    sv["ys"] = sg_fwd(sv["proj"], P["sg_ln_g"], P["sg_ln_b"], P["sg_wt"], P["sg_bt"], l, f"sg_fwd_{l}")
    sv["qkv"] = qkv_fwd(sv["proj"], P["q_norm_g"], P["k_norm_g"], l, f"qkv_fwd_{l}")
    sv["yb"], sv["rt"], *moved = attn_fwd(sv["qkv"], f"attn_fwd_{l}", ex)
    if after is not None:
        W = after(moved)
    sv["yn"] = outnorm_fwd(sv["yc"], sv["ys"], sv["yb"], P["out_norm_g"], l, f"outnorm_fwd_{l}")
    sv["x1"] = mm_res(sv["yn"], W["w_out"].reshape(D_MODEL, D_MODEL), x, f"mm_out_{l}")
    sv["h2"] = rms_fwd(sv["x1"], P["ffn_norm_g"], l, f"rms_ffn_{l}")
    sv["g"], sv["u"], sv["act"], *moved2 = ffn_up(sv["h2"], W["w_gate_up"], f"ffn_up_{l}", ex=ex2)
    if after2 is not None:
        W = after2(moved2)
    x2 = mm_res(sv["act"], W["w_down"].reshape(FFN, D_MODEL), sv["x1"], f"mm_down_{l}")
    return x2, sv, moved


def layer_backward(dx2, l, P, W, sv, mid=None):
    gb, gs = {}, {}
    wdown = W["w_down"].reshape(FFN, D_MODEL)
    dgu = ffn_down_bwd(dx2, wdown, sv["g"], sv["u"], f"ffn_down_bwd_{l}")
    gb["w_down"] = mm_wgrad(sv["act"], dx2, 1408, 512, f"wgrad_down_{l}", False).reshape(N_CHIP, FFN // N_CHIP, D_MODEL)
    dh2 = mm_dgrad_colblk(dgu, W["w_gate_up"], f"dgrad_gu_{l}")
    gb["w_gate_up"] = mm_wgrad(sv["h2"], dgu, 512, 1408, f"wgrad_gu_{l}", True)
    dx1, gs["ffn_norm_g"] = rms_bwd(dh2, sv["x1"], P["ffn_norm_g"], l, dx2, f"rms_ffn_bwd_{l}")
    dyn = mm_dgrad(dx1, W["w_out"].reshape(D_MODEL, D_MODEL), f"dgrad_out_{l}")
    gb["w_out"] = mm_wgrad(sv["yn"], dx1, 512, 1024, f"wgrad_out_{l}", False).reshape(N_CHIP, D_MODEL // N_CHIP, D_MODEL)
    dyc, dys, dyb, gs["out_norm_g"] = outnorm_bwd(dyn, sv["yc"], sv["ys"], sv["yb"], P["out_norm_g"], l,
                                                  f"outnorm_bwd_{l}")
    ex, done = mid(gb) if mid is not None else (None, None)
    dq, dk, dv, *moved = attn_bwd(sv["qkv"], sv["rt"], dyb, f"attn_bwd_{l}", ex)
    if done is not None:
        done(moved)
    dps, gs["sg_ln_g"], gs["sg_ln_b"], gs["sg_w"], dbt = sg_bwd(
        sv["proj"], dys, P["sg_ln_g"], P["sg_ln_b"], P["sg_wt"], P["sg_wtt"], P["sg_bt"], l, f"sg_bwd_{l}")
    gs["sg_b"] = dbt.T
    dpc, dcw, gs["conv_b"], gs["conv_ln_g"], gs["conv_ln_b"] = conv_bwd(
        sv["proj"], dyc, P["conv_w"], P["conv_b"], P["conv_ln_g"], P["conv_ln_b"], l, f"conv_bwd_{l}")
    gs["conv_w"] = dcw[:CONV_K]
    dproj, dgq, dgk = qkv_bwd(sv["proj"], dq, dk, dv, P["q_norm_g"], P["k_norm_g"], dpc, dps, l, f"qkv_bwd_{l}")
    gs["q_norm_g"] = dgq[0, :HEAD_DIM] + dgq[0, HEAD_DIM:]
    gs["k_norm_g"] = dgk[0, :HEAD_DIM] + dgk[0, HEAD_DIM:]
    dh = mm_dgrad_colblk(dproj, W["w_in"], f"dgrad_in_{l}")
    gb["w_in"] = mm_wgrad(sv["h"], dproj, 1024, IN_W // N_CHIP, f"wgrad_in_{l}", True)
    dx, gs["mix_norm_g"] = rms_bwd(dh, sv["x"], P["mix_norm_g"], l, dx1, f"rms_mix_bwd_{l}")
    return dx, gb, gs


def reduce_start(gl, items, c1):
    n0, l0 = items[0]
    l1 = exchange_sibling_half(gl, f"rs_sibling_{n0}_{l0}")
    return add_halves(gl, l1, c1, f"rs_add_{n0}_{l0}")


def reduce_finish(ps, l2, items, kc):
    n0, l0 = items[0]
    fs = sum_chips_all(ps, l2, kc, f"rs_sum_{n0}_{l0}")
    full = join_sibling_halves(fs, f"rs_join_{n0}_{l0}")
    return [f.reshape(2 * f.shape[1], f.shape[2]) for f in full]


def kernel(x, mix_norm_g, w_in, conv_w, conv_b, conv_ln_g, conv_ln_b, sg_ln_g, sg_ln_b, sg_w, sg_b, q_norm_g, k_norm_g, out_norm_g, w_out, ffn_norm_g, w_gate_up, w_down, loss_target, m_mix_norm_g, m_w_in, m_conv_w, m_conv_b, m_conv_ln_g, m_conv_ln_b, m_sg_ln_g, m_sg_ln_b, m_sg_w, m_sg_b, m_q_norm_g, m_k_norm_g, m_out_norm_g, m_w_out, m_ffn_norm_g, m_w_gate_up, m_w_down, v_mix_norm_g, v_w_in, v_conv_w, v_conv_b, v_conv_ln_g, v_conv_ln_b, v_sg_ln_g, v_sg_ln_b, v_sg_w, v_sg_b, v_q_norm_g, v_k_norm_g, v_out_norm_g, v_w_out, v_ffn_norm_g, v_w_gate_up, v_w_down):
    w = dict(mix_norm_g=mix_norm_g, w_in=w_in, conv_w=conv_w, conv_b=conv_b, conv_ln_g=conv_ln_g, conv_ln_b=conv_ln_b,
             sg_ln_g=sg_ln_g, sg_ln_b=sg_ln_b, sg_w=sg_w, sg_b=sg_b, q_norm_g=q_norm_g, k_norm_g=k_norm_g,
             out_norm_g=out_norm_g, w_out=w_out, ffn_norm_g=ffn_norm_g, w_gate_up=w_gate_up, w_down=w_down)
    m = dict(mix_norm_g=m_mix_norm_g, w_in=m_w_in, conv_w=m_conv_w, conv_b=m_conv_b, conv_ln_g=m_conv_ln_g,
             conv_ln_b=m_conv_ln_b, sg_ln_g=m_sg_ln_g, sg_ln_b=m_sg_ln_b, sg_w=m_sg_w, sg_b=m_sg_b,
             q_norm_g=m_q_norm_g, k_norm_g=m_k_norm_g, out_norm_g=m_out_norm_g, w_out=m_w_out,
             ffn_norm_g=m_ffn_norm_g, w_gate_up=m_w_gate_up, w_down=m_w_down)
    v = dict(mix_norm_g=v_mix_norm_g, w_in=v_w_in, conv_w=v_conv_w, conv_b=v_conv_b, conv_ln_g=v_conv_ln_g,
             conv_ln_b=v_conv_ln_b, sg_ln_g=v_sg_ln_g, sg_ln_b=v_sg_ln_b, sg_w=v_sg_w, sg_b=v_sg_b,
             q_norm_g=v_q_norm_g, k_norm_g=v_k_norm_g, out_norm_g=v_out_norm_g, w_out=v_w_out,
             ffn_norm_g=v_ffn_norm_g, w_gate_up=v_w_gate_up, w_down=v_w_down)
    L = DEPTH
    xi, yi, ci = lax.axis_index("x"), lax.axis_index("y"), lax.axis_index("c")
    kme = 2 * xi + yi
    c1 = ci.astype(jnp.int32).reshape(1)
    k1 = kme.astype(jnp.int32).reshape(1)
    kc = jnp.stack([kme, ci]).astype(jnp.int32)

    def gather_parts(items):
        return [place_slot(w[n], l, k1, N_CHIP, BF16, f"ag_own_{n}_{l}", AG_ROWS[n]) for n, l in items]

    cw_sh = jnp.pad(conv_w.reshape(L * CONV_K, CONV_W // N_CHIP), ((0, 128 - L * CONV_K), (0, 0)))
    cw_buf = place_slot(cw_sh[None], 0, k1, N_CHIP, F32, "ag_own_conv", 128)
    *bufs, cw_buf = run_exchange(gather_exchange(gather_parts([("w_in", 0)]) + [cw_buf]), "ag_chips_first")
    *bufs, cw_buf = gather_forward(bufs + [cw_buf], "ag_sibling_first")
    W = [{} for _ in range(L)]
    W[0]["w_in"] = bufs[0]
    cw_all = cw_buf[:, :L * CONV_K].reshape(N_CHIP, L, CONV_K, CONV_W // N_CHIP)
    cw_full = jnp.transpose(cw_all, (1, 2, 0, 3)).reshape(L, CONV_K, CONV_W)

    tril = jnp.tril(jnp.ones((CHUNK, CHUNK), bool))
    sg_wt = jnp.where(tril, sg_w, 0.0)
    P = {
        "mix_norm_g": mix_norm_g.reshape(L, 1, D_MODEL), "ffn_norm_g": ffn_norm_g.reshape(L, 1, D_MODEL),
        "out_norm_g": out_norm_g.reshape(L, 1, D_MODEL),
        "conv_w": jnp.pad(cw_full, ((0, 0), (0, 1), (0, 0))), "conv_b": conv_b.reshape(L, 1, CONV_W),
        "conv_ln_g": conv_ln_g.reshape(L, 1, CONV_W), "conv_ln_b": conv_ln_b.reshape(L, 1, CONV_W),
        "sg_ln_g": sg_ln_g.reshape(L, 1, SG_W), "sg_ln_b": sg_ln_b.reshape(L, 1, SG_W),
        "sg_wt": sg_wt.astype(MXU_DT), "sg_wtt": jnp.swapaxes(sg_wt, 2, 3).astype(MXU_DT),
        "sg_bt": jnp.swapaxes(sg_b, 1, 2),
        "q_norm_g": jnp.tile(q_norm_g, (1, 2)).reshape(L, 1, 128), "k_norm_g": jnp.tile(k_norm_g, (1, 2)).reshape(L, 1, 128),
    }

    h = x[0]
    saved = []
    for l in range(L):
        items = [("w_out", l), ("w_gate_up", l)] + ([("w_in", l + 1)] if l + 1 < L else [])
        items2 = [("w_down", l)]

        def after(moved, l=l, items=items, tag="a"):
            for (n, ll), a in zip(items, gather_forward(moved, f"ag_sibling_{tag}_{l}")):
                W[ll][n] = a
            return W[l]

        h, sv, _ = layer_forward(h, l, P, W[l], gather_exchange(gather_parts(items)), after,
                                 gather_exchange(gather_parts(items2)),
                                 functools.partial(after, l=l, items=items2, tag="b"))
        saved.append(sv)
    dy, loss_part = loss_head(h, loss_target[0], "loss_head")
    loss = lax.psum(loss_part[0, 0], ("x", "y", "c"))

    outs = {n: None for n in BIG}
    small_grads = [None] * L

    def finish(items, ps, l2):
        for (n, lyr), g in zip(items, reduce_finish(ps, l2, items, kc)):
            outs[n] = adamw_layer(w[n], m[n], v[n], g, outs[n], lyr, f"adamw_{n}_{lyr}", ADAM_ROWS[n])

    g_in = None
    for l in reversed(range(L)):
        def mid(gb, l=l, g_in=g_in):
            items = [("w_down", l), ("w_gate_up", l), ("w_out", l)] + ([("w_in", l + 1)] if g_in is not None else [])
            ps = reduce_start([gb[n] for n, _ in items[:3]] + ([g_in] if g_in is not None else []), items, c1)
            return scatter_exchange(ps), lambda moved: finish(items, ps, moved)

        dy, gb, small_grads[l] = layer_backward(dy, l, P, W[l], saved[l], mid)
        g_in = gb["w_in"]
    ps = reduce_start([g_in], [("w_in", 0)], c1)
    finish([("w_in", 0)], ps, run_exchange(scatter_exchange(ps), "rs_chips_last"))

    shapes = [(L,) + small_grads[0][n].shape for n in SMALL]
    packed = _pack([jnp.stack([small_grads[l][n] for l in range(L)]) for n in SMALL])
    gsum = dict(zip(SMALL, _unpack(all_reduce_small(packed, "ar_small"), shapes)))
    gsum["conv_w"] = lax.dynamic_slice_in_dim(gsum["conv_w"], kme * (CONV_W // N_CHIP), CONV_W // N_CHIP, axis=2)
    gsum = {n: gsum[n].reshape(w[n].shape) for n in SMALL}
    lshapes = [w[n].shape for n in SMALL]
    d_p, m_p, v_p = adamw_flat(_pack([w[n] for n in SMALL]), _pack([m[n] for n in SMALL]),
                               _pack([v[n] for n in SMALL]), _pack([gsum[n] for n in SMALL]), "adamw_small")
    d_s = dict(zip(SMALL, _unpack(d_p, lshapes)))
    m_s = dict(zip(SMALL, _unpack(m_p, lshapes)))
    v_s = dict(zip(SMALL, _unpack(v_p, lshapes)))

    grads = {n: (outs[n][0] if n in BIG else gsum[n]) for n in WEIGHTS}
    delta = {n: (outs[n][1] if n in BIG else d_s[n]) for n in WEIGHTS}
    new_m = {n: (outs[n][2] if n in BIG else m_s[n]) for n in WEIGHTS}
    new_v = {n: (outs[n][3] if n in BIG else v_s[n]) for n in WEIGHTS}
    return (loss, dy[None], *[grads[n] for n in WEIGHTS], *[delta[n] for n in WEIGHTS],
            *[new_m[n] for n in WEIGHTS], *[new_v[n] for n in WEIGHTS])
```

```python
import functools

import jax
import jax.numpy as jnp
from jax import lax
from jax.experimental import pallas as pl
from jax.experimental.pallas import tpu as pltpu

F32 = jnp.float32
BF16 = jnp.bfloat16
MXU_DT = jnp.bfloat16
GRAD_WIRE_DT = jnp.bfloat16

D_MODEL = 1024
DEPTH = 4
HEAD_DIM = 64
CONV_W = 256
SG_W = 256
SB_W = 512
CONV_K = 31
CHUNK = 128
OFF_SG = 2 * CONV_W
OFF_SB = OFF_SG + 2 * SG_W
IN_W = OFF_SB + 3 * SB_W
FFN = 2816
N_CHIP = 4
RMS_EPS = 1e-6
LN_EPS = 1e-5
ADAM_LR = 0.001
ADAM_B1 = 0.9
ADAM_B2 = 0.999
ADAM_EPS = 1e-08
ADAM_WD = 0.01
ADAM_STEP = 10
BC1 = 1.0 - ADAM_B1 ** ADAM_STEP
BC2 = 1.0 - ADAM_B2 ** ADAM_STEP
HALO = 32
MESH = pl.DeviceIdType.MESH
ANY = pl.BlockSpec(memory_space=pl.ANY)
VMEM_LIMIT = 56 * 1024 * 1024


def _cp(**kw):
    return pltpu.CompilerParams(vmem_limit_bytes=VMEM_LIMIT, **kw)


def _dot(a, b, dims):
    dn = {"nn": (((1,), (0,)), ((), ())), "nt": (((1,), (1,)), ((), ())), "tn": (((0,), (0,)), ((), ()))}[dims]
    return lax.dot_general(a.astype(MXU_DT), b.astype(MXU_DT), dn, preferred_element_type=F32)


def _iota(shape, axis):
    return lax.broadcasted_iota(jnp.int32, shape, axis)


def _rms(x, g):
    return x * lax.rsqrt(jnp.mean(x * x, axis=-1, keepdims=True) + RMS_EPS) * g


def _ln(x, g, b):
    mu = jnp.mean(x, axis=-1, keepdims=True)
    xc = x - mu
    var = jnp.mean(xc * xc, axis=-1, keepdims=True)
    return xc * lax.rsqrt(var + LN_EPS) * g + b


def _glu(val, gate):
    return val * jax.nn.sigmoid(gate)


def _ln_silu(c, g, b):
    return jax.nn.silu(_ln(c, g, b))


_ERF_ALPHA = (-2.72614225801306e-10, 2.77068142495902e-08, -2.10102402082508e-06, -5.69250639462346e-05,
              -7.34990630326855e-04, -2.95459980854025e-03, -1.60960333262415e-02)
_ERF_BETA = (-1.45660718464996e-05, -2.13374055278905e-04, -1.68282697438203e-03, -7.37332916720468e-03,
             -1.42647390514189e-02)


def _erf(x):
    x = jnp.clip(x, -4.0, 4.0)
    x2 = x * x
    p = jnp.full_like(x, _ERF_ALPHA[0])
    for a in _ERF_ALPHA[1:]:
        p = p * x2 + a
    q = jnp.full_like(x, _ERF_BETA[0])
    for b in _ERF_BETA[1:]:
        q = q * x2 + b
    return x * p / q


@jax.custom_jvp
def _gelu(x):
    return 0.5 * x * (1.0 + _erf(x * (2.0 ** -0.5)))


@_gelu.defjvp
def _gelu_jvp(primals, tangents):
    (x,), (t,) = primals, tangents
    cdf = 0.5 * (1.0 + _erf(x * (2.0 ** -0.5)))
    pdf = jnp.exp(-0.5 * x * x) * ((2.0 * jnp.pi) ** -0.5)
    return x * cdf, t * (cdf + x * pdf)


def _sg_pre(uvp, g, b):
    uv = _gelu(uvp)
    return uv[:, :SG_W], _ln(uv[:, SG_W:], g, b)


def _outnorm(yc, ys, yb, g):
    return jnp.concatenate([_rms(yc, g[:, :CONV_W]), _rms(ys, g[:, CONV_W:CONV_W + SG_W]),
                            _rms(yb, g[:, CONV_W + SG_W:])], axis=-1)


def _qkv_fn(pq, pk, pv, gq, gk):
    outs = []
    for p, g, sc in ((pq, gq, HEAD_DIM ** -0.5), (pk, gk, 1.0)):
        for s in range(SB_W // 128):
            x = p[:, 128 * s:128 * (s + 1)]
            seg = _iota(x.shape, 1) < HEAD_DIM
            x2 = x * x
            s0 = jnp.sum(jnp.where(seg, x2, 0.0), axis=-1, keepdims=True)
            s1 = jnp.sum(jnp.where(seg, 0.0, x2), axis=-1, keepdims=True)
            ms = jnp.where(seg, s0, s1) * (1.0 / HEAD_DIM)
            outs.append(x * lax.rsqrt(ms + RMS_EPS) * (g * sc))
    outs.append(pv)
    return jnp.concatenate(outs, axis=-1)


def _swiglu(g, u):
    return jax.nn.silu(g) * u


def _softplus(z):
    return jnp.maximum(z, 0.0) + jnp.log(1.0 + jnp.exp(-jnp.abs(z)))


def mm_colblk(a, wb, out_dtype, name, tm=2048):
    S, K = a.shape
    nb, _, C = wb.shape

    def body(a_ref, w_ref, o_ref):
        o_ref[...] = _dot(a_ref[...], w_ref[0], "nn").astype(o_ref.dtype)

    return pl.pallas_call(
        body, grid=(nb, S // tm),
        in_specs=[pl.BlockSpec((tm, K), lambda k, i: (i, 0)), pl.BlockSpec((1, K, C), lambda k, i: (k, 0, 0))],
        out_specs=pl.BlockSpec((tm, C), lambda k, i: (i, k)),
        out_shape=jax.ShapeDtypeStruct((S, nb * C), out_dtype), name=name, compiler_params=_cp())(a, wb)


def mm_res(a, w, res, name, tm=512):
    S, K = a.shape
    N = w.shape[1]

    def body(a_ref, w_ref, r_ref, o_ref):
        o_ref[...] = r_ref[...] + _dot(a_ref[...], w_ref[...], "nn")

    return pl.pallas_call(
        body, grid=(S // tm,),
        in_specs=[pl.BlockSpec((tm, K), lambda i: (i, 0)), pl.BlockSpec((K, N), lambda i: (0, 0)),
                  pl.BlockSpec((tm, N), lambda i: (i, 0))],
        out_specs=pl.BlockSpec((tm, N), lambda i: (i, 0)),
        out_shape=jax.ShapeDtypeStruct((S, N), F32), name=name, compiler_params=_cp())(a, w, res)


def ffn_up(h2, wgu, name, tm=512, ex=None):
    S, K = h2.shape
    C = wgu.shape[2]
    nm = S // tm

    def body(*refs):
        if ex is None:
            (h_ref, wg_ref, wu_ref), (g_ref, u_ref, a_ref), copies = refs[:3], refs[3:6], None
        else:
            (h_ref, wg_ref, wu_ref), (g_ref, u_ref, a_ref), copies = ex.split(refs, 3, 3)

            @pl.when(jnp.logical_and(pl.program_id(0) == 0, pl.program_id(1) == 0))
            def _():
                for cp in copies():
                    cp.start()

        h = h_ref[...]
        g = _dot(h, wg_ref[0], "nn")
        u = _dot(h, wu_ref[0], "nn")
        g_ref[...] = g.astype(BF16)
        u_ref[...] = u.astype(BF16)
        a_ref[...] = _swiglu(g, u).astype(BF16)

        if ex is not None:
            @pl.when(jnp.logical_and(pl.program_id(0) == 1, pl.program_id(1) == nm - 1))
            def _():
                for cp in copies():
                    cp.wait()

    o = pl.BlockSpec((tm, C), lambda j, i: (i, j))
    sh = jax.ShapeDtypeStruct((S, 2 * C), BF16)
    in_specs, out_specs, out_shape, aliases, scratch, extra = _with_exchange(
        ex, 3, 3,
        [pl.BlockSpec((tm, K), lambda j, i: (i, 0)), pl.BlockSpec((1, K, C), lambda j, i: (j, 0, 0)),
         pl.BlockSpec((1, K, C), lambda j, i: (2 + j, 0, 0))],
        [o, o, o], [sh, sh, sh])
    return pl.pallas_call(
        body, grid=(2, nm), in_specs=in_specs, out_specs=out_specs, out_shape=out_shape,
        input_output_aliases=aliases, scratch_shapes=scratch, name=name, compiler_params=_cp())(h2, wgu, wgu, *extra)


def ffn_down_bwd(dx2, wdown, g, u, name, tm=512):
    S, N = dx2.shape
    C = FFN // 2

    def body(d_ref, w_ref, g_ref, u_ref, o_ref):
        d = d_ref[...]
        for j in range(2):
            cols = slice(j * C, (j + 1) * C)
            dact = _dot(d, w_ref[cols, :], "nt")
            g = g_ref[:, cols].astype(F32)
            s = jax.nn.sigmoid(g)
            gs = g * s
            o_ref[:, cols] = (dact * u_ref[:, cols].astype(F32) * (s + gs * (1.0 - s))).astype(BF16)
            o_ref[:, FFN + j * C:FFN + (j + 1) * C] = (dact * gs).astype(BF16)

    row = pl.BlockSpec((tm, FFN), lambda i: (i, 0))
    return pl.pallas_call(
        body, grid=(S // tm,),
        in_specs=[pl.BlockSpec((tm, N), lambda i: (i, 0)), pl.BlockSpec((FFN, N), lambda i: (0, 0)), row, row],
        out_specs=pl.BlockSpec((tm, 2 * FFN), lambda i: (i, 0)),
        out_shape=jax.ShapeDtypeStruct((S, 2 * FFN), BF16), name=name, compiler_params=_cp())(dx2, wdown, g, u)


def mm_dgrad_colblk(do, wb, name, tm=2048):
    S = do.shape[0]
    nb, K, C = wb.shape

    def body(d_ref, w_ref, o_ref):
        k = pl.program_id(1)
        r = _dot(d_ref[...], w_ref[0], "nt")

        @pl.when(k == 0)
        def _():
            o_ref[...] = r

        @pl.when(k != 0)
        def _():
            o_ref[...] += r

    return pl.pallas_call(
        body, grid=(S // tm, nb),
        in_specs=[pl.BlockSpec((tm, C), lambda i, k: (i, k)), pl.BlockSpec((1, K, C), lambda i, k: (k, 0, 0))],
        out_specs=pl.BlockSpec((tm, K), lambda i, k: (i, 0)),
        out_shape=jax.ShapeDtypeStruct((S, K), F32), name=name, compiler_params=_cp())(do, wb)


def mm_dgrad(do, w, name, tm=1024):
    S, N = do.shape
    K = w.shape[0]

    def body(d_ref, w_ref, o_ref):
        o_ref[...] = _dot(d_ref[...], w_ref[...], "nt")

    return pl.pallas_call(
        body, grid=(S // tm,),
        in_specs=[pl.BlockSpec((tm, N), lambda i: (i, 0)), pl.BlockSpec((K, N), lambda i: (0, 0))],
        out_specs=pl.BlockSpec((tm, K), lambda i: (i, 0)),
        out_shape=jax.ShapeDtypeStruct((S, K), F32), name=name, compiler_params=_cp())(do, w)


def mm_wgrad(a, do, tk, tn, name, blocked):
    S, K = a.shape
    N = do.shape[1]

    def body(a_ref, d_ref, o_ref):
        r = _dot(a_ref[...], d_ref[...], "tn").astype(GRAD_WIRE_DT)
        if blocked:
            o_ref[0] = r
        else:
            o_ref[...] = r

    if blocked:
        out_spec = pl.BlockSpec((1, tk, tn), lambda n, j: (n, j, 0))
        out_shape = jax.ShapeDtypeStruct((N // tn, K, tn), GRAD_WIRE_DT)
    else:
        out_spec = pl.BlockSpec((tk, tn), lambda n, j: (j, n))
        out_shape = jax.ShapeDtypeStruct((K, N), GRAD_WIRE_DT)
    return pl.pallas_call(
        body, grid=(N // tn, K // tk),
        in_specs=[pl.BlockSpec((S, tk), lambda n, j: (0, j)), pl.BlockSpec((S, tn), lambda n, j: (0, n))],
        out_specs=out_spec, out_shape=out_shape, name=name, compiler_params=_cp())(a, do)


def rms_fwd(x, g3, l, name, tm=512):
    S, N = x.shape

    def body(x_ref, g_ref, o_ref):
        o_ref[...] = _rms(x_ref[...], g_ref[0]).astype(BF16)

    return pl.pallas_call(
        body, grid=(S // tm,),
        in_specs=[pl.BlockSpec((tm, N), lambda i: (i, 0)), pl.BlockSpec((1, 1, N), lambda i: (l, 0, 0))],
        out_specs=pl.BlockSpec((tm, N), lambda i: (i, 0)),
        out_shape=jax.ShapeDtypeStruct((S, N), BF16), name=name, compiler_params=_cp())(x, g3)


def rms_bwd(dh, x, g3, l, dres, name, tm=512):
    S, N = x.shape

    def body(dh_ref, x_ref, g_ref, r_ref, dx_ref, dg_ref):
        _, vjp = jax.vjp(_rms, x_ref[...], g_ref[0])
        dx, dg = vjp(dh_ref[...])
        dx_ref[...] = r_ref[...] + dx

        @pl.when(pl.program_id(0) == 0)
        def _():
            dg_ref[...] = jnp.zeros_like(dg_ref)

        dg_ref[...] += dg

    row = pl.BlockSpec((tm, N), lambda i: (i, 0))
    return pl.pallas_call(
        body, grid=(S // tm,),
        in_specs=[row, row, pl.BlockSpec((1, 1, N), lambda i: (l, 0, 0)), row],
        out_specs=[row, pl.BlockSpec((1, N), lambda i: (0, 0))],
        out_shape=[jax.ShapeDtypeStruct((S, N), F32), jax.ShapeDtypeStruct((1, N), F32)],
        name=name, compiler_params=_cp())(dh, x, g3, dres)


def outnorm_fwd(yc, ys, yb, g3, l, name, tm=512):
    S = yc.shape[0]

    def body(c_ref, s_ref, b_ref, g_ref, o_ref):
        o_ref[...] = _outnorm(c_ref[...], s_ref[...], b_ref[...], g_ref[0]).astype(BF16)

    return pl.pallas_call(
        body, grid=(S // tm,),
        in_specs=[pl.BlockSpec((tm, CONV_W), lambda i: (i, 0)), pl.BlockSpec((tm, SG_W), lambda i: (i, 0)),
                  pl.BlockSpec((tm, SB_W), lambda i: (i, 0)), pl.BlockSpec((1, 1, D_MODEL), lambda i: (l, 0, 0))],
        out_specs=pl.BlockSpec((tm, D_MODEL), lambda i: (i, 0)),
        out_shape=jax.ShapeDtypeStruct((S, D_MODEL), BF16), name=name, compiler_params=_cp())(yc, ys, yb, g3)


def outnorm_bwd(dyn, yc, ys, yb, g3, l, name, tm=512):
    S = yc.shape[0]

    def body(d_ref, c_ref, s_ref, b_ref, g_ref, dc_ref, ds_ref, db_ref, dg_ref):
        _, vjp = jax.vjp(_outnorm, c_ref[...], s_ref[...], b_ref[...], g_ref[0])
        dc, ds, db, dg = vjp(d_ref[...])
        dc_ref[...] = dc
        ds_ref[...] = ds
        db_ref[...] = db

        @pl.when(pl.program_id(0) == 0)
        def _():
            dg_ref[...] = jnp.zeros_like(dg_ref)

        dg_ref[...] += dg

    sc = pl.BlockSpec((tm, CONV_W), lambda i: (i, 0))
    ss = pl.BlockSpec((tm, SG_W), lambda i: (i, 0))
    sb = pl.BlockSpec((tm, SB_W), lambda i: (i, 0))
    return pl.pallas_call(
        body, grid=(S // tm,),
        in_specs=[pl.BlockSpec((tm, D_MODEL), lambda i: (i, 0)), sc, ss, sb,
                  pl.BlockSpec((1, 1, D_MODEL), lambda i: (l, 0, 0))],
        out_specs=[sc, ss, sb, pl.BlockSpec((1, D_MODEL), lambda i: (0, 0))],
        out_shape=[jax.ShapeDtypeStruct((S, CONV_W), F32), jax.ShapeDtypeStruct((S, SG_W), F32),
                   jax.ShapeDtypeStruct((S, SB_W), F32), jax.ShapeDtypeStruct((1, D_MODEL), F32)],
        name=name, compiler_params=_cp())(dyn, yc, ys, yb, g3)


def loss_head(y, t, name, tm=512):
    S, N = y.shape

    def body(y_ref, t_ref, dy_ref, l_ref):
        e = y_ref[...] - t_ref[...]
        dy_ref[...] = e * (1.0 / N)

        @pl.when(pl.program_id(0) == 0)
        def _():
            l_ref[...] = jnp.zeros_like(l_ref)

        l_ref[...] += (0.5 / N) * jnp.sum(jnp.sum(e * e, axis=-1, keepdims=True), axis=0, keepdims=True)

    row = pl.BlockSpec((tm, N), lambda i: (i, 0))
    return pl.pallas_call(
        body, grid=(S // tm,), in_specs=[row, row],
        out_specs=[row, pl.BlockSpec((1, 1), lambda i: (0, 0))],
        out_shape=[jax.ShapeDtypeStruct((S, N), F32), jax.ShapeDtypeStruct((1, 1), F32)],
        name=name, compiler_params=_cp())(y, t)


def _conv_taps(a, w_ref, T):
    acc = jnp.zeros((T, CONV_W), F32)
    for j in range(CONV_K):
        sh = CONV_K - 1 - j
        r = a if sh == 0 else pltpu.roll(a, sh, 0)
        acc = acc + r[HALO:HALO + T] * w_ref[pl.ds(j, 1), :]
    return acc


def conv_fwd(proj, cw, cb, lg, lb, l, name, T=256):
    S = proj.shape[0]
    nt = S // T

    def body(p_ref, w_ref, cb_ref, lg_ref, lb_ref, y_ref, hc_s):
        hc_s[0:HALO, :] = jnp.zeros((HALO, CONV_W), F32)

        def fill(i, _):
            r0 = pl.multiple_of(i * T, T)
            pc = p_ref[pl.ds(r0, T), :]
            hc_s[pl.ds(r0 + HALO, T), :] = _glu(pc[:, :CONV_W], pc[:, CONV_W:])
            return 0

        lax.fori_loop(0, nt, fill, 0)

        def tile(i, _):
            r0 = pl.multiple_of(i * T, T)
            c = _conv_taps(hc_s[pl.ds(r0, T + HALO), :], w_ref.at[0], T) + cb_ref[0]
            y_ref[pl.ds(r0, T), :] = _ln_silu(c, lg_ref[0], lb_ref[0])
            return 0

        lax.fori_loop(0, nt, tile, 0)

    vec = pl.BlockSpec((1, 1, CONV_W), lambda i: (l, 0, 0))
    return pl.pallas_call(
        body, grid=(1,),
        in_specs=[pl.BlockSpec((S, 2 * CONV_W), lambda i: (0, 0)), pl.BlockSpec((1, 32, CONV_W), lambda i: (l, 0, 0)),
                  vec, vec, vec],
        out_specs=pl.BlockSpec((S, CONV_W), lambda i: (0, 0)),
        out_shape=jax.ShapeDtypeStruct((S, CONV_W), F32),
        scratch_shapes=[pltpu.VMEM((S + HALO, CONV_W), F32)], name=name, compiler_params=_cp())(proj, cw, cb, lg, lb)


def conv_bwd(proj, dy, cw, cb, lg, lb, l, name, T=256):
    S = proj.shape[0]
    nt = S // T

    def body(p_ref, dy_ref, w_ref, cb_ref, lg_ref, lb_ref, dp_ref, dw_ref, dcb_ref, dlg_ref, dlb_ref, hc_s, dc_s):
        hc_s[0:HALO, :] = jnp.zeros((HALO, CONV_W), F32)
        dc_s[S:S + HALO, :] = jnp.zeros((HALO, CONV_W), F32)
        dw_ref[...] = jnp.zeros_like(dw_ref)
        dcb_ref[...] = jnp.zeros_like(dcb_ref)
        dlg_ref[...] = jnp.zeros_like(dlg_ref)
        dlb_ref[...] = jnp.zeros_like(dlb_ref)

        def fill(i, _):
            r0 = pl.multiple_of(i * T, T)
            pc = p_ref[pl.ds(r0, T), :]
            hc_s[pl.ds(r0 + HALO, T), :] = _glu(pc[:, :CONV_W], pc[:, CONV_W:])
            return 0

        lax.fori_loop(0, nt, fill, 0)

        def tile(i, _):
            r0 = pl.multiple_of(i * T, T)
            a = hc_s[pl.ds(r0, T + HALO), :]
            c = _conv_taps(a, w_ref.at[0], T) + cb_ref[0]
            _, vjp = jax.vjp(_ln_silu, c, lg_ref[0], lb_ref[0])
            dc, dlg, dlb = vjp(dy_ref[pl.ds(r0, T), :])
            dc_s[pl.ds(r0, T), :] = dc
            dcb_ref[...] += jnp.sum(dc, axis=0, keepdims=True)
            dlg_ref[...] += dlg
            dlb_ref[...] += dlb
            for j in range(CONV_K):
                sh = CONV_K - 1 - j
                r = a if sh == 0 else pltpu.roll(a, sh, 0)
                dw_ref[pl.ds(j, 1), :] += jnp.sum(dc * r[HALO:HALO + T], axis=0, keepdims=True)
            return 0

        lax.fori_loop(0, nt, tile, 0)

        def back(i, _):
            r0 = pl.multiple_of(i * T, T)
            de = dc_s[pl.ds(r0, T + HALO), :]
            n = T + HALO
            dh = jnp.zeros((T, CONV_W), F32)
            for j in range(CONV_K):
                sh = CONV_K - 1 - j
                r = de if sh == 0 else pltpu.roll(de, n - sh, 0)
                dh = dh + r[0:T] * w_ref[0, pl.ds(j, 1), :]
            pc = p_ref[pl.ds(r0, T), :]
            _, vjp = jax.vjp(_glu, pc[:, :CONV_W], pc[:, CONV_W:])
            dval, dgate = vjp(dh)
            dp_ref[pl.ds(r0, T), :] = jnp.concatenate([dval, dgate], axis=-1).astype(BF16)
            return 0

        lax.fori_loop(0, nt, back, 0)

    vec = pl.BlockSpec((1, 1, CONV_W), lambda i: (l, 0, 0))
    ovec = pl.BlockSpec((1, CONV_W), lambda i: (0, 0))
    vsh = jax.ShapeDtypeStruct((1, CONV_W), F32)
    return pl.pallas_call(
        body, grid=(1,),
        in_specs=[pl.BlockSpec((S, 2 * CONV_W), lambda i: (0, 0)), pl.BlockSpec((S, CONV_W), lambda i: (0, 0)),
                  pl.BlockSpec((1, 32, CONV_W), lambda i: (l, 0, 0)), vec, vec, vec],
        out_specs=[pl.BlockSpec((S, 2 * CONV_W), lambda i: (0, 0)), pl.BlockSpec((32, CONV_W), lambda i: (0, 0)),
                   ovec, ovec, ovec],
        out_shape=[jax.ShapeDtypeStruct((S, 2 * CONV_W), BF16), jax.ShapeDtypeStruct((32, CONV_W), F32), vsh, vsh, vsh],
        scratch_shapes=[pltpu.VMEM((S + HALO, CONV_W), F32), pltpu.VMEM((S + HALO, CONV_W), F32)],
        name=name, compiler_params=_cp())(proj, dy, cw, cb, lg, lb)


def _sg_mix(wt_ref, v, bt):
    slabs = []
    for s in range(2):
        vs = v[:, 128 * s:128 * (s + 1)]
        seg = _iota(vs.shape, 1) < HEAD_DIM
        p0 = _dot(wt_ref[2 * s], vs, "nn") + bt[:, 2 * s:2 * s + 1]
        p1 = _dot(wt_ref[2 * s + 1], vs, "nn") + bt[:, 2 * s + 1:2 * s + 2]
        slabs.append(jnp.where(seg, p0, p1))
    return jnp.concatenate(slabs, axis=-1)


def sg_fwd(proj, lg, lb, wt, bt, l, name):
    S = proj.shape[0]

    def body(p_ref, lg_ref, lb_ref, w_ref, b_ref, y_ref):
        u, v = _sg_pre(p_ref[...], lg_ref[0], lb_ref[0])
        y_ref[...] = u * _sg_mix(w_ref.at[0], v, b_ref[0])

    vec = pl.BlockSpec((1, 1, SG_W), lambda i: (l, 0, 0))
    return pl.pallas_call(
        body, grid=(S // CHUNK,),
        in_specs=[pl.BlockSpec((CHUNK, 2 * SG_W), lambda i: (i, 1)), vec, vec,
                  pl.BlockSpec((1, 4, CHUNK, CHUNK), lambda i: (l, 0, 0, 0)),
                  pl.BlockSpec((1, CHUNK, 4), lambda i: (l, 0, 0))],
        out_specs=pl.BlockSpec((CHUNK, SG_W), lambda i: (i, 0)),
        out_shape=jax.ShapeDtypeStruct((S, SG_W), F32), name=name, compiler_params=_cp())(proj, lg, lb, wt, bt)


def sg_bwd(proj, dy, lg, lb, wt, wtt, bt, l, name):
    S = proj.shape[0]

    def body(p_ref, dy_ref, lg_ref, lb_ref, w_ref, wt_ref, b_ref, dp_ref, dlg_ref, dlb_ref, dw_ref, db_ref):
        @pl.when(pl.program_id(0) == 0)
        def _():
            dlg_ref[...] = jnp.zeros_like(dlg_ref)
            dlb_ref[...] = jnp.zeros_like(dlb_ref)
            dw_ref[...] = jnp.zeros_like(dw_ref)
            db_ref[...] = jnp.zeros_like(db_ref)

        (u, v), vjp = jax.vjp(_sg_pre, p_ref[...], lg_ref[0], lb_ref[0])
        dy = dy_ref[...]
        mixed = _sg_mix(w_ref.at[0], v, b_ref[0])
        du = dy * mixed
        dm = dy * u
        tril = _iota((CHUNK, CHUNK), 1) <= _iota((CHUNK, CHUNK), 0)
        dvs = []
        for s in range(2):
            dms = dm[:, 128 * s:128 * (s + 1)]
            vs = v[:, 128 * s:128 * (s + 1)]
            seg = _iota(dms.shape, 1) < HEAD_DIM
            halves = (jnp.where(seg, dms, 0.0), jnp.where(seg, 0.0, dms))
            dv_h = []
            for e in range(2):
                h = 2 * s + e
                db_ref[:, h:h + 1] += jnp.sum(halves[e], axis=-1, keepdims=True)
                dw_ref[h] += jnp.where(tril, _dot(halves[e], vs, "nt"), 0.0)
                dv_h.append(_dot(wt_ref[0, h], halves[e], "nn"))
            dvs.append(dv_h[0] + dv_h[1])
        dp, dlg, dlb = vjp((du, jnp.concatenate(dvs, axis=-1)))
        dp_ref[...] = dp.astype(BF16)
        dlg_ref[...] += dlg
        dlb_ref[...] += dlb

    vec = pl.BlockSpec((1, 1, SG_W), lambda i: (l, 0, 0))
    wsp = pl.BlockSpec((1, 4, CHUNK, CHUNK), lambda i: (l, 0, 0, 0))
    ovec = pl.BlockSpec((1, SG_W), lambda i: (0, 0))
    return pl.pallas_call(
        body, grid=(S // CHUNK,),
        in_specs=[pl.BlockSpec((CHUNK, 2 * SG_W), lambda i: (i, 1)), pl.BlockSpec((CHUNK, SG_W), lambda i: (i, 0)),
                  vec, vec, wsp, wsp, pl.BlockSpec((1, CHUNK, 4), lambda i: (l, 0, 0))],
        out_specs=[pl.BlockSpec((CHUNK, 2 * SG_W), lambda i: (i, 0)), ovec, ovec,
                   pl.BlockSpec((4, CHUNK, CHUNK), lambda i: (0, 0, 0)), pl.BlockSpec((CHUNK, 4), lambda i: (0, 0))],
        out_shape=[jax.ShapeDtypeStruct((S, 2 * SG_W), BF16), jax.ShapeDtypeStruct((1, SG_W), F32),
                   jax.ShapeDtypeStruct((1, SG_W), F32), jax.ShapeDtypeStruct((4, CHUNK, CHUNK), F32),
                   jax.ShapeDtypeStruct((CHUNK, 4), F32)],
        name=name, compiler_params=_cp())(proj, dy, lg, lb, wt, wtt, bt)


def qkv_fwd(proj, gq, gk, l, name, tm=512):
    S = proj.shape[0]

    def body(pq_ref, pk_ref, pv_ref, gq_ref, gk_ref, o_ref):
        o_ref[...] = _qkv_fn(pq_ref[...], pk_ref[...], pv_ref[...], gq_ref[0], gk_ref[0]).astype(BF16)

    vec = pl.BlockSpec((1, 1, 128), lambda i: (l, 0, 0))
    qb = OFF_SB // SB_W
    return pl.pallas_call(
        body, grid=(S // tm,),
        in_specs=[pl.BlockSpec((tm, SB_W), lambda i: (i, qb)), pl.BlockSpec((tm, SB_W), lambda i: (i, qb + 1)),
                  pl.BlockSpec((tm, SB_W), lambda i: (i, qb + 2)), vec, vec],
        out_specs=pl.BlockSpec((tm, 3 * SB_W), lambda i: (i, 0)),
        out_shape=jax.ShapeDtypeStruct((S, 3 * SB_W), BF16), name=name, compiler_params=_cp())(proj, proj, proj, gq, gk)


def qkv_bwd(proj, dq, dk, dv, gq, gk, dpc, dps, l, name, tm=512):
    S = proj.shape[0]

    def body(pq_ref, pk_ref, pv_ref, dq_ref, dk_ref, dv_ref, gq_ref, gk_ref, dpc_ref, dps_ref, dp_ref, dgq_ref,
             dgk_ref):
        _, vjp = jax.vjp(_qkv_fn, pq_ref[...], pk_ref[...], pv_ref[...], gq_ref[0], gk_ref[0])
        dpq, dpk, dpv, dgq, dgk = vjp(jnp.concatenate([dq_ref[...], dk_ref[...], dv_ref[...]], axis=-1))
        dp_ref[:, :OFF_SG] = dpc_ref[...]
        dp_ref[:, OFF_SG:OFF_SB] = dps_ref[...]
        dp_ref[:, OFF_SB:] = jnp.concatenate([dpq, dpk, dpv], axis=-1).astype(BF16)

        @pl.when(pl.program_id(0) == 0)
        def _():
            dgq_ref[...] = jnp.zeros_like(dgq_ref)
            dgk_ref[...] = jnp.zeros_like(dgk_ref)

        dgq_ref[...] += dgq
        dgk_ref[...] += dgk

    vec = pl.BlockSpec((1, 1, 128), lambda i: (l, 0, 0))
    part = pl.BlockSpec((tm, SB_W), lambda i: (i, 0))
    ovec = pl.BlockSpec((1, 128), lambda i: (0, 0))
    qb = OFF_SB // SB_W
    return pl.pallas_call(
        body, grid=(S // tm,),
        in_specs=[pl.BlockSpec((tm, SB_W), lambda i: (i, qb)), pl.BlockSpec((tm, SB_W), lambda i: (i, qb + 1)),
                  pl.BlockSpec((tm, SB_W), lambda i: (i, qb + 2)), part, part, part, vec, vec, part, part],
        out_specs=[pl.BlockSpec((tm, IN_W), lambda i: (i, 0)), ovec, ovec],
        out_shape=[jax.ShapeDtypeStruct((S, IN_W), BF16), jax.ShapeDtypeStruct((1, 128), F32),
                   jax.ShapeDtypeStruct((1, 128), F32)],
        name=name, compiler_params=_cp())(proj, proj, proj, dq, dk, dv, gq, gk, dpc, dps)


QSUB_FWD = 8
QSUB_BWD = 8


def _sb_consts():
    row = _iota((CHUNK, CHUNK), 0)
    col = _iota((CHUNK, CHUNK), 1)
    ones = jnp.ones((CHUNK, CHUNK), MXU_DT)
    m_gt = jnp.concatenate([(row > col).astype(MXU_DT), ones], axis=1)
    m_lt = jnp.concatenate([(row < col).astype(MXU_DT), ones], axis=1)
    return (col < HEAD_DIM, col - row, jnp.concatenate([m_gt, m_gt], axis=0), jnp.concatenate([m_lt, m_lt], axis=0))


def _split_heads(x, seg):
    z = jnp.zeros_like(x)
    return (jnp.where(seg, x, z), jnp.where(seg, z, x))


def _stack_heads(x, seg):
    return jnp.concatenate(_split_heads(x, seg), axis=0)


def _cumdot2(x, m2):
    hi = x.astype(MXU_DT)
    lo = (x - hi.astype(F32)).astype(MXU_DT)
    dn = (((1,), (0,)), ((), ()))
    return [lax.dot_general(jnp.concatenate([hi[:, CHUNK * h:CHUNK * (h + 1)], lo[:, CHUNK * h:CHUNK * (h + 1)]], axis=1),
                            m2, dn, preferred_element_type=F32) for h in range(2)]


class Exchange:
    def __init__(self, ins, bufs, outs, n, build):
        self.ins, self.bufs, self.outs, self.n, self.build = list(ins), list(bufs), list(outs), n, build

    def split(self, refs, n_in, n_out):
        a, b, o = len(self.ins), len(self.bufs), len(self.outs)
        main_in = refs[:n_in]
        c_in = refs[n_in:n_in + a]
        rest = refs[n_in + a + b:]
        main_out = rest[:n_out]
        c_buf = rest[n_out:n_out + b]
        c_out = rest[n_out + b:n_out + b + o]
        send, recv = rest[n_out + b + o:]
        return main_in, main_out, lambda: self.build(c_in, c_buf, c_out, send, recv)


def _with_exchange(ex, n_in, n_out, in_specs, out_specs, out_shape):
    if ex is None:
        return in_specs, out_specs, out_shape, {}, [], []
    a, b = len(ex.ins), len(ex.bufs)
    in_specs = list(in_specs) + [ANY] * (a + b)
    out_specs = list(out_specs) + [ANY] * (b + len(ex.outs))
    out_shape = list(out_shape) + [jax.ShapeDtypeStruct(x.shape, x.dtype) for x in ex.bufs] + list(ex.outs)
    aliases = {n_in + a + i: n_out + i for i in range(b)}
    scratch = [pltpu.SemaphoreType.DMA((ex.n,)), pltpu.SemaphoreType.DMA((ex.n,))]
    return in_specs, out_specs, out_shape, aliases, scratch, ex.ins + ex.bufs


def attn_fwd(qkv, name, ex=None):
    S = qkv.shape[0]
    npair = SB_W // 128
    QSUB = QSUB_FWD
    QT = QSUB * CHUNK
    nq = S // QT

    def body(*refs):
        if ex is None:
            (q_ref, k_ref, v_ref), (o_ref, rt_ref), copies = refs[:3], refs[3:5], None
        else:
            (q_ref, k_ref, v_ref), (o_ref, rt_ref), copies = ex.split(refs, 3, 2)

            @pl.when(jnp.logical_and(pl.program_id(0) == 0, pl.program_id(1) == 0))
            def _():
                for cp in copies():
                    cp.start()

        qi = pl.program_id(1)
        seg, dcol, m_gt, _ = _sb_consts()
        qs = [q_ref[a * CHUNK:(a + 1) * CHUNK, :] for a in range(QSUB)]
        causal2 = jnp.concatenate([dcol < 0, dcol < 0], axis=1)

        def step(kb, carry, diag):
            off = pl.multiple_of(kb * CHUNK, CHUNK)
            kk = _stack_heads(k_ref[pl.ds(off, CHUNK), :], seg)
            vv = _stack_heads(v_ref[pl.ds(off, CHUNK), :], seg)
            act = [a for a in range(QSUB) if diag is None or a >= diag]
            z = [_dot(qs[a], kk, "nt") for a in act]
            sp = [_softplus(x) for x in z]
            lnb = [jnp.where(causal2, -s, 0.0) if a == diag else -s for a, s in zip(act, sp)]
            cs = [_cumdot2(x, m_gt) for x in lnb]
            att = []
            for n, a in enumerate(act):
                base = z[n] - sp[n]
                e = jnp.exp(base + jnp.concatenate([cs[n][h][:, :CHUNK] + carry[3 * a + 1 + h] for h in range(2)], axis=1))
                att.append(jnp.where(causal2, e, 0.0) if a == diag else e)
            pv = [_dot(x, vv, "nn") for x in att]
            new = list(carry)
            for n, a in enumerate(act):
                new[3 * a] = carry[3 * a] + pv[n]
                for h in range(2):
                    new[3 * a + 1 + h] = carry[3 * a + 1 + h] + cs[n][h][:, CHUNK:]
            return tuple(new)

        z0 = jnp.zeros((CHUNK, CHUNK), F32)
        res = (z0,) * (3 * QSUB)
        for j in reversed(range(QSUB)):
            res = step(QSUB * qi + j, res, j)
        res = lax.fori_loop(0, QSUB * qi, lambda it, c: step(QSUB * qi - 1 - it, c, None), res)
        for a in range(QSUB):
            rows = slice(a * CHUNK, (a + 1) * CHUNK)
            o_ref[rows, :] = res[3 * a]
            rt_ref[rows, :] = jnp.concatenate([res[3 * a + 1], res[3 * a + 2]], axis=1)

        if ex is not None:
            @pl.when(jnp.logical_and(pl.program_id(0) == npair - 1, pl.program_id(1) == nq - 1))
            def _():
                for cp in copies():
                    cp.wait()

    in_specs, out_specs, out_shape, aliases, scratch, extra = _with_exchange(
        ex, 3, 2,
        [pl.BlockSpec((QT, 128), lambda p, i: (i, p)), pl.BlockSpec((S, 128), lambda p, i: (0, npair + p)),
         pl.BlockSpec((S, 128), lambda p, i: (0, 2 * npair + p))],
        [pl.BlockSpec((QT, 128), lambda p, i: (i, p)), pl.BlockSpec((QT, 256), lambda p, i: (i, p))],
        [jax.ShapeDtypeStruct((S, SB_W), F32), jax.ShapeDtypeStruct((S, 2 * SB_W), F32)])
    return pl.pallas_call(
        body, grid=(npair, nq), in_specs=in_specs, out_specs=out_specs, out_shape=out_shape,
        input_output_aliases=aliases, scratch_shapes=scratch, name=name, compiler_params=_cp())(qkv, qkv, qkv, *extra)


def attn_bwd(qkv, rt, do, name, ex=None):
    S = qkv.shape[0]
    npair = SB_W // 128
    QSUB = QSUB_BWD
    QT = QSUB * CHUNK
    nq = S // QT

    def body(*refs):
        if ex is None:
            (q_ref, k_ref, v_ref, rt_ref, do_ref), (dq_ref, dk_ref, dv_ref), copies = refs[:5], refs[5:8], None
        else:
            (q_ref, k_ref, v_ref, rt_ref, do_ref), (dq_ref, dk_ref, dv_ref), copies = ex.split(refs, 5, 3)

            @pl.when(jnp.logical_and(pl.program_id(0) == 0, pl.program_id(1) == 0))
            def _():
                for cp in copies():
                    cp.start()

        qi = pl.program_id(1)

        @pl.when(qi == 0)
        def _():
            dk_ref[...] = jnp.zeros_like(dk_ref)
            dv_ref[...] = jnp.zeros_like(dv_ref)

        seg, dcol, m_gt, m_lt = _sb_consts()
        qs, dos, qq, dd, rtot = [], [], [], [], []
        for a in range(QSUB):
            rows = slice(a * CHUNK, (a + 1) * CHUNK)
            qs.append(q_ref[rows, :])
            dos.append(do_ref[rows, :].astype(MXU_DT))
            qq.append(_stack_heads(qs[a], seg))
            dd.append(_stack_heads(dos[a], seg))
            rtot.append(rt_ref[rows, :])
        causal2 = jnp.concatenate([dcol < 0, dcol < 0], axis=1)

        def step(kb, carry, diag):
            off = pl.multiple_of(kb * CHUNK, CHUNK)
            kk = _stack_heads(k_ref[pl.ds(off, CHUNK), :], seg)
            vv = _stack_heads(v_ref[pl.ds(off, CHUNK), :], seg)
            act = [a for a in range(QSUB) if diag is None or a >= diag]
            z = [_dot(qs[a], kk, "nt") for a in act]
            da = [_dot(dos[a], vv, "nt") for a in act]
            sp = [_softplus(x) for x in z]
            lnb = [jnp.where(causal2, -s, 0.0) if a == diag else -s for a, s in zip(act, sp)]
            cs = [_cumdot2(x, m_gt) for x in lnb]
            lc = [[carry[5 * a + 1 + h] + cs[n][h][:, CHUNK:] for h in range(2)] for n, a in enumerate(act)]
            att = []
            for n, a in enumerate(act):
                btw = jnp.concatenate([cs[n][h][:, :CHUNK] - lc[n][h] for h in range(2)], axis=1)
                e = jnp.exp(z[n] - sp[n] + btw + rtot[a])
                att.append(jnp.where(causal2, e, 0.0) if a == diag else e)
            g = [da[n] * att[n] for n in range(len(act))]
            cg = [_cumdot2(x, m_lt) for x in g]
            dz = []
            for n, a in enumerate(act):
                sig = jnp.exp(z[n] - sp[n])
                pre = jnp.concatenate([carry[5 * a + 3 + h] + cg[n][h][:, :CHUNK] for h in range(2)], axis=1)
                d = g[n] * (1.0 - sig) - pre * sig
                dz.append((jnp.where(causal2, d, 0.0) if a == diag else d).astype(MXU_DT))
            attb = [x.astype(MXU_DT) for x in att]
            dqc = [_dot(x, kk, "nn") for x in dz]
            dkc = [_dot(jnp.concatenate([dz[n][:, :CHUNK], dz[n][:, CHUNK:]], axis=0), qq[a], "tn")
                   for n, a in enumerate(act)]
            dvc = [_dot(jnp.concatenate([attb[n][:, :CHUNK], attb[n][:, CHUNK:]], axis=0), dd[a], "tn")
                   for n, a in enumerate(act)]
            dk_ref[pl.ds(off, CHUNK), :] += functools.reduce(lambda x, y: x + y, dkc)
            dv_ref[pl.ds(off, CHUNK), :] += functools.reduce(lambda x, y: x + y, dvc)
            new = list(carry)
            for n, a in enumerate(act):
                new[5 * a] = carry[5 * a] + dqc[n]
                for h in range(2):
                    new[5 * a + 1 + h] = lc[n][h]
                    new[5 * a + 3 + h] = carry[5 * a + 3 + h] + cg[n][h][:, CHUNK:]
            return tuple(new)

        z0 = jnp.zeros((CHUNK, 128), F32)
        res = lax.fori_loop(0, QSUB * qi, lambda kb, c: step(kb, c, None), (z0,) * (5 * QSUB))
        for j in range(QSUB):
            res = step(QSUB * qi + j, res, j)
        for a in range(QSUB):
            dq_ref[a * CHUNK:(a + 1) * CHUNK, :] = res[5 * a]

        if ex is not None:
            @pl.when(jnp.logical_and(pl.program_id(0) == npair - 1, pl.program_id(1) == nq - 1))
            def _():
                for cp in copies():
                    cp.wait()

    blk = pl.BlockSpec((QT, 128), lambda p, i: (i, p))
    full = pl.BlockSpec((S, 128), lambda p, i: (0, p))
    sh = jax.ShapeDtypeStruct((S, SB_W), F32)
    in_specs, out_specs, out_shape, aliases, scratch, extra = _with_exchange(
        ex, 5, 3,
        [blk, pl.BlockSpec((S, 128), lambda p, i: (0, npair + p)), pl.BlockSpec((S, 128), lambda p, i: (0, 2 * npair + p)),
         pl.BlockSpec((QT, 256), lambda p, i: (i, p)), blk],
        [blk, full, full], [sh, sh, sh])
    return pl.pallas_call(
        body, grid=(npair, nq), in_specs=in_specs, out_specs=out_specs, out_shape=out_shape,
        input_output_aliases=aliases, scratch_shapes=scratch, name=name,
        compiler_params=_cp())(qkv, qkv, qkv, rt, do, *extra)


def _adamw_math(w, g, m, v):
    m = ADAM_B1 * m + (1.0 - ADAM_B1) * g
    v = ADAM_B2 * v + (1.0 - ADAM_B2) * (g * g)
    m_hat = m / BC1
    v_hat = v / BC2
    delta = -ADAM_LR * (m_hat / (jnp.sqrt(v_hat) + ADAM_EPS) + ADAM_WD * w)
    return delta, m, v


def adamw_layer(w4, m4, v4, g, outs, l, name, tr):
    L, R, C = w4.shape
    n_alias = 0 if outs is None else 4

    def body(*refs):
        w_ref, m_ref, v_ref, g_ref = refs[:4]
        go_ref, d_ref, mo_ref, vo_ref = refs[4 + n_alias:]
        g = g_ref[...]
        d, m, v = _adamw_math(w_ref[0], g, m_ref[0], v_ref[0])
        go_ref[0] = g
        d_ref[0] = d
        mo_ref[0] = m
        vo_ref[0] = v

    st = pl.BlockSpec((1, tr, C), lambda i: (l, i, 0))
    sh = jax.ShapeDtypeStruct((L, R, C), F32)
    return pl.pallas_call(
        body, grid=(R // tr,),
        in_specs=[st, st, st, pl.BlockSpec((tr, C), lambda i: (i, 0))] + [ANY] * n_alias,
        out_specs=[st, st, st, st], out_shape=[sh, sh, sh, sh],
        input_output_aliases={4 + i: i for i in range(n_alias)}, name=name,
        compiler_params=_cp())(w4, m4, v4, g, *(outs or ()))


def adamw_flat(w, m, v, g, name):
    R, C = w.shape

    def body(w_ref, m_ref, v_ref, g_ref, d_ref, mo_ref, vo_ref):
        d, m2, v2 = _adamw_math(w_ref[...], g_ref[...], m_ref[...], v_ref[...])
        d_ref[...] = d
        mo_ref[...] = m2
        vo_ref[...] = v2

    full = pl.BlockSpec((R, C), lambda i: (0, 0))
    sh = jax.ShapeDtypeStruct((R, C), F32)
    return pl.pallas_call(body, grid=(1,), in_specs=[full] * 4, out_specs=[full] * 3, out_shape=[sh] * 3,
                          name=name, compiler_params=_cp())(w, m, v, g)


def add_halves(gl, l1s, c, name):
    n = len(gl)
    nk = gl[0].shape[0]

    def body(c_ref, *refs):
        for g_ref, l_ref, o_ref in zip(refs[:n], refs[n:2 * n], refs[2 * n:]):
            o_ref[...] = (g_ref[...].astype(F32) + l_ref[...].astype(F32)).astype(o_ref.dtype)

    def half(g, own):
        blk = (1, g.shape[1] // 2, g.shape[2])
        return pl.BlockSpec(blk, (lambda k, c_ref: (k, c_ref[0], 0)) if own else (lambda k, c_ref: (k, 0, 0)))

    gs = pltpu.PrefetchScalarGridSpec(
        num_scalar_prefetch=1, grid=(nk,),
        in_specs=[half(g, True) for g in gl] + [half(g, False) for g in gl],
        out_specs=[half(g, False) for g in gl])
    return pl.pallas_call(
        body, grid_spec=gs,
        out_shape=[jax.ShapeDtypeStruct((nk, g.shape[1] // 2, g.shape[2]), g.dtype) for g in gl],
        name=name, compiler_params=_cp())(c, *gl, *l1s)


def sum_chips_all(ps, l2s, kc, name):
    n = len(ps)

    def body(k_ref, *refs):
        for i in range(n):
            p_ref, (a_ref, b_ref, c_ref), o_ref = refs[i], refs[n + 3 * i:n + 3 * i + 3], refs[4 * n + i]
            o_ref[0] = ((p_ref[0].astype(F32) + a_ref[0].astype(F32)) + b_ref[0].astype(F32)) + c_ref[0].astype(F32)

    def blk(p):
        return (1,) + p.shape[1:]

    gs = pltpu.PrefetchScalarGridSpec(
        num_scalar_prefetch=1, grid=(1,),
        in_specs=[pl.BlockSpec(blk(p), lambda t, k_ref: (k_ref[0], 0, 0)) for p in ps]
        + [pl.BlockSpec(blk(p), functools.partial(lambda t, k_ref, j: (j, 0, 0), j=j)) for p in ps for j in range(3)],
        out_specs=[pl.BlockSpec(blk(p), lambda t, k_ref: (k_ref[1], 0, 0)) for p in ps])
    return pl.pallas_call(
        body, grid_spec=gs, out_shape=[jax.ShapeDtypeStruct((2,) + p.shape[1:], F32) for p in ps],
        name=name, compiler_params=_cp())(kc, *ps, *[l2 for l2 in l2s for _ in range(3)])


def place_slot(a3, l, idx, nslot, dtype, name, tr):
    _, R, C = a3.shape

    def body(i_ref, a_ref, o_ref):
        o_ref[0] = a_ref[0].astype(dtype)

    gs = pltpu.PrefetchScalarGridSpec(
        num_scalar_prefetch=1, grid=(R // tr,),
        in_specs=[pl.BlockSpec((1, tr, C), lambda t, i_ref: (l, t, 0))],
        out_specs=pl.BlockSpec((1, tr, C), lambda t, i_ref: (i_ref[0], t, 0)))
    return pl.pallas_call(body, grid_spec=gs, out_shape=jax.ShapeDtypeStruct((nslot, R, C), dtype), name=name,
                          compiler_params=_cp())(idx, a3)


def _place():
    x, y, c = lax.axis_index("x"), lax.axis_index("y"), lax.axis_index("c")
    chips = [(1 - x, y), (x, 1 - y), (1 - x, 1 - y)]
    return x, y, c, chips


def _rcopy(src, dst, send_sems, recv_sems, k, to):
    return pltpu.make_async_remote_copy(src_ref=src, dst_ref=dst, send_sem=send_sems.at[k], recv_sem=recv_sems.at[k],
                                        device_id=to, device_id_type=MESH)


def gather_exchange(bufs):
    def build(in_refs, buf_refs, out_refs, send_sems, recv_sems):
        x, y, c, chips = _place()
        cps = []
        for i, ref in enumerate(buf_refs):
            rh = bufs[i].shape[1] // 2
            mine = ref.at[2 * x + y].at[pl.ds(c * rh, rh), :]
            for j, chip in enumerate(chips):
                cps.append(_rcopy(mine, mine, send_sems, recv_sems, 3 * i + j, (*chip, c)))
        return cps

    return Exchange([], bufs, [], 3 * len(bufs), build)


def scatter_exchange(ps):
    def build(in_refs, buf_refs, out_refs, send_sems, recv_sems):
        x, y, c, chips = _place()
        return [_rcopy(ref.at[2 * chip[0] + chip[1]], out_refs[i].at[j], send_sems, recv_sems, 3 * i + j, (*chip, c))
                for i, ref in enumerate(in_refs) for j, chip in enumerate(chips)]

    outs = [jax.ShapeDtypeStruct((3,) + p.shape[1:], p.dtype) for p in ps]
    return Exchange(ps, [], outs, 3 * len(ps), build)


def run_exchange(ex, name):
    def body(*refs):
        _, _, copies = ex.split(refs, 0, 0)
        cps = copies()
        for cp in cps:
            cp.start()
        for cp in cps:
            cp.wait()

    in_specs, out_specs, out_shape, aliases, scratch, extra = _with_exchange(ex, 0, 0, [], [], [])
    return pl.pallas_call(body, in_specs=in_specs, out_specs=out_specs, out_shape=out_shape,
                          input_output_aliases=aliases, scratch_shapes=scratch, name=name)(*extra)


def gather_forward(bufs, name):
    n = len(bufs)

    def body(*refs):
        outs = refs[n:2 * n]
        send_sems, recv_sems = refs[2 * n:]
        x, y, c, chips = _place()
        cps = []
        for i in range(n):
            rh = bufs[i].shape[1] // 2
            for j, chip in enumerate(chips):
                landed = outs[i].at[2 * chip[0] + chip[1]].at[pl.ds(c * rh, rh), :]
                cps.append(_rcopy(landed, landed, send_sems, recv_sems, 3 * i + j, (x, y, 1 - c)))
        for cp in cps:
            cp.start()
        for cp in cps:
            cp.wait()

    return pl.pallas_call(
        body, in_specs=[ANY] * n, out_specs=[ANY] * n,
        out_shape=[jax.ShapeDtypeStruct(b.shape, b.dtype) for b in bufs],
        input_output_aliases={i: i for i in range(n)},
        scratch_shapes=[pltpu.SemaphoreType.DMA((3 * n,)), pltpu.SemaphoreType.DMA((3 * n,))], name=name)(*bufs)


def exchange_sibling_half(gs, name):
    n = len(gs)

    def body(*refs):
        ins, outs = refs[:n], refs[n:2 * n]
        send_sems, recv_sems = refs[2 * n:]
        x, y, c, _ = _place()
        cps = []
        for i in range(n):
            rh = gs[i].shape[1] // 2
            cp = pltpu.make_async_remote_copy(
                src_ref=ins[i].at[:, pl.ds((1 - c) * rh, rh), :], dst_ref=outs[i], send_sem=send_sems.at[i],
                recv_sem=recv_sems.at[i], device_id=(x, y, 1 - c), device_id_type=MESH)
            cp.start()
            cps.append(cp)
        for cp in cps:
            cp.wait()

    return pl.pallas_call(
        body, in_specs=[ANY] * n, out_specs=[ANY] * n,
        out_shape=[jax.ShapeDtypeStruct((g.shape[0], g.shape[1] // 2, g.shape[2]), g.dtype) for g in gs],
        scratch_shapes=[pltpu.SemaphoreType.DMA((n,)), pltpu.SemaphoreType.DMA((n,))],
        name=name)(*gs)


def join_sibling_halves(fs, name):
    n = len(fs)

    def body(*refs):
        outs = refs[n:2 * n]
        send_sems, recv_sems = refs[2 * n:]
        x, y, c, _ = _place()
        cps = [_rcopy(outs[i].at[c], outs[i].at[c], send_sems, recv_sems, i, (x, y, 1 - c)) for i in range(n)]
        for cp in cps:
            cp.start()
        for cp in cps:
            cp.wait()

    return pl.pallas_call(
        body, in_specs=[ANY] * n, out_specs=[ANY] * n,
        out_shape=[jax.ShapeDtypeStruct(f.shape, f.dtype) for f in fs],
        input_output_aliases={i: i for i in range(n)},
        scratch_shapes=[pltpu.SemaphoreType.DMA((n,)), pltpu.SemaphoreType.DMA((n,))], name=name)(*fs)


def all_reduce_small(v, name):
    R, C = v.shape

    def body(v_ref, sum_ref, all_ref, send_sems, recv_sems, local_sem):
        x, y, c, chips = _place()
        me, sib = (x, y, c), (x, y, 1 - c)

        def slab(px, py, pc):
            return all_ref.at[4 * px + 2 * py + pc]

        def copy(k, block, to, src=None):
            return pltpu.make_async_remote_copy(
                src_ref=slab(*block) if src is None else src, dst_ref=slab(*block), send_sem=send_sems.at[k],
                recv_sem=recv_sems.at[k], device_id=to, device_id_type=MESH)

        mine = pltpu.make_async_copy(v_ref, slab(*me), local_sem)
        mine.start()
        first = [copy(0, me, sib, src=v_ref)] + [copy(1 + j, me, (*chip, c), src=v_ref) for j, chip in enumerate(chips)]
        for cp in first:
            cp.start()
        passed = [copy(4 + j, (*chip, c), sib) for j, chip in enumerate(chips)]
        for j, chip in enumerate(chips):
            copy(1 + j, (*chip, c), me).wait_recv()
            passed[j].start()
        copy(0, sib, me).wait_recv()
        for j, chip in enumerate(chips):
            copy(4 + j, (*chip, 1 - c), me).wait_recv()
        for cp in first + passed:
            cp.wait_send()
        mine.wait()
        acc = all_ref[0]
        for d in range(1, 8):
            acc = acc + all_ref[d]
        sum_ref[...] = acc

    vm = pl.BlockSpec(memory_space=pltpu.VMEM)
    return pl.pallas_call(
        body, in_specs=[vm], out_specs=[vm, vm],
        out_shape=[jax.ShapeDtypeStruct((R, C), F32), jax.ShapeDtypeStruct((8, R, C), F32)],
        scratch_shapes=[pltpu.SemaphoreType.DMA((7,)), pltpu.SemaphoreType.DMA((7,)), pltpu.SemaphoreType.DMA],
        name=name, compiler_params=_cp())(v)[0]


SMALL = ("mix_norm_g", "conv_w", "conv_b", "conv_ln_g", "conv_ln_b", "sg_ln_g", "sg_ln_b", "sg_w", "sg_b",
         "q_norm_g", "k_norm_g", "out_norm_g", "ffn_norm_g")
BIG = ("w_in", "w_out", "w_gate_up", "w_down")
WEIGHTS = ("mix_norm_g", "w_in", "conv_w", "conv_b", "conv_ln_g", "conv_ln_b", "sg_ln_g", "sg_ln_b", "sg_w", "sg_b",
           "q_norm_g", "k_norm_g", "out_norm_g", "w_out", "ffn_norm_g", "w_gate_up", "w_down")
ADAM_ROWS = {"w_in": 512, "w_out": 256, "w_gate_up": 256, "w_down": 352}
AG_ROWS = {"w_in": 512, "w_out": 256, "w_gate_up": 512, "w_down": 352}


def _pack(parts):
    flat = jnp.concatenate([p.reshape(-1) for p in parts])
    n = flat.shape[0]
    rows = -(-n // (8 * 128)) * 8
    return jnp.pad(flat, (0, rows * 128 - n)).reshape(rows, 128)


def _unpack(buf, shapes):
    flat = buf.reshape(-1)
    out, off = [], 0
    for s in shapes:
        n = 1
        for d in s:
            n *= d
        out.append(flat[off:off + n].reshape(s))
        off += n
    return out


def layer_forward(x, l, P, W, ex=None, after=None, ex2=None, after2=None):
    sv = {"x": x}
    sv["h"] = rms_fwd(x, P["mix_norm_g"], l, f"rms_mix_{l}")
    sv["proj"] = mm_colblk(sv["h"], W["w_in"], F32, f"mm_in_{l}")
    sv["yc"] = conv_fwd(sv["proj"], P["conv_w"], P["conv_b"], P["conv_ln_g"], P["conv_ln_b"], l, f"conv_fwd_{l}")
    sv["ys"] = sg_fwd(sv["proj"], P["sg_ln_g"], P["sg_ln_b"], P["sg_wt"], P["sg_bt"], l, f"sg_fwd_{l}")
    sv["qkv"] = qkv_fwd(sv["proj"], P["q_norm_g"], P["k_norm_g"], l, f"qkv_fwd_{l}")
    sv["yb"], sv["rt"], *moved = attn_fwd(sv["qkv"], f"attn_fwd_{l}", ex)
    if after is not None:
        W = after(moved)
    sv["yn"] = outnorm_fwd(sv["yc"], sv["ys"], sv["yb"], P["out_norm_g"], l, f"outnorm_fwd_{l}")
    sv["x1"] = mm_res(sv["yn"], W["w_out"].reshape(D_MODEL, D_MODEL), x, f"mm_out_{l}")
    sv["h2"] = rms_fwd(sv["x1"], P["ffn_norm_g"], l, f"rms_ffn_{l}")
    sv["g"], sv["u"], sv["act"], *moved2 = ffn_up(sv["h2"], W["w_gate_up"], f"ffn_up_{l}", ex=ex2)
    if after2 is not None:
        W = after2(moved2)
    x2 = mm_res(sv["act"], W["w_down"].reshape(FFN, D_MODEL), sv["x1"], f"mm_down_{l}")
    return x2, sv, moved


def layer_backward(dx2, l, P, W, sv, mid=None):
    gb, gs = {}, {}
    wdown = W["w_down"].reshape(FFN, D_MODEL)
    dgu = ffn_down_bwd(dx2, wdown, sv["g"], sv["u"], f"ffn_down_bwd_{l}")
    gb["w_down"] = mm_wgrad(sv["act"], dx2, 1408, 512, f"wgrad_down_{l}", False).reshape(N_CHIP, FFN // N_CHIP, D_MODEL)
    dh2 = mm_dgrad_colblk(dgu, W["w_gate_up"], f"dgrad_gu_{l}")
    gb["w_gate_up"] = mm_wgrad(sv["h2"], dgu, 512, 1408, f"wgrad_gu_{l}", True)
    dx1, gs["ffn_norm_g"] = rms_bwd(dh2, sv["x1"], P["ffn_norm_g"], l, dx2, f"rms_ffn_bwd_{l}")
    dyn = mm_dgrad(dx1, W["w_out"].reshape(D_MODEL, D_MODEL), f"dgrad_out_{l}")
    gb["w_out"] = mm_wgrad(sv["yn"], dx1, 512, 1024, f"wgrad_out_{l}", False).reshape(N_CHIP, D_MODEL // N_CHIP, D_MODEL)
    dyc, dys, dyb, gs["out_norm_g"] = outnorm_bwd(dyn, sv["yc"], sv["ys"], sv["yb"], P["out_norm_g"], l,
                                                  f"outnorm_bwd_{l}")
    ex, done = mid(gb) if mid is not None else (None, None)
    dq, dk, dv, *moved = attn_bwd(sv["qkv"], sv["rt"], dyb, f"attn_bwd_{l}", ex)
    if done is not None:
        done(moved)
    dps, gs["sg_ln_g"], gs["sg_ln_b"], gs["sg_w"], dbt = sg_bwd(
        sv["proj"], dys, P["sg_ln_g"], P["sg_ln_b"], P["sg_wt"], P["sg_wtt"], P["sg_bt"], l, f"sg_bwd_{l}")
    gs["sg_b"] = dbt.T
    dpc, dcw, gs["conv_b"], gs["conv_ln_g"], gs["conv_ln_b"] = conv_bwd(
        sv["proj"], dyc, P["conv_w"], P["conv_b"], P["conv_ln_g"], P["conv_ln_b"], l, f"conv_bwd_{l}")
    gs["conv_w"] = dcw[:CONV_K]
    dproj, dgq, dgk = qkv_bwd(sv["proj"], dq, dk, dv, P["q_norm_g"], P["k_norm_g"], dpc, dps, l, f"qkv_bwd_{l}")
    gs["q_norm_g"] = dgq[0, :HEAD_DIM] + dgq[0, HEAD_DIM:]
    gs["k_norm_g"] = dgk[0, :HEAD_DIM] + dgk[0, HEAD_DIM:]
    dh = mm_dgrad_colblk(dproj, W["w_in"], f"dgrad_in_{l}")
    gb["w_in"] = mm_wgrad(sv["h"], dproj, 1024, IN_W // N_CHIP, f"wgrad_in_{l}", True)
    dx, gs["mix_norm_g"] = rms_bwd(dh, sv["x"], P["mix_norm_g"], l, dx1, f"rms_mix_bwd_{l}")
    return dx, gb, gs


def reduce_start(gl, items, c1):
    n0, l0 = items[0]
    l1 = exchange_sibling_half(gl, f"rs_sibling_{n0}_{l0}")
    return add_halves(gl, l1, c1, f"rs_add_{n0}_{l0}")


def reduce_finish(ps, l2, items, kc):
    n0, l0 = items[0]
    fs = sum_chips_all(ps, l2, kc, f"rs_sum_{n0}_{l0}")
    full = join_sibling_halves(fs, f"rs_join_{n0}_{l0}")
    return [f.reshape(2 * f.shape[1], f.shape[2]) for f in full]


def kernel(x, mix_norm_g, w_in, conv_w, conv_b, conv_ln_g, conv_ln_b, sg_ln_g, sg_ln_b, sg_w, sg_b, q_norm_g, k_norm_g, out_norm_g, w_out, ffn_norm_g, w_gate_up, w_down, loss_target, m_mix_norm_g, m_w_in, m_conv_w, m_conv_b, m_conv_ln_g, m_conv_ln_b, m_sg_ln_g, m_sg_ln_b, m_sg_w, m_sg_b, m_q_norm_g, m_k_norm_g, m_out_norm_g, m_w_out, m_ffn_norm_g, m_w_gate_up, m_w_down, v_mix_norm_g, v_w_in, v_conv_w, v_conv_b, v_conv_ln_g, v_conv_ln_b, v_sg_ln_g, v_sg_ln_b, v_sg_w, v_sg_b, v_q_norm_g, v_k_norm_g, v_out_norm_g, v_w_out, v_ffn_norm_g, v_w_gate_up, v_w_down):
    w = dict(mix_norm_g=mix_norm_g, w_in=w_in, conv_w=conv_w, conv_b=conv_b, conv_ln_g=conv_ln_g, conv_ln_b=conv_ln_b,
             sg_ln_g=sg_ln_g, sg_ln_b=sg_ln_b, sg_w=sg_w, sg_b=sg_b, q_norm_g=q_norm_g, k_norm_g=k_norm_g,
             out_norm_g=out_norm_g, w_out=w_out, ffn_norm_g=ffn_norm_g, w_gate_up=w_gate_up, w_down=w_down)
    m = dict(mix_norm_g=m_mix_norm_g, w_in=m_w_in, conv_w=m_conv_w, conv_b=m_conv_b, conv_ln_g=m_conv_ln_g,
             conv_ln_b=m_conv_ln_b, sg_ln_g=m_sg_ln_g, sg_ln_b=m_sg_ln_b, sg_w=m_sg_w, sg_b=m_sg_b,
             q_norm_g=m_q_norm_g, k_norm_g=m_k_norm_g, out_norm_g=m_out_norm_g, w_out=m_w_out,
             ffn_norm_g=m_ffn_norm_g, w_gate_up=m_w_gate_up, w_down=m_w_down)
    v = dict(mix_norm_g=v_mix_norm_g, w_in=v_w_in, conv_w=v_conv_w, conv_b=v_conv_b, conv_ln_g=v_conv_ln_g,
             conv_ln_b=v_conv_ln_b, sg_ln_g=v_sg_ln_g, sg_ln_b=v_sg_ln_b, sg_w=v_sg_w, sg_b=v_sg_b,
             q_norm_g=v_q_norm_g, k_norm_g=v_k_norm_g, out_norm_g=v_out_norm_g, w_out=v_w_out,
             ffn_norm_g=v_ffn_norm_g, w_gate_up=v_w_gate_up, w_down=v_w_down)
    L = DEPTH
    xi, yi, ci = lax.axis_index("x"), lax.axis_index("y"), lax.axis_index("c")
    kme = 2 * xi + yi
    c1 = ci.astype(jnp.int32).reshape(1)
    k1 = kme.astype(jnp.int32).reshape(1)
    kc = jnp.stack([kme, ci]).astype(jnp.int32)

    def gather_parts(items):
        return [place_slot(w[n], l, k1, N_CHIP, BF16, f"ag_own_{n}_{l}", AG_ROWS[n]) for n, l in items]

    cw_sh = jnp.pad(conv_w.reshape(L * CONV_K, CONV_W // N_CHIP), ((0, 128 - L * CONV_K), (0, 0)))
    cw_buf = place_slot(cw_sh[None], 0, k1, N_CHIP, F32, "ag_own_conv", 128)
    *bufs, cw_buf = run_exchange(gather_exchange(gather_parts([("w_in", 0)]) + [cw_buf]), "ag_chips_first")
    *bufs, cw_buf = gather_forward(bufs + [cw_buf], "ag_sibling_first")
    W = [{} for _ in range(L)]
    W[0]["w_in"] = bufs[0]
    cw_all = cw_buf[:, :L * CONV_K].reshape(N_CHIP, L, CONV_K, CONV_W // N_CHIP)
    cw_full = jnp.transpose(cw_all, (1, 2, 0, 3)).reshape(L, CONV_K, CONV_W)

    tril = jnp.tril(jnp.ones((CHUNK, CHUNK), bool))
    sg_wt = jnp.where(tril, sg_w, 0.0)
    P = {
        "mix_norm_g": mix_norm_g.reshape(L, 1, D_MODEL), "ffn_norm_g": ffn_norm_g.reshape(L, 1, D_MODEL),
        "out_norm_g": out_norm_g.reshape(L, 1, D_MODEL),
        "conv_w": jnp.pad(cw_full, ((0, 0), (0, 1), (0, 0))), "conv_b": conv_b.reshape(L, 1, CONV_W),
        "conv_ln_g": conv_ln_g.reshape(L, 1, CONV_W), "conv_ln_b": conv_ln_b.reshape(L, 1, CONV_W),
        "sg_ln_g": sg_ln_g.reshape(L, 1, SG_W), "sg_ln_b": sg_ln_b.reshape(L, 1, SG_W),
        "sg_wt": sg_wt.astype(MXU_DT), "sg_wtt": jnp.swapaxes(sg_wt, 2, 3).astype(MXU_DT),
        "sg_bt": jnp.swapaxes(sg_b, 1, 2),
        "q_norm_g": jnp.tile(q_norm_g, (1, 2)).reshape(L, 1, 128), "k_norm_g": jnp.tile(k_norm_g, (1, 2)).reshape(L, 1, 128),
    }

    h = x[0]
    saved = []
    for l in range(L):
        items = [("w_out", l), ("w_gate_up", l)] + ([("w_in", l + 1)] if l + 1 < L else [])
        items2 = [("w_down", l)]

        def after(moved, l=l, items=items, tag="a"):
            for (n, ll), a in zip(items, gather_forward(moved, f"ag_sibling_{tag}_{l}")):
                W[ll][n] = a
            return W[l]

        h, sv, _ = layer_forward(h, l, P, W[l], gather_exchange(gather_parts(items)), after,
                                 gather_exchange(gather_parts(items2)),
                                 functools.partial(after, l=l, items=items2, tag="b"))
        saved.append(sv)
    dy, loss_part = loss_head(h, loss_target[0], "loss_head")
    loss = lax.psum(loss_part[0, 0], ("x", "y", "c"))

    outs = {n: None for n in BIG}
    small_grads = [None] * L

    def finish(items, ps, l2):
        for (n, lyr), g in zip(items, reduce_finish(ps, l2, items, kc)):
            outs[n] = adamw_layer(w[n], m[n], v[n], g, outs[n], lyr, f"adamw_{n}_{lyr}", ADAM_ROWS[n])

    g_in = None
    for l in reversed(range(L)):
        def mid(gb, l=l, g_in=g_in):
            items = [("w_down", l), ("w_gate_up", l), ("w_out", l)] + ([("w_in", l + 1)] if g_in is not None else [])
            ps = reduce_start([gb[n] for n, _ in items[:3]] + ([g_in] if g_in is not None else []), items, c1)
            return scatter_exchange(ps), lambda moved: finish(items, ps, moved)

        dy, gb, small_grads[l] = layer_backward(dy, l, P, W[l], saved[l], mid)
        g_in = gb["w_in"]
    ps = reduce_start([g_in], [("w_in", 0)], c1)
    finish([("w_in", 0)], ps, run_exchange(scatter_exchange(ps), "rs_chips_last"))

    shapes = [(L,) + small_grads[0][n].shape for n in SMALL]
    packed = _pack([jnp.stack([small_grads[l][n] for l in range(L)]) for n in SMALL])
    gsum = dict(zip(SMALL, _unpack(all_reduce_small(packed, "ar_small"), shapes)))
    gsum["conv_w"] = lax.dynamic_slice_in_dim(gsum["conv_w"], kme * (CONV_W // N_CHIP), CONV_W // N_CHIP, axis=2)
    gsum = {n: gsum[n].reshape(w[n].shape) for n in SMALL}
    lshapes = [w[n].shape for n in SMALL]
    d_p, m_p, v_p = adamw_flat(_pack([w[n] for n in SMALL]), _pack([m[n] for n in SMALL]),
                               _pack([v[n] for n in SMALL]), _pack([gsum[n] for n in SMALL]), "adamw_small")
    d_s = dict(zip(SMALL, _unpack(d_p, lshapes)))
    m_s = dict(zip(SMALL, _unpack(m_p, lshapes)))
    v_s = dict(zip(SMALL, _unpack(v_p, lshapes)))

    grads = {n: (outs[n][0] if n in BIG else gsum[n]) for n in WEIGHTS}
    delta = {n: (outs[n][1] if n in BIG else d_s[n]) for n in WEIGHTS}
    new_m = {n: (outs[n][2] if n in BIG else m_s[n]) for n in WEIGHTS}
    new_v = {n: (outs[n][3] if n in BIG else v_s[n]) for n in WEIGHTS}
    return (loss, dy[None], *[grads[n] for n in WEIGHTS], *[delta[n] for n in WEIGHTS],
            *[new_m[n] for n in WEIGHTS], *[new_v[n] for n in WEIGHTS])
```
